```python
import math
import jax, jax.numpy as jnp
from jax import lax
import numpy as np

D_MODEL = 2048
BATCH = 2
SEQ = 4096
DEPTH = 1

MEM_LEN = 256
ATT_HEADS = 8
HEAD_DIM = 128
ATT_WIDTH = ATT_HEADS * HEAD_DIM
IDX_HEADS = 16
IDX_DIM = 64
TOPK_MAX = 256
Q_BLOCK = 128
POOL_WINDOWS = (2, 4, 8, 16)
N_POOL = len(POOL_WINDOWS)
POOL_GROUP = 128
POOL_WIDTH = N_POOL * POOL_GROUP
MEM_HEADS = 4
MEM_WIDTH = MEM_HEADS * HEAD_DIM
N_BRANCH = 3
D_FF = 5632
ALPHA = (2 * DEPTH) ** 0.25
BETA = (8 * DEPTH) ** -0.25
LN_EPS = 1e-5
SPLIT_SIZES = (ATT_WIDTH, ATT_WIDTH, ATT_WIDTH, IDX_HEADS * IDX_DIM, IDX_DIM, IDX_HEADS, POOL_WIDTH, MEM_WIDTH, N_BRANCH * D_MODEL)
SPLIT_IDX = tuple(int(v) for v in np.cumsum(SPLIT_SIZES)[:-1])
D_IN = int(sum(SPLIT_SIZES))

kernel_name = "hybrid_dsa_pool_mem_macaron_deepnorm"

f32 = jnp.float32


def layer_norm(x, g, b):
    x32 = x.astype(f32)
    mu = jnp.mean(x32, axis=-1, keepdims=True)
    var = jnp.mean(jnp.square(x32 - mu), axis=-1, keepdims=True)
    y = (x32 - mu) * lax.rsqrt(var + LN_EPS) * g.astype(f32) + b.astype(f32)
    return y.astype(x.dtype)


def swiglu(x, w_up, w_down):
    a, u = jnp.split(x @ w_up, 2, axis=-1)
    return (jax.nn.silu(a) * u) @ w_down


def alibi_slopes(n):
    return jnp.exp2(-8.0 * jnp.arange(1, n + 1, dtype=f32) / n)


def dsa_attention(q, k, v, q_idx, k_idx, w_idx):
    B, S = q.shape[0], q.shape[1]
    topk = min(TOPK_MAX, S // 4)
    nb = S // Q_BLOCK
    slopes = alibi_slopes(ATT_HEADS)
    key_pos = jnp.arange(S, dtype=jnp.int32)
    scale = HEAD_DIM ** -0.5

    def blockify(a):
        return a.reshape((B, nb, Q_BLOCK) + a.shape[2:]).swapaxes(0, 1)

    def one_block(args):
        qb, qib, wb, start = args
        q_pos = start + jnp.arange(Q_BLOCK, dtype=jnp.int32)
        causal = key_pos[None, :] <= q_pos[:, None]
        logits = jnp.einsum('bqhd,bsd->bqhs', qib, k_idx, preferred_element_type=f32) * (IDX_DIM ** -0.5)
        wts = wb.astype(f32) * (IDX_HEADS ** -0.5)
        score = jnp.einsum('bqh,bqhs->bqs', wts, jax.nn.relu(logits))
        score = jnp.where(causal[None], score, -jnp.inf)
        _, sel = lax.top_k(score, topk)
        valid = sel <= q_pos[None, :, None]
        k_sel = jax.vmap(lambda a, i: a[i])(k, sel)
        v_sel = jax.vmap(lambda a, i: a[i])(v, sel)
        s = jnp.einsum('bqhd,bqkhd->bqhk', qb, k_sel, preferred_element_type=f32) * scale
        dist = (q_pos[None, :, None] - sel).astype(f32)
        s = s - slopes[None, None, :, None] * dist[:, :, None, :]
        s = jnp.where(valid[:, :, None, :], s, -jnp.inf)
        p = jax.nn.softmax(s, axis=-1).astype(v.dtype)
        return jnp.einsum('bqhk,bqkhd->bqhd', p, v_sel)

    starts = jnp.arange(nb, dtype=jnp.int32) * Q_BLOCK
    out = lax.map(one_block, (blockify(q), blockify(q_idx), blockify(w_idx), starts))
    return out.swapaxes(0, 1).reshape(B, S, ATT_HEADS * HEAD_DIM)


def multiscale_pool(u, w_pool, pool_scale):
    B, S, _ = u.shape
    ug = u.reshape(B, S, N_POOL, POOL_GROUP).astype(f32)
    c = jnp.pad(jnp.cumsum(ug, axis=1), ((0, 0), (1, 0), (0, 0), (0, 0)))
    t = jnp.arange(S)
    outs = []
    for g, w in enumerate(POOL_WINDOWS):
        lo = jnp.maximum(t - w + 1, 0)
        cnt = (t - lo + 1).astype(f32)
        mean = (c[:, t + 1, g] - c[:, lo, g]) / cnt[None, :, None]
        outs.append(mean - ug[:, :, g])
    pooled = jnp.stack(outs, axis=2).astype(u.dtype)
    mixed = jnp.einsum('bsgc,gcd->bsgd', pooled, w_pool)
    return mixed.reshape(B, S, POOL_WIDTH) * pool_scale


def memory_attention(qm, mem, w_mem_kv):
    B, S = qm.shape[0], qm.shape[1]
    kv = (mem @ w_mem_kv).reshape(B, mem.shape[1], 2, MEM_HEADS, HEAD_DIM)
    km, vm = kv[:, :, 0], kv[:, :, 1]
    s = jnp.einsum('bqhd,bmhd->bhqm', qm, km, preferred_element_type=f32) * (HEAD_DIM ** -0.5)
    p = jax.nn.softmax(s, axis=-1).astype(vm.dtype)
    return jnp.einsum('bhqm,bmhd->bqhd', p, vm).reshape(B, S, MEM_WIDTH)


def token_mix(h, mem, w_in, b_gate, w_mem_kv, w_pool, pool_scale, w_br_att, w_br_pool, w_br_mem, w_out):
    B, S, D = h.shape
    z = h @ w_in
    q, k, v, qi, ki, wi, u, qm, gl = jnp.split(z, SPLIT_IDX, axis=-1)
    hd = (B, S, ATT_HEADS, HEAD_DIM)
    a = dsa_attention(q.reshape(hd), k.reshape(hd), v.reshape(hd), qi.reshape(B, S, IDX_HEADS, IDX_DIM), ki, wi)
    p = multiscale_pool(u, w_pool, pool_scale)
    m = memory_attention(qm.reshape(B, S, MEM_HEADS, HEAD_DIM), mem, w_mem_kv)
    gates = jax.nn.sigmoid((gl + b_gate).astype(f32)).astype(h.dtype).reshape(B, S, N_BRANCH, D)
    y = gates[:, :, 0] * (a @ w_br_att) + gates[:, :, 1] * (p @ w_br_pool) + gates[:, :, 2] * (m @ w_br_mem)
    return y @ w_out


def setup_inputs(seed: int = 0) -> dict:
    key = jax.random.key(seed)
    ks = jax.random.split(key, 21)
    L, D = DEPTH, D_MODEL

    def n(k, shape, scale):
        return jax.random.normal(k, shape, jnp.float32) * scale

    return {
        "x": n(ks[0], (BATCH, SEQ, D), 1.0),
        "mem": n(ks[1], (BATCH, MEM_LEN, D), 1.0),
        "w_ffn1_up": n(ks[2], (L, D, 2 * D_FF), D ** -0.5),
        "w_ffn1_down": n(ks[3], (L, D_FF, D), BETA * D_FF ** -0.5),
        "ln1_g": 1.0 + n(ks[4], (L, D), 0.02),
        "ln1_b": n(ks[5], (L, D), 0.02),
        "w_in": n(ks[6], (L, D, D_IN), D ** -0.5),
        "b_gate": n(ks[7], (L, N_BRANCH * D), 0.1),
        "w_mem_kv": n(ks[8], (L, D, 2 * MEM_WIDTH), D ** -0.5),
        "w_pool": n(ks[9], (L, N_POOL, POOL_GROUP, POOL_GROUP), POOL_GROUP ** -0.5),
        "pool_scale": 1.0 + n(ks[10], (L, POOL_WIDTH), 0.02),
        "w_br_att": n(ks[11], (L, ATT_WIDTH, D), ATT_WIDTH ** -0.5),
        "w_br_pool": n(ks[12], (L, POOL_WIDTH, D), POOL_WIDTH ** -0.5),
        "w_br_mem": n(ks[13], (L, MEM_WIDTH, D), MEM_WIDTH ** -0.5),
        "w_out": n(ks[14], (L, D, D), BETA * D ** -0.5),
        "ln2_g": 1.0 + n(ks[15], (L, D), 0.02),
        "ln2_b": n(ks[16], (L, D), 0.02),
        "w_ffn2_up": n(ks[17], (L, D, 2 * D_FF), D ** -0.5),
        "w_ffn2_down": n(ks[18], (L, D_FF, D), BETA * D_FF ** -0.5),
        "ln3_g": 1.0 + n(ks[19], (L, D), 0.02),
        "ln3_b": n(ks[20], (L, D), 0.02),
    }


def reference(x, mem, w_ffn1_up, w_ffn1_down, ln1_g, ln1_b, w_in, b_gate, w_mem_kv, w_pool, pool_scale, w_br_att, w_br_pool, w_br_mem, w_out, ln2_g, ln2_b, w_ffn2_up, w_ffn2_down, ln3_g, ln3_b):
    h = x
    for l in range(DEPTH):
        h = layer_norm(ALPHA * h + 0.5 * swiglu(h, w_ffn1_up[l], w_ffn1_down[l]), ln1_g[l], ln1_b[l])
        mix = token_mix(h, mem, w_in[l], b_gate[l], w_mem_kv[l], w_pool[l], pool_scale[l], w_br_att[l], w_br_pool[l], w_br_mem[l], w_out[l])
        h = layer_norm(ALPHA * h + mix, ln2_g[l], ln2_b[l])
        h = layer_norm(ALPHA * h + 0.5 * swiglu(h, w_ffn2_up[l], w_ffn2_down[l]), ln3_g[l], ln3_b[l])
    return h
```

```python
import functools
import math

import jax
import jax.numpy as jnp
from jax import lax
from jax.experimental import pallas as pl
from jax.experimental.pallas import tpu as pltpu

f32 = jnp.float32
bf16 = jnp.bfloat16
i32 = jnp.int32

D_MODEL = 2048
BATCH = 2
SEQ = 4096
DEPTH = 1
MEM_LEN = 256
ATT_HEADS = 8
HEAD_DIM = 128
ATT_WIDTH = ATT_HEADS * HEAD_DIM
IDX_HEADS = 16
IDX_DIM = 64
TOPK = min(256, SEQ // 4)
POOL_WINDOWS = (2, 4, 8, 16)
N_POOL = len(POOL_WINDOWS)
POOL_GROUP = 128
POOL_WIDTH = N_POOL * POOL_GROUP
MEM_HEADS = 4
MEM_WIDTH = MEM_HEADS * HEAD_DIM
N_BRANCH = 3
D_FF = 5632
ALPHA = (2 * DEPTH) ** 0.25
LN_EPS = 1e-5
TOKENS = BATCH * SEQ

LANES = 128
VMEM_LIMIT = 56 * 1024 * 1024

INT_MIN = -(2 ** 31)
LOG2E = math.log2(math.e)


def _params(*sem):
    return pltpu.CompilerParams(dimension_semantics=sem, vmem_limit_bytes=VMEM_LIMIT)


def _layer_norm(y, g, b):
    mu = jnp.mean(y, axis=-1, keepdims=True)
    d = y - mu
    var = jnp.mean(d * d, axis=-1, keepdims=True)
    return d * lax.rsqrt(var + LN_EPS) * g + b


FFN_TM = 512
FFN_TF = 512


def _ffn_ln_kernel(x_ref, wa_ref, wu_ref, wd_ref, g_ref, b_ref, o_ref, xb_ref):
    j = pl.program_id(1)

    @pl.when(j == 0)
    def _():
        xb_ref[...] = x_ref[...].astype(bf16)
        o_ref[...] = jnp.zeros_like(o_ref)

    xb = xb_ref[...]
    a = jnp.dot(xb, wa_ref[...], preferred_element_type=f32)
    u = jnp.dot(xb, wu_ref[...], preferred_element_type=f32)
    act = (a * jax.nn.sigmoid(a) * u).astype(bf16)
    o_ref[...] += jnp.dot(act, wd_ref[...], preferred_element_type=f32)

    @pl.when(j == pl.num_programs(1) - 1)
    def _():
        y = ALPHA * x_ref[...] + 0.5 * o_ref[...]
        o_ref[...] = _layer_norm(y, g_ref[...], b_ref[...])


def _ffn_ln(x, w_up, w_down, g, b):
    nf = D_FF // FFN_TF
    return pl.pallas_call(
        _ffn_ln_kernel,
        grid=(TOKENS // FFN_TM, nf),
        in_specs=[
            pl.BlockSpec((FFN_TM, D_MODEL), lambda i, j: (i, 0)),
            pl.BlockSpec((D_MODEL, FFN_TF), lambda i, j: (0, j)),
            pl.BlockSpec((D_MODEL, FFN_TF), lambda i, j: (0, j + nf)),
            pl.BlockSpec((FFN_TF, D_MODEL), lambda i, j: (j, 0)),
            pl.BlockSpec((1, D_MODEL), lambda i, j: (0, 0)),
            pl.BlockSpec((1, D_MODEL), lambda i, j: (0, 0)),
        ],
        out_specs=pl.BlockSpec((FFN_TM, D_MODEL), lambda i, j: (i, 0)),
        out_shape=jax.ShapeDtypeStruct((TOKENS, D_MODEL), f32),
        scratch_shapes=[pltpu.VMEM((FFN_TM, D_MODEL), bf16)],
        compiler_params=_params("parallel", "arbitrary"),
        name="ffn_ln",
    )(x, w_up, w_up, w_down, g, b)


def _matmul_kernel(x_ref, w_ref, o_ref, xb_ref):
    @pl.when(pl.program_id(1) == 0)
    def _():
        xb_ref[...] = x_ref[...].astype(bf16)

    o_ref[...] = jnp.dot(xb_ref[...], w_ref[...], preferred_element_type=f32).astype(o_ref.dtype)


def _matmul(x, w, out_dtype, tm, tn, name):
    m, k = x.shape
    n = w.shape[1]
    return pl.pallas_call(
        _matmul_kernel,
        grid=(m // tm, n // tn),
        in_specs=[
            pl.BlockSpec((tm, k), lambda i, j: (i, 0)),
            pl.BlockSpec((k, tn), lambda i, j: (0, j)),
        ],
        out_specs=pl.BlockSpec((tm, tn), lambda i, j: (i, j)),
        out_shape=jax.ShapeDtypeStruct((m, n), out_dtype),
        scratch_shapes=[pltpu.VMEM((tm, k), bf16)],
        compiler_params=_params("parallel", "arbitrary"),
        name=name,
    )(x, w)


DSA_TQ = 256
DSA_TK = 512
DSA_NCH = SEQ // DSA_TK
DSA_LT = DSA_TK // LANES


def _dsa_kernel(q_ref, k_ref, v_ref, qi_ref, kk_ref, w_ref, o_ref,
                key_scr, wb_scr, m_scr, l_scr, acc_scr):
    i = pl.program_id(1)
    q0 = i * DSA_TQ
    nk = (q0 + DSA_TQ - 1) // DSA_TK + 1

    qpos = q0 + lax.broadcasted_iota(i32, (DSA_TQ, DSA_TK), 0)
    kofs = lax.broadcasted_iota(i32, (DSA_TQ, DSA_TK), 1)

    w = w_ref[...] * (IDX_DIM ** -0.5 * IDX_HEADS ** -0.5)
    for h in range(IDX_HEADS):
        wb_scr[h] = jnp.broadcast_to(w[:, h:h + 1], (DSA_TQ, LANES))

    def score_chunk(c, carry):
        r0 = pl.multiple_of(c * DSA_TK, DSA_TK)
        acc = jnp.zeros((DSA_TQ, DSA_TK), f32)
        for p in range(IDX_HEADS // 2):
            slab = qi_ref[:, p * LANES:(p + 1) * LANES]
            for e in range(2):
                kk = kk_ref[e, pl.ds(r0, DSA_TK), :]
                lg = lax.dot_general(slab, kk, (((1,), (1,)), ((), ())),
                                     preferred_element_type=f32)
                wbh = wb_scr[2 * p + e]
                acc = acc + jnp.concatenate([wbh] * DSA_LT, axis=1) * jnp.maximum(lg, 0.0)
        bits = lax.bitcast_convert_type(acc, i32)
        key = bits ^ ((bits >> 31) & jnp.int32(0x7FFFFFFF))
        key = jnp.where(kofs + r0 <= qpos, key, jnp.int32(INT_MIN))
        key_scr[c] = key
        return carry

    lax.fori_loop(0, nk, score_chunk, 0)

    def count_ge(cand):
        def body(c, acc):
            m = (key_scr[c] >= cand).astype(i32)
            for t in range(DSA_LT):
                acc = acc + m[:, t * LANES:(t + 1) * LANES]
            return acc
        acc = lax.fori_loop(0, nk, body, jnp.zeros((DSA_TQ, LANES), i32))
        return jnp.sum(acc, axis=1, keepdims=True)

    thr = jnp.where(count_ge(jnp.zeros((DSA_TQ, 1), i32)) >= TOPK,
                    jnp.int32(0), jnp.int32(INT_MIN))

    def bit_step(b, thr):
        cand = thr | jnp.left_shift(jnp.int32(1), 30 - b)
        return jnp.where(count_ge(cand) >= TOPK, cand, thr)

    thr = lax.fori_loop(0, 31, bit_step, thr)
    thr = jnp.maximum(thr, jnp.int32(INT_MIN + 1))

    n_ge = count_ge(thr)

    @pl.when(jnp.max(n_ge) > TOPK)
    def _():
        def count_gt_eq(c, accs):
            a_gt, a_eq = accs
            kc = key_scr[c]
            gt = (kc > thr).astype(i32)
            eq = (kc == thr).astype(i32)
            for t in range(DSA_LT):
                a_gt = a_gt + gt[:, t * LANES:(t + 1) * LANES]
                a_eq = a_eq + eq[:, t * LANES:(t + 1) * LANES]
            return a_gt, a_eq
        z = jnp.zeros((DSA_TQ, LANES), i32)
        a_gt, _ = lax.fori_loop(0, nk, count_gt_eq, (z, z))
        need = TOPK - jnp.sum(a_gt, axis=1, keepdims=True)

        def count_eq_below(pos):
            def body(c, acc):
                kp = kofs + c * DSA_TK
                m = ((key_scr[c] == thr) & (kp < pos)).astype(i32)
                for t in range(DSA_LT):
                    acc = acc + m[:, t * LANES:(t + 1) * LANES]
                return acc
            acc = lax.fori_loop(0, nk, body, jnp.zeros((DSA_TQ, LANES), i32))
            return jnp.sum(acc, axis=1, keepdims=True)

        def pos_step(b, r):
            cand = r | jnp.left_shift(jnp.int32(1), (SEQ.bit_length() - 2) - b)
            return jnp.where(count_eq_below(cand) < need, cand, r)
        r = lax.fori_loop(0, SEQ.bit_length() - 1, pos_step, jnp.zeros((DSA_TQ, 1), i32))
        r = jnp.where(n_ge > TOPK, r, jnp.int32(SEQ))

        def drop(c, carry):
            kc = key_scr[c]
            kp = kofs + c * DSA_TK
            key_scr[c] = jnp.where((kc == thr) & (kp > r), jnp.int32(INT_MIN), kc)
            return carry
        lax.fori_loop(0, nk, drop, 0)

    m_scr[...] = jnp.full(m_scr.shape, -1e30, f32)
    l_scr[...] = jnp.zeros(l_scr.shape, f32)
    acc_scr[...] = jnp.zeros(acc_scr.shape, f32)
    c1 = HEAD_DIM ** -0.5 * LOG2E

    def attn_chunk(c, carry):
        r0 = pl.multiple_of(c * DSA_TK, DSA_TK)
        sel = key_scr[c] >= thr
        kd = (lax.broadcasted_iota(i32, (1, DSA_TK), 1) + (r0 - q0)).astype(f32)
        for h in range(ATT_HEADS):
            slope2 = 2.0 ** (-8.0 * (h + 1) / ATT_HEADS) * LOG2E
            qh = q_ref[:, h * HEAD_DIM:(h + 1) * HEAD_DIM]
            kh = k_ref[pl.ds(r0, DSA_TK), h * HEAD_DIM:(h + 1) * HEAD_DIM]
            vh = v_ref[pl.ds(r0, DSA_TK), h * HEAD_DIM:(h + 1) * HEAD_DIM]
            s = lax.dot_general(qh, kh, (((1,), (1,)), ((), ())), preferred_element_type=f32)
            t = s * c1 + kd * slope2
            t = jnp.where(sel, t, -jnp.inf)
            m_old = m_scr[h]
            m_new = jnp.maximum(m_old, jnp.max(t, axis=1, keepdims=True))
            alpha = jnp.exp2(m_old - m_new)
            p = jnp.exp2(t - m_new)
            l_scr[h] = alpha * l_scr[h] + jnp.sum(p, axis=1, keepdims=True)
            acc_scr[h] = alpha * acc_scr[h] + jnp.dot(p.astype(bf16), vh,
                                                      preferred_element_type=f32)
            m_scr[h] = m_new
        return carry

    lax.fori_loop(0, nk, attn_chunk, 0)

    for h in range(ATT_HEADS):
        o_ref[:, h * HEAD_DIM:(h + 1) * HEAD_DIM] = (acc_scr[h] / l_scr[h]).astype(o_ref.dtype)


def _dsa(za, kk, w_idx):
    nq = SEQ // DSA_TQ
    once = pl.Buffered(1)
    return pl.pallas_call(
        _dsa_kernel,
        grid=(BATCH, nq),
        in_specs=[
            pl.BlockSpec((DSA_TQ, ATT_WIDTH), lambda b, i: (b * nq + i, 0)),
            pl.BlockSpec((SEQ, ATT_WIDTH), lambda b, i: (b, 1), pipeline_mode=once),
            pl.BlockSpec((SEQ, ATT_WIDTH), lambda b, i: (b, 2), pipeline_mode=once),
            pl.BlockSpec((DSA_TQ, IDX_HEADS * IDX_DIM), lambda b, i: (b * nq + i, 3)),
            pl.BlockSpec((2, SEQ, LANES), lambda b, i: (0, b, 0), pipeline_mode=once),
            pl.BlockSpec((DSA_TQ, IDX_HEADS), lambda b, i: (b * nq + i, 0)),
        ],
        out_specs=pl.BlockSpec((DSA_TQ, ATT_WIDTH), lambda b, i: (b * nq + i, 0)),
        out_shape=jax.ShapeDtypeStruct((TOKENS, ATT_WIDTH), bf16),
        scratch_shapes=[
            pltpu.VMEM((DSA_NCH, DSA_TQ, DSA_TK), i32),
            pltpu.VMEM((IDX_HEADS, DSA_TQ, LANES), f32),
            pltpu.VMEM((ATT_HEADS, DSA_TQ, 1), f32),
            pltpu.VMEM((ATT_HEADS, DSA_TQ, 1), f32),
            pltpu.VMEM((ATT_HEADS, DSA_TQ, HEAD_DIM), f32),
        ],
        compiler_params=_params("arbitrary", "arbitrary"),
        name="dsa",
    )(za, za, za, za, kk, w_idx)


def _pool_kernel(u_ref, wp_ref, ps_ref, o_ref):
    row = lax.broadcasted_iota(i32, (SEQ, POOL_GROUP), 0)
    for g, win in enumerate(POOL_WINDOWS):
        x = u_ref[:, g * POOL_GROUP:(g + 1) * POOL_GROUP]
        s = x
        k = 1
        while k < win:
            s = s + jnp.where(row >= k, pltpu.roll(s, k, axis=0), 0.0)
            k *= 2
        cnt = jnp.minimum(row + 1, win).astype(f32)
        pooled = (s / cnt - x).astype(bf16)
        mixed = jnp.dot(pooled, wp_ref[g], preferred_element_type=f32)
        o_ref[:, g * POOL_GROUP:(g + 1) * POOL_GROUP] = (
            mixed * ps_ref[:, g * POOL_GROUP:(g + 1) * POOL_GROUP]).astype(o_ref.dtype)


def _pool(zb, w_pool, pool_scale):
    return pl.pallas_call(
        _pool_kernel,
        grid=(BATCH,),
        in_specs=[
            pl.BlockSpec((SEQ, POOL_WIDTH), lambda b: (b, 0)),
            pl.BlockSpec((N_POOL, POOL_GROUP, POOL_GROUP), lambda b: (0, 0, 0)),
            pl.BlockSpec((1, POOL_WIDTH), lambda b: (0, 0)),
        ],
        out_specs=pl.BlockSpec((SEQ, POOL_WIDTH), lambda b: (b, 0)),
        out_shape=jax.ShapeDtypeStruct((TOKENS, POOL_WIDTH), bf16),
        compiler_params=_params("parallel"),
        name="pool",
    )(zb, w_pool, pool_scale)


MEM_TQ = 512


def _mem_attn_kernel(q_ref, k_ref, v_ref, o_ref):
    for h in range(MEM_HEADS):
        sl = slice(h * HEAD_DIM, (h + 1) * HEAD_DIM)
        s = lax.dot_general(q_ref[:, sl], k_ref[:, sl], (((1,), (1,)), ((), ())),
                            preferred_element_type=f32) * (HEAD_DIM ** -0.5)
        p = jnp.exp(s - jnp.max(s, axis=1, keepdims=True))
        l = jnp.sum(p, axis=1, keepdims=True)
        o = jnp.dot(p.astype(bf16), v_ref[:, sl], preferred_element_type=f32)
        o_ref[:, sl] = (o / l).astype(o_ref.dtype)


def _mem_attn(za, kv):
    nq = SEQ // MEM_TQ
    qcol = (3 * ATT_WIDTH + IDX_HEADS * IDX_DIM) // MEM_WIDTH
    return pl.pallas_call(
        _mem_attn_kernel,
        grid=(BATCH, nq),
        in_specs=[
            pl.BlockSpec((MEM_TQ, MEM_WIDTH), lambda b, i: (b * nq + i, qcol)),
            pl.BlockSpec((MEM_LEN, MEM_WIDTH), lambda b, i: (b, 0)),
            pl.BlockSpec((MEM_LEN, MEM_WIDTH), lambda b, i: (b, 1)),
        ],
        out_specs=pl.BlockSpec((MEM_TQ, MEM_WIDTH), lambda b, i: (b * nq + i, 0)),
        out_shape=jax.ShapeDtypeStruct((TOKENS, MEM_WIDTH), bf16),
        compiler_params=_params("parallel", "parallel"),
        name="mem_attn",
    )(za, kv, kv)


CMB_TM = 512
CMB_TC = 512


def _combine_ln_kernel(h_ref, a_ref, p_ref, m_ref, wg0_ref, wg1_ref, wg2_ref,
                       bg0_ref, bg1_ref, bg2_ref, wa_ref, wp_ref, wm_ref, wo_ref,
                       g_ref, b_ref, o_ref, hb_ref):
    j = pl.program_id(1)

    @pl.when(j == 0)
    def _():
        hb_ref[...] = h_ref[...].astype(bf16)
        o_ref[...] = jnp.zeros_like(o_ref)

    hb = hb_ref[...]

    def gate(wg_ref, bg_ref):
        return jax.nn.sigmoid(jnp.dot(hb, wg_ref[...], preferred_element_type=f32) + bg_ref[...])

    y = gate(wg0_ref, bg0_ref) * jnp.dot(a_ref[...], wa_ref[...], preferred_element_type=f32)
    y = y + gate(wg1_ref, bg1_ref) * jnp.dot(p_ref[...], wp_ref[...], preferred_element_type=f32)
    y = y + gate(wg2_ref, bg2_ref) * jnp.dot(m_ref[...], wm_ref[...], preferred_element_type=f32)
    o_ref[...] += jnp.dot(y.astype(bf16), wo_ref[...], preferred_element_type=f32)

    @pl.when(j == pl.num_programs(1) - 1)
    def _():
        z = ALPHA * h_ref[...] + o_ref[...]
        o_ref[...] = _layer_norm(z, g_ref[...], b_ref[...])


def _combine_ln(h, a, p, m, w_gate, b_gate, w_a, w_p, w_m, w_out, g, b):
    nc = D_MODEL // CMB_TC
    row = lambda i, j: (i, 0)
    col = lambda i, j: (0, j)
    return pl.pallas_call(
        _combine_ln_kernel,
        grid=(TOKENS // CMB_TM, nc),
        in_specs=[
            pl.BlockSpec((CMB_TM, D_MODEL), row),
            pl.BlockSpec((CMB_TM, ATT_WIDTH), row),
            pl.BlockSpec((CMB_TM, POOL_WIDTH), row),
            pl.BlockSpec((CMB_TM, MEM_WIDTH), row),
            pl.BlockSpec((D_MODEL, CMB_TC), lambda i, j: (0, j)),
            pl.BlockSpec((D_MODEL, CMB_TC), lambda i, j: (0, j + nc)),
            pl.BlockSpec((D_MODEL, CMB_TC), lambda i, j: (0, j + 2 * nc)),
            pl.BlockSpec((1, CMB_TC), lambda i, j: (0, j)),
            pl.BlockSpec((1, CMB_TC), lambda i, j: (0, j + nc)),
            pl.BlockSpec((1, CMB_TC), lambda i, j: (0, j + 2 * nc)),
            pl.BlockSpec((ATT_WIDTH, CMB_TC), col),
            pl.BlockSpec((POOL_WIDTH, CMB_TC), col),
            pl.BlockSpec((MEM_WIDTH, CMB_TC), col),
            pl.BlockSpec((CMB_TC, D_MODEL), lambda i, j: (j, 0)),
            pl.BlockSpec((1, D_MODEL), lambda i, j: (0, 0)),
            pl.BlockSpec((1, D_MODEL), lambda i, j: (0, 0)),
        ],
        out_specs=pl.BlockSpec((CMB_TM, D_MODEL), row),
        out_shape=jax.ShapeDtypeStruct((TOKENS, D_MODEL), f32),
        scratch_shapes=[pltpu.VMEM((CMB_TM, D_MODEL), bf16)],
        compiler_params=_params("parallel", "arbitrary"),
        name="combine_ln",
    )(h, a, p, m, w_gate, w_gate, w_gate, b_gate, b_gate, b_gate, w_a, w_p, w_m, w_out, g, b)


def kernel(x, mem, w_ffn1_up, w_ffn1_down, ln1_g, ln1_b, w_in, b_gate, w_mem_kv, w_pool,
           pool_scale, w_br_att, w_br_pool, w_br_mem, w_out, ln2_g, ln2_b, w_ffn2_up,
           w_ffn2_down, ln3_g, ln3_b):
    h = x.reshape(TOKENS, D_MODEL)
    memf = mem.reshape(BATCH * MEM_LEN, D_MODEL)
    n_idx = IDX_HEADS * IDX_DIM
    for l in range(DEPTH):
        c_qkv = 3 * ATT_WIDTH
        c_qi = c_qkv + n_idx
        c_ki = c_qi + IDX_DIM
        c_wi = c_ki + IDX_HEADS
        c_u = c_wi + POOL_WIDTH
        c_qm = c_u + MEM_WIDTH
        wi = w_in[l]
        w_a = jnp.concatenate([wi[:, :c_qi], wi[:, c_u:c_qm]], axis=1).astype(bf16)
        pad = jnp.zeros((D_MODEL, LANES - IDX_DIM - IDX_HEADS), wi.dtype)
        w_b = jnp.concatenate([wi[:, c_wi:c_u], wi[:, c_qi:c_wi], pad], axis=1).astype(bf16)
        w_g = wi[:, c_qm:].astype(bf16)

        h = _ffn_ln(h, w_ffn1_up[l].astype(bf16), w_ffn1_down[l].astype(bf16),
                    ln1_g[l][None], ln1_b[l][None])

        za = _matmul(h, w_a, bf16, 512, 512, "in_proj_a")
        zb = _matmul(h, w_b, f32, 512, POOL_WIDTH + LANES, "in_proj_b")
        kv = _matmul(memf, w_mem_kv[l].astype(bf16), bf16, BATCH * MEM_LEN, 512, "mem_kv")

        ki = zb[:, POOL_WIDTH:POOL_WIDTH + IDX_DIM].astype(bf16)
        zk = jnp.zeros_like(ki)
        kk = jnp.stack([jnp.concatenate([ki, zk], axis=1), jnp.concatenate([zk, ki], axis=1)])
        w_idx = zb[:, POOL_WIDTH + IDX_DIM:POOL_WIDTH + IDX_DIM + IDX_HEADS]

        a = _dsa(za, kk, w_idx)
        p = _pool(zb, w_pool[l].astype(bf16), pool_scale[l][None])
        m = _mem_attn(za, kv)

        h = _combine_ln(h, a, p, m, w_g, b_gate[l][None], w_br_att[l].astype(bf16),
                        w_br_pool[l].astype(bf16), w_br_mem[l].astype(bf16),
                        w_out[l].astype(bf16), ln2_g[l][None], ln2_b[l][None])

        h = _ffn_ln(h, w_ffn2_up[l].astype(bf16), w_ffn2_down[l].astype(bf16),
                    ln3_g[l][None], ln3_b[l][None])
    return h.reshape(BATCH, SEQ, D_MODEL)
```

```python
import functools
import math

import jax
import jax.numpy as jnp
from jax import lax
from jax.experimental import pallas as pl
from jax.experimental.pallas import tpu as pltpu

f32 = jnp.float32
bf16 = jnp.bfloat16
i32 = jnp.int32

D_MODEL = 2048
BATCH = 2
SEQ = 4096
DEPTH = 1
MEM_LEN = 256
ATT_HEADS = 8
HEAD_DIM = 128
ATT_WIDTH = ATT_HEADS * HEAD_DIM
IDX_HEADS = 16
IDX_DIM = 64
TOPK = min(256, SEQ // 4)
POOL_WINDOWS = (2, 4, 8, 16)
N_POOL = len(POOL_WINDOWS)
POOL_GROUP = 128
POOL_WIDTH = N_POOL * POOL_GROUP
MEM_HEADS = 4
MEM_WIDTH = MEM_HEADS * HEAD_DIM
N_BRANCH = 3
D_FF = 5632
ALPHA = (2 * DEPTH) ** 0.25
LN_EPS = 1e-5
TOKENS = BATCH * SEQ

LANES = 128
VMEM_LIMIT = 56 * 1024 * 1024

INT_MIN = -(2 ** 31)
LOG2E = math.log2(math.e)


def _params(*sem):
    return pltpu.CompilerParams(dimension_semantics=sem, vmem_limit_bytes=VMEM_LIMIT)


def _layer_norm(y, g, b):
    mu = jnp.mean(y, axis=-1, keepdims=True)
    d = y - mu
    var = jnp.mean(d * d, axis=-1, keepdims=True)
    return d * lax.rsqrt(var + LN_EPS) * g + b


FFN_TM = 512
FFN_TF = 512


def _ffn_ln_kernel(x_ref, wa_ref, wu_ref, wd_ref, g_ref, b_ref, o_ref, xb_ref):
    j = pl.program_id(1)

    @pl.when(j == 0)
    def _():
        xb_ref[...] = x_ref[...].astype(bf16)
        o_ref[...] = jnp.zeros_like(o_ref)

    xb = xb_ref[...]
    a = jnp.dot(xb, wa_ref[...], preferred_element_type=f32)
    u = jnp.dot(xb, wu_ref[...], preferred_element_type=f32)
    act = (a * jax.nn.sigmoid(a) * u).astype(bf16)
    o_ref[...] += jnp.dot(act, wd_ref[...], preferred_element_type=f32)

    @pl.when(j == pl.num_programs(1) - 1)
    def _():
        y = ALPHA * x_ref[...] + 0.5 * o_ref[...]
        o_ref[...] = _layer_norm(y, g_ref[...], b_ref[...])


def _ffn_ln(x, w_up, w_down, g, b):
    nf = D_FF // FFN_TF
    return pl.pallas_call(
        _ffn_ln_kernel,
        grid=(TOKENS // FFN_TM, nf),
        in_specs=[
            pl.BlockSpec((FFN_TM, D_MODEL), lambda i, j: (i, 0)),
            pl.BlockSpec((D_MODEL, FFN_TF), lambda i, j: (0, j)),
            pl.BlockSpec((D_MODEL, FFN_TF), lambda i, j: (0, j + nf)),
            pl.BlockSpec((FFN_TF, D_MODEL), lambda i, j: (j, 0)),
            pl.BlockSpec((1, D_MODEL), lambda i, j: (0, 0)),
            pl.BlockSpec((1, D_MODEL), lambda i, j: (0, 0)),
        ],
        out_specs=pl.BlockSpec((FFN_TM, D_MODEL), lambda i, j: (i, 0)),
        out_shape=jax.ShapeDtypeStruct((TOKENS, D_MODEL), f32),
        scratch_shapes=[pltpu.VMEM((FFN_TM, D_MODEL), bf16)],
        compiler_params=_params("parallel", "arbitrary"),
        name="ffn_ln",
    )(x, w_up, w_up, w_down, g, b)


def _matmul_kernel(x_ref, w_ref, o_ref, xb_ref):
    @pl.when(pl.program_id(1) == 0)
    def _():
        xb_ref[...] = x_ref[...].astype(bf16)

    o_ref[...] = jnp.dot(xb_ref[...], w_ref[...], preferred_element_type=f32).astype(o_ref.dtype)


def _matmul(x, w, out_dtype, tm, tn, name):
    m, k = x.shape
    n = w.shape[1]
    return pl.pallas_call(
        _matmul_kernel,
        grid=(m // tm, n // tn),
        in_specs=[
            pl.BlockSpec((tm, k), lambda i, j: (i, 0)),
            pl.BlockSpec((k, tn), lambda i, j: (0, j)),
        ],
        out_specs=pl.BlockSpec((tm, tn), lambda i, j: (i, j)),
        out_shape=jax.ShapeDtypeStruct((m, n), out_dtype),
        scratch_shapes=[pltpu.VMEM((tm, k), bf16)],
        compiler_params=_params("parallel", "arbitrary"),
        name=name,
    )(x, w)


DSA_TQ = 256
DSA_TK = 512
DSA_NCH = SEQ // DSA_TK
DSA_ONES = 16
KEY_NEG_FLT_MAX = -(2 ** 31) + (1 << 23)


def _key_to_f32(key):
    return lax.bitcast_convert_type(key ^ ((key >> 31) & jnp.int32(0x7FFFFFFF)), f32)


def _dsa_kernel(q_ref, k_ref, vt_ref, qi_ref, kk_ref, wt_ref, o_ref, sc_scr, acc_scr):
    i = pl.program_id(1)
    q0 = i * DSA_TQ
    nk = (q0 + DSA_TQ - 1) // DSA_TK + 1

    qpos = q0 + lax.broadcasted_iota(i32, (DSA_TK, DSA_TQ), 1)
    kofs = lax.broadcasted_iota(i32, (DSA_TK, DSA_TQ), 0)
    contract_last = (((1,), (1,)), ((), ()))

    def rows8(x):
        return x.reshape(DSA_TK // 8, 8, DSA_TQ)

    wt = wt_ref[...] * (IDX_DIM ** -0.5 * IDX_HEADS ** -0.5)

    def score_chunk(c, carry):
        r0 = pl.multiple_of(c * DSA_TK, DSA_TK)
        kk0 = kk_ref[0, pl.ds(r0, DSA_TK), :]
        kk1 = kk_ref[1, pl.ds(r0, DSA_TK), :]
        acc = jnp.zeros((DSA_TK, DSA_TQ), f32)
        for p in range(IDX_HEADS // 2):
            slab = qi_ref[:, p * LANES:(p + 1) * LANES]
            l0 = lax.dot_general(kk0, slab, contract_last, preferred_element_type=f32)
            l1 = lax.dot_general(kk1, slab, contract_last, preferred_element_type=f32)
            acc = acc + (jnp.maximum(l0, 0.0) * wt[2 * p:2 * p + 1]
                         + jnp.maximum(l1, 0.0) * wt[2 * p + 1:2 * p + 2])
        sc_scr[c] = jnp.where(kofs + r0 <= qpos, acc, -jnp.inf)
        return carry

    lax.fori_loop(0, nk, score_chunk, 0)

    def count(pred):
        def body(c, acc):
            return acc + jnp.sum(rows8(pred(c).astype(i32)), axis=0)
        acc = lax.fori_loop(0, nk, body, jnp.zeros((8, DSA_TQ), i32))
        return jnp.sum(acc, axis=0, keepdims=True)

    def count_ge(cand):
        return count(lambda c: sc_scr[c] >= cand)

    key = jnp.where(count_ge(jnp.zeros((1, DSA_TQ), f32)) >= TOPK, jnp.int32(0), jnp.int32(INT_MIN))

    def bit_step(b, key):
        cand = key | jnp.left_shift(jnp.int32(1), 30 - b)
        return jnp.where(count_ge(_key_to_f32(cand)) >= TOPK, cand, key)

    key = lax.fori_loop(0, 31, bit_step, key)
    thr = _key_to_f32(jnp.maximum(key, jnp.int32(KEY_NEG_FLT_MAX)))

    n_ge = count_ge(thr)

    @pl.when(jnp.max(n_ge) > TOPK)
    def _():
        need = TOPK - count(lambda c: sc_scr[c] > thr)

        def count_eq_below(pos):
            return count(lambda c: (sc_scr[c] == thr) & (kofs + c * DSA_TK < pos))

        def pos_step(b, r):
            cand = r | jnp.left_shift(jnp.int32(1), (SEQ.bit_length() - 2) - b)
            return jnp.where(count_eq_below(cand) < need, cand, r)
        r = lax.fori_loop(0, SEQ.bit_length() - 1, pos_step, jnp.zeros((1, DSA_TQ), i32))
        r = jnp.where(n_ge > TOPK, r, jnp.int32(SEQ))

        def drop(c, carry):
            sc = sc_scr[c]
            sc_scr[c] = jnp.where((sc == thr) & (kofs + c * DSA_TK > r), -jnp.inf, sc)
            return carry
        lax.fori_loop(0, nk, drop, 0)

    acc_scr[...] = jnp.zeros(acc_scr.shape, f32)
    c1 = HEAD_DIM ** -0.5 * LOG2E
    ones = jnp.ones((DSA_ONES, DSA_TK), bf16)

    def attn_chunk(c, ms):
        r0 = pl.multiple_of(c * DSA_TK, DSA_TK)
        madd = jnp.where(sc_scr[c] >= thr, 0.0, -jnp.inf)
        kd = (lax.broadcasted_iota(i32, (DSA_TK, LANES), 0) + (r0 - q0)).astype(f32)
        def qk(h):
            hd = slice(h * HEAD_DIM, (h + 1) * HEAD_DIM)
            return lax.dot_general(k_ref[pl.ds(r0, DSA_TK), hd], q_ref[:, hd], contract_last,
                                   preferred_element_type=f32)

        new_ms = []
        s_next = qk(0)
        for h in range(ATT_HEADS):
            slope2 = 2.0 ** (-8.0 * (h + 1) / ATT_HEADS) * LOG2E
            hd = slice(h * HEAD_DIM, (h + 1) * HEAD_DIM)
            s = s_next
            if h + 1 < ATT_HEADS:
                s_next = qk(h + 1)
            bias = kd * slope2
            t = s * c1 + jnp.concatenate([bias] * (DSA_TQ // LANES), axis=1) + madd
            m_old = ms[h]
            m_new = jnp.maximum(m_old, jnp.max(t, axis=0, keepdims=True))
            alpha = jnp.exp2(m_old - m_new)
            p = jnp.exp2(t - m_new).astype(bf16)
            vt1 = jnp.concatenate([vt_ref[c, hd, :], ones], axis=0)
            acc_scr[h] = alpha * acc_scr[h] + jnp.dot(vt1, p, preferred_element_type=f32)
            new_ms.append(m_new)
        return tuple(new_ms)

    m0 = jnp.full((1, DSA_TQ), -1e30, f32)
    lax.fori_loop(0, nk, attn_chunk, (m0,) * ATT_HEADS)

    for h in range(ATT_HEADS):
        o_ref[h * HEAD_DIM:(h + 1) * HEAD_DIM, :] = (
            acc_scr[h, :HEAD_DIM, :] / acc_scr[h, HEAD_DIM:HEAD_DIM + 1, :]).astype(o_ref.dtype)


def _dsa(za, vt, kk, wt_idx):
    nq = SEQ // DSA_TQ
    once = pl.Buffered(1)
    return pl.pallas_call(
        _dsa_kernel,
        grid=(BATCH, nq),
        in_specs=[
            pl.BlockSpec((DSA_TQ, ATT_WIDTH), lambda b, i: (b * nq + i, 0)),
            pl.BlockSpec((SEQ, ATT_WIDTH), lambda b, i: (b, 1), pipeline_mode=once),
            pl.BlockSpec((None, DSA_NCH, ATT_WIDTH, DSA_TK), lambda b, i: (b, 0, 0, 0),
                         pipeline_mode=once),
            pl.BlockSpec((DSA_TQ, IDX_HEADS * IDX_DIM), lambda b, i: (b * nq + i, 3)),
            pl.BlockSpec((2, SEQ, LANES), lambda b, i: (0, b, 0), pipeline_mode=once),
            pl.BlockSpec((IDX_HEADS, DSA_TQ), lambda b, i: (0, b * nq + i)),
        ],
        out_specs=pl.BlockSpec((None, ATT_WIDTH, DSA_TQ), lambda b, i: (b, 0, i)),
        out_shape=jax.ShapeDtypeStruct((BATCH, ATT_WIDTH, SEQ), bf16),
        scratch_shapes=[
            pltpu.VMEM((DSA_NCH, DSA_TK, DSA_TQ), f32),
            pltpu.VMEM((ATT_HEADS, HEAD_DIM + DSA_ONES, DSA_TQ), f32),
        ],
        compiler_params=_params("arbitrary", "arbitrary"),
        name="dsa",
    )(za, za, vt, za, kk, wt_idx)


def _pool_kernel(u_ref, wp_ref, ps_ref, o_ref):
    row = lax.broadcasted_iota(i32, (SEQ, POOL_GROUP), 0)
    for g, win in enumerate(POOL_WINDOWS):
        x = u_ref[:, g * POOL_GROUP:(g + 1) * POOL_GROUP]
        s = x
        k = 1
        while k < win:
            s = s + jnp.where(row >= k, pltpu.roll(s, k, axis=0), 0.0)
            k *= 2
        cnt = jnp.minimum(row + 1, win).astype(f32)
        pooled = (s / cnt - x).astype(bf16)
        mixed = jnp.dot(pooled, wp_ref[g], preferred_element_type=f32)
        o_ref[:, g * POOL_GROUP:(g + 1) * POOL_GROUP] = (
            mixed * ps_ref[:, g * POOL_GROUP:(g + 1) * POOL_GROUP]).astype(o_ref.dtype)


def _pool(zb, w_pool, pool_scale):
    return pl.pallas_call(
        _pool_kernel,
        grid=(BATCH,),
        in_specs=[
            pl.BlockSpec((SEQ, POOL_WIDTH), lambda b: (b, 0)),
            pl.BlockSpec((N_POOL, POOL_GROUP, POOL_GROUP), lambda b: (0, 0, 0)),
            pl.BlockSpec((1, POOL_WIDTH), lambda b: (0, 0)),
        ],
        out_specs=pl.BlockSpec((SEQ, POOL_WIDTH), lambda b: (b, 0)),
        out_shape=jax.ShapeDtypeStruct((TOKENS, POOL_WIDTH), bf16),
        compiler_params=_params("parallel"),
        name="pool",
    )(zb, w_pool, pool_scale)


MEM_TQ = 512


def _mem_attn_kernel(q_ref, k_ref, v_ref, o_ref):
    for h in range(MEM_HEADS):
        sl = slice(h * HEAD_DIM, (h + 1) * HEAD_DIM)
        s = lax.dot_general(q_ref[:, sl], k_ref[:, sl], (((1,), (1,)), ((), ())),
                            preferred_element_type=f32) * (HEAD_DIM ** -0.5)
        p = jnp.exp(s - jnp.max(s, axis=1, keepdims=True))
        l = jnp.sum(p, axis=1, keepdims=True)
        o = jnp.dot(p.astype(bf16), v_ref[:, sl], preferred_element_type=f32)
        o_ref[:, sl] = (o / l).astype(o_ref.dtype)


def _mem_attn(za, kv):
    nq = SEQ // MEM_TQ
    qcol = (3 * ATT_WIDTH + IDX_HEADS * IDX_DIM) // MEM_WIDTH
    return pl.pallas_call(
        _mem_attn_kernel,
        grid=(BATCH, nq),
        in_specs=[
            pl.BlockSpec((MEM_TQ, MEM_WIDTH), lambda b, i: (b * nq + i, qcol)),
            pl.BlockSpec((MEM_LEN, MEM_WIDTH), lambda b, i: (b, 0)),
            pl.BlockSpec((MEM_LEN, MEM_WIDTH), lambda b, i: (b, 1)),
        ],
        out_specs=pl.BlockSpec((MEM_TQ, MEM_WIDTH), lambda b, i: (b * nq + i, 0)),
        out_shape=jax.ShapeDtypeStruct((TOKENS, MEM_WIDTH), bf16),
        compiler_params=_params("parallel", "parallel"),
        name="mem_attn",
    )(za, kv, kv)


CMB_TM = 512
CMB_TC = 512


def _combine_ln_kernel(h_ref, a_ref, p_ref, m_ref, wg0_ref, wg1_ref, wg2_ref,
                       bg0_ref, bg1_ref, bg2_ref, wa_ref, wp_ref, wm_ref, wo_ref,
                       g_ref, b_ref, o_ref, hb_ref):
    j = pl.program_id(1)

    @pl.when(j == 0)
    def _():
        hb_ref[...] = h_ref[...].astype(bf16)
        o_ref[...] = jnp.zeros_like(o_ref)

    hb = hb_ref[...]

    def gate(wg_ref, bg_ref):
        return jax.nn.sigmoid(jnp.dot(hb, wg_ref[...], preferred_element_type=f32) + bg_ref[...])

    y = gate(wg0_ref, bg0_ref) * jnp.dot(a_ref[...], wa_ref[...], preferred_element_type=f32)
    y = y + gate(wg1_ref, bg1_ref) * jnp.dot(p_ref[...], wp_ref[...], preferred_element_type=f32)
    y = y + gate(wg2_ref, bg2_ref) * jnp.dot(m_ref[...], wm_ref[...], preferred_element_type=f32)
    o_ref[...] += jnp.dot(y.astype(bf16), wo_ref[...], preferred_element_type=f32)

    @pl.when(j == pl.num_programs(1) - 1)
    def _():
        z = ALPHA * h_ref[...] + o_ref[...]
        o_ref[...] = _layer_norm(z, g_ref[...], b_ref[...])


def _combine_ln(h, a, p, m, w_gate, b_gate, w_a, w_p, w_m, w_out, g, b):
    nc = D_MODEL // CMB_TC
    row = lambda i, j: (i, 0)
    col = lambda i, j: (0, j)
    return pl.pallas_call(
        _combine_ln_kernel,
        grid=(TOKENS // CMB_TM, nc),
        in_specs=[
            pl.BlockSpec((CMB_TM, D_MODEL), row),
            pl.BlockSpec((CMB_TM, ATT_WIDTH), row),
            pl.BlockSpec((CMB_TM, POOL_WIDTH), row),
            pl.BlockSpec((CMB_TM, MEM_WIDTH), row),
            pl.BlockSpec((D_MODEL, CMB_TC), lambda i, j: (0, j)),
            pl.BlockSpec((D_MODEL, CMB_TC), lambda i, j: (0, j + nc)),
            pl.BlockSpec((D_MODEL, CMB_TC), lambda i, j: (0, j + 2 * nc)),
            pl.BlockSpec((1, CMB_TC), lambda i, j: (0, j)),
            pl.BlockSpec((1, CMB_TC), lambda i, j: (0, j + nc)),
            pl.BlockSpec((1, CMB_TC), lambda i, j: (0, j + 2 * nc)),
            pl.BlockSpec((ATT_WIDTH, CMB_TC), col),
            pl.BlockSpec((POOL_WIDTH, CMB_TC), col),
            pl.BlockSpec((MEM_WIDTH, CMB_TC), col),
            pl.BlockSpec((CMB_TC, D_MODEL), lambda i, j: (j, 0)),
            pl.BlockSpec((1, D_MODEL), lambda i, j: (0, 0)),
            pl.BlockSpec((1, D_MODEL), lambda i, j: (0, 0)),
        ],
        out_specs=pl.BlockSpec((CMB_TM, D_MODEL), row),
        out_shape=jax.ShapeDtypeStruct((TOKENS, D_MODEL), f32),
        scratch_shapes=[pltpu.VMEM((CMB_TM, D_MODEL), bf16)],
        compiler_params=_params("parallel", "arbitrary"),
        name="combine_ln",
    )(h, a, p, m, w_gate, w_gate, w_gate, b_gate, b_gate, b_gate, w_a, w_p, w_m, w_out, g, b)


def kernel(x, mem, w_ffn1_up, w_ffn1_down, ln1_g, ln1_b, w_in, b_gate, w_mem_kv, w_pool,
           pool_scale, w_br_att, w_br_pool, w_br_mem, w_out, ln2_g, ln2_b, w_ffn2_up,
           w_ffn2_down, ln3_g, ln3_b):
    h = x.reshape(TOKENS, D_MODEL)
    memf = mem.reshape(BATCH * MEM_LEN, D_MODEL)
    n_idx = IDX_HEADS * IDX_DIM
    for l in range(DEPTH):
        c_qkv = 3 * ATT_WIDTH
        c_qi = c_qkv + n_idx
        c_ki = c_qi + IDX_DIM
        c_wi = c_ki + IDX_HEADS
        c_u = c_wi + POOL_WIDTH
        c_qm = c_u + MEM_WIDTH
        wi = w_in[l]
        w_a = jnp.concatenate([wi[:, :c_qi], wi[:, c_u:c_qm]], axis=1).astype(bf16)
        pad = jnp.zeros((D_MODEL, LANES - IDX_DIM - IDX_HEADS), wi.dtype)
        w_b = jnp.concatenate([wi[:, c_wi:c_u], wi[:, c_qi:c_wi], pad], axis=1).astype(bf16)
        w_g = wi[:, c_qm:].astype(bf16)

        h = _ffn_ln(h, w_ffn1_up[l].astype(bf16), w_ffn1_down[l].astype(bf16),
                    ln1_g[l][None], ln1_b[l][None])

        za = _matmul(h, w_a, bf16, 512, 512, "in_proj_a")
        zb = _matmul(h, w_b, f32, 512, POOL_WIDTH + LANES, "in_proj_b")
        kv = _matmul(memf, w_mem_kv[l].astype(bf16), bf16, BATCH * MEM_LEN, 512, "mem_kv")

        ki = zb[:, POOL_WIDTH:POOL_WIDTH + IDX_DIM].astype(bf16)
        zk = jnp.zeros_like(ki)
        kk = jnp.stack([jnp.concatenate([ki, zk], axis=1), jnp.concatenate([zk, ki], axis=1)])
        wt_idx = zb[:, POOL_WIDTH + IDX_DIM:POOL_WIDTH + IDX_DIM + IDX_HEADS].T
        vt = za[:, 2 * ATT_WIDTH:3 * ATT_WIDTH].reshape(BATCH, DSA_NCH, DSA_TK, ATT_WIDTH)
        vt = vt.transpose(0, 1, 3, 2)

        a = _dsa(za, vt, kk, wt_idx).transpose(0, 2, 1).reshape(TOKENS, ATT_WIDTH)
        p = _pool(zb, w_pool[l].astype(bf16), pool_scale[l][None])
        m = _mem_attn(za, kv)

        h = _combine_ln(h, a, p, m, w_g, b_gate[l][None], w_br_att[l].astype(bf16),
                        w_br_pool[l].astype(bf16), w_br_mem[l].astype(bf16),
                        w_out[l].astype(bf16), ln2_g[l][None], ln2_b[l][None])

        h = _ffn_ln(h, w_ffn2_up[l].astype(bf16), w_ffn2_down[l].astype(bf16),
                    ln3_g[l][None], ln3_b[l][None])
    return h.reshape(BATCH, SEQ, D_MODEL)
```

```python
import functools
import math

import jax
import jax.numpy as jnp
from jax import lax
from jax.experimental import pallas as pl
from jax.experimental.pallas import tpu as pltpu

f32 = jnp.float32
bf16 = jnp.bfloat16
i32 = jnp.int32

D_MODEL = 2048
BATCH = 2
SEQ = 4096
DEPTH = 1
MEM_LEN = 256
ATT_HEADS = 8
HEAD_DIM = 128
ATT_WIDTH = ATT_HEADS * HEAD_DIM
IDX_HEADS = 16
IDX_DIM = 64
TOPK = min(256, SEQ // 4)
POOL_WINDOWS = (2, 4, 8, 16)
N_POOL = len(POOL_WINDOWS)
POOL_GROUP = 128
POOL_WIDTH = N_POOL * POOL_GROUP
MEM_HEADS = 4
MEM_WIDTH = MEM_HEADS * HEAD_DIM
N_BRANCH = 3
D_FF = 5632
ALPHA = (2 * DEPTH) ** 0.25
LN_EPS = 1e-5
TOKENS = BATCH * SEQ

LANES = 128
VMEM_LIMIT = 60 * 1024 * 1024

INT_MIN = -(2 ** 31)
LOG2E = math.log2(math.e)


def _params(*sem):
    return pltpu.CompilerParams(dimension_semantics=sem, vmem_limit_bytes=VMEM_LIMIT)


def _layer_norm(y, g, b):
    mu = jnp.mean(y, axis=-1, keepdims=True)
    d = y - mu
    var = jnp.mean(d * d, axis=-1, keepdims=True)
    return d * lax.rsqrt(var + LN_EPS) * g + b


FFN_TM = 1024
FFN_TF = 256


def _ffn_ln_kernel(x_ref, wa_ref, wu_ref, wd_ref, g_ref, b_ref, o_ref, xb_ref):
    j = pl.program_id(1)

    @pl.when(j == 0)
    def _():
        xb_ref[...] = x_ref[...].astype(bf16)
        o_ref[...] = jnp.zeros_like(o_ref)

    xb = xb_ref[...]
    a = jnp.dot(xb, wa_ref[...], preferred_element_type=f32)
    u = jnp.dot(xb, wu_ref[...], preferred_element_type=f32)
    act = (a * jax.nn.sigmoid(a) * u).astype(bf16)
    o_ref[...] += jnp.dot(act, wd_ref[...], preferred_element_type=f32)

    @pl.when(j == pl.num_programs(1) - 1)
    def _():
        y = ALPHA * x_ref[...] + 0.5 * o_ref[...]
        o_ref[...] = _layer_norm(y, g_ref[...], b_ref[...])


def _ffn_ln(x, w_up, w_down, g, b):
    nf = D_FF // FFN_TF
    return pl.pallas_call(
        _ffn_ln_kernel,
        grid=(TOKENS // FFN_TM, nf),
        in_specs=[
            pl.BlockSpec((FFN_TM, D_MODEL), lambda i, j: (i, 0)),
            pl.BlockSpec((D_MODEL, FFN_TF), lambda i, j: (0, j)),
            pl.BlockSpec((D_MODEL, FFN_TF), lambda i, j: (0, j + nf)),
            pl.BlockSpec((FFN_TF, D_MODEL), lambda i, j: (j, 0)),
            pl.BlockSpec((1, D_MODEL), lambda i, j: (0, 0)),
            pl.BlockSpec((1, D_MODEL), lambda i, j: (0, 0)),
        ],
        out_specs=pl.BlockSpec((FFN_TM, D_MODEL), lambda i, j: (i, 0)),
        out_shape=jax.ShapeDtypeStruct((TOKENS, D_MODEL), f32),
        scratch_shapes=[pltpu.VMEM((FFN_TM, D_MODEL), bf16)],
        compiler_params=_params("parallel", "arbitrary"),
        name="ffn_ln",
    )(x, w_up, w_up, w_down, g, b)


def _matmul_kernel(x_ref, w_ref, o_ref, xb_ref):
    @pl.when(pl.program_id(1) == 0)
    def _():
        xb_ref[...] = x_ref[...].astype(bf16)

    o_ref[...] = jnp.dot(xb_ref[...], w_ref[...], preferred_element_type=f32).astype(o_ref.dtype)


def _matmul(x, w, out_dtype, tm, tn, name, col0=0, n=None):
    m, k = x.shape
    n = w.shape[1] if n is None else n
    assert col0 % tn == 0 and n % tn == 0 and m % tm == 0
    jb = col0 // tn
    return pl.pallas_call(
        _matmul_kernel,
        grid=(m // tm, n // tn),
        in_specs=[
            pl.BlockSpec((tm, k), lambda i, j: (i, 0)),
            pl.BlockSpec((k, tn), lambda i, j: (0, j + jb)),
        ],
        out_specs=pl.BlockSpec((tm, tn), lambda i, j: (i, j)),
        out_shape=jax.ShapeDtypeStruct((m, n), out_dtype),
        scratch_shapes=[pltpu.VMEM((tm, k), bf16)],
        compiler_params=_params("parallel", "arbitrary"),
        name=name,
    )(x, w)


DSA_TQ = 256
DSA_TK = 512
DSA_NCH = SEQ // DSA_TK
DSA_ONES = 16
KEY_NEG_FLT_MAX = -(2 ** 31) + (1 << 23)


def _key_to_f32(key):
    return lax.bitcast_convert_type(key ^ ((key >> 31) & jnp.int32(0x7FFFFFFF)), f32)


def _dsa_kernel(q_ref, k_ref, vt_ref, qi_ref, kk_ref, wt_ref, o_ref, sc_scr, acc_scr):
    i = pl.program_id(1)
    q0 = i * DSA_TQ
    nk = (q0 + DSA_TQ - 1) // DSA_TK + 1

    qpos = q0 + lax.broadcasted_iota(i32, (DSA_TK, DSA_TQ), 1)
    kofs = lax.broadcasted_iota(i32, (DSA_TK, DSA_TQ), 0)
    contract_last = (((1,), (1,)), ((), ()))

    def rows8(x):
        return x.reshape(DSA_TK // 8, 8, DSA_TQ)

    wt = wt_ref[...] * (IDX_DIM ** -0.5 * IDX_HEADS ** -0.5)

    def score_chunk(c, carry):
        r0 = pl.multiple_of(c * DSA_TK, DSA_TK)
        kk0 = kk_ref[0, pl.ds(r0, DSA_TK), :]
        kk1 = kk_ref[1, pl.ds(r0, DSA_TK), :]
        acc = jnp.zeros((DSA_TK, DSA_TQ), f32)
        for p in range(IDX_HEADS // 2):
            slab = qi_ref[:, p * LANES:(p + 1) * LANES]
            l0 = lax.dot_general(kk0, slab, contract_last, preferred_element_type=f32)
            l1 = lax.dot_general(kk1, slab, contract_last, preferred_element_type=f32)
            acc = acc + (jnp.maximum(l0, 0.0) * wt[2 * p:2 * p + 1]
                         + jnp.maximum(l1, 0.0) * wt[2 * p + 1:2 * p + 2])
        sc_scr[c] = jnp.where(kofs + r0 <= qpos, acc, -jnp.inf)
        return carry

    lax.fori_loop(0, nk, score_chunk, 0)

    @pl.when(nk % 2 == 1)
    def _():
        sc_scr[nk] = jnp.full((DSA_TK, DSA_TQ), -jnp.inf, f32)

    def count(pred):
        def body(c2, acc):
            return (acc + jnp.sum(rows8(pred(2 * c2).astype(i32)), axis=0)
                    + jnp.sum(rows8(pred(2 * c2 + 1).astype(i32)), axis=0))
        acc = lax.fori_loop(0, (nk + 1) // 2, body, jnp.zeros((8, DSA_TQ), i32))
        return jnp.sum(acc, axis=0, keepdims=True)

    def count_ge(cand):
        return count(lambda c: sc_scr[c] >= cand)

    key = jnp.where(count_ge(jnp.zeros((1, DSA_TQ), f32)) >= TOPK, jnp.int32(0), jnp.int32(INT_MIN))

    def bit_step(b, key):
        cand = key | jnp.left_shift(jnp.int32(1), 30 - b)
        return jnp.where(count_ge(_key_to_f32(cand)) >= TOPK, cand, key)

    key = lax.fori_loop(0, 31, bit_step, key)
    thr = _key_to_f32(jnp.maximum(key, jnp.int32(KEY_NEG_FLT_MAX)))

    n_ge = count_ge(thr)

    @pl.when(jnp.max(n_ge) > TOPK)
    def _():
        need = TOPK - count(lambda c: sc_scr[c] > thr)

        def count_eq_below(pos):
            return count(lambda c: (sc_scr[c] == thr) & (kofs + c * DSA_TK < pos))

        def pos_step(b, r):
            cand = r | jnp.left_shift(jnp.int32(1), (SEQ.bit_length() - 2) - b)
            return jnp.where(count_eq_below(cand) < need, cand, r)
        r = lax.fori_loop(0, SEQ.bit_length() - 1, pos_step, jnp.zeros((1, DSA_TQ), i32))
        r = jnp.where(n_ge > TOPK, r, jnp.int32(SEQ))

        def drop(c, carry):
            sc = sc_scr[c]
            sc_scr[c] = jnp.where((sc == thr) & (kofs + c * DSA_TK > r), -jnp.inf, sc)
            return carry
        lax.fori_loop(0, nk, drop, 0)

    acc_scr[...] = jnp.zeros(acc_scr.shape, f32)
    c1 = HEAD_DIM ** -0.5 * LOG2E
    ones = jnp.ones((DSA_ONES, DSA_TK), bf16)

    def attn_chunk(c, ms):
        r0 = pl.multiple_of(c * DSA_TK, DSA_TK)
        madd = jnp.where(sc_scr[c] >= thr, 0.0, -jnp.inf)
        kd = (lax.broadcasted_iota(i32, (DSA_TK, LANES), 0) + (r0 - q0)).astype(f32)
        def qk(h):
            hd = slice(h * HEAD_DIM, (h + 1) * HEAD_DIM)
            return lax.dot_general(k_ref[pl.ds(r0, DSA_TK), hd], q_ref[:, hd], contract_last,
                                   preferred_element_type=f32)

        new_ms = []
        s_next = qk(0)
        for h in range(ATT_HEADS):
            slope2 = 2.0 ** (-8.0 * (h + 1) / ATT_HEADS) * LOG2E
            hd = slice(h * HEAD_DIM, (h + 1) * HEAD_DIM)
            s = s_next
            if h + 1 < ATT_HEADS:
                s_next = qk(h + 1)
            bias = kd * slope2
            t = s * c1 + jnp.concatenate([bias] * (DSA_TQ // LANES), axis=1) + madd
            m_old = ms[h]
            m_new = jnp.maximum(m_old, jnp.max(t, axis=0, keepdims=True))
            alpha = jnp.exp2(m_old - m_new)
            p = jnp.exp2(t - m_new).astype(bf16)
            vt1 = jnp.concatenate([vt_ref[c, hd, :], ones], axis=0)
            acc_scr[h] = alpha * acc_scr[h] + jnp.dot(vt1, p, preferred_element_type=f32)
            new_ms.append(m_new)
        return tuple(new_ms)

    m0 = jnp.full((1, DSA_TQ), -1e30, f32)
    lax.fori_loop(0, nk, attn_chunk, (m0,) * ATT_HEADS)

    for h in range(ATT_HEADS):
        o_ref[h * HEAD_DIM:(h + 1) * HEAD_DIM, :] = (
            acc_scr[h, :HEAD_DIM, :] / acc_scr[h, HEAD_DIM:HEAD_DIM + 1, :]).astype(o_ref.dtype)


def _dsa(za, vt, kk, wt_idx):
    nq = SEQ // DSA_TQ
    once = pl.Buffered(1)
    return pl.pallas_call(
        _dsa_kernel,
        grid=(BATCH, nq),
        in_specs=[
            pl.BlockSpec((DSA_TQ, ATT_WIDTH), lambda b, i: (b * nq + i, 0)),
            pl.BlockSpec((SEQ, ATT_WIDTH), lambda b, i: (b, 1), pipeline_mode=once),
            pl.BlockSpec((None, DSA_NCH, ATT_WIDTH, DSA_TK), lambda b, i: (b, 0, 0, 0),
                         pipeline_mode=once),
            pl.BlockSpec((DSA_TQ, IDX_HEADS * IDX_DIM), lambda b, i: (b * nq + i, 3)),
            pl.BlockSpec((2, SEQ, LANES), lambda b, i: (0, b, 0), pipeline_mode=once),
            pl.BlockSpec((IDX_HEADS, DSA_TQ), lambda b, i: (0, b * nq + i)),
        ],
        out_specs=pl.BlockSpec((None, ATT_WIDTH, DSA_TQ), lambda b, i: (b, 0, i)),
        out_shape=jax.ShapeDtypeStruct((BATCH, ATT_WIDTH, SEQ), bf16),
        scratch_shapes=[
            pltpu.VMEM((DSA_NCH, DSA_TK, DSA_TQ), f32),
            pltpu.VMEM((ATT_HEADS, HEAD_DIM + DSA_ONES, DSA_TQ), f32),
        ],
        compiler_params=_params("arbitrary", "arbitrary"),
        name="dsa",
    )(za, za, vt, za, kk, wt_idx)


def _pool_kernel(u_ref, wp_ref, ps_ref, o_ref):
    row = lax.broadcasted_iota(i32, (SEQ, POOL_GROUP), 0)
    for g, win in enumerate(POOL_WINDOWS):
        x = u_ref[:, g * POOL_GROUP:(g + 1) * POOL_GROUP]
        s = x
        k = 1
        while k < win:
            s = s + jnp.where(row >= k, pltpu.roll(s, k, axis=0), 0.0)
            k *= 2
        cnt = jnp.minimum(row + 1, win).astype(f32)
        pooled = (s / cnt - x).astype(bf16)
        mixed = jnp.dot(pooled, wp_ref[g], preferred_element_type=f32)
        o_ref[:, g * POOL_GROUP:(g + 1) * POOL_GROUP] = (
            mixed * ps_ref[:, g * POOL_GROUP:(g + 1) * POOL_GROUP]).astype(o_ref.dtype)


def _pool(zb, w_pool, pool_scale):
    return pl.pallas_call(
        _pool_kernel,
        grid=(BATCH,),
        in_specs=[
            pl.BlockSpec((SEQ, POOL_WIDTH), lambda b: (b, 0)),
            pl.BlockSpec((N_POOL, POOL_GROUP, POOL_GROUP), lambda b: (0, 0, 0)),
            pl.BlockSpec((1, POOL_WIDTH), lambda b: (0, 0)),
        ],
        out_specs=pl.BlockSpec((SEQ, POOL_WIDTH), lambda b: (b, 0)),
        out_shape=jax.ShapeDtypeStruct((TOKENS, POOL_WIDTH), bf16),
        compiler_params=_params("parallel"),
        name="pool",
    )(zb, w_pool, pool_scale)


MEM_TQ = 512


def _mem_attn_kernel(q_ref, k_ref, v_ref, o_ref):
    for h in range(MEM_HEADS):
        sl = slice(h * HEAD_DIM, (h + 1) * HEAD_DIM)
        s = lax.dot_general(q_ref[:, sl], k_ref[:, sl], (((1,), (1,)), ((), ())),
                            preferred_element_type=f32) * (HEAD_DIM ** -0.5)
        p = jnp.exp(s - jnp.max(s, axis=1, keepdims=True))
        l = jnp.sum(p, axis=1, keepdims=True)
        o = jnp.dot(p.astype(bf16), v_ref[:, sl], preferred_element_type=f32)
        o_ref[:, sl] = (o / l).astype(o_ref.dtype)


def _mem_attn(za, kv):
    nq = SEQ // MEM_TQ
    qcol = (3 * ATT_WIDTH + IDX_HEADS * IDX_DIM) // MEM_WIDTH
    return pl.pallas_call(
        _mem_attn_kernel,
        grid=(BATCH, nq),
        in_specs=[
            pl.BlockSpec((MEM_TQ, MEM_WIDTH), lambda b, i: (b * nq + i, qcol)),
            pl.BlockSpec((MEM_LEN, MEM_WIDTH), lambda b, i: (b, 0)),
            pl.BlockSpec((MEM_LEN, MEM_WIDTH), lambda b, i: (b, 1)),
        ],
        out_specs=pl.BlockSpec((MEM_TQ, MEM_WIDTH), lambda b, i: (b * nq + i, 0)),
        out_shape=jax.ShapeDtypeStruct((TOKENS, MEM_WIDTH), bf16),
        compiler_params=_params("parallel", "parallel"),
        name="mem_attn",
    )(za, kv, kv)


CMB_TM = 512
CMB_TC = 512


def _combine_ln_kernel(h_ref, a_ref, p_ref, m_ref, wg0_ref, wg1_ref, wg2_ref,
                       bg0_ref, bg1_ref, bg2_ref, wa_ref, wp_ref, wm_ref, wo_ref,
                       g_ref, b_ref, o_ref, hb_ref):
    j = pl.program_id(1)

    @pl.when(j == 0)
    def _():
        hb_ref[...] = h_ref[...].astype(bf16)
        o_ref[...] = jnp.zeros_like(o_ref)

    hb = hb_ref[...]

    def gate(wg_ref, bg_ref):
        return jax.nn.sigmoid(jnp.dot(hb, wg_ref[...], preferred_element_type=f32) + bg_ref[...])

    y = gate(wg0_ref, bg0_ref) * jnp.dot(a_ref[...], wa_ref[...], preferred_element_type=f32)
    y = y + gate(wg1_ref, bg1_ref) * jnp.dot(p_ref[...], wp_ref[...], preferred_element_type=f32)
    y = y + gate(wg2_ref, bg2_ref) * jnp.dot(m_ref[...], wm_ref[...], preferred_element_type=f32)
    o_ref[...] += jnp.dot(y.astype(bf16), wo_ref[...], preferred_element_type=f32)

    @pl.when(j == pl.num_programs(1) - 1)
    def _():
        z = ALPHA * h_ref[...] + o_ref[...]
        o_ref[...] = _layer_norm(z, g_ref[...], b_ref[...])


def _combine_ln(h, a, p, m, w_gate, gate_col0, b_gate, w_a, w_p, w_m, w_out, g, b):
    nc = D_MODEL // CMB_TC
    assert gate_col0 % CMB_TC == 0
    g0 = gate_col0 // CMB_TC
    row = lambda i, j: (i, 0)
    col = lambda i, j: (0, j)
    return pl.pallas_call(
        _combine_ln_kernel,
        grid=(TOKENS // CMB_TM, nc),
        in_specs=[
            pl.BlockSpec((CMB_TM, D_MODEL), row),
            pl.BlockSpec((CMB_TM, ATT_WIDTH), row),
            pl.BlockSpec((CMB_TM, POOL_WIDTH), row),
            pl.BlockSpec((CMB_TM, MEM_WIDTH), row),
            pl.BlockSpec((D_MODEL, CMB_TC), lambda i, j: (0, g0 + j)),
            pl.BlockSpec((D_MODEL, CMB_TC), lambda i, j: (0, g0 + j + nc)),
            pl.BlockSpec((D_MODEL, CMB_TC), lambda i, j: (0, g0 + j + 2 * nc)),
            pl.BlockSpec((1, CMB_TC), lambda i, j: (0, j)),
            pl.BlockSpec((1, CMB_TC), lambda i, j: (0, j + nc)),
            pl.BlockSpec((1, CMB_TC), lambda i, j: (0, j + 2 * nc)),
            pl.BlockSpec((ATT_WIDTH, CMB_TC), col),
            pl.BlockSpec((POOL_WIDTH, CMB_TC), col),
            pl.BlockSpec((MEM_WIDTH, CMB_TC), col),
            pl.BlockSpec((CMB_TC, D_MODEL), lambda i, j: (j, 0)),
            pl.BlockSpec((1, D_MODEL), lambda i, j: (0, 0)),
            pl.BlockSpec((1, D_MODEL), lambda i, j: (0, 0)),
        ],
        out_specs=pl.BlockSpec((CMB_TM, D_MODEL), row),
        out_shape=jax.ShapeDtypeStruct((TOKENS, D_MODEL), f32),
        scratch_shapes=[pltpu.VMEM((CMB_TM, D_MODEL), bf16)],
        compiler_params=_params("parallel", "arbitrary"),
        name="combine_ln",
    )(h, a, p, m, w_gate, w_gate, w_gate, b_gate, b_gate, b_gate, w_a, w_p, w_m, w_out, g, b)


def kernel(x, mem, w_ffn1_up, w_ffn1_down, ln1_g, ln1_b, w_in, b_gate, w_mem_kv, w_pool,
           pool_scale, w_br_att, w_br_pool, w_br_mem, w_out, ln2_g, ln2_b, w_ffn2_up,
           w_ffn2_down, ln3_g, ln3_b):
    h = x.reshape(TOKENS, D_MODEL)
    memf = mem.reshape(BATCH * MEM_LEN, D_MODEL)
    n_idx = IDX_HEADS * IDX_DIM
    for l in range(DEPTH):
        c_qkv = 3 * ATT_WIDTH
        c_qi = c_qkv + n_idx
        c_ki = c_qi + IDX_DIM
        c_wi = c_ki + IDX_HEADS
        c_u = c_wi + POOL_WIDTH
        c_qm = c_u + MEM_WIDTH
        wi = w_in[l]
        pad = jnp.zeros((D_MODEL, LANES - IDX_DIM - IDX_HEADS), wi.dtype)
        w_all = jnp.concatenate([wi[:, :c_qi], wi[:, c_u:c_qm], wi[:, c_qm:], wi[:, c_wi:c_u],
                                 wi[:, c_qi:c_wi], pad], axis=1).astype(bf16)
        n_a = c_qi + MEM_WIDTH
        col_gate = n_a
        col_b = col_gate + N_BRANCH * D_MODEL

        h = _ffn_ln(h, w_ffn1_up[l].astype(bf16), w_ffn1_down[l].astype(bf16),
                    ln1_g[l][None], ln1_b[l][None])

        za = _matmul(h, w_all, bf16, 1024, 512, "in_proj_a", col0=0, n=n_a)
        zb = _matmul(h, w_all, f32, 1024, LANES, "in_proj_b", col0=col_b, n=POOL_WIDTH + LANES)
        kv = _matmul(memf, w_mem_kv[l].astype(bf16), bf16, BATCH * MEM_LEN, 512, "mem_kv")

        ki = zb[:, POOL_WIDTH:POOL_WIDTH + IDX_DIM].astype(bf16)
        zk = jnp.zeros_like(ki)
        kk = jnp.stack([jnp.concatenate([ki, zk], axis=1), jnp.concatenate([zk, ki], axis=1)])
        wt_idx = zb[:, POOL_WIDTH + IDX_DIM:POOL_WIDTH + IDX_DIM + IDX_HEADS].T
        vt = za[:, 2 * ATT_WIDTH:3 * ATT_WIDTH].reshape(BATCH, DSA_NCH, DSA_TK, ATT_WIDTH)
        vt = vt.transpose(0, 1, 3, 2)

        a = _dsa(za, vt, kk, wt_idx).transpose(0, 2, 1).reshape(TOKENS, ATT_WIDTH)
        p = _pool(zb, w_pool[l].astype(bf16), pool_scale[l][None])
        m = _mem_attn(za, kv)

        h = _combine_ln(h, a, p, m, w_all, col_gate, b_gate[l][None], w_br_att[l].astype(bf16),
                        w_br_pool[l].astype(bf16), w_br_mem[l].astype(bf16),
                        w_out[l].astype(bf16), ln2_g[l][None], ln2_b[l][None])

        h = _ffn_ln(h, w_ffn2_up[l].astype(bf16), w_ffn2_down[l].astype(bf16),
                    ln3_g[l][None], ln3_b[l][None])
    return h.reshape(BATCH, SEQ, D_MODEL)
```

```python
import functools
import math

import jax
import jax.numpy as jnp
from jax import lax
from jax.experimental import pallas as pl
from jax.experimental.pallas import tpu as pltpu

f32 = jnp.float32
bf16 = jnp.bfloat16
i32 = jnp.int32

D_MODEL = 2048
BATCH = 2
SEQ = 4096
DEPTH = 1
MEM_LEN = 256
ATT_HEADS = 8
HEAD_DIM = 128
ATT_WIDTH = ATT_HEADS * HEAD_DIM
IDX_HEADS = 16
IDX_DIM = 64
TOPK = min(256, SEQ // 4)
POOL_WINDOWS = (2, 4, 8, 16)
N_POOL = len(POOL_WINDOWS)
POOL_GROUP = 128
POOL_WIDTH = N_POOL * POOL_GROUP
MEM_HEADS = 4
MEM_WIDTH = MEM_HEADS * HEAD_DIM
N_BRANCH = 3
D_FF = 5632
ALPHA = (2 * DEPTH) ** 0.25
LN_EPS = 1e-5
TOKENS = BATCH * SEQ

LANES = 128
VMEM_LIMIT = 60 * 1024 * 1024

INT_MIN = -(2 ** 31)
LOG2E = math.log2(math.e)


def _params(*sem):
    return pltpu.CompilerParams(dimension_semantics=sem, vmem_limit_bytes=VMEM_LIMIT)


def _layer_norm(y, g, b):
    mu = jnp.mean(y, axis=-1, keepdims=True)
    d = y - mu
    var = jnp.mean(d * d, axis=-1, keepdims=True)
    return d * lax.rsqrt(var + LN_EPS) * g + b


FFN_TM = 1024
FFN_TF = 256


def _ffn_ln_kernel(x_ref, wa_ref, wu_ref, wd_ref, g_ref, b_ref, o_ref, xb_ref):
    j = pl.program_id(1)

    @pl.when(j == 0)
    def _():
        xb_ref[...] = x_ref[...].astype(bf16)
        o_ref[...] = jnp.zeros_like(o_ref)

    xb = xb_ref[...]
    a = jnp.dot(xb, wa_ref[...], preferred_element_type=f32)
    u = jnp.dot(xb, wu_ref[...], preferred_element_type=f32)
    act = (a * jax.nn.sigmoid(a) * u).astype(bf16)
    o_ref[...] += jnp.dot(act, wd_ref[...], preferred_element_type=f32)

    @pl.when(j == pl.num_programs(1) - 1)
    def _():
        y = ALPHA * x_ref[...] + 0.5 * o_ref[...]
        o_ref[...] = _layer_norm(y, g_ref[...], b_ref[...])


def _ffn_ln(x, w_up, w_down, g, b):
    nf = D_FF // FFN_TF
    return pl.pallas_call(
        _ffn_ln_kernel,
        grid=(TOKENS // FFN_TM, nf),
        in_specs=[
            pl.BlockSpec((FFN_TM, D_MODEL), lambda i, j: (i, 0)),
            pl.BlockSpec((D_MODEL, FFN_TF), lambda i, j: (0, j)),
            pl.BlockSpec((D_MODEL, FFN_TF), lambda i, j: (0, j + nf)),
            pl.BlockSpec((FFN_TF, D_MODEL), lambda i, j: (j, 0)),
            pl.BlockSpec((1, D_MODEL), lambda i, j: (0, 0)),
            pl.BlockSpec((1, D_MODEL), lambda i, j: (0, 0)),
        ],
        out_specs=pl.BlockSpec((FFN_TM, D_MODEL), lambda i, j: (i, 0)),
        out_shape=jax.ShapeDtypeStruct((TOKENS, D_MODEL), f32),
        scratch_shapes=[pltpu.VMEM((FFN_TM, D_MODEL), bf16)],
        compiler_params=_params("parallel", "arbitrary"),
        name="ffn_ln",
    )(x, w_up, w_up, w_down, g, b)


def _matmul_kernel(x_ref, w_ref, o_ref, xb_ref):
    @pl.when(pl.program_id(1) == 0)
    def _():
        xb_ref[...] = x_ref[...].astype(bf16)

    o_ref[...] = jnp.dot(xb_ref[...], w_ref[...], preferred_element_type=f32).astype(o_ref.dtype)


def _matmul(x, w, out_dtype, tm, tn, name, col0=0, n=None):
    m, k = x.shape
    n = w.shape[1] if n is None else n
    assert col0 % tn == 0 and n % tn == 0 and m % tm == 0
    jb = col0 // tn
    return pl.pallas_call(
        _matmul_kernel,
        grid=(m // tm, n // tn),
        in_specs=[
            pl.BlockSpec((tm, k), lambda i, j: (i, 0)),
            pl.BlockSpec((k, tn), lambda i, j: (0, j + jb)),
        ],
        out_specs=pl.BlockSpec((tm, tn), lambda i, j: (i, j)),
        out_shape=jax.ShapeDtypeStruct((m, n), out_dtype),
        scratch_shapes=[pltpu.VMEM((tm, k), bf16)],
        compiler_params=_params("parallel", "arbitrary"),
        name=name,
    )(x, w)


TAIL_COL0 = 3 * ATT_WIDTH + IDX_HEADS * IDX_DIM
TAIL_SHIFT = IDX_DIM + IDX_HEADS
TAIL_N = POOL_WIDTH + MEM_WIDTH + N_BRANCH * D_MODEL
RL_TR = 256
RL_TC = 1024


def _tail_relayout_kernel(a_ref, b_ref, o_ref):
    cat = jnp.concatenate([a_ref[...], b_ref[...]], axis=1)
    o_ref[...] = cat[:, TAIL_SHIFT:TAIL_SHIFT + RL_TC].astype(o_ref.dtype)


def _tail_relayout(w):
    assert TAIL_COL0 % RL_TC == 0 and TAIL_N % RL_TC == 0 and TAIL_SHIFT < LANES
    a0 = TAIL_COL0 // RL_TC
    b0 = TAIL_COL0 // LANES
    bs = RL_TC // LANES
    return pl.pallas_call(
        _tail_relayout_kernel,
        grid=(D_MODEL // RL_TR, TAIL_N // RL_TC),
        in_specs=[
            pl.BlockSpec((RL_TR, RL_TC), lambda i, j: (i, a0 + j)),
            pl.BlockSpec((RL_TR, LANES), lambda i, j: (i, b0 + bs * (j + 1))),
        ],
        out_specs=pl.BlockSpec((RL_TR, RL_TC), lambda i, j: (i, j)),
        out_shape=jax.ShapeDtypeStruct((D_MODEL, TAIL_N), bf16),
        compiler_params=_params("parallel", "parallel"),
        name="w_tail_relayout",
    )(w, w)


PB_TM = 512
PB_N = POOL_WIDTH + MEM_WIDTH + LANES


def _in_proj_b_kernel(x_ref, wt_ref, wk_ref, o_ref):
    xb = x_ref[...].astype(bf16)
    n_t = POOL_WIDTH + MEM_WIDTH
    o_ref[:, :n_t] = jnp.dot(xb, wt_ref[...], preferred_element_type=f32)
    o_ref[:, n_t:] = jnp.dot(xb, wk_ref[...], preferred_element_type=f32)


def _in_proj_b(x, w_tail, w_kiwi):
    n_t = POOL_WIDTH + MEM_WIDTH
    return pl.pallas_call(
        _in_proj_b_kernel,
        grid=(TOKENS // PB_TM,),
        in_specs=[
            pl.BlockSpec((PB_TM, D_MODEL), lambda i: (i, 0)),
            pl.BlockSpec((D_MODEL, n_t), lambda i: (0, 0)),
            pl.BlockSpec((D_MODEL, LANES), lambda i: (0, 0)),
        ],
        out_specs=pl.BlockSpec((PB_TM, PB_N), lambda i: (i, 0)),
        out_shape=jax.ShapeDtypeStruct((TOKENS, PB_N), f32),
        compiler_params=_params("parallel"),
        name="in_proj_b",
    )(x, w_tail, w_kiwi)


DSA_TQ = 256
DSA_TK = 512
DSA_NCH = SEQ // DSA_TK
DSA_ONES = 16
KEY_NEG_FLT_MAX = -(2 ** 31) + (1 << 23)


def _key_to_f32(key):
    return lax.bitcast_convert_type(key ^ ((key >> 31) & jnp.int32(0x7FFFFFFF)), f32)


def _clear_low16(x):
    return lax.bitcast_convert_type(lax.bitcast_convert_type(x, i32) & jnp.int32(-65536), f32)


def _dsa_kernel(q_ref, k_ref, vt_ref, qi_ref, kk_ref, wt_ref, o_ref, sc_scr, sch_scr, acc_scr):
    i = pl.program_id(1)
    q0 = i * DSA_TQ
    nk = (q0 + DSA_TQ - 1) // DSA_TK + 1

    qpos = q0 + lax.broadcasted_iota(i32, (DSA_TK, DSA_TQ), 1)
    kofs = lax.broadcasted_iota(i32, (DSA_TK, DSA_TQ), 0)
    contract_last = (((1,), (1,)), ((), ()))

    def rows8(x):
        return x.reshape(DSA_TK // 8, 8, DSA_TQ)

    wt = wt_ref[...] * (IDX_DIM ** -0.5 * IDX_HEADS ** -0.5)

    def score_chunk(c, carry):
        r0 = pl.multiple_of(c * DSA_TK, DSA_TK)
        kk0 = kk_ref[0, pl.ds(r0, DSA_TK), :]
        kk1 = kk_ref[1, pl.ds(r0, DSA_TK), :]
        acc = jnp.zeros((DSA_TK, DSA_TQ), f32)
        for p in range(IDX_HEADS // 2):
            slab = qi_ref[:, p * LANES:(p + 1) * LANES]
            l0 = lax.dot_general(kk0, slab, contract_last, preferred_element_type=f32)
            l1 = lax.dot_general(kk1, slab, contract_last, preferred_element_type=f32)
            acc = acc + (jnp.maximum(l0, 0.0) * wt[2 * p:2 * p + 1]
                         + jnp.maximum(l1, 0.0) * wt[2 * p + 1:2 * p + 2])
        sc = jnp.where(kofs + r0 <= qpos, acc, -jnp.inf)
        sc_scr[c] = sc
        sch_scr[c] = _clear_low16(sc).astype(bf16)
        return carry

    lax.fori_loop(0, nk, score_chunk, 0)

    def count(pred):
        def body(c, acc):
            return acc + jnp.sum(rows8(pred(c).astype(i32)), axis=0)
        acc = lax.fori_loop(0, nk, body, jnp.zeros((8, DSA_TQ), i32))
        return jnp.sum(acc, axis=0, keepdims=True)

    def count_ge(cand):
        return count(lambda c: sc_scr[c] >= cand)

    def count_ge_hi(cand):
        one = jnp.ones((), bf16)
        zero = jnp.zeros((), bf16)

        def body(c, acc):
            m = jnp.where(sch_scr[c] >= cand, one, zero).reshape(DSA_TK // 16, 16, DSA_TQ)
            n = DSA_TK // 16
            while n > 1:
                n //= 2
                m = m[:n] + m[n:]
            return acc + m[0]
        acc = lax.fori_loop(0, nk, body, jnp.zeros((16, DSA_TQ), bf16))
        return jnp.sum(acc.astype(f32), axis=0, keepdims=True)

    key = jnp.where(count_ge(jnp.zeros((1, DSA_TQ), f32)) >= TOPK, jnp.int32(0), jnp.int32(INT_MIN))

    def hi_step(b, key):
        cand = key | jnp.left_shift(jnp.int32(1), 30 - b)
        cand_hi = _clear_low16(_key_to_f32(cand)).astype(bf16)
        return jnp.where(count_ge_hi(cand_hi) >= TOPK, cand, key)

    key = lax.fori_loop(0, 15, hi_step, key)

    def lo_step(b, key):
        cand = key | jnp.left_shift(jnp.int32(1), 15 - b)
        return jnp.where(count_ge(_key_to_f32(cand)) >= TOPK, cand, key)

    key = lax.fori_loop(0, 16, lo_step, key)
    thr = _key_to_f32(jnp.maximum(key, jnp.int32(KEY_NEG_FLT_MAX)))

    n_ge = count_ge(thr)

    @pl.when(jnp.max(n_ge) > TOPK)
    def _():
        need = TOPK - count(lambda c: sc_scr[c] > thr)

        def count_eq_below(pos):
            return count(lambda c: (sc_scr[c] == thr) & (kofs + c * DSA_TK < pos))

        def pos_step(b, r):
            cand = r | jnp.left_shift(jnp.int32(1), (SEQ.bit_length() - 2) - b)
            return jnp.where(count_eq_below(cand) < need, cand, r)
        r = lax.fori_loop(0, SEQ.bit_length() - 1, pos_step, jnp.zeros((1, DSA_TQ), i32))
        r = jnp.where(n_ge > TOPK, r, jnp.int32(SEQ))

        def drop(c, carry):
            sc = sc_scr[c]
            sc_scr[c] = jnp.where((sc == thr) & (kofs + c * DSA_TK > r), -jnp.inf, sc)
            return carry
        lax.fori_loop(0, nk, drop, 0)

    acc_scr[...] = jnp.zeros(acc_scr.shape, f32)
    c1 = HEAD_DIM ** -0.5 * LOG2E
    ones = jnp.ones((DSA_ONES, DSA_TK), bf16)

    def attn_chunk(c, ms):
        r0 = pl.multiple_of(c * DSA_TK, DSA_TK)
        madd = jnp.where(sc_scr[c] >= thr, 0.0, -jnp.inf)
        kd = (lax.broadcasted_iota(i32, (DSA_TK, LANES), 0) + (r0 - q0)).astype(f32)
        def qk(h):
            hd = slice(h * HEAD_DIM, (h + 1) * HEAD_DIM)
            return lax.dot_general(k_ref[pl.ds(r0, DSA_TK), hd], q_ref[:, hd], contract_last,
                                   preferred_element_type=f32)

        new_ms = []
        s_next = qk(0)
        for h in range(ATT_HEADS):
            slope2 = 2.0 ** (-8.0 * (h + 1) / ATT_HEADS) * LOG2E
            hd = slice(h * HEAD_DIM, (h + 1) * HEAD_DIM)
            s = s_next
            if h + 1 < ATT_HEADS:
                s_next = qk(h + 1)
            bias = kd * slope2
            t = s * c1 + jnp.concatenate([bias] * (DSA_TQ // LANES), axis=1) + madd
            m_old = ms[h]
            m_new = jnp.maximum(m_old, jnp.max(t, axis=0, keepdims=True))
            alpha = jnp.exp2(m_old - m_new)
            p = jnp.exp2(t - m_new).astype(bf16)
            vt1 = jnp.concatenate([vt_ref[c, hd, :], ones], axis=0)
            acc_scr[h] = alpha * acc_scr[h] + jnp.dot(vt1, p, preferred_element_type=f32)
            new_ms.append(m_new)
        return tuple(new_ms)

    m0 = jnp.full((1, DSA_TQ), -1e30, f32)
    lax.fori_loop(0, nk, attn_chunk, (m0,) * ATT_HEADS)

    for h in range(ATT_HEADS):
        o_ref[h * HEAD_DIM:(h + 1) * HEAD_DIM, :] = (
            acc_scr[h, :HEAD_DIM, :] / acc_scr[h, HEAD_DIM:HEAD_DIM + 1, :]).astype(o_ref.dtype)


def _dsa(za, vt, kk, wt_idx):
    nq = SEQ // DSA_TQ
    once = pl.Buffered(1)
    return pl.pallas_call(
        _dsa_kernel,
        grid=(BATCH, nq),
        in_specs=[
            pl.BlockSpec((DSA_TQ, ATT_WIDTH), lambda b, i: (b * nq + i, 0)),
            pl.BlockSpec((SEQ, ATT_WIDTH), lambda b, i: (b, 1), pipeline_mode=once),
            pl.BlockSpec((None, DSA_NCH, ATT_WIDTH, DSA_TK), lambda b, i: (b, 0, 0, 0),
                         pipeline_mode=once),
            pl.BlockSpec((DSA_TQ, IDX_HEADS * IDX_DIM), lambda b, i: (b * nq + i, 3)),
            pl.BlockSpec((2, SEQ, LANES), lambda b, i: (0, b, 0), pipeline_mode=once),
            pl.BlockSpec((IDX_HEADS, DSA_TQ), lambda b, i: (0, b * nq + i)),
        ],
        out_specs=pl.BlockSpec((None, ATT_WIDTH, DSA_TQ), lambda b, i: (b, 0, i)),
        out_shape=jax.ShapeDtypeStruct((BATCH, ATT_WIDTH, SEQ), bf16),
        scratch_shapes=[
            pltpu.VMEM((DSA_NCH, DSA_TK, DSA_TQ), f32),
            pltpu.VMEM((DSA_NCH, DSA_TK, DSA_TQ), bf16),
            pltpu.VMEM((ATT_HEADS, HEAD_DIM + DSA_ONES, DSA_TQ), f32),
        ],
        compiler_params=_params("arbitrary", "arbitrary"),
        name="dsa",
    )(za, za, vt, za, kk, wt_idx)


def _pool_kernel(u_ref, wp_ref, ps_ref, o_ref):
    row = lax.broadcasted_iota(i32, (SEQ, POOL_GROUP), 0)
    for g, win in enumerate(POOL_WINDOWS):
        x = u_ref[:, g * POOL_GROUP:(g + 1) * POOL_GROUP]
        s = x
        k = 1
        while k < win:
            s = s + jnp.where(row >= k, pltpu.roll(s, k, axis=0), 0.0)
            k *= 2
        cnt = jnp.minimum(row + 1, win).astype(f32)
        pooled = (s / cnt - x).astype(bf16)
        mixed = jnp.dot(pooled, wp_ref[g], preferred_element_type=f32)
        o_ref[:, g * POOL_GROUP:(g + 1) * POOL_GROUP] = (
            mixed * ps_ref[:, g * POOL_GROUP:(g + 1) * POOL_GROUP]).astype(o_ref.dtype)


def _pool(zb, w_pool, pool_scale):
    return pl.pallas_call(
        _pool_kernel,
        grid=(BATCH,),
        in_specs=[
            pl.BlockSpec((SEQ, POOL_WIDTH), lambda b: (b, 0)),
            pl.BlockSpec((N_POOL, POOL_GROUP, POOL_GROUP), lambda b: (0, 0, 0)),
            pl.BlockSpec((1, POOL_WIDTH), lambda b: (0, 0)),
        ],
        out_specs=pl.BlockSpec((SEQ, POOL_WIDTH), lambda b: (b, 0)),
        out_shape=jax.ShapeDtypeStruct((TOKENS, POOL_WIDTH), bf16),
        compiler_params=_params("parallel"),
        name="pool",
    )(zb, w_pool, pool_scale)


MEM_TQ = 512


def _mem_attn_kernel(q_ref, k_ref, v_ref, o_ref):
    for h in range(MEM_HEADS):
        sl = slice(h * HEAD_DIM, (h + 1) * HEAD_DIM)
        s = lax.dot_general(q_ref[:, sl].astype(bf16), k_ref[:, sl], (((1,), (1,)), ((), ())),
                            preferred_element_type=f32) * (HEAD_DIM ** -0.5)
        p = jnp.exp(s - jnp.max(s, axis=1, keepdims=True))
        l = jnp.sum(p, axis=1, keepdims=True)
        o = jnp.dot(p.astype(bf16), v_ref[:, sl], preferred_element_type=f32)
        o_ref[:, sl] = (o / l).astype(o_ref.dtype)


def _mem_attn(zb, qcol, kv):
    nq = SEQ // MEM_TQ
    return pl.pallas_call(
        _mem_attn_kernel,
        grid=(BATCH, nq),
        in_specs=[
            pl.BlockSpec((MEM_TQ, MEM_WIDTH), lambda b, i: (b * nq + i, qcol)),
            pl.BlockSpec((MEM_LEN, MEM_WIDTH), lambda b, i: (b, 0)),
            pl.BlockSpec((MEM_LEN, MEM_WIDTH), lambda b, i: (b, 1)),
        ],
        out_specs=pl.BlockSpec((MEM_TQ, MEM_WIDTH), lambda b, i: (b * nq + i, 0)),
        out_shape=jax.ShapeDtypeStruct((TOKENS, MEM_WIDTH), bf16),
        compiler_params=_params("parallel", "parallel"),
        name="mem_attn",
    )(zb, kv, kv)


CMB_TM = 512
CMB_TC = 512


def _combine_ln_kernel(h_ref, a_ref, p_ref, m_ref, wg0_ref, wg1_ref, wg2_ref,
                       bg0_ref, bg1_ref, bg2_ref, wa_ref, wp_ref, wm_ref, wo_ref,
                       g_ref, b_ref, o_ref, hb_ref):
    j = pl.program_id(1)

    @pl.when(j == 0)
    def _():
        hb_ref[...] = h_ref[...].astype(bf16)
        o_ref[...] = jnp.zeros_like(o_ref)

    hb = hb_ref[...]

    def gate(wg_ref, bg_ref):
        return jax.nn.sigmoid(jnp.dot(hb, wg_ref[...], preferred_element_type=f32) + bg_ref[...])

    y = gate(wg0_ref, bg0_ref) * jnp.dot(a_ref[...], wa_ref[...], preferred_element_type=f32)
    y = y + gate(wg1_ref, bg1_ref) * jnp.dot(p_ref[...], wp_ref[...], preferred_element_type=f32)
    y = y + gate(wg2_ref, bg2_ref) * jnp.dot(m_ref[...], wm_ref[...], preferred_element_type=f32)
    o_ref[...] += jnp.dot(y.astype(bf16), wo_ref[...], preferred_element_type=f32)

    @pl.when(j == pl.num_programs(1) - 1)
    def _():
        z = ALPHA * h_ref[...] + o_ref[...]
        o_ref[...] = _layer_norm(z, g_ref[...], b_ref[...])


def _combine_ln(h, a, p, m, w_gate, gate_col0, b_gate, w_a, w_p, w_m, w_out, g, b):
    nc = D_MODEL // CMB_TC
    assert gate_col0 % CMB_TC == 0
    g0 = gate_col0 // CMB_TC
    row = lambda i, j: (i, 0)
    col = lambda i, j: (0, j)
    return pl.pallas_call(
        _combine_ln_kernel,
        grid=(TOKENS // CMB_TM, nc),
        in_specs=[
            pl.BlockSpec((CMB_TM, D_MODEL), row),
            pl.BlockSpec((CMB_TM, ATT_WIDTH), row),
            pl.BlockSpec((CMB_TM, POOL_WIDTH), row),
            pl.BlockSpec((CMB_TM, MEM_WIDTH), row),
            pl.BlockSpec((D_MODEL, CMB_TC), lambda i, j: (0, g0 + j)),
            pl.BlockSpec((D_MODEL, CMB_TC), lambda i, j: (0, g0 + j + nc)),
            pl.BlockSpec((D_MODEL, CMB_TC), lambda i, j: (0, g0 + j + 2 * nc)),
            pl.BlockSpec((1, CMB_TC), lambda i, j: (0, j)),
            pl.BlockSpec((1, CMB_TC), lambda i, j: (0, j + nc)),
            pl.BlockSpec((1, CMB_TC), lambda i, j: (0, j + 2 * nc)),
            pl.BlockSpec((ATT_WIDTH, CMB_TC), col),
            pl.BlockSpec((POOL_WIDTH, CMB_TC), col),
            pl.BlockSpec((MEM_WIDTH, CMB_TC), col),
            pl.BlockSpec((CMB_TC, D_MODEL), lambda i, j: (j, 0)),
            pl.BlockSpec((1, D_MODEL), lambda i, j: (0, 0)),
            pl.BlockSpec((1, D_MODEL), lambda i, j: (0, 0)),
        ],
        out_specs=pl.BlockSpec((CMB_TM, D_MODEL), row),
        out_shape=jax.ShapeDtypeStruct((TOKENS, D_MODEL), f32),
        scratch_shapes=[pltpu.VMEM((CMB_TM, D_MODEL), bf16)],
        compiler_params=_params("parallel", "arbitrary"),
        name="combine_ln",
    )(h, a, p, m, w_gate, w_gate, w_gate, b_gate, b_gate, b_gate, w_a, w_p, w_m, w_out, g, b)


def kernel(x, mem, w_ffn1_up, w_ffn1_down, ln1_g, ln1_b, w_in, b_gate, w_mem_kv, w_pool,
           pool_scale, w_br_att, w_br_pool, w_br_mem, w_out, ln2_g, ln2_b, w_ffn2_up,
           w_ffn2_down, ln3_g, ln3_b):
    h = x.reshape(TOKENS, D_MODEL)
    memf = mem.reshape(BATCH * MEM_LEN, D_MODEL)
    for l in range(DEPTH):
        wi = w_in[l]
        w_head = wi[:, :TAIL_COL0].astype(bf16)
        w_tail = _tail_relayout(wi)
        w_kiwi = jnp.pad(wi[:, TAIL_COL0:TAIL_COL0 + TAIL_SHIFT],
                         ((0, 0), (0, LANES - TAIL_SHIFT))).astype(bf16)

        h = _ffn_ln(h, w_ffn1_up[l].astype(bf16), w_ffn1_down[l].astype(bf16),
                    ln1_g[l][None], ln1_b[l][None])

        za = _matmul(h, w_head, bf16, 1024, 512, "in_proj_a")
        zb = _in_proj_b(h, w_tail, w_kiwi)
        kv = _matmul(memf, w_mem_kv[l].astype(bf16), bf16, BATCH * MEM_LEN, 512, "mem_kv")

        c_ki = POOL_WIDTH + MEM_WIDTH
        ki = zb[:, c_ki:c_ki + IDX_DIM].astype(bf16)
        zk = jnp.zeros_like(ki)
        kk = jnp.stack([jnp.concatenate([ki, zk], axis=1), jnp.concatenate([zk, ki], axis=1)])
        wt_idx = zb[:, c_ki + IDX_DIM:c_ki + IDX_DIM + IDX_HEADS].T
        vt = za[:, 2 * ATT_WIDTH:3 * ATT_WIDTH].reshape(BATCH, DSA_NCH, DSA_TK, ATT_WIDTH)
        vt = vt.transpose(0, 1, 3, 2)

        a = _dsa(za, vt, kk, wt_idx).transpose(0, 2, 1).reshape(TOKENS, ATT_WIDTH)
        p = _pool(zb, w_pool[l].astype(bf16), pool_scale[l][None])
        m = _mem_attn(zb, POOL_WIDTH // MEM_WIDTH, kv)

        h = _combine_ln(h, a, p, m, w_tail, POOL_WIDTH + MEM_WIDTH, b_gate[l][None],
                        w_br_att[l].astype(bf16),
                        w_br_pool[l].astype(bf16), w_br_mem[l].astype(bf16),
                        w_out[l].astype(bf16), ln2_g[l][None], ln2_b[l][None])

        h = _ffn_ln(h, w_ffn2_up[l].astype(bf16), w_ffn2_down[l].astype(bf16),
                    ln3_g[l][None], ln3_b[l][None])
    return h.reshape(BATCH, SEQ, D_MODEL)
```

```python
import functools
import math

import jax
import jax.numpy as jnp
from jax import lax
from jax.experimental import pallas as pl
from jax.experimental.pallas import tpu as pltpu

f32 = jnp.float32
bf16 = jnp.bfloat16
i32 = jnp.int32

D_MODEL = 2048
BATCH = 2
SEQ = 4096
DEPTH = 1
MEM_LEN = 256
ATT_HEADS = 8
HEAD_DIM = 128
ATT_WIDTH = ATT_HEADS * HEAD_DIM
IDX_HEADS = 16
IDX_DIM = 64
TOPK = min(256, SEQ // 4)
POOL_WINDOWS = (2, 4, 8, 16)
N_POOL = len(POOL_WINDOWS)
POOL_GROUP = 128
POOL_WIDTH = N_POOL * POOL_GROUP
MEM_HEADS = 4
MEM_WIDTH = MEM_HEADS * HEAD_DIM
N_BRANCH = 3
D_FF = 5632
ALPHA = (2 * DEPTH) ** 0.25
LN_EPS = 1e-5
TOKENS = BATCH * SEQ

LANES = 128
BF16_ROWS = 16
VMEM_LIMIT = 60 * 1024 * 1024

INT_MIN = -(2 ** 31)
LOG2E = math.log2(math.e)


def _params(*sem):
    return pltpu.CompilerParams(dimension_semantics=sem, vmem_limit_bytes=VMEM_LIMIT)


def _layer_norm(y, g, b):
    mu = jnp.mean(y, axis=-1, keepdims=True)
    d = y - mu
    var = jnp.mean(d * d, axis=-1, keepdims=True)
    return d * lax.rsqrt(var + LN_EPS) * g + b


FFN_TM = 1024
FFN_TF = 256


def _ffn_ln_kernel(x_ref, wa_ref, wu_ref, wd_ref, g_ref, b_ref, o_ref, xb_ref):
    j = pl.program_id(1)

    @pl.when(j == 0)
    def _():
        xb_ref[...] = x_ref[...].astype(bf16)
        o_ref[...] = jnp.zeros_like(o_ref)

    xb = xb_ref[...]
    a = jnp.dot(xb, wa_ref[...], preferred_element_type=f32)
    u = jnp.dot(xb, wu_ref[...], preferred_element_type=f32)
    act = (a * jax.nn.sigmoid(a) * u).astype(bf16)
    o_ref[...] += jnp.dot(act, wd_ref[...], preferred_element_type=f32)

    @pl.when(j == pl.num_programs(1) - 1)
    def _():
        y = ALPHA * x_ref[...] + 0.5 * o_ref[...]
        o_ref[...] = _layer_norm(y, g_ref[...], b_ref[...])


def _ffn_ln(x, w_up, w_down, g, b):
    nf = D_FF // FFN_TF
    return pl.pallas_call(
        _ffn_ln_kernel,
        grid=(TOKENS // FFN_TM, nf),
        in_specs=[
            pl.BlockSpec((FFN_TM, D_MODEL), lambda i, j: (i, 0)),
            pl.BlockSpec((D_MODEL, FFN_TF), lambda i, j: (0, j)),
            pl.BlockSpec((D_MODEL, FFN_TF), lambda i, j: (0, j + nf)),
            pl.BlockSpec((FFN_TF, D_MODEL), lambda i, j: (j, 0)),
            pl.BlockSpec((1, D_MODEL), lambda i, j: (0, 0)),
            pl.BlockSpec((1, D_MODEL), lambda i, j: (0, 0)),
        ],
        out_specs=pl.BlockSpec((FFN_TM, D_MODEL), lambda i, j: (i, 0)),
        out_shape=jax.ShapeDtypeStruct((TOKENS, D_MODEL), f32),
        scratch_shapes=[pltpu.VMEM((FFN_TM, D_MODEL), bf16)],
        compiler_params=_params("parallel", "arbitrary"),
        name="ffn_ln",
    )(x, w_up, w_up, w_down, g, b)


_NT = (((1,), (1,)), ((), ()))


def _matmul_kernel(x_ref, w_ref, o_ref, xb_ref, *, w_is_transposed):
    @pl.when(pl.program_id(1) == 0)
    def _():
        xb_ref[...] = x_ref[...].astype(bf16)

    if w_is_transposed:
        y = lax.dot_general(xb_ref[...], w_ref[...], _NT, preferred_element_type=f32)
    else:
        y = jnp.dot(xb_ref[...], w_ref[...], preferred_element_type=f32)
    o_ref[...] = y.astype(o_ref.dtype)


def _matmul(x, w, out_dtype, tm, tn, name, w_is_transposed=False, n=None):
    m, k = x.shape
    if n is None:
        n = w.shape[0] if w_is_transposed else w.shape[1]
    assert n % tn == 0 and m % tm == 0
    if w_is_transposed:
        w_spec = pl.BlockSpec((tn, k), lambda i, j: (j, 0))
    else:
        w_spec = pl.BlockSpec((k, tn), lambda i, j: (0, j))
    return pl.pallas_call(
        functools.partial(_matmul_kernel, w_is_transposed=w_is_transposed),
        grid=(m // tm, n // tn),
        in_specs=[pl.BlockSpec((tm, k), lambda i, j: (i, 0)), w_spec],
        out_specs=pl.BlockSpec((tm, tn), lambda i, j: (i, j)),
        out_shape=jax.ShapeDtypeStruct((m, n), out_dtype),
        scratch_shapes=[pltpu.VMEM((tm, k), bf16)],
        compiler_params=_params("parallel", "arbitrary"),
        name=name,
    )(x, w)


PB_TM = 512
PB_N = POOL_WIDTH + MEM_WIDTH + LANES


def _in_proj_b_kernel(x_ref, wum_ref, wk_ref, o_ref):
    xb = x_ref[...].astype(bf16)
    n_um = POOL_WIDTH + MEM_WIDTH
    o_ref[:, :n_um] = lax.dot_general(xb, wum_ref[...], _NT, preferred_element_type=f32)
    o_ref[:, n_um:] = lax.dot_general(xb, wk_ref[...], _NT, preferred_element_type=f32)


def _in_proj_b(x, wt, row_um, row_kiwi):
    n_um = POOL_WIDTH + MEM_WIDTH
    return pl.pallas_call(
        _in_proj_b_kernel,
        grid=(TOKENS // PB_TM,),
        in_specs=[
            pl.BlockSpec((PB_TM, D_MODEL), lambda i: (i, 0)),
            pl.BlockSpec((pl.Element(n_um), pl.Element(D_MODEL)), lambda i: (row_um, 0)),
            pl.BlockSpec((pl.Element(LANES), pl.Element(D_MODEL)), lambda i: (row_kiwi, 0)),
        ],
        out_specs=pl.BlockSpec((PB_TM, PB_N), lambda i: (i, 0)),
        out_shape=jax.ShapeDtypeStruct((TOKENS, PB_N), f32),
        compiler_params=_params("parallel"),
        name="in_proj_b",
    )(x, wt, wt)


DSA_TQ = 256
DSA_TK = 512
DSA_NCH = SEQ // DSA_TK
DSA_ONES = 16
KEY_NEG_FLT_MAX = -(2 ** 31) + (1 << 23)


def _key_to_f32(key):
    return lax.bitcast_convert_type(key ^ ((key >> 31) & jnp.int32(0x7FFFFFFF)), f32)


def _dsa_kernel(q_ref, k_ref, vt_ref, qi_ref, kk_ref, wt_ref, o_ref, sc_scr, acc_scr):
    i = pl.program_id(1)
    q0 = i * DSA_TQ
    nk = (q0 + DSA_TQ - 1) // DSA_TK + 1

    qpos = q0 + lax.broadcasted_iota(i32, (DSA_TK, DSA_TQ), 1)
    kofs = lax.broadcasted_iota(i32, (DSA_TK, DSA_TQ), 0)
    contract_last = (((1,), (1,)), ((), ()))

    def rows8(x):
        return x.reshape(DSA_TK // 8, 8, DSA_TQ)

    wt = wt_ref[...] * (IDX_DIM ** -0.5 * IDX_HEADS ** -0.5)

    def score_chunk(c, carry):
        r0 = pl.multiple_of(c * DSA_TK, DSA_TK)
        kk0 = kk_ref[0, pl.ds(r0, DSA_TK), :]
        kk1 = kk_ref[1, pl.ds(r0, DSA_TK), :]
        acc = jnp.zeros((DSA_TK, DSA_TQ), f32)
        for p in range(IDX_HEADS // 2):
            slab = qi_ref[:, p * LANES:(p + 1) * LANES]
            l0 = lax.dot_general(kk0, slab, contract_last, preferred_element_type=f32)
            l1 = lax.dot_general(kk1, slab, contract_last, preferred_element_type=f32)
            acc = acc + (jnp.maximum(l0, 0.0) * wt[2 * p:2 * p + 1]
                         + jnp.maximum(l1, 0.0) * wt[2 * p + 1:2 * p + 2])
        sc_scr[c] = jnp.where(kofs + r0 <= qpos, acc, -jnp.inf)
        return carry

    lax.fori_loop(0, nk, score_chunk, 0)

    def count(pred):
        def body(c, acc):
            return acc + jnp.sum(rows8(pred(c).astype(i32)), axis=0)
        acc = lax.fori_loop(0, nk, body, jnp.zeros((8, DSA_TQ), i32))
        return jnp.sum(acc, axis=0, keepdims=True)

    def count_ge(cand):
        return count(lambda c: sc_scr[c] >= cand)

    key = jnp.where(count_ge(jnp.zeros((1, DSA_TQ), f32)) >= TOPK, jnp.int32(0), jnp.int32(INT_MIN))

    def bit_step(b, key):
        cand = key | jnp.left_shift(jnp.int32(1), 30 - b)
        return jnp.where(count_ge(_key_to_f32(cand)) >= TOPK, cand, key)

    key = lax.fori_loop(0, 31, bit_step, key)
    thr = _key_to_f32(jnp.maximum(key, jnp.int32(KEY_NEG_FLT_MAX)))

    n_ge = count_ge(thr)

    @pl.when(jnp.max(n_ge) > TOPK)
    def _():
        need = TOPK - count(lambda c: sc_scr[c] > thr)

        def count_eq_below(pos):
            return count(lambda c: (sc_scr[c] == thr) & (kofs + c * DSA_TK < pos))

        def pos_step(b, r):
            cand = r | jnp.left_shift(jnp.int32(1), (SEQ.bit_length() - 2) - b)
            return jnp.where(count_eq_below(cand) < need, cand, r)
        r = lax.fori_loop(0, SEQ.bit_length() - 1, pos_step, jnp.zeros((1, DSA_TQ), i32))
        r = jnp.where(n_ge > TOPK, r, jnp.int32(SEQ))

        def drop(c, carry):
            sc = sc_scr[c]
            sc_scr[c] = jnp.where((sc == thr) & (kofs + c * DSA_TK > r), -jnp.inf, sc)
            return carry
        lax.fori_loop(0, nk, drop, 0)

    acc_scr[...] = jnp.zeros(acc_scr.shape, f32)
    c1 = HEAD_DIM ** -0.5 * LOG2E
    ones = jnp.ones((DSA_ONES, DSA_TK), bf16)

    def attn_chunk(c, ms):
        r0 = pl.multiple_of(c * DSA_TK, DSA_TK)
        madd = jnp.where(sc_scr[c] >= thr, 0.0, -jnp.inf)
        kd = (lax.broadcasted_iota(i32, (DSA_TK, LANES), 0) + (r0 - q0)).astype(f32)
        def qk(h):
            hd = slice(h * HEAD_DIM, (h + 1) * HEAD_DIM)
            return lax.dot_general(k_ref[pl.ds(r0, DSA_TK), hd], q_ref[:, hd], contract_last,
                                   preferred_element_type=f32)

        new_ms = []
        s_next = qk(0)
        for h in range(ATT_HEADS):
            slope2 = 2.0 ** (-8.0 * (h + 1) / ATT_HEADS) * LOG2E
            hd = slice(h * HEAD_DIM, (h + 1) * HEAD_DIM)
            s = s_next
            if h + 1 < ATT_HEADS:
                s_next = qk(h + 1)
            bias = kd * slope2
            t = s * c1 + jnp.concatenate([bias] * (DSA_TQ // LANES), axis=1) + madd
            m_old = ms[h]
            m_new = jnp.maximum(m_old, jnp.max(t, axis=0, keepdims=True))
            alpha = jnp.exp2(m_old - m_new)
            p = jnp.exp2(t - m_new).astype(bf16)
            vt1 = jnp.concatenate([vt_ref[c, hd, :], ones], axis=0)
            acc_scr[h] = alpha * acc_scr[h] + jnp.dot(vt1, p, preferred_element_type=f32)
            new_ms.append(m_new)
        return tuple(new_ms)

    m0 = jnp.full((1, DSA_TQ), -1e30, f32)
    lax.fori_loop(0, nk, attn_chunk, (m0,) * ATT_HEADS)

    for h in range(ATT_HEADS):
        o_ref[h * HEAD_DIM:(h + 1) * HEAD_DIM, :] = (
            acc_scr[h, :HEAD_DIM, :] / acc_scr[h, HEAD_DIM:HEAD_DIM + 1, :]).astype(o_ref.dtype)


def _dsa(za, vt, kk, wt_idx):
    nq = SEQ // DSA_TQ
    once = pl.Buffered(1)
    return pl.pallas_call(
        _dsa_kernel,
        grid=(BATCH, nq),
        in_specs=[
            pl.BlockSpec((DSA_TQ, ATT_WIDTH), lambda b, i: (b * nq + i, 0)),
            pl.BlockSpec((SEQ, ATT_WIDTH), lambda b, i: (b, 1), pipeline_mode=once),
            pl.BlockSpec((None, DSA_NCH, ATT_WIDTH, DSA_TK), lambda b, i: (b, 0, 0, 0),
                         pipeline_mode=once),
            pl.BlockSpec((DSA_TQ, IDX_HEADS * IDX_DIM), lambda b, i: (b * nq + i, 3)),
            pl.BlockSpec((2, SEQ, LANES), lambda b, i: (0, b, 0), pipeline_mode=once),
            pl.BlockSpec((IDX_HEADS, DSA_TQ), lambda b, i: (0, b * nq + i)),
        ],
        out_specs=pl.BlockSpec((None, ATT_WIDTH, DSA_TQ), lambda b, i: (b, 0, i)),
        out_shape=jax.ShapeDtypeStruct((BATCH, ATT_WIDTH, SEQ), bf16),
        scratch_shapes=[
            pltpu.VMEM((DSA_NCH, DSA_TK, DSA_TQ), f32),
            pltpu.VMEM((ATT_HEADS, HEAD_DIM + DSA_ONES, DSA_TQ), f32),
        ],
        compiler_params=_params("arbitrary", "arbitrary"),
        name="dsa",
    )(za, za, vt, za, kk, wt_idx)


def _pool_kernel(u_ref, wp_ref, ps_ref, o_ref):
    row = lax.broadcasted_iota(i32, (SEQ, POOL_GROUP), 0)
    for g, win in enumerate(POOL_WINDOWS):
        x = u_ref[:, g * POOL_GROUP:(g + 1) * POOL_GROUP]
        s = x
        k = 1
        while k < win:
            s = s + jnp.where(row >= k, pltpu.roll(s, k, axis=0), 0.0)
            k *= 2
        cnt = jnp.minimum(row + 1, win).astype(f32)
        pooled = (s / cnt - x).astype(bf16)
        mixed = jnp.dot(pooled, wp_ref[g], preferred_element_type=f32)
        o_ref[:, g * POOL_GROUP:(g + 1) * POOL_GROUP] = (
            mixed * ps_ref[:, g * POOL_GROUP:(g + 1) * POOL_GROUP]).astype(o_ref.dtype)


def _pool(zb, w_pool, pool_scale):
    return pl.pallas_call(
        _pool_kernel,
        grid=(BATCH,),
        in_specs=[
            pl.BlockSpec((SEQ, POOL_WIDTH), lambda b: (b, 0)),
            pl.BlockSpec((N_POOL, POOL_GROUP, POOL_GROUP), lambda b: (0, 0, 0)),
            pl.BlockSpec((1, POOL_WIDTH), lambda b: (0, 0)),
        ],
        out_specs=pl.BlockSpec((SEQ, POOL_WIDTH), lambda b: (b, 0)),
        out_shape=jax.ShapeDtypeStruct((TOKENS, POOL_WIDTH), bf16),
        compiler_params=_params("parallel"),
        name="pool",
    )(zb, w_pool, pool_scale)


MEM_TQ = 512


def _mem_attn_kernel(q_ref, k_ref, v_ref, o_ref):
    for h in range(MEM_HEADS):
        sl = slice(h * HEAD_DIM, (h + 1) * HEAD_DIM)
        s = lax.dot_general(q_ref[:, sl].astype(bf16), k_ref[:, sl], (((1,), (1,)), ((), ())),
                            preferred_element_type=f32) * (HEAD_DIM ** -0.5)
        p = jnp.exp(s - jnp.max(s, axis=1, keepdims=True))
        l = jnp.sum(p, axis=1, keepdims=True)
        o = jnp.dot(p.astype(bf16), v_ref[:, sl], preferred_element_type=f32)
        o_ref[:, sl] = (o / l).astype(o_ref.dtype)


def _mem_attn(zb, qcol, kv):
    nq = SEQ // MEM_TQ
    return pl.pallas_call(
        _mem_attn_kernel,
        grid=(BATCH, nq),
        in_specs=[
            pl.BlockSpec((MEM_TQ, MEM_WIDTH), lambda b, i: (b * nq + i, qcol)),
            pl.BlockSpec((MEM_LEN, MEM_WIDTH), lambda b, i: (b, 0)),
            pl.BlockSpec((MEM_LEN, MEM_WIDTH), lambda b, i: (b, 1)),
        ],
        out_specs=pl.BlockSpec((MEM_TQ, MEM_WIDTH), lambda b, i: (b * nq + i, 0)),
        out_shape=jax.ShapeDtypeStruct((TOKENS, MEM_WIDTH), bf16),
        compiler_params=_params("parallel", "parallel"),
        name="mem_attn",
    )(zb, kv, kv)


CMB_TM = 512
CMB_TC = 512


def _combine_ln_kernel(h_ref, a_ref, p_ref, m_ref, wg0_ref, wg1_ref, wg2_ref,
                       bg0_ref, bg1_ref, bg2_ref, wa_ref, wp_ref, wm_ref, wo_ref,
                       g_ref, b_ref, o_ref, hb_ref):
    j = pl.program_id(1)

    @pl.when(j == 0)
    def _():
        hb_ref[...] = h_ref[...].astype(bf16)
        o_ref[...] = jnp.zeros_like(o_ref)

    hb = hb_ref[...]

    def gate(wg_ref, bg_ref):
        logits = lax.dot_general(hb, wg_ref[...], _NT, preferred_element_type=f32)
        return jax.nn.sigmoid(logits + bg_ref[...])

    y = gate(wg0_ref, bg0_ref) * jnp.dot(a_ref[...], wa_ref[...], preferred_element_type=f32)
    y = y + gate(wg1_ref, bg1_ref) * jnp.dot(p_ref[...], wp_ref[...], preferred_element_type=f32)
    y = y + gate(wg2_ref, bg2_ref) * jnp.dot(m_ref[...], wm_ref[...], preferred_element_type=f32)
    o_ref[...] += jnp.dot(y.astype(bf16), wo_ref[...], preferred_element_type=f32)

    @pl.when(j == pl.num_programs(1) - 1)
    def _():
        z = ALPHA * h_ref[...] + o_ref[...]
        o_ref[...] = _layer_norm(z, g_ref[...], b_ref[...])


def _combine_ln(h, a, p, m, wt_gate, row_gate, b_gate, w_a, w_p, w_m, w_out, g, b):
    nc = D_MODEL // CMB_TC
    gate_rows = pl.Element(CMB_TC)
    all_cols = pl.Element(D_MODEL)
    assert row_gate % BF16_ROWS == 0

    def gate_row(blk):
        return pl.multiple_of(row_gate + blk * CMB_TC, BF16_ROWS)

    row = lambda i, j: (i, 0)
    col = lambda i, j: (0, j)
    return pl.pallas_call(
        _combine_ln_kernel,
        grid=(TOKENS // CMB_TM, nc),
        in_specs=[
            pl.BlockSpec((CMB_TM, D_MODEL), row),
            pl.BlockSpec((CMB_TM, ATT_WIDTH), row),
            pl.BlockSpec((CMB_TM, POOL_WIDTH), row),
            pl.BlockSpec((CMB_TM, MEM_WIDTH), row),
            pl.BlockSpec((gate_rows, all_cols), lambda i, j: (gate_row(j), 0)),
            pl.BlockSpec((gate_rows, all_cols), lambda i, j: (gate_row(j + nc), 0)),
            pl.BlockSpec((gate_rows, all_cols), lambda i, j: (gate_row(j + 2 * nc), 0)),
            pl.BlockSpec((1, CMB_TC), lambda i, j: (0, j)),
            pl.BlockSpec((1, CMB_TC), lambda i, j: (0, j + nc)),
            pl.BlockSpec((1, CMB_TC), lambda i, j: (0, j + 2 * nc)),
            pl.BlockSpec((ATT_WIDTH, CMB_TC), col),
            pl.BlockSpec((POOL_WIDTH, CMB_TC), col),
            pl.BlockSpec((MEM_WIDTH, CMB_TC), col),
            pl.BlockSpec((CMB_TC, D_MODEL), lambda i, j: (j, 0)),
            pl.BlockSpec((1, D_MODEL), lambda i, j: (0, 0)),
            pl.BlockSpec((1, D_MODEL), lambda i, j: (0, 0)),
        ],
        out_specs=pl.BlockSpec((CMB_TM, D_MODEL), row),
        out_shape=jax.ShapeDtypeStruct((TOKENS, D_MODEL), f32),
        scratch_shapes=[pltpu.VMEM((CMB_TM, D_MODEL), bf16)],
        compiler_params=_params("parallel", "arbitrary"),
        name="combine_ln",
    )(h, a, p, m, wt_gate, wt_gate, wt_gate, b_gate, b_gate, b_gate, w_a, w_p, w_m, w_out, g, b)


def kernel(x, mem, w_ffn1_up, w_ffn1_down, ln1_g, ln1_b, w_in, b_gate, w_mem_kv, w_pool,
           pool_scale, w_br_att, w_br_pool, w_br_mem, w_out, ln2_g, ln2_b, w_ffn2_up,
           w_ffn2_down, ln3_g, ln3_b):
    h = x.reshape(TOKENS, D_MODEL)
    memf = mem.reshape(BATCH * MEM_LEN, D_MODEL)
    for l in range(DEPTH):
        wt = w_in[l].T.astype(bf16)
        c_qi = 3 * ATT_WIDTH + IDX_HEADS * IDX_DIM
        c_wi = c_qi + IDX_DIM + IDX_HEADS
        c_qm = c_wi + POOL_WIDTH + MEM_WIDTH

        h = _ffn_ln(h, w_ffn1_up[l].astype(bf16), w_ffn1_down[l].astype(bf16),
                    ln1_g[l][None], ln1_b[l][None])

        za = _matmul(h, wt, bf16, 1024, 512, "in_proj_a", w_is_transposed=True, n=c_qi)
        zb = _in_proj_b(h, wt, c_wi, c_qi)
        kv = _matmul(memf, w_mem_kv[l].astype(bf16), bf16, BATCH * MEM_LEN, 512, "mem_kv")

        c_ki = POOL_WIDTH + MEM_WIDTH
        ki = zb[:, c_ki:c_ki + IDX_DIM].astype(bf16)
        zk = jnp.zeros_like(ki)
        kk = jnp.stack([jnp.concatenate([ki, zk], axis=1), jnp.concatenate([zk, ki], axis=1)])
        wt_idx = zb[:, c_ki + IDX_DIM:c_ki + IDX_DIM + IDX_HEADS].T
        vt = za[:, 2 * ATT_WIDTH:3 * ATT_WIDTH].reshape(BATCH, DSA_NCH, DSA_TK, ATT_WIDTH)
        vt = vt.transpose(0, 1, 3, 2)

        a = _dsa(za, vt, kk, wt_idx).transpose(0, 2, 1).reshape(TOKENS, ATT_WIDTH)
        p = _pool(zb, w_pool[l].astype(bf16), pool_scale[l][None])
        m = _mem_attn(zb, POOL_WIDTH // MEM_WIDTH, kv)

        h = _combine_ln(h, a, p, m, wt, c_qm, b_gate[l][None], w_br_att[l].astype(bf16),
                        w_br_pool[l].astype(bf16), w_br_mem[l].astype(bf16),
                        w_out[l].astype(bf16), ln2_g[l][None], ln2_b[l][None])

        h = _ffn_ln(h, w_ffn2_up[l].astype(bf16), w_ffn2_down[l].astype(bf16),
                    ln3_g[l][None], ln3_b[l][None])
    return h.reshape(BATCH, SEQ, D_MODEL)
```

```python
import functools
import math

import jax
import jax.numpy as jnp
import numpy as np
from jax import lax
from jax.experimental import pallas as pl
from jax.experimental.pallas import tpu as pltpu

f32 = jnp.float32
bf16 = jnp.bfloat16
i32 = jnp.int32

D_MODEL = 2048
BATCH = 2
SEQ = 4096
DEPTH = 1
MEM_LEN = 256
ATT_HEADS = 8
HEAD_DIM = 128
ATT_WIDTH = ATT_HEADS * HEAD_DIM
IDX_HEADS = 16
IDX_DIM = 64
TOPK = min(256, SEQ // 4)
POOL_WINDOWS = (2, 4, 8, 16)
N_POOL = len(POOL_WINDOWS)
POOL_GROUP = 128
POOL_WIDTH = N_POOL * POOL_GROUP
MEM_HEADS = 4
MEM_WIDTH = MEM_HEADS * HEAD_DIM
N_BRANCH = 3
D_FF = 5632
ALPHA = (2 * DEPTH) ** 0.25
LN_EPS = 1e-5
TOKENS = BATCH * SEQ

LANES = 128
BF16_ROWS = 16
VMEM_LIMIT = 60 * 1024 * 1024

INT_MIN = -(2 ** 31)
LOG2E = math.log2(math.e)


def _params(*sem):
    return pltpu.CompilerParams(dimension_semantics=sem, vmem_limit_bytes=VMEM_LIMIT)


def _layer_norm(y, g, b):
    mu = jnp.mean(y, axis=-1, keepdims=True)
    d = y - mu
    var = jnp.mean(d * d, axis=-1, keepdims=True)
    return d * lax.rsqrt(var + LN_EPS) * g + b


FFN_TM = 512
FFN_TF = 512


def _ffn_ln_kernel(x_ref, wa_ref, wu_ref, wd_ref, g_ref, b_ref, *rest, n_cast):
    cast_in = rest[:n_cast]
    o_ref = rest[n_cast]
    cast_out = rest[n_cast + 1:2 * n_cast + 1]
    xb_ref = rest[2 * n_cast + 1]
    j = pl.program_id(1)

    for src, dst in zip(cast_in, cast_out):
        dst[...] = src[...].astype(dst.dtype)

    @pl.when(j == 0)
    def _():
        xb_ref[...] = x_ref[...].astype(bf16)
        o_ref[...] = jnp.zeros_like(o_ref)

    xb = xb_ref[...]
    a = jnp.dot(xb, wa_ref[...], preferred_element_type=f32)
    u = jnp.dot(xb, wu_ref[...], preferred_element_type=f32)
    act = (a * jax.nn.sigmoid(a) * u).astype(bf16)
    o_ref[...] += jnp.dot(act, wd_ref[...], preferred_element_type=f32)

    @pl.when(j == pl.num_programs(1) - 1)
    def _():
        y = ALPHA * x_ref[...] + 0.5 * o_ref[...]
        o_ref[...] = _layer_norm(y, g_ref[...], b_ref[...])


def _cast_block_spec(shape, rows_on_j, ni, nj):
    r, c = shape
    n_r, n_c = (nj, ni) if rows_on_j else (ni, nj)
    br = -(-r // n_r)
    br = -(-br // BF16_ROWS) * BF16_ROWS
    assert c % n_c == 0 and (c // n_c) % LANES == 0
    index_map = (lambda i, j: (j, i)) if rows_on_j else (lambda i, j: (i, j))
    return pl.BlockSpec((br, c // n_c), index_map)


def _ffn_ln(x, w_up, w_down, g, b, casts=()):
    ni = TOKENS // FFN_TM
    nf = D_FF // FFN_TF
    cast_specs = [_cast_block_spec(w.shape, rows_on_j, ni, nf) for w, rows_on_j in casts]
    outs = pl.pallas_call(
        functools.partial(_ffn_ln_kernel, n_cast=len(casts)),
        grid=(ni, nf),
        in_specs=[
            pl.BlockSpec((FFN_TM, D_MODEL), lambda i, j: (i, 0)),
            pl.BlockSpec((D_MODEL, FFN_TF), lambda i, j: (0, j)),
            pl.BlockSpec((D_MODEL, FFN_TF), lambda i, j: (0, j + nf)),
            pl.BlockSpec((FFN_TF, D_MODEL), lambda i, j: (j, 0)),
            pl.BlockSpec((1, D_MODEL), lambda i, j: (0, 0)),
            pl.BlockSpec((1, D_MODEL), lambda i, j: (0, 0)),
        ] + cast_specs,
        out_specs=[pl.BlockSpec((FFN_TM, D_MODEL), lambda i, j: (i, 0))] + cast_specs,
        out_shape=[jax.ShapeDtypeStruct((TOKENS, D_MODEL), f32)]
        + [jax.ShapeDtypeStruct(w.shape, bf16) for w, _ in casts],
        scratch_shapes=[pltpu.VMEM((FFN_TM, D_MODEL), bf16)],
        compiler_params=_params("arbitrary", "arbitrary"),
        name="ffn_ln",
    )(x, w_up, w_up, w_down, g, b, *[w for w, _ in casts])
    return outs[0], outs[1:]


_NT = (((1,), (1,)), ((), ()))


def _matmul_kernel(x_ref, w_ref, o_ref, xb_ref, *, w_is_transposed):
    @pl.when(pl.program_id(1) == 0)
    def _():
        xb_ref[...] = x_ref[...].astype(bf16)

    if w_is_transposed:
        y = lax.dot_general(xb_ref[...], w_ref[...], _NT, preferred_element_type=f32)
    else:
        y = jnp.dot(xb_ref[...], w_ref[...], preferred_element_type=f32)
    o_ref[...] = y.astype(o_ref.dtype)


def _matmul(x, w, out_dtype, tm, tn, name, w_is_transposed=False, n=None):
    m, k = x.shape
    if n is None:
        n = w.shape[0] if w_is_transposed else w.shape[1]
    assert n % tn == 0 and m % tm == 0
    if w_is_transposed:
        w_spec = pl.BlockSpec((tn, k), lambda i, j: (j, 0))
    else:
        w_spec = pl.BlockSpec((k, tn), lambda i, j: (0, j))
    return pl.pallas_call(
        functools.partial(_matmul_kernel, w_is_transposed=w_is_transposed),
        grid=(m // tm, n // tn),
        in_specs=[pl.BlockSpec((tm, k), lambda i, j: (i, 0)), w_spec],
        out_specs=pl.BlockSpec((tm, tn), lambda i, j: (i, j)),
        out_shape=jax.ShapeDtypeStruct((m, n), out_dtype),
        scratch_shapes=[pltpu.VMEM((tm, k), bf16)],
        compiler_params=_params("parallel", "arbitrary"),
        name=name,
    )(x, w)


PB_TM = 512
PB_N = POOL_WIDTH + MEM_WIDTH + LANES


def _in_proj_b_kernel(x_ref, wum_ref, wk_ref, o_ref):
    xb = x_ref[...].astype(bf16)
    n_um = POOL_WIDTH + MEM_WIDTH
    o_ref[:, :n_um] = lax.dot_general(xb, wum_ref[...], _NT, preferred_element_type=f32)
    o_ref[:, n_um:] = lax.dot_general(xb, wk_ref[...], _NT, preferred_element_type=f32)


def _in_proj_b(x, wt, row_um, row_kiwi):
    n_um = POOL_WIDTH + MEM_WIDTH
    return pl.pallas_call(
        _in_proj_b_kernel,
        grid=(TOKENS // PB_TM,),
        in_specs=[
            pl.BlockSpec((PB_TM, D_MODEL), lambda i: (i, 0)),
            pl.BlockSpec((pl.Element(n_um), pl.Element(D_MODEL)), lambda i: (row_um, 0)),
            pl.BlockSpec((pl.Element(LANES), pl.Element(D_MODEL)), lambda i: (row_kiwi, 0)),
        ],
        out_specs=pl.BlockSpec((PB_TM, PB_N), lambda i: (i, 0)),
        out_shape=jax.ShapeDtypeStruct((TOKENS, PB_N), f32),
        compiler_params=_params("parallel"),
        name="in_proj_b",
    )(x, wt, wt)


DSA_TQ = 512
DSA_TK = 512
DSA_NCH = SEQ // DSA_TK
DSA_ONES = 16
KEY_NEG_FLT_MAX = -(2 ** 31) + (1 << 23)


def _key_to_f32(key):
    return lax.bitcast_convert_type(key ^ ((key >> 31) & jnp.int32(0x7FFFFFFF)), f32)


def _dsa_kernel(q_ref, k_ref, vt_ref, qi_ref, kk_ref, wt_ref, kf_ref, qf_ref, o_ref,
                sc_scr, acc_scr, qa_scr):
    i = pl.program_id(1)
    q0 = i * DSA_TQ
    nk = (q0 + DSA_TQ - 1) // DSA_TK + 1

    qpos = q0 + lax.broadcasted_iota(i32, (DSA_TK, DSA_TQ), 1)
    kofs = lax.broadcasted_iota(i32, (DSA_TK, DSA_TQ), 0)
    contract_last = (((1,), (1,)), ((), ()))

    def rows8(x):
        return x.reshape(DSA_TK // 8, 8, DSA_TQ)

    wt = wt_ref[...] * (IDX_DIM ** -0.5 * IDX_HEADS ** -0.5)

    def score_chunk(c, carry):
        r0 = pl.multiple_of(c * DSA_TK, DSA_TK)
        kk0 = kk_ref[0, pl.ds(r0, DSA_TK), :]
        kk1 = kk_ref[1, pl.ds(r0, DSA_TK), :]
        acc = jnp.zeros((DSA_TK, DSA_TQ), f32)
        for p in range(IDX_HEADS // 2):
            slab = qi_ref[:, p * LANES:(p + 1) * LANES]
            l0 = lax.dot_general(kk0, slab, contract_last, preferred_element_type=f32)
            l1 = lax.dot_general(kk1, slab, contract_last, preferred_element_type=f32)
            acc = acc + (jnp.maximum(l0, 0.0) * wt[2 * p:2 * p + 1]
                         + jnp.maximum(l1, 0.0) * wt[2 * p + 1:2 * p + 2])
        sc_scr[c] = jnp.where(kofs + r0 <= qpos, acc, -jnp.inf)
        return carry

    lax.fori_loop(0, nk, score_chunk, 0)

    def count(pred):
        n_acc = 8

        def body(c, accs):
            m = rows8(pred(c))
            accs = list(accs)
            for r in range(DSA_TK // 8):
                accs[r % n_acc] = jnp.where(m[r], accs[r % n_acc] + 1, accs[r % n_acc])
            return tuple(accs)
        accs = lax.fori_loop(0, nk, body, (jnp.zeros((8, DSA_TQ), i32),) * n_acc)
        acc = functools.reduce(lambda a, b: a + b, accs)
        return jnp.sum(acc, axis=0, keepdims=True)

    def count_ge(cand):
        return count(lambda c: sc_scr[c] >= cand)

    key = jnp.where(count_ge(jnp.zeros((1, DSA_TQ), f32)) >= TOPK, jnp.int32(0), jnp.int32(INT_MIN))

    def bit_step(b, key):
        cand = key | jnp.left_shift(jnp.int32(1), 30 - b)
        return jnp.where(count_ge(_key_to_f32(cand)) >= TOPK, cand, key)

    key = lax.fori_loop(0, 31, bit_step, key)
    thr = _key_to_f32(jnp.maximum(key, jnp.int32(KEY_NEG_FLT_MAX)))

    n_ge = count_ge(thr)

    @pl.when(jnp.max(n_ge) > TOPK)
    def _():
        need = TOPK - count(lambda c: sc_scr[c] > thr)

        def count_eq_below(pos):
            return count(lambda c: (sc_scr[c] == thr) & (kofs + c * DSA_TK < pos))

        def pos_step(b, r):
            cand = r | jnp.left_shift(jnp.int32(1), (SEQ.bit_length() - 2) - b)
            return jnp.where(count_eq_below(cand) < need, cand, r)
        r = lax.fori_loop(0, SEQ.bit_length() - 1, pos_step, jnp.zeros((1, DSA_TQ), i32))
        r = jnp.where(n_ge > TOPK, r, jnp.int32(SEQ))

        def drop(c, carry):
            sc = sc_scr[c]
            sc_scr[c] = jnp.where((sc == thr) & (kofs + c * DSA_TK > r), -jnp.inf, sc)
            return carry
        lax.fori_loop(0, nk, drop, 0)

    acc_scr[...] = jnp.zeros(acc_scr.shape, f32)
    c1 = HEAD_DIM ** -0.5 * LOG2E
    ones = jnp.ones((DSA_ONES, DSA_TK), bf16)

    for h in range(ATT_HEADS):
        qa_scr[h, :, :HEAD_DIM] = q_ref[:, h * HEAD_DIM:(h + 1) * HEAD_DIM]
        qa_scr[h, :, HEAD_DIM:] = jnp.broadcast_to(qf_ref[h, 0:1, :], (DSA_TQ, LANES))

    def attn_chunk(c, ms):
        r0 = pl.multiple_of(c * DSA_TK, DSA_TK)
        madd = jnp.where(sc_scr[c] >= thr, 0.0, -jnp.inf)
        kf = kf_ref[pl.ds(r0, DSA_TK), :]

        def qk(h):
            hd = slice(h * HEAD_DIM, (h + 1) * HEAD_DIM)
            k_aug = jnp.concatenate([k_ref[pl.ds(r0, DSA_TK), hd], kf], axis=1)
            return lax.dot_general(k_aug, qa_scr[h], contract_last, preferred_element_type=f32)

        new_ms = []
        s_next = qk(0)
        for h in range(ATT_HEADS):
            hd = slice(h * HEAD_DIM, (h + 1) * HEAD_DIM)
            s = s_next
            if h + 1 < ATT_HEADS:
                s_next = qk(h + 1)
            t = s * c1 + madd
            m_old = ms[h]
            m_new = jnp.maximum(m_old, jnp.max(t, axis=0, keepdims=True))
            alpha = jnp.exp2(m_old - m_new)
            p = jnp.exp2(t - m_new).astype(bf16)
            vt1 = jnp.concatenate([vt_ref[c, hd, :], ones], axis=0)
            acc_scr[h] = alpha * acc_scr[h] + jnp.dot(vt1, p, preferred_element_type=f32)
            new_ms.append(m_new)
        return tuple(new_ms)

    m0 = jnp.full((1, DSA_TQ), -1e30, f32)
    lax.fori_loop(0, nk, attn_chunk, (m0,) * ATT_HEADS)

    for h in range(ATT_HEADS):
        o_ref[h * HEAD_DIM:(h + 1) * HEAD_DIM, :] = (
            acc_scr[h, :HEAD_DIM, :] / acc_scr[h, HEAD_DIM:HEAD_DIM + 1, :]).astype(o_ref.dtype)


def _alibi_features():
    assert ATT_HEADS == 8 and HEAD_DIM == 128 and SEQ <= 64 * 64
    pieces, rest = [], math.sqrt(2.0)
    for _ in range(7):
        p = float(np.asarray(rest, np.float32).astype(jnp.bfloat16).astype(np.float64))
        pieces.append(p)
        rest -= p
    qf = np.zeros((ATT_HEADS, BF16_ROWS, LANES), np.float32)
    for h in range(ATT_HEADS):
        for i, p in enumerate(pieces):
            qf[h, :, 2 * i] = qf[h, :, 2 * i + 1] = p * 2.0 ** (2 - h)
    pos = np.arange(SEQ)
    kf = np.zeros((SEQ, LANES), np.float32)
    for i in range(len(pieces)):
        kf[:, 2 * i] = pos - pos % 64
        kf[:, 2 * i + 1] = pos % 64
    return jnp.asarray(kf, bf16), jnp.asarray(qf, bf16)


def _dsa(za, vt, kk, wt_idx):
    nq = SEQ // DSA_TQ
    kf, qf = _alibi_features()
    once = pl.Buffered(1)
    return pl.pallas_call(
        _dsa_kernel,
        grid=(BATCH, nq),
        in_specs=[
            pl.BlockSpec((DSA_TQ, ATT_WIDTH), lambda b, i: (b * nq + i, 0)),
            pl.BlockSpec((SEQ, ATT_WIDTH), lambda b, i: (b, 1), pipeline_mode=once),
            pl.BlockSpec((None, DSA_NCH, ATT_WIDTH, DSA_TK), lambda b, i: (b, 0, 0, 0),
                         pipeline_mode=once),
            pl.BlockSpec((DSA_TQ, IDX_HEADS * IDX_DIM), lambda b, i: (b * nq + i, 3)),
            pl.BlockSpec((2, SEQ, LANES), lambda b, i: (0, b, 0), pipeline_mode=once),
            pl.BlockSpec((IDX_HEADS, DSA_TQ), lambda b, i: (0, b * nq + i)),
            pl.BlockSpec((SEQ, LANES), lambda b, i: (0, 0), pipeline_mode=once),
            pl.BlockSpec((ATT_HEADS, BF16_ROWS, LANES), lambda b, i: (0, 0, 0)),
        ],
        out_specs=pl.BlockSpec((None, ATT_WIDTH, DSA_TQ), lambda b, i: (b, 0, i)),
        out_shape=jax.ShapeDtypeStruct((BATCH, ATT_WIDTH, SEQ), bf16),
        scratch_shapes=[
            pltpu.VMEM((DSA_NCH, DSA_TK, DSA_TQ), f32),
            pltpu.VMEM((ATT_HEADS, HEAD_DIM + DSA_ONES, DSA_TQ), f32),
            pltpu.VMEM((ATT_HEADS, DSA_TQ, HEAD_DIM + LANES), bf16),
        ],
        compiler_params=_params("arbitrary", "arbitrary"),
        name="dsa",
    )(za, za, vt, za, kk, wt_idx, kf, qf)


def _pool_kernel(u_ref, wp_ref, ps_ref, o_ref):
    row = lax.broadcasted_iota(i32, (SEQ, POOL_GROUP), 0)
    for g, win in enumerate(POOL_WINDOWS):
        x = u_ref[:, g * POOL_GROUP:(g + 1) * POOL_GROUP]
        s = x
        k = 1
        while k < win:
            s = s + jnp.where(row >= k, pltpu.roll(s, k, axis=0), 0.0)
            k *= 2
        cnt = jnp.minimum(row + 1, win).astype(f32)
        pooled = (s / cnt - x).astype(bf16)
        mixed = jnp.dot(pooled, wp_ref[g], preferred_element_type=f32)
        o_ref[:, g * POOL_GROUP:(g + 1) * POOL_GROUP] = (
            mixed * ps_ref[:, g * POOL_GROUP:(g + 1) * POOL_GROUP]).astype(o_ref.dtype)


def _pool(zb, w_pool, pool_scale):
    return pl.pallas_call(
        _pool_kernel,
        grid=(BATCH,),
        in_specs=[
            pl.BlockSpec((SEQ, POOL_WIDTH), lambda b: (b, 0)),
            pl.BlockSpec((N_POOL, POOL_GROUP, POOL_GROUP), lambda b: (0, 0, 0)),
            pl.BlockSpec((1, POOL_WIDTH), lambda b: (0, 0)),
        ],
        out_specs=pl.BlockSpec((SEQ, POOL_WIDTH), lambda b: (b, 0)),
        out_shape=jax.ShapeDtypeStruct((TOKENS, POOL_WIDTH), bf16),
        compiler_params=_params("parallel"),
        name="pool",
    )(zb, w_pool, pool_scale)


MEM_TQ = 512


def _mem_attn_kernel(q_ref, k_ref, v_ref, o_ref):
    for h in range(MEM_HEADS):
        sl = slice(h * HEAD_DIM, (h + 1) * HEAD_DIM)
        s = lax.dot_general(q_ref[:, sl].astype(bf16), k_ref[:, sl], (((1,), (1,)), ((), ())),
                            preferred_element_type=f32) * (HEAD_DIM ** -0.5)
        p = jnp.exp(s - jnp.max(s, axis=1, keepdims=True))
        l = jnp.sum(p, axis=1, keepdims=True)
        o = jnp.dot(p.astype(bf16), v_ref[:, sl], preferred_element_type=f32)
        o_ref[:, sl] = (o / l).astype(o_ref.dtype)


def _mem_attn(zb, qcol, kv):
    nq = SEQ // MEM_TQ
    return pl.pallas_call(
        _mem_attn_kernel,
        grid=(BATCH, nq),
        in_specs=[
            pl.BlockSpec((MEM_TQ, MEM_WIDTH), lambda b, i: (b * nq + i, qcol)),
            pl.BlockSpec((MEM_LEN, MEM_WIDTH), lambda b, i: (b, 0)),
            pl.BlockSpec((MEM_LEN, MEM_WIDTH), lambda b, i: (b, 1)),
        ],
        out_specs=pl.BlockSpec((MEM_TQ, MEM_WIDTH), lambda b, i: (b * nq + i, 0)),
        out_shape=jax.ShapeDtypeStruct((TOKENS, MEM_WIDTH), bf16),
        compiler_params=_params("parallel", "parallel"),
        name="mem_attn",
    )(zb, kv, kv)


CMB_TM = 512
CMB_TC = 512


def _combine_ln_kernel(h_ref, a_ref, p_ref, m_ref, wg0_ref, wg1_ref, wg2_ref,
                       bg0_ref, bg1_ref, bg2_ref, wa_ref, wp_ref, wm_ref, wo_ref,
                       g_ref, b_ref, o_ref, hb_ref):
    j = pl.program_id(1)

    @pl.when(j == 0)
    def _():
        hb_ref[...] = h_ref[...].astype(bf16)
        o_ref[...] = jnp.zeros_like(o_ref)

    hb = hb_ref[...]

    def gate(wg_ref, bg_ref):
        logits = lax.dot_general(hb, wg_ref[...], _NT, preferred_element_type=f32)
        return jax.nn.sigmoid(logits + bg_ref[...])

    y = gate(wg0_ref, bg0_ref) * jnp.dot(a_ref[...], wa_ref[...], preferred_element_type=f32)
    y = y + gate(wg1_ref, bg1_ref) * jnp.dot(p_ref[...], wp_ref[...], preferred_element_type=f32)
    y = y + gate(wg2_ref, bg2_ref) * jnp.dot(m_ref[...], wm_ref[...], preferred_element_type=f32)
    o_ref[...] += jnp.dot(y.astype(bf16), wo_ref[...], preferred_element_type=f32)

    @pl.when(j == pl.num_programs(1) - 1)
    def _():
        z = ALPHA * h_ref[...] + o_ref[...]
        o_ref[...] = _layer_norm(z, g_ref[...], b_ref[...])


def _combine_ln(h, a, p, m, wt_gate, row_gate, b_gate, w_a, w_p, w_m, w_out, g, b):
    nc = D_MODEL // CMB_TC
    gate_rows = pl.Element(CMB_TC)
    all_cols = pl.Element(D_MODEL)
    assert row_gate % BF16_ROWS == 0

    def gate_row(blk):
        return pl.multiple_of(row_gate + blk * CMB_TC, BF16_ROWS)

    row = lambda i, j: (i, 0)
    col = lambda i, j: (0, j)
    return pl.pallas_call(
        _combine_ln_kernel,
        grid=(TOKENS // CMB_TM, nc),
        in_specs=[
            pl.BlockSpec((CMB_TM, D_MODEL), row),
            pl.BlockSpec((CMB_TM, ATT_WIDTH), row),
            pl.BlockSpec((CMB_TM, POOL_WIDTH), row),
            pl.BlockSpec((CMB_TM, MEM_WIDTH), row),
            pl.BlockSpec((gate_rows, all_cols), lambda i, j: (gate_row(j), 0)),
            pl.BlockSpec((gate_rows, all_cols), lambda i, j: (gate_row(j + nc), 0)),
            pl.BlockSpec((gate_rows, all_cols), lambda i, j: (gate_row(j + 2 * nc), 0)),
            pl.BlockSpec((1, CMB_TC), lambda i, j: (0, j)),
            pl.BlockSpec((1, CMB_TC), lambda i, j: (0, j + nc)),
            pl.BlockSpec((1, CMB_TC), lambda i, j: (0, j + 2 * nc)),
            pl.BlockSpec((ATT_WIDTH, CMB_TC), col),
            pl.BlockSpec((POOL_WIDTH, CMB_TC), col),
            pl.BlockSpec((MEM_WIDTH, CMB_TC), col),
            pl.BlockSpec((CMB_TC, D_MODEL), lambda i, j: (j, 0)),
            pl.BlockSpec((1, D_MODEL), lambda i, j: (0, 0)),
            pl.BlockSpec((1, D_MODEL), lambda i, j: (0, 0)),
        ],
        out_specs=pl.BlockSpec((CMB_TM, D_MODEL), row),
        out_shape=jax.ShapeDtypeStruct((TOKENS, D_MODEL), f32),
        scratch_shapes=[pltpu.VMEM((CMB_TM, D_MODEL), bf16)],
        compiler_params=_params("parallel", "arbitrary"),
        name="combine_ln",
    )(h, a, p, m, wt_gate, wt_gate, wt_gate, b_gate, b_gate, b_gate, w_a, w_p, w_m, w_out, g, b)


def kernel(x, mem, w_ffn1_up, w_ffn1_down, ln1_g, ln1_b, w_in, b_gate, w_mem_kv, w_pool,
           pool_scale, w_br_att, w_br_pool, w_br_mem, w_out, ln2_g, ln2_b, w_ffn2_up,
           w_ffn2_down, ln3_g, ln3_b):
    h = x.reshape(TOKENS, D_MODEL)
    memf = mem.reshape(BATCH * MEM_LEN, D_MODEL)
    for l in range(DEPTH):
        c_qi = 3 * ATT_WIDTH + IDX_HEADS * IDX_DIM
        c_wi = c_qi + IDX_DIM + IDX_HEADS
        c_qm = c_wi + POOL_WIDTH + MEM_WIDTH

        h, (wt, w2_up, w2_down) = _ffn_ln(
            h, w_ffn1_up[l].astype(bf16), w_ffn1_down[l].astype(bf16), ln1_g[l][None], ln1_b[l][None],
            casts=((w_in[l].T, True), (w_ffn2_up[l], False), (w_ffn2_down[l], True)))

        za = _matmul(h, wt, bf16, 1024, 512, "in_proj_a", w_is_transposed=True, n=c_qi)
        zb = _in_proj_b(h, wt, c_wi, c_qi)
        kv = _matmul(memf, w_mem_kv[l].astype(bf16), bf16, BATCH * MEM_LEN, 512, "mem_kv")

        c_ki = POOL_WIDTH + MEM_WIDTH
        ki = zb[:, c_ki:c_ki + IDX_DIM].astype(bf16)
        zk = jnp.zeros_like(ki)
        kk = jnp.stack([jnp.concatenate([ki, zk], axis=1), jnp.concatenate([zk, ki], axis=1)])
        wt_idx = zb[:, c_ki + IDX_DIM:c_ki + IDX_DIM + IDX_HEADS].T
        vt = za[:, 2 * ATT_WIDTH:3 * ATT_WIDTH].reshape(BATCH, DSA_NCH, DSA_TK, ATT_WIDTH)
        vt = vt.transpose(0, 1, 3, 2)

        a = _dsa(za, vt, kk, wt_idx).transpose(0, 2, 1).reshape(TOKENS, ATT_WIDTH)
        p = _pool(zb, w_pool[l].astype(bf16), pool_scale[l][None])
        m = _mem_attn(zb, POOL_WIDTH // MEM_WIDTH, kv)

        h = _combine_ln(h, a, p, m, wt, c_qm, b_gate[l][None], w_br_att[l].astype(bf16),
                        w_br_pool[l].astype(bf16), w_br_mem[l].astype(bf16),
                        w_out[l].astype(bf16), ln2_g[l][None], ln2_b[l][None])

        h, _ = _ffn_ln(h, w2_up, w2_down, ln3_g[l][None], ln3_b[l][None])
    return h.reshape(BATCH, SEQ, D_MODEL)
```

```python
import functools
import math

import jax
import jax.numpy as jnp
import numpy as np
from jax import lax
from jax.experimental import pallas as pl
from jax.experimental.pallas import tpu as pltpu

f32 = jnp.float32
bf16 = jnp.bfloat16
i32 = jnp.int32

D_MODEL = 2048
BATCH = 2
SEQ = 4096
DEPTH = 1
MEM_LEN = 256
ATT_HEADS = 8
HEAD_DIM = 128
ATT_WIDTH = ATT_HEADS * HEAD_DIM
IDX_HEADS = 16
IDX_DIM = 64
TOPK = min(256, SEQ // 4)
POOL_WINDOWS = (2, 4, 8, 16)
N_POOL = len(POOL_WINDOWS)
POOL_GROUP = 128
POOL_WIDTH = N_POOL * POOL_GROUP
MEM_HEADS = 4
MEM_WIDTH = MEM_HEADS * HEAD_DIM
N_BRANCH = 3
D_FF = 5632
ALPHA = (2 * DEPTH) ** 0.25
LN_EPS = 1e-5
TOKENS = BATCH * SEQ

LANES = 128
BF16_ROWS = 16
VMEM_LIMIT = 60 * 1024 * 1024

INT_MIN = -(2 ** 31)
LOG2E = math.log2(math.e)


def _params(*sem):
    return pltpu.CompilerParams(dimension_semantics=sem, vmem_limit_bytes=VMEM_LIMIT)


def _layer_norm(y, g, b):
    mu = jnp.mean(y, axis=-1, keepdims=True)
    d = y - mu
    var = jnp.mean(d * d, axis=-1, keepdims=True)
    return d * lax.rsqrt(var + LN_EPS) * g + b


FFN_TM = 512
FFN_TF = 512


FFN_LN_STEPS = 8
FFN_LN_ROWS = FFN_TM // FFN_LN_STEPS


def _ffn_ln_kernel(x_ref, wa_ref, wu_ref, wd_ref, g_ref, b_ref, *rest, n_cast):
    cast_in = rest[:n_cast]
    o_ref = rest[n_cast]
    cast_out = rest[n_cast + 1:2 * n_cast + 1]
    xb_ref, acc_ref = rest[2 * n_cast + 1:]
    i = pl.program_id(0)
    j = pl.program_id(1)
    n_tiles = pl.num_programs(0) - 1
    slot = i % 2

    for src, dst in zip(cast_in, cast_out):
        dst[...] = src[...].astype(dst.dtype)

    @pl.when((i == 0) & (j == 0))
    def _():
        acc_ref[1] = jnp.zeros(acc_ref.shape[1:], f32)

    @pl.when((i < n_tiles) & (j == 0))
    def _():
        x = x_ref[...]
        xb_ref[...] = x.astype(bf16)
        acc_ref[slot] = (2.0 * ALPHA) * x

    def matmuls():
        xb = xb_ref[...]
        a = jnp.dot(xb, wa_ref[...], preferred_element_type=f32)
        u = jnp.dot(xb, wu_ref[...], preferred_element_type=f32)
        act = (a * jax.nn.sigmoid(a) * u).astype(bf16)
        acc_ref[slot] += jnp.dot(act, wd_ref[...], preferred_element_type=f32)

    def layer_norm_slab():
        rows = pl.ds(pl.multiple_of(j * FFN_LN_ROWS, FFN_LN_ROWS), FFN_LN_ROWS)
        z = acc_ref[1 - slot, rows, :]
        mu = jnp.mean(z, axis=-1, keepdims=True)
        d = z - mu
        var = jnp.mean(d * d, axis=-1, keepdims=True)
        o_ref[rows, :] = d * lax.rsqrt(var + 4.0 * LN_EPS) * g_ref[...] + b_ref[...]

    @pl.when((i < n_tiles) & (j < FFN_LN_STEPS))
    def _():
        layer_norm_slab()
        matmuls()

    @pl.when((i < n_tiles) & (j >= FFN_LN_STEPS))
    def _():
        matmuls()

    @pl.when((i == n_tiles) & (j < FFN_LN_STEPS))
    def _():
        layer_norm_slab()


def _cast_block_spec(shape, rows_on_j, ni, nj):
    r, c = shape
    n_r, n_c = (nj, ni) if rows_on_j else (ni, nj)
    br = -(-r // n_r)
    br = -(-br // BF16_ROWS) * BF16_ROWS
    assert c % n_c == 0 and (c // n_c) % LANES == 0
    def clamp(i, j):
        return jnp.minimum(i, ni - 1), jnp.where(i < ni, j, nj - 1)

    if rows_on_j:
        index_map = lambda i, j: clamp(i, j)[::-1]
    else:
        index_map = lambda i, j: clamp(i, j)
    return pl.BlockSpec((br, c // n_c), index_map)


def _ffn_ln(x, w_up, w_down, g, b, casts=()):
    ni = TOKENS // FFN_TM
    nf = D_FF // FFN_TF
    assert FFN_LN_STEPS <= nf
    cast_specs = [_cast_block_spec(w.shape, rows_on_j, ni, nf) for w, rows_on_j in casts]

    def wj(i, j):
        return jnp.where(i < ni, j, nf - 1)

    outs = pl.pallas_call(
        functools.partial(_ffn_ln_kernel, n_cast=len(casts)),
        grid=(ni + 1, nf),
        in_specs=[
            pl.BlockSpec((FFN_TM, D_MODEL), lambda i, j: (jnp.minimum(i, ni - 1), 0)),
            pl.BlockSpec((D_MODEL, FFN_TF), lambda i, j: (0, wj(i, j))),
            pl.BlockSpec((D_MODEL, FFN_TF), lambda i, j: (0, wj(i, j) + nf)),
            pl.BlockSpec((FFN_TF, D_MODEL), lambda i, j: (wj(i, j), 0)),
            pl.BlockSpec((1, D_MODEL), lambda i, j: (0, 0)),
            pl.BlockSpec((1, D_MODEL), lambda i, j: (0, 0)),
        ] + cast_specs,
        out_specs=[pl.BlockSpec((FFN_TM, D_MODEL), lambda i, j: (jnp.maximum(i - 1, 0), 0))]
        + cast_specs,
        out_shape=[jax.ShapeDtypeStruct((TOKENS, D_MODEL), f32)]
        + [jax.ShapeDtypeStruct(w.shape, bf16) for w, _ in casts],
        scratch_shapes=[pltpu.VMEM((FFN_TM, D_MODEL), bf16),
                        pltpu.VMEM((2, FFN_TM, D_MODEL), f32)],
        compiler_params=_params("arbitrary", "arbitrary"),
        name="ffn_ln",
    )(x, w_up, w_up, w_down, g, b, *[w for w, _ in casts])
    return outs[0], outs[1:]


_NT = (((1,), (1,)), ((), ()))


def _matmul_kernel(x_ref, w_ref, o_ref, xb_ref, *, w_is_transposed):
    @pl.when(pl.program_id(1) == 0)
    def _():
        xb_ref[...] = x_ref[...].astype(bf16)

    if w_is_transposed:
        y = lax.dot_general(xb_ref[...], w_ref[...], _NT, preferred_element_type=f32)
    else:
        y = jnp.dot(xb_ref[...], w_ref[...], preferred_element_type=f32)
    o_ref[...] = y.astype(o_ref.dtype)


def _matmul(x, w, out_dtype, tm, tn, name, w_is_transposed=False, n=None):
    m, k = x.shape
    if n is None:
        n = w.shape[0] if w_is_transposed else w.shape[1]
    assert n % tn == 0 and m % tm == 0
    if w_is_transposed:
        w_spec = pl.BlockSpec((tn, k), lambda i, j: (j, 0))
    else:
        w_spec = pl.BlockSpec((k, tn), lambda i, j: (0, j))
    return pl.pallas_call(
        functools.partial(_matmul_kernel, w_is_transposed=w_is_transposed),
        grid=(m // tm, n // tn),
        in_specs=[pl.BlockSpec((tm, k), lambda i, j: (i, 0)), w_spec],
        out_specs=pl.BlockSpec((tm, tn), lambda i, j: (i, j)),
        out_shape=jax.ShapeDtypeStruct((m, n), out_dtype),
        scratch_shapes=[pltpu.VMEM((tm, k), bf16)],
        compiler_params=_params("parallel", "arbitrary"),
        name=name,
    )(x, w)


PB_TM = 512
PB_N = POOL_WIDTH + MEM_WIDTH + LANES


def _in_proj_b_kernel(x_ref, wum_ref, wk_ref, o_ref):
    xb = x_ref[...].astype(bf16)
    n_um = POOL_WIDTH + MEM_WIDTH
    o_ref[:, :n_um] = lax.dot_general(xb, wum_ref[...], _NT, preferred_element_type=f32)
    o_ref[:, n_um:] = lax.dot_general(xb, wk_ref[...], _NT, preferred_element_type=f32)


def _in_proj_b(x, wt, row_um, row_kiwi):
    n_um = POOL_WIDTH + MEM_WIDTH
    return pl.pallas_call(
        _in_proj_b_kernel,
        grid=(TOKENS // PB_TM,),
        in_specs=[
            pl.BlockSpec((PB_TM, D_MODEL), lambda i: (i, 0)),
            pl.BlockSpec((pl.Element(n_um), pl.Element(D_MODEL)), lambda i: (row_um, 0)),
            pl.BlockSpec((pl.Element(LANES), pl.Element(D_MODEL)), lambda i: (row_kiwi, 0)),
        ],
        out_specs=pl.BlockSpec((PB_TM, PB_N), lambda i: (i, 0)),
        out_shape=jax.ShapeDtypeStruct((TOKENS, PB_N), f32),
        compiler_params=_params("parallel"),
        name="in_proj_b",
    )(x, wt, wt)


DSA_TQ = 512
DSA_TK = 512
DSA_NCH = SEQ // DSA_TK
DSA_ONES = 16
KEY_NEG_FLT_MAX = -(2 ** 31) + (1 << 23)


def _key_to_f32(key):
    return lax.bitcast_convert_type(key ^ ((key >> 31) & jnp.int32(0x7FFFFFFF)), f32)


def _dsa_kernel(q_ref, k_ref, vt_ref, qi_ref, kk_ref, wt_ref, kf_ref, qf_ref, o_ref,
                sc_scr, acc_scr, qa_scr):
    i = pl.program_id(1)
    q0 = i * DSA_TQ
    nk = (q0 + DSA_TQ - 1) // DSA_TK + 1

    qpos = q0 + lax.broadcasted_iota(i32, (DSA_TK, DSA_TQ), 1)
    kofs = lax.broadcasted_iota(i32, (DSA_TK, DSA_TQ), 0)
    contract_last = (((1,), (1,)), ((), ()))

    def rows8(x):
        return x.reshape(DSA_TK // 8, 8, DSA_TQ)

    wt = wt_ref[...] * (IDX_DIM ** -0.5 * IDX_HEADS ** -0.5)

    def score_chunk(c, carry):
        r0 = pl.multiple_of(c * DSA_TK, DSA_TK)
        kk0 = kk_ref[0, pl.ds(r0, DSA_TK), :]
        kk1 = kk_ref[1, pl.ds(r0, DSA_TK), :]
        acc = jnp.zeros((DSA_TK, DSA_TQ), f32)
        for p in range(IDX_HEADS // 2):
            slab = qi_ref[:, p * LANES:(p + 1) * LANES]
            l0 = lax.dot_general(kk0, slab, contract_last, preferred_element_type=f32)
            l1 = lax.dot_general(kk1, slab, contract_last, preferred_element_type=f32)
            acc = acc + (jnp.maximum(l0, 0.0) * wt[2 * p:2 * p + 1]
                         + jnp.maximum(l1, 0.0) * wt[2 * p + 1:2 * p + 2])
        sc_scr[c] = jnp.where(kofs + r0 <= qpos, acc, -jnp.inf)
        return carry

    lax.fori_loop(0, nk, score_chunk, 0)

    def count(pred):
        n_acc = 8

        def body(c, accs):
            m = rows8(pred(c))
            accs = list(accs)
            for r in range(DSA_TK // 8):
                accs[r % n_acc] = jnp.where(m[r], accs[r % n_acc] + 1, accs[r % n_acc])
            return tuple(accs)
        accs = lax.fori_loop(0, nk, body, (jnp.zeros((8, DSA_TQ), i32),) * n_acc)
        acc = functools.reduce(lambda a, b: a + b, accs)
        return jnp.sum(acc, axis=0, keepdims=True)

    def count_ge(cand):
        return count(lambda c: sc_scr[c] >= cand)

    key = jnp.where(count_ge(jnp.zeros((1, DSA_TQ), f32)) >= TOPK, jnp.int32(0), jnp.int32(INT_MIN))

    def bit_step(b, key):
        cand = key | jnp.left_shift(jnp.int32(1), 30 - b)
        return jnp.where(count_ge(_key_to_f32(cand)) >= TOPK, cand, key)

    key = lax.fori_loop(0, 31, bit_step, key)
    thr = _key_to_f32(jnp.maximum(key, jnp.int32(KEY_NEG_FLT_MAX)))

    n_ge = count_ge(thr)

    @pl.when(jnp.max(n_ge) > TOPK)
    def _():
        need = TOPK - count(lambda c: sc_scr[c] > thr)

        def count_eq_below(pos):
            return count(lambda c: (sc_scr[c] == thr) & (kofs + c * DSA_TK < pos))

        def pos_step(b, r):
            cand = r | jnp.left_shift(jnp.int32(1), (SEQ.bit_length() - 2) - b)
            return jnp.where(count_eq_below(cand) < need, cand, r)
        r = lax.fori_loop(0, SEQ.bit_length() - 1, pos_step, jnp.zeros((1, DSA_TQ), i32))
        r = jnp.where(n_ge > TOPK, r, jnp.int32(SEQ))

        def drop(c, carry):
            sc = sc_scr[c]
            sc_scr[c] = jnp.where((sc == thr) & (kofs + c * DSA_TK > r), -jnp.inf, sc)
            return carry
        lax.fori_loop(0, nk, drop, 0)

    acc_scr[...] = jnp.zeros(acc_scr.shape, f32)
    c1 = HEAD_DIM ** -0.5 * LOG2E
    ones = jnp.ones((DSA_ONES, DSA_TK), bf16)

    for h in range(ATT_HEADS):
        qa_scr[h, :, :HEAD_DIM] = q_ref[:, h * HEAD_DIM:(h + 1) * HEAD_DIM]
        qa_scr[h, :, HEAD_DIM:] = jnp.broadcast_to(qf_ref[h, 0:1, :], (DSA_TQ, LANES))

    def attn_chunk(c, ms):
        r0 = pl.multiple_of(c * DSA_TK, DSA_TK)
        madd = jnp.where(sc_scr[c] >= thr, 0.0, -jnp.inf)
        kf = kf_ref[pl.ds(r0, DSA_TK), :]

        def qk(h):
            hd = slice(h * HEAD_DIM, (h + 1) * HEAD_DIM)
            k_aug = jnp.concatenate([k_ref[pl.ds(r0, DSA_TK), hd], kf], axis=1)
            return lax.dot_general(k_aug, qa_scr[h], contract_last, preferred_element_type=f32)

        new_ms = []
        s_next = qk(0)
        for h in range(ATT_HEADS):
            hd = slice(h * HEAD_DIM, (h + 1) * HEAD_DIM)
            s = s_next
            if h + 1 < ATT_HEADS:
                s_next = qk(h + 1)
            t = s * c1 + madd
            m_old = ms[h]
            m_new = jnp.maximum(m_old, jnp.max(t, axis=0, keepdims=True))
            alpha = jnp.exp2(m_old - m_new)
            p = jnp.exp2(t - m_new).astype(bf16)
            vt1 = jnp.concatenate([vt_ref[c, hd, :], ones], axis=0)
            acc_scr[h] = alpha * acc_scr[h] + jnp.dot(vt1, p, preferred_element_type=f32)
            new_ms.append(m_new)
        return tuple(new_ms)

    m0 = jnp.full((1, DSA_TQ), -1e30, f32)
    lax.fori_loop(0, nk, attn_chunk, (m0,) * ATT_HEADS)

    for h in range(ATT_HEADS):
        o_ref[h * HEAD_DIM:(h + 1) * HEAD_DIM, :] = (
            acc_scr[h, :HEAD_DIM, :] / acc_scr[h, HEAD_DIM:HEAD_DIM + 1, :]).astype(o_ref.dtype)


def _alibi_features():
    assert ATT_HEADS == 8 and HEAD_DIM == 128 and SEQ <= 64 * 64
    pieces, rest = [], math.sqrt(2.0)
    for _ in range(7):
        p = float(np.asarray(rest, np.float32).astype(jnp.bfloat16).astype(np.float64))
        pieces.append(p)
        rest -= p
    qf = np.zeros((ATT_HEADS, BF16_ROWS, LANES), np.float32)
    for h in range(ATT_HEADS):
        for i, p in enumerate(pieces):
            qf[h, :, 2 * i] = qf[h, :, 2 * i + 1] = p * 2.0 ** (2 - h)
    pos = np.arange(SEQ)
    kf = np.zeros((SEQ, LANES), np.float32)
    for i in range(len(pieces)):
        kf[:, 2 * i] = pos - pos % 64
        kf[:, 2 * i + 1] = pos % 64
    return jnp.asarray(kf, bf16), jnp.asarray(qf, bf16)


def _dsa(za, vt, kk, wt_idx):
    nq = SEQ // DSA_TQ
    kf, qf = _alibi_features()
    once = pl.Buffered(1)
    return pl.pallas_call(
        _dsa_kernel,
        grid=(BATCH, nq),
        in_specs=[
            pl.BlockSpec((DSA_TQ, ATT_WIDTH), lambda b, i: (b * nq + i, 0)),
            pl.BlockSpec((SEQ, ATT_WIDTH), lambda b, i: (b, 1), pipeline_mode=once),
            pl.BlockSpec((None, DSA_NCH, ATT_WIDTH, DSA_TK), lambda b, i: (b, 0, 0, 0),
                         pipeline_mode=once),
            pl.BlockSpec((DSA_TQ, IDX_HEADS * IDX_DIM), lambda b, i: (b * nq + i, 3)),
            pl.BlockSpec((2, SEQ, LANES), lambda b, i: (0, b, 0), pipeline_mode=once),
            pl.BlockSpec((IDX_HEADS, DSA_TQ), lambda b, i: (0, b * nq + i)),
            pl.BlockSpec((SEQ, LANES), lambda b, i: (0, 0), pipeline_mode=once),
            pl.BlockSpec((ATT_HEADS, BF16_ROWS, LANES), lambda b, i: (0, 0, 0)),
        ],
        out_specs=pl.BlockSpec((None, ATT_WIDTH, DSA_TQ), lambda b, i: (b, 0, i)),
        out_shape=jax.ShapeDtypeStruct((BATCH, ATT_WIDTH, SEQ), bf16),
        scratch_shapes=[
            pltpu.VMEM((DSA_NCH, DSA_TK, DSA_TQ), f32),
            pltpu.VMEM((ATT_HEADS, HEAD_DIM + DSA_ONES, DSA_TQ), f32),
            pltpu.VMEM((ATT_HEADS, DSA_TQ, HEAD_DIM + LANES), bf16),
        ],
        compiler_params=_params("arbitrary", "arbitrary"),
        name="dsa",
    )(za, za, vt, za, kk, wt_idx, kf, qf)


def _pool_kernel(u_ref, wp_ref, ps_ref, o_ref):
    row = lax.broadcasted_iota(i32, (SEQ, POOL_GROUP), 0)
    for g, win in enumerate(POOL_WINDOWS):
        x = u_ref[:, g * POOL_GROUP:(g + 1) * POOL_GROUP]
        s = x
        k = 1
        while k < win:
            s = s + jnp.where(row >= k, pltpu.roll(s, k, axis=0), 0.0)
            k *= 2
        cnt = jnp.minimum(row + 1, win).astype(f32)
        pooled = (s / cnt - x).astype(bf16)
        mixed = jnp.dot(pooled, wp_ref[g], preferred_element_type=f32)
        o_ref[:, g * POOL_GROUP:(g + 1) * POOL_GROUP] = (
            mixed * ps_ref[:, g * POOL_GROUP:(g + 1) * POOL_GROUP]).astype(o_ref.dtype)


def _pool(zb, w_pool, pool_scale):
    return pl.pallas_call(
        _pool_kernel,
        grid=(BATCH,),
        in_specs=[
            pl.BlockSpec((SEQ, POOL_WIDTH), lambda b: (b, 0)),
            pl.BlockSpec((N_POOL, POOL_GROUP, POOL_GROUP), lambda b: (0, 0, 0)),
            pl.BlockSpec((1, POOL_WIDTH), lambda b: (0, 0)),
        ],
        out_specs=pl.BlockSpec((SEQ, POOL_WIDTH), lambda b: (b, 0)),
        out_shape=jax.ShapeDtypeStruct((TOKENS, POOL_WIDTH), bf16),
        compiler_params=_params("parallel"),
        name="pool",
    )(zb, w_pool, pool_scale)


MEM_TQ = 512


def _mem_attn_kernel(q_ref, k_ref, v_ref, o_ref):
    for h in range(MEM_HEADS):
        sl = slice(h * HEAD_DIM, (h + 1) * HEAD_DIM)
        s = lax.dot_general(q_ref[:, sl].astype(bf16), k_ref[:, sl], (((1,), (1,)), ((), ())),
                            preferred_element_type=f32) * (HEAD_DIM ** -0.5)
        p = jnp.exp(s - jnp.max(s, axis=1, keepdims=True))
        l = jnp.sum(p, axis=1, keepdims=True)
        o = jnp.dot(p.astype(bf16), v_ref[:, sl], preferred_element_type=f32)
        o_ref[:, sl] = (o / l).astype(o_ref.dtype)


def _mem_attn(zb, qcol, kv):
    nq = SEQ // MEM_TQ
    return pl.pallas_call(
        _mem_attn_kernel,
        grid=(BATCH, nq),
        in_specs=[
            pl.BlockSpec((MEM_TQ, MEM_WIDTH), lambda b, i: (b * nq + i, qcol)),
            pl.BlockSpec((MEM_LEN, MEM_WIDTH), lambda b, i: (b, 0)),
            pl.BlockSpec((MEM_LEN, MEM_WIDTH), lambda b, i: (b, 1)),
        ],
        out_specs=pl.BlockSpec((MEM_TQ, MEM_WIDTH), lambda b, i: (b * nq + i, 0)),
        out_shape=jax.ShapeDtypeStruct((TOKENS, MEM_WIDTH), bf16),
        compiler_params=_params("parallel", "parallel"),
        name="mem_attn",
    )(zb, kv, kv)


CMB_TM = 512
CMB_TC = 512


def _combine_ln_kernel(h_ref, a_ref, p_ref, m_ref, wg0_ref, wg1_ref, wg2_ref,
                       bg0_ref, bg1_ref, bg2_ref, wa_ref, wp_ref, wm_ref, wo_ref,
                       g_ref, b_ref, o_ref, hb_ref, acc_ref):
    i = pl.program_id(0)
    j = pl.program_id(1)
    n_tiles = pl.num_programs(0) - 1
    slot = i % 2
    ln_rows = CMB_TM // (D_MODEL // CMB_TC)

    @pl.when((i == 0) & (j == 0))
    def _():
        acc_ref[1] = jnp.zeros(acc_ref.shape[1:], f32)

    @pl.when((i < n_tiles) & (j == 0))
    def _():
        h = h_ref[...]
        hb_ref[...] = h.astype(bf16)
        acc_ref[slot] = ALPHA * h

    def layer_norm_slab():
        rows = pl.ds(pl.multiple_of(j * ln_rows, ln_rows), ln_rows)
        o_ref[rows, :] = _layer_norm(acc_ref[1 - slot, rows, :], g_ref[...], b_ref[...])

    def matmuls():
        hb = hb_ref[...]

        def gate(wg_ref, bg_ref):
            logits = lax.dot_general(hb, wg_ref[...], _NT, preferred_element_type=f32)
            return jax.nn.sigmoid(logits + bg_ref[...])

        y = gate(wg0_ref, bg0_ref) * jnp.dot(a_ref[...], wa_ref[...], preferred_element_type=f32)
        y = y + gate(wg1_ref, bg1_ref) * jnp.dot(p_ref[...], wp_ref[...], preferred_element_type=f32)
        y = y + gate(wg2_ref, bg2_ref) * jnp.dot(m_ref[...], wm_ref[...], preferred_element_type=f32)
        acc_ref[slot] += jnp.dot(y.astype(bf16), wo_ref[...], preferred_element_type=f32)

    @pl.when(i < n_tiles)
    def _():
        layer_norm_slab()
        matmuls()

    @pl.when(i == n_tiles)
    def _():
        layer_norm_slab()


def _combine_ln(h, a, p, m, wt_gate, row_gate, b_gate, w_a, w_p, w_m, w_out, g, b):
    nc = D_MODEL // CMB_TC
    gate_rows = pl.Element(CMB_TC)
    all_cols = pl.Element(D_MODEL)
    assert row_gate % BF16_ROWS == 0

    def gate_row(blk):
        return pl.multiple_of(row_gate + blk * CMB_TC, BF16_ROWS)

    ni = TOKENS // CMB_TM

    def cj(i, j):
        return jnp.where(i < ni, j, nc - 1)

    row = lambda i, j: (jnp.minimum(i, ni - 1), 0)
    col = lambda i, j: (0, cj(i, j))
    return pl.pallas_call(
        _combine_ln_kernel,
        grid=(ni + 1, nc),
        in_specs=[
            pl.BlockSpec((CMB_TM, D_MODEL), row),
            pl.BlockSpec((CMB_TM, ATT_WIDTH), row),
            pl.BlockSpec((CMB_TM, POOL_WIDTH), row),
            pl.BlockSpec((CMB_TM, MEM_WIDTH), row),
            pl.BlockSpec((gate_rows, all_cols), lambda i, j: (gate_row(cj(i, j)), 0)),
            pl.BlockSpec((gate_rows, all_cols), lambda i, j: (gate_row(cj(i, j) + nc), 0)),
            pl.BlockSpec((gate_rows, all_cols), lambda i, j: (gate_row(cj(i, j) + 2 * nc), 0)),
            pl.BlockSpec((1, CMB_TC), lambda i, j: (0, cj(i, j))),
            pl.BlockSpec((1, CMB_TC), lambda i, j: (0, cj(i, j) + nc)),
            pl.BlockSpec((1, CMB_TC), lambda i, j: (0, cj(i, j) + 2 * nc)),
            pl.BlockSpec((ATT_WIDTH, CMB_TC), col),
            pl.BlockSpec((POOL_WIDTH, CMB_TC), col),
            pl.BlockSpec((MEM_WIDTH, CMB_TC), col),
            pl.BlockSpec((CMB_TC, D_MODEL), lambda i, j: (cj(i, j), 0)),
            pl.BlockSpec((1, D_MODEL), lambda i, j: (0, 0)),
            pl.BlockSpec((1, D_MODEL), lambda i, j: (0, 0)),
        ],
        out_specs=pl.BlockSpec((CMB_TM, D_MODEL), lambda i, j: (jnp.maximum(i - 1, 0), 0)),
        out_shape=jax.ShapeDtypeStruct((TOKENS, D_MODEL), f32),
        scratch_shapes=[pltpu.VMEM((CMB_TM, D_MODEL), bf16),
                        pltpu.VMEM((2, CMB_TM, D_MODEL), f32)],
        compiler_params=_params("arbitrary", "arbitrary"),
        name="combine_ln",
    )(h, a, p, m, wt_gate, wt_gate, wt_gate, b_gate, b_gate, b_gate, w_a, w_p, w_m, w_out, g, b)


def kernel(x, mem, w_ffn1_up, w_ffn1_down, ln1_g, ln1_b, w_in, b_gate, w_mem_kv, w_pool,
           pool_scale, w_br_att, w_br_pool, w_br_mem, w_out, ln2_g, ln2_b, w_ffn2_up,
           w_ffn2_down, ln3_g, ln3_b):
    h = x.reshape(TOKENS, D_MODEL)
    memf = mem.reshape(BATCH * MEM_LEN, D_MODEL)
    for l in range(DEPTH):
        c_qi = 3 * ATT_WIDTH + IDX_HEADS * IDX_DIM
        c_wi = c_qi + IDX_DIM + IDX_HEADS
        c_qm = c_wi + POOL_WIDTH + MEM_WIDTH

        h, (wt, w2_up, w2_down) = _ffn_ln(
            h, w_ffn1_up[l].astype(bf16), w_ffn1_down[l].astype(bf16), ln1_g[l][None], ln1_b[l][None],
            casts=((w_in[l].T, True), (w_ffn2_up[l], False), (w_ffn2_down[l], True)))

        za = _matmul(h, wt, bf16, 1024, 1024, "in_proj_a", w_is_transposed=True, n=c_qi)
        zb = _in_proj_b(h, wt, c_wi, c_qi)
        kv = _matmul(memf, w_mem_kv[l].astype(bf16), bf16, BATCH * MEM_LEN, 512, "mem_kv")

        c_ki = POOL_WIDTH + MEM_WIDTH
        ki = zb[:, c_ki:c_ki + IDX_DIM].astype(bf16)
        zk = jnp.zeros_like(ki)
        kk = jnp.stack([jnp.concatenate([ki, zk], axis=1), jnp.concatenate([zk, ki], axis=1)])
        wt_idx = zb[:, c_ki + IDX_DIM:c_ki + IDX_DIM + IDX_HEADS].T
        vt = za[:, 2 * ATT_WIDTH:3 * ATT_WIDTH].reshape(BATCH, DSA_NCH, DSA_TK, ATT_WIDTH)
        vt = vt.transpose(0, 1, 3, 2)

        a = _dsa(za, vt, kk, wt_idx).transpose(0, 2, 1).reshape(TOKENS, ATT_WIDTH)
        p = _pool(zb, w_pool[l].astype(bf16), pool_scale[l][None])
        m = _mem_attn(zb, POOL_WIDTH // MEM_WIDTH, kv)

        h = _combine_ln(h, a, p, m, wt, c_qm, b_gate[l][None], w_br_att[l].astype(bf16),
                        w_br_pool[l].astype(bf16), w_br_mem[l].astype(bf16),
                        w_out[l].astype(bf16), ln2_g[l][None], ln2_b[l][None])

        h, _ = _ffn_ln(h, w2_up, w2_down, ln3_g[l][None], ln3_b[l][None])
    return h.reshape(BATCH, SEQ, D_MODEL)
```

```python
import functools
import math

import jax
import jax.numpy as jnp
import numpy as np
from jax import lax
from jax.experimental import pallas as pl
from jax.experimental.pallas import tpu as pltpu

f32 = jnp.float32
bf16 = jnp.bfloat16
i32 = jnp.int32

D_MODEL = 2048
BATCH = 2
SEQ = 4096
DEPTH = 1
MEM_LEN = 256
ATT_HEADS = 8
HEAD_DIM = 128
ATT_WIDTH = ATT_HEADS * HEAD_DIM
IDX_HEADS = 16
IDX_DIM = 64
TOPK = min(256, SEQ // 4)
POOL_WINDOWS = (2, 4, 8, 16)
N_POOL = len(POOL_WINDOWS)
POOL_GROUP = 128
POOL_WIDTH = N_POOL * POOL_GROUP
MEM_HEADS = 4
MEM_WIDTH = MEM_HEADS * HEAD_DIM
N_BRANCH = 3
D_FF = 5632
ALPHA = (2 * DEPTH) ** 0.25
LN_EPS = 1e-5
TOKENS = BATCH * SEQ

LANES = 128
BF16_ROWS = 16
VMEM_LIMIT = 60 * 1024 * 1024

INT_MIN = -(2 ** 31)
LOG2E = math.log2(math.e)


def _params(*sem):
    return pltpu.CompilerParams(dimension_semantics=sem, vmem_limit_bytes=VMEM_LIMIT)


def _layer_norm(y, g, b):
    mu = jnp.mean(y, axis=-1, keepdims=True)
    d = y - mu
    var = jnp.mean(d * d, axis=-1, keepdims=True)
    return d * lax.rsqrt(var + LN_EPS) * g + b


FFN_TM = 512
FFN_TF = 512


FFN_LN_STEPS = 8
FFN_LN_ROWS = FFN_TM // FFN_LN_STEPS


def _ffn_ln_kernel(x_ref, wa_ref, wu_ref, wd_ref, g_ref, b_ref, *rest, n_cast):
    cast_in = rest[:n_cast]
    o_ref = rest[n_cast]
    cast_out = rest[n_cast + 1:2 * n_cast + 1]
    xb_ref, acc_ref = rest[2 * n_cast + 1:]
    i = pl.program_id(0)
    j = pl.program_id(1)
    n_tiles = pl.num_programs(0) - 1
    slot = i % 2

    for src, dst in zip(cast_in, cast_out):
        dst[...] = src[...].astype(dst.dtype)

    @pl.when((i == 0) & (j == 0))
    def _():
        acc_ref[1] = jnp.zeros(acc_ref.shape[1:], f32)

    @pl.when((i < n_tiles) & (j == 0))
    def _():
        x = x_ref[...]
        xb_ref[...] = x.astype(bf16)
        acc_ref[slot] = (2.0 * ALPHA) * x

    def matmuls():
        xb = xb_ref[...]
        a = lax.dot_general(xb, wa_ref[...], _NT, preferred_element_type=f32)
        u = lax.dot_general(xb, wu_ref[...], _NT, preferred_element_type=f32)
        act = (a * jax.nn.sigmoid(a) * u).astype(bf16)
        acc_ref[slot] += jnp.dot(act, wd_ref[...], preferred_element_type=f32)

    def layer_norm_slab():
        rows = pl.ds(pl.multiple_of(j * FFN_LN_ROWS, FFN_LN_ROWS), FFN_LN_ROWS)
        z = acc_ref[1 - slot, rows, :]
        mu = jnp.mean(z, axis=-1, keepdims=True)
        d = z - mu
        var = jnp.mean(d * d, axis=-1, keepdims=True)
        o_ref[rows, :] = d * lax.rsqrt(var + 4.0 * LN_EPS) * g_ref[...] + b_ref[...]

    @pl.when((i < n_tiles) & (j < FFN_LN_STEPS))
    def _():
        layer_norm_slab()
        matmuls()

    @pl.when((i < n_tiles) & (j >= FFN_LN_STEPS))
    def _():
        matmuls()

    @pl.when((i == n_tiles) & (j < FFN_LN_STEPS))
    def _():
        layer_norm_slab()


def _cast_block_spec(shape, rows_on_j, ni, nj):
    r, c = shape
    n_r, n_c = (nj, ni) if rows_on_j else (ni, nj)
    br = -(-r // n_r)
    br = -(-br // BF16_ROWS) * BF16_ROWS
    assert c % n_c == 0 and (c // n_c) % LANES == 0
    def clamp(i, j):
        return jnp.minimum(i, ni - 1), jnp.where(i < ni, j, nj - 1)

    if rows_on_j:
        index_map = lambda i, j: clamp(i, j)[::-1]
    else:
        index_map = lambda i, j: clamp(i, j)
    return pl.BlockSpec((br, c // n_c), index_map)


def _ffn_ln(x, w_up, w_down, g, b, casts=()):
    ni = TOKENS // FFN_TM
    nf = D_FF // FFN_TF
    assert FFN_LN_STEPS <= nf
    cast_specs = [_cast_block_spec(w.shape, rows_on_j, ni, nf) for w, rows_on_j in casts]

    def wj(i, j):
        return jnp.where(i < ni, j, nf - 1)

    outs = pl.pallas_call(
        functools.partial(_ffn_ln_kernel, n_cast=len(casts)),
        grid=(ni + 1, nf),
        in_specs=[
            pl.BlockSpec((FFN_TM, D_MODEL), lambda i, j: (jnp.minimum(i, ni - 1), 0)),
            pl.BlockSpec((FFN_TF, D_MODEL), lambda i, j: (wj(i, j), 0)),
            pl.BlockSpec((FFN_TF, D_MODEL), lambda i, j: (wj(i, j) + nf, 0)),
            pl.BlockSpec((FFN_TF, D_MODEL), lambda i, j: (wj(i, j), 0)),
            pl.BlockSpec((1, D_MODEL), lambda i, j: (0, 0)),
            pl.BlockSpec((1, D_MODEL), lambda i, j: (0, 0)),
        ] + cast_specs,
        out_specs=[pl.BlockSpec((FFN_TM, D_MODEL), lambda i, j: (jnp.maximum(i - 1, 0), 0))]
        + cast_specs,
        out_shape=[jax.ShapeDtypeStruct((TOKENS, D_MODEL), f32)]
        + [jax.ShapeDtypeStruct(w.shape, bf16) for w, _ in casts],
        scratch_shapes=[pltpu.VMEM((FFN_TM, D_MODEL), bf16),
                        pltpu.VMEM((2, FFN_TM, D_MODEL), f32)],
        compiler_params=_params("arbitrary", "arbitrary"),
        name="ffn_ln",
    )(x, w_up, w_up, w_down, g, b, *[w for w, _ in casts])
    return outs[0], outs[1:]


_NT = (((1,), (1,)), ((), ()))


def _matmul_kernel(x_ref, w_ref, o_ref, xb_ref, *, w_is_transposed):
    @pl.when(pl.program_id(1) == 0)
    def _():
        xb_ref[...] = x_ref[...].astype(bf16)

    if w_is_transposed:
        y = lax.dot_general(xb_ref[...], w_ref[...], _NT, preferred_element_type=f32)
    else:
        y = jnp.dot(xb_ref[...], w_ref[...], preferred_element_type=f32)
    o_ref[...] = y.astype(o_ref.dtype)


def _matmul(x, w, out_dtype, tm, tn, name, w_is_transposed=False, n=None):
    m, k = x.shape
    if n is None:
        n = w.shape[0] if w_is_transposed else w.shape[1]
    assert n % tn == 0 and m % tm == 0
    if w_is_transposed:
        w_spec = pl.BlockSpec((tn, k), lambda i, j: (j, 0))
    else:
        w_spec = pl.BlockSpec((k, tn), lambda i, j: (0, j))
    return pl.pallas_call(
        functools.partial(_matmul_kernel, w_is_transposed=w_is_transposed),
        grid=(m // tm, n // tn),
        in_specs=[pl.BlockSpec((tm, k), lambda i, j: (i, 0)), w_spec],
        out_specs=pl.BlockSpec((tm, tn), lambda i, j: (i, j)),
        out_shape=jax.ShapeDtypeStruct((m, n), out_dtype),
        scratch_shapes=[pltpu.VMEM((tm, k), bf16)],
        compiler_params=_params("parallel", "arbitrary"),
        name=name,
    )(x, w)


PB_TM = 512
PB_N = POOL_WIDTH + MEM_WIDTH + LANES


def _in_proj_b_kernel(x_ref, wum_ref, wk_ref, o_ref):
    xb = x_ref[...].astype(bf16)
    n_um = POOL_WIDTH + MEM_WIDTH
    o_ref[:, :n_um] = lax.dot_general(xb, wum_ref[...], _NT, preferred_element_type=f32)
    o_ref[:, n_um:] = lax.dot_general(xb, wk_ref[...], _NT, preferred_element_type=f32)


def _in_proj_b(x, wt, row_um, row_kiwi):
    n_um = POOL_WIDTH + MEM_WIDTH
    return pl.pallas_call(
        _in_proj_b_kernel,
        grid=(TOKENS // PB_TM,),
        in_specs=[
            pl.BlockSpec((PB_TM, D_MODEL), lambda i: (i, 0)),
            pl.BlockSpec((pl.Element(n_um), pl.Element(D_MODEL)), lambda i: (row_um, 0)),
            pl.BlockSpec((pl.Element(LANES), pl.Element(D_MODEL)), lambda i: (row_kiwi, 0)),
        ],
        out_specs=pl.BlockSpec((PB_TM, PB_N), lambda i: (i, 0)),
        out_shape=jax.ShapeDtypeStruct((TOKENS, PB_N), f32),
        compiler_params=_params("parallel"),
        name="in_proj_b",
    )(x, wt, wt)


DSA_TQ = 512
DSA_TK = 512
DSA_NCH = SEQ // DSA_TK
DSA_ONES = 16
KEY_NEG_FLT_MAX = -(2 ** 31) + (1 << 23)


def _key_to_f32(key):
    return lax.bitcast_convert_type(key ^ ((key >> 31) & jnp.int32(0x7FFFFFFF)), f32)


def _dsa_kernel(q_ref, k_ref, vt_ref, qi_ref, kk_ref, wt_ref, kf_ref, qf_ref, o_ref,
                sc_scr, acc_scr, qa_scr):
    i = pl.program_id(1)
    q0 = i * DSA_TQ
    nk = (q0 + DSA_TQ - 1) // DSA_TK + 1

    qpos = q0 + lax.broadcasted_iota(i32, (DSA_TK, DSA_TQ), 1)
    kofs = lax.broadcasted_iota(i32, (DSA_TK, DSA_TQ), 0)
    contract_last = (((1,), (1,)), ((), ()))

    def rows8(x):
        return x.reshape(DSA_TK // 8, 8, DSA_TQ)

    wt = wt_ref[...] * (IDX_DIM ** -0.5 * IDX_HEADS ** -0.5)

    def score_chunk(c, carry):
        r0 = pl.multiple_of(c * DSA_TK, DSA_TK)
        kk0 = kk_ref[0, pl.ds(r0, DSA_TK), :]
        kk1 = kk_ref[1, pl.ds(r0, DSA_TK), :]
        acc = jnp.zeros((DSA_TK, DSA_TQ), f32)
        for p in range(IDX_HEADS // 2):
            slab = qi_ref[:, p * LANES:(p + 1) * LANES]
            l0 = lax.dot_general(kk0, slab, contract_last, preferred_element_type=f32)
            l1 = lax.dot_general(kk1, slab, contract_last, preferred_element_type=f32)
            acc = acc + (jnp.maximum(l0, 0.0) * wt[2 * p:2 * p + 1]
                         + jnp.maximum(l1, 0.0) * wt[2 * p + 1:2 * p + 2])
        sc_scr[c] = jnp.where(kofs + r0 <= qpos, acc, -jnp.inf)
        return carry

    lax.fori_loop(0, nk, score_chunk, 0)

    def count(pred):
        n_acc = 8

        def body(c, accs):
            m = rows8(pred(c))
            accs = list(accs)
            for r in range(DSA_TK // 8):
                accs[r % n_acc] = jnp.where(m[r], accs[r % n_acc] + 1, accs[r % n_acc])
            return tuple(accs)
        accs = lax.fori_loop(0, nk, body, (jnp.zeros((8, DSA_TQ), i32),) * n_acc)
        acc = functools.reduce(lambda a, b: a + b, accs)
        return jnp.sum(acc, axis=0, keepdims=True)

    def count_ge(cand):
        return count(lambda c: sc_scr[c] >= cand)

    key = jnp.where(count_ge(jnp.zeros((1, DSA_TQ), f32)) >= TOPK, jnp.int32(0), jnp.int32(INT_MIN))

    def bit_step(b, key):
        cand = key | jnp.left_shift(jnp.int32(1), 30 - b)
        return jnp.where(count_ge(_key_to_f32(cand)) >= TOPK, cand, key)

    key = lax.fori_loop(0, 31, bit_step, key)
    thr = _key_to_f32(jnp.maximum(key, jnp.int32(KEY_NEG_FLT_MAX)))

    n_ge = count_ge(thr)

    @pl.when(jnp.max(n_ge) > TOPK)
    def _():
        need = TOPK - count(lambda c: sc_scr[c] > thr)

        def count_eq_below(pos):
            return count(lambda c: (sc_scr[c] == thr) & (kofs + c * DSA_TK < pos))

        def pos_step(b, r):
            cand = r | jnp.left_shift(jnp.int32(1), (SEQ.bit_length() - 2) - b)
            return jnp.where(count_eq_below(cand) < need, cand, r)
        r = lax.fori_loop(0, SEQ.bit_length() - 1, pos_step, jnp.zeros((1, DSA_TQ), i32))
        r = jnp.where(n_ge > TOPK, r, jnp.int32(SEQ))

        def drop(c, carry):
            sc = sc_scr[c]
            sc_scr[c] = jnp.where((sc == thr) & (kofs + c * DSA_TK > r), -jnp.inf, sc)
            return carry
        lax.fori_loop(0, nk, drop, 0)

    acc_scr[...] = jnp.zeros(acc_scr.shape, f32)
    c1 = HEAD_DIM ** -0.5 * LOG2E
    ones = jnp.ones((DSA_ONES, DSA_TK), bf16)

    for h in range(ATT_HEADS):
        qa_scr[h, :, :HEAD_DIM] = q_ref[:, h * HEAD_DIM:(h + 1) * HEAD_DIM]
        qa_scr[h, :, HEAD_DIM:] = jnp.broadcast_to(qf_ref[h, 0:1, :], (DSA_TQ, LANES))

    def attn_chunk(c, ms):
        r0 = pl.multiple_of(c * DSA_TK, DSA_TK)
        madd = jnp.where(sc_scr[c] >= thr, 0.0, -jnp.inf)
        kf = kf_ref[pl.ds(r0, DSA_TK), :]

        def qk(h):
            hd = slice(h * HEAD_DIM, (h + 1) * HEAD_DIM)
            k_aug = jnp.concatenate([k_ref[pl.ds(r0, DSA_TK), hd], kf], axis=1)
            return lax.dot_general(k_aug, qa_scr[h], contract_last, preferred_element_type=f32)

        new_ms = []
        s_next = qk(0)
        for h in range(ATT_HEADS):
            hd = slice(h * HEAD_DIM, (h + 1) * HEAD_DIM)
            s = s_next
            if h + 1 < ATT_HEADS:
                s_next = qk(h + 1)
            t = s * c1 + madd
            m_old = ms[h]
            m_new = jnp.maximum(m_old, jnp.max(t, axis=0, keepdims=True))
            alpha = jnp.exp2(m_old - m_new)
            p = jnp.exp2(t - m_new).astype(bf16)
            vt1 = jnp.concatenate([vt_ref[c, hd, :], ones], axis=0)
            acc_scr[h] = alpha * acc_scr[h] + jnp.dot(vt1, p, preferred_element_type=f32)
            new_ms.append(m_new)
        return tuple(new_ms)

    m0 = jnp.full((1, DSA_TQ), -1e30, f32)
    lax.fori_loop(0, nk, attn_chunk, (m0,) * ATT_HEADS)

    for h in range(ATT_HEADS):
        o_ref[h * HEAD_DIM:(h + 1) * HEAD_DIM, :] = (
            acc_scr[h, :HEAD_DIM, :] / acc_scr[h, HEAD_DIM:HEAD_DIM + 1, :]).astype(o_ref.dtype)


def _alibi_features():
    assert ATT_HEADS == 8 and HEAD_DIM == 128 and SEQ <= 64 * 64
    pieces, rest = [], math.sqrt(2.0)
    for _ in range(7):
        p = float(np.asarray(rest, np.float32).astype(jnp.bfloat16).astype(np.float64))
        pieces.append(p)
        rest -= p
    qf = np.zeros((ATT_HEADS, BF16_ROWS, LANES), np.float32)
    for h in range(ATT_HEADS):
        for i, p in enumerate(pieces):
            qf[h, :, 2 * i] = qf[h, :, 2 * i + 1] = p * 2.0 ** (2 - h)
    pos = np.arange(SEQ)
    kf = np.zeros((SEQ, LANES), np.float32)
    for i in range(len(pieces)):
        kf[:, 2 * i] = pos - pos % 64
        kf[:, 2 * i + 1] = pos % 64
    return jnp.asarray(kf, bf16), jnp.asarray(qf, bf16)


def _dsa(za, vt, kk, wt_idx):
    nq = SEQ // DSA_TQ
    kf, qf = _alibi_features()
    once = pl.Buffered(1)
    return pl.pallas_call(
        _dsa_kernel,
        grid=(BATCH, nq),
        in_specs=[
            pl.BlockSpec((DSA_TQ, ATT_WIDTH), lambda b, i: (b * nq + i, 0)),
            pl.BlockSpec((SEQ, ATT_WIDTH), lambda b, i: (b, 1), pipeline_mode=once),
            pl.BlockSpec((None, DSA_NCH, ATT_WIDTH, DSA_TK), lambda b, i: (b, 0, 0, 0),
                         pipeline_mode=once),
            pl.BlockSpec((DSA_TQ, IDX_HEADS * IDX_DIM), lambda b, i: (b * nq + i, 3)),
            pl.BlockSpec((2, SEQ, LANES), lambda b, i: (0, b, 0), pipeline_mode=once),
            pl.BlockSpec((IDX_HEADS, DSA_TQ), lambda b, i: (0, b * nq + i)),
            pl.BlockSpec((SEQ, LANES), lambda b, i: (0, 0), pipeline_mode=once),
            pl.BlockSpec((ATT_HEADS, BF16_ROWS, LANES), lambda b, i: (0, 0, 0)),
        ],
        out_specs=pl.BlockSpec((None, ATT_WIDTH, DSA_TQ), lambda b, i: (b, 0, i)),
        out_shape=jax.ShapeDtypeStruct((BATCH, ATT_WIDTH, SEQ), bf16),
        scratch_shapes=[
            pltpu.VMEM((DSA_NCH, DSA_TK, DSA_TQ), f32),
            pltpu.VMEM((ATT_HEADS, HEAD_DIM + DSA_ONES, DSA_TQ), f32),
            pltpu.VMEM((ATT_HEADS, DSA_TQ, HEAD_DIM + LANES), bf16),
        ],
        compiler_params=_params("arbitrary", "arbitrary"),
        name="dsa",
    )(za, za, vt, za, kk, wt_idx, kf, qf)


def _pool_kernel(u_ref, wp_ref, ps_ref, o_ref):
    row = lax.broadcasted_iota(i32, (SEQ, POOL_GROUP), 0)
    for g, win in enumerate(POOL_WINDOWS):
        x = u_ref[:, g * POOL_GROUP:(g + 1) * POOL_GROUP]
        s = x
        k = 1
        while k < win:
            s = s + jnp.where(row >= k, pltpu.roll(s, k, axis=0), 0.0)
            k *= 2
        cnt = jnp.minimum(row + 1, win).astype(f32)
        pooled = (s / cnt - x).astype(bf16)
        mixed = jnp.dot(pooled, wp_ref[g], preferred_element_type=f32)
        o_ref[:, g * POOL_GROUP:(g + 1) * POOL_GROUP] = (
            mixed * ps_ref[:, g * POOL_GROUP:(g + 1) * POOL_GROUP]).astype(o_ref.dtype)


def _pool(zb, w_pool, pool_scale):
    return pl.pallas_call(
        _pool_kernel,
        grid=(BATCH,),
        in_specs=[
            pl.BlockSpec((SEQ, POOL_WIDTH), lambda b: (b, 0)),
            pl.BlockSpec((N_POOL, POOL_GROUP, POOL_GROUP), lambda b: (0, 0, 0)),
            pl.BlockSpec((1, POOL_WIDTH), lambda b: (0, 0)),
        ],
        out_specs=pl.BlockSpec((SEQ, POOL_WIDTH), lambda b: (b, 0)),
        out_shape=jax.ShapeDtypeStruct((TOKENS, POOL_WIDTH), bf16),
        compiler_params=_params("parallel"),
        name="pool",
    )(zb, w_pool, pool_scale)


MEM_TQ = 512


def _mem_attn_kernel(q_ref, k_ref, v_ref, o_ref):
    for h in range(MEM_HEADS):
        sl = slice(h * HEAD_DIM, (h + 1) * HEAD_DIM)
        s = lax.dot_general(q_ref[:, sl].astype(bf16), k_ref[:, sl], (((1,), (1,)), ((), ())),
                            preferred_element_type=f32) * (HEAD_DIM ** -0.5)
        p = jnp.exp(s - jnp.max(s, axis=1, keepdims=True))
        l = jnp.sum(p, axis=1, keepdims=True)
        o = jnp.dot(p.astype(bf16), v_ref[:, sl], preferred_element_type=f32)
        o_ref[:, sl] = (o / l).astype(o_ref.dtype)


def _mem_attn(zb, qcol, kv):
    nq = SEQ // MEM_TQ
    return pl.pallas_call(
        _mem_attn_kernel,
        grid=(BATCH, nq),
        in_specs=[
            pl.BlockSpec((MEM_TQ, MEM_WIDTH), lambda b, i: (b * nq + i, qcol)),
            pl.BlockSpec((MEM_LEN, MEM_WIDTH), lambda b, i: (b, 0)),
            pl.BlockSpec((MEM_LEN, MEM_WIDTH), lambda b, i: (b, 1)),
        ],
        out_specs=pl.BlockSpec((MEM_TQ, MEM_WIDTH), lambda b, i: (b * nq + i, 0)),
        out_shape=jax.ShapeDtypeStruct((TOKENS, MEM_WIDTH), bf16),
        compiler_params=_params("parallel", "parallel"),
        name="mem_attn",
    )(zb, kv, kv)


CMB_TM = 512
CMB_TC = 512


def _combine_ln_kernel(h_ref, a_ref, p_ref, m_ref, wg0_ref, wg1_ref, wg2_ref,
                       bg0_ref, bg1_ref, bg2_ref, wa_ref, wp_ref, wm_ref, wo_ref,
                       g_ref, b_ref, o_ref, hb_ref, acc_ref):
    i = pl.program_id(0)
    j = pl.program_id(1)
    n_tiles = pl.num_programs(0) - 1
    slot = i % 2
    ln_rows = CMB_TM // (D_MODEL // CMB_TC)

    @pl.when((i == 0) & (j == 0))
    def _():
        acc_ref[1] = jnp.zeros(acc_ref.shape[1:], f32)

    @pl.when((i < n_tiles) & (j == 0))
    def _():
        h = h_ref[...]
        hb_ref[...] = h.astype(bf16)
        acc_ref[slot] = ALPHA * h

    def layer_norm_slab():
        rows = pl.ds(pl.multiple_of(j * ln_rows, ln_rows), ln_rows)
        o_ref[rows, :] = _layer_norm(acc_ref[1 - slot, rows, :], g_ref[...], b_ref[...])

    def matmuls():
        hb = hb_ref[...]

        def gate(wg_ref, bg_ref):
            logits = lax.dot_general(hb, wg_ref[...], _NT, preferred_element_type=f32)
            return jax.nn.sigmoid(logits + bg_ref[...])

        y = gate(wg0_ref, bg0_ref) * jnp.dot(a_ref[...], wa_ref[...], preferred_element_type=f32)
        y = y + gate(wg1_ref, bg1_ref) * jnp.dot(p_ref[...], wp_ref[...], preferred_element_type=f32)
        y = y + gate(wg2_ref, bg2_ref) * jnp.dot(m_ref[...], wm_ref[...], preferred_element_type=f32)
        acc_ref[slot] += jnp.dot(y.astype(bf16), wo_ref[...], preferred_element_type=f32)

    @pl.when(i < n_tiles)
    def _():
        layer_norm_slab()
        matmuls()

    @pl.when(i == n_tiles)
    def _():
        layer_norm_slab()


def _combine_ln(h, a, p, m, wt_gate, row_gate, b_gate, w_a, w_p, w_m, w_out, g, b):
    nc = D_MODEL // CMB_TC
    gate_rows = pl.Element(CMB_TC)
    all_cols = pl.Element(D_MODEL)
    assert row_gate % BF16_ROWS == 0

    def gate_row(blk):
        return pl.multiple_of(row_gate + blk * CMB_TC, BF16_ROWS)

    ni = TOKENS // CMB_TM

    def cj(i, j):
        return jnp.where(i < ni, j, nc - 1)

    row = lambda i, j: (jnp.minimum(i, ni - 1), 0)
    col = lambda i, j: (0, cj(i, j))
    return pl.pallas_call(
        _combine_ln_kernel,
        grid=(ni + 1, nc),
        in_specs=[
            pl.BlockSpec((CMB_TM, D_MODEL), row),
            pl.BlockSpec((CMB_TM, ATT_WIDTH), row),
            pl.BlockSpec((CMB_TM, POOL_WIDTH), row),
            pl.BlockSpec((CMB_TM, MEM_WIDTH), row),
            pl.BlockSpec((gate_rows, all_cols), lambda i, j: (gate_row(cj(i, j)), 0)),
            pl.BlockSpec((gate_rows, all_cols), lambda i, j: (gate_row(cj(i, j) + nc), 0)),
            pl.BlockSpec((gate_rows, all_cols), lambda i, j: (gate_row(cj(i, j) + 2 * nc), 0)),
            pl.BlockSpec((1, CMB_TC), lambda i, j: (0, cj(i, j))),
            pl.BlockSpec((1, CMB_TC), lambda i, j: (0, cj(i, j) + nc)),
            pl.BlockSpec((1, CMB_TC), lambda i, j: (0, cj(i, j) + 2 * nc)),
            pl.BlockSpec((ATT_WIDTH, CMB_TC), col),
            pl.BlockSpec((POOL_WIDTH, CMB_TC), col),
            pl.BlockSpec((MEM_WIDTH, CMB_TC), col),
            pl.BlockSpec((CMB_TC, D_MODEL), lambda i, j: (cj(i, j), 0)),
            pl.BlockSpec((1, D_MODEL), lambda i, j: (0, 0)),
            pl.BlockSpec((1, D_MODEL), lambda i, j: (0, 0)),
        ],
        out_specs=pl.BlockSpec((CMB_TM, D_MODEL), lambda i, j: (jnp.maximum(i - 1, 0), 0)),
        out_shape=jax.ShapeDtypeStruct((TOKENS, D_MODEL), f32),
        scratch_shapes=[pltpu.VMEM((CMB_TM, D_MODEL), bf16),
                        pltpu.VMEM((2, CMB_TM, D_MODEL), f32)],
        compiler_params=_params("arbitrary", "arbitrary"),
        name="combine_ln",
    )(h, a, p, m, wt_gate, wt_gate, wt_gate, b_gate, b_gate, b_gate, w_a, w_p, w_m, w_out, g, b)


def kernel(x, mem, w_ffn1_up, w_ffn1_down, ln1_g, ln1_b, w_in, b_gate, w_mem_kv, w_pool,
           pool_scale, w_br_att, w_br_pool, w_br_mem, w_out, ln2_g, ln2_b, w_ffn2_up,
           w_ffn2_down, ln3_g, ln3_b):
    h = x.reshape(TOKENS, D_MODEL)
    memf = mem.reshape(BATCH * MEM_LEN, D_MODEL)
    for l in range(DEPTH):
        c_qi = 3 * ATT_WIDTH + IDX_HEADS * IDX_DIM
        c_wi = c_qi + IDX_DIM + IDX_HEADS
        c_qm = c_wi + POOL_WIDTH + MEM_WIDTH

        h, (wt, w2_down) = _ffn_ln(
            h, w_ffn1_up[l].T.astype(bf16), w_ffn1_down[l].astype(bf16), ln1_g[l][None], ln1_b[l][None],
            casts=((w_in[l].T, True), (w_ffn2_down[l], True)))

        za = _matmul(h, wt, bf16, 1024, 1024, "in_proj_a", w_is_transposed=True, n=c_qi)
        zb = _in_proj_b(h, wt, c_wi, c_qi)
        kv = _matmul(memf, w_mem_kv[l].astype(bf16), bf16, BATCH * MEM_LEN, 512, "mem_kv")

        c_ki = POOL_WIDTH + MEM_WIDTH
        ki = zb[:, c_ki:c_ki + IDX_DIM].astype(bf16)
        zk = jnp.zeros_like(ki)
        kk = jnp.stack([jnp.concatenate([ki, zk], axis=1), jnp.concatenate([zk, ki], axis=1)])
        wt_idx = zb[:, c_ki + IDX_DIM:c_ki + IDX_DIM + IDX_HEADS].T
        vt = za[:, 2 * ATT_WIDTH:3 * ATT_WIDTH].reshape(BATCH, DSA_NCH, DSA_TK, ATT_WIDTH)
        vt = vt.transpose(0, 1, 3, 2)

        a = _dsa(za, vt, kk, wt_idx).transpose(0, 2, 1).reshape(TOKENS, ATT_WIDTH)
        p = _pool(zb, w_pool[l].astype(bf16), pool_scale[l][None])
        m = _mem_attn(zb, POOL_WIDTH // MEM_WIDTH, kv)

        h = _combine_ln(h, a, p, m, wt, c_qm, b_gate[l][None], w_br_att[l].astype(bf16),
                        w_br_pool[l].astype(bf16), w_br_mem[l].astype(bf16),
                        w_out[l].astype(bf16), ln2_g[l][None], ln2_b[l][None])

        h, _ = _ffn_ln(h, w_ffn2_up[l].T.astype(bf16), w2_down, ln3_g[l][None], ln3_b[l][None])
    return h.reshape(BATCH, SEQ, D_MODEL)
```

```python
import functools
import math

import jax
import jax.numpy as jnp
import numpy as np
from jax import lax
from jax.experimental import pallas as pl
from jax.experimental.pallas import tpu as pltpu

f32 = jnp.float32
bf16 = jnp.bfloat16
i32 = jnp.int32

D_MODEL = 2048
BATCH = 2
SEQ = 4096
DEPTH = 1
MEM_LEN = 256
ATT_HEADS = 8
HEAD_DIM = 128
ATT_WIDTH = ATT_HEADS * HEAD_DIM
IDX_HEADS = 16
IDX_DIM = 64
TOPK = min(256, SEQ // 4)
POOL_WINDOWS = (2, 4, 8, 16)
N_POOL = len(POOL_WINDOWS)
POOL_GROUP = 128
POOL_WIDTH = N_POOL * POOL_GROUP
MEM_HEADS = 4
MEM_WIDTH = MEM_HEADS * HEAD_DIM
N_BRANCH = 3
D_FF = 5632
ALPHA = (2 * DEPTH) ** 0.25
LN_EPS = 1e-5
TOKENS = BATCH * SEQ

LANES = 128
BF16_ROWS = 16
VMEM_LIMIT = 60 * 1024 * 1024

INT_MIN = -(2 ** 31)
LOG2E = math.log2(math.e)


def _params(*sem):
    return pltpu.CompilerParams(dimension_semantics=sem, vmem_limit_bytes=VMEM_LIMIT)


def _layer_norm(y, g, b):
    mu = jnp.mean(y, axis=-1, keepdims=True)
    d = y - mu
    var = jnp.mean(d * d, axis=-1, keepdims=True)
    return d * lax.rsqrt(var + LN_EPS) * g + b


FFN_TM = 512
FFN_TF = 512


FFN_LN_STEPS = 8
FFN_LN_ROWS = FFN_TM // FFN_LN_STEPS


def _ffn_ln_kernel(x_ref, wa_ref, wu_ref, wd_ref, g_ref, b_ref, *rest, n_cast):
    cast_in = rest[:n_cast]
    o_ref = rest[n_cast]
    cast_out = rest[n_cast + 1:2 * n_cast + 1]
    xb_ref, acc_ref = rest[2 * n_cast + 1:]
    i = pl.program_id(0)
    j = pl.program_id(1)
    n_tiles = pl.num_programs(0) - 1
    slot = i % 2

    for src, dst in zip(cast_in, cast_out):
        dst[...] = src[...].astype(dst.dtype)

    @pl.when((i == 0) & (j == 0))
    def _():
        acc_ref[1] = jnp.zeros(acc_ref.shape[1:], f32)

    @pl.when((i < n_tiles) & (j == 0))
    def _():
        x = x_ref[...]
        xb_ref[...] = x.astype(bf16)
        acc_ref[slot] = (2.0 * ALPHA) * x

    def matmuls():
        xb = xb_ref[...]
        a = jnp.dot(xb, wa_ref[...], preferred_element_type=f32)
        u = jnp.dot(xb, wu_ref[...], preferred_element_type=f32)
        act = (a * jax.nn.sigmoid(a) * u).astype(bf16)
        acc_ref[slot] += jnp.dot(act, wd_ref[...], preferred_element_type=f32)

    def layer_norm_slab():
        rows = pl.ds(pl.multiple_of(j * FFN_LN_ROWS, FFN_LN_ROWS), FFN_LN_ROWS)
        z = acc_ref[1 - slot, rows, :]
        mu = jnp.mean(z, axis=-1, keepdims=True)
        d = z - mu
        var = jnp.mean(d * d, axis=-1, keepdims=True)
        o_ref[rows, :] = d * lax.rsqrt(var + 4.0 * LN_EPS) * g_ref[...] + b_ref[...]

    @pl.when((i < n_tiles) & (j < FFN_LN_STEPS))
    def _():
        layer_norm_slab()
        matmuls()

    @pl.when((i < n_tiles) & (j >= FFN_LN_STEPS))
    def _():
        matmuls()

    @pl.when((i == n_tiles) & (j < FFN_LN_STEPS))
    def _():
        layer_norm_slab()


def _cast_block_spec(shape, rows_on_j, ni, nj):
    r, c = shape
    n_r, n_c = (nj, ni) if rows_on_j else (ni, nj)
    br = -(-r // n_r)
    br = -(-br // BF16_ROWS) * BF16_ROWS
    assert c % n_c == 0 and (c // n_c) % LANES == 0
    def clamp(i, j):
        return jnp.minimum(i, ni - 1), jnp.where(i < ni, j, nj - 1)

    if rows_on_j:
        index_map = lambda i, j: clamp(i, j)[::-1]
    else:
        index_map = lambda i, j: clamp(i, j)
    return pl.BlockSpec((br, c // n_c), index_map)


def _ffn_ln(x, w_up, w_down, g, b, casts=()):
    ni = TOKENS // FFN_TM
    nf = D_FF // FFN_TF
    assert FFN_LN_STEPS <= nf
    cast_specs = [_cast_block_spec(w.shape, rows_on_j, ni, nf) for w, rows_on_j in casts]

    def wj(i, j):
        return jnp.where(i < ni, j, nf - 1)

    outs = pl.pallas_call(
        functools.partial(_ffn_ln_kernel, n_cast=len(casts)),
        grid=(ni + 1, nf),
        in_specs=[
            pl.BlockSpec((FFN_TM, D_MODEL), lambda i, j: (jnp.minimum(i, ni - 1), 0)),
            pl.BlockSpec((D_MODEL, FFN_TF), lambda i, j: (0, wj(i, j))),
            pl.BlockSpec((D_MODEL, FFN_TF), lambda i, j: (0, wj(i, j) + nf)),
            pl.BlockSpec((FFN_TF, D_MODEL), lambda i, j: (wj(i, j), 0)),
            pl.BlockSpec((1, D_MODEL), lambda i, j: (0, 0)),
            pl.BlockSpec((1, D_MODEL), lambda i, j: (0, 0)),
        ] + cast_specs,
        out_specs=[pl.BlockSpec((FFN_TM, D_MODEL), lambda i, j: (jnp.maximum(i - 1, 0), 0))]
        + cast_specs,
        out_shape=[jax.ShapeDtypeStruct((TOKENS, D_MODEL), f32)]
        + [jax.ShapeDtypeStruct(w.shape, bf16) for w, _ in casts],
        scratch_shapes=[pltpu.VMEM((FFN_TM, D_MODEL), bf16),
                        pltpu.VMEM((2, FFN_TM, D_MODEL), f32)],
        compiler_params=_params("arbitrary", "arbitrary"),
        name="ffn_ln",
    )(x, w_up, w_up, w_down, g, b, *[w for w, _ in casts])
    return outs[0], outs[1:]


_NT = (((1,), (1,)), ((), ()))


def _matmul_kernel(x_ref, w_ref, o_ref, xb_ref, *, w_is_transposed):
    @pl.when(pl.program_id(1) == 0)
    def _():
        xb_ref[...] = x_ref[...].astype(bf16)

    if w_is_transposed:
        y = lax.dot_general(xb_ref[...], w_ref[...], _NT, preferred_element_type=f32)
    else:
        y = jnp.dot(xb_ref[...], w_ref[...], preferred_element_type=f32)
    o_ref[...] = y.astype(o_ref.dtype)


def _matmul(x, w, out_dtype, tm, tn, name, w_is_transposed=False, n=None):
    m, k = x.shape
    if n is None:
        n = w.shape[0] if w_is_transposed else w.shape[1]
    assert n % tn == 0 and m % tm == 0
    if w_is_transposed:
        w_spec = pl.BlockSpec((tn, k), lambda i, j: (j, 0))
    else:
        w_spec = pl.BlockSpec((k, tn), lambda i, j: (0, j))
    return pl.pallas_call(
        functools.partial(_matmul_kernel, w_is_transposed=w_is_transposed),
        grid=(m // tm, n // tn),
        in_specs=[pl.BlockSpec((tm, k), lambda i, j: (i, 0)), w_spec],
        out_specs=pl.BlockSpec((tm, tn), lambda i, j: (i, j)),
        out_shape=jax.ShapeDtypeStruct((m, n), out_dtype),
        scratch_shapes=[pltpu.VMEM((tm, k), bf16)],
        compiler_params=_params("parallel", "arbitrary"),
        name=name,
    )(x, w)


PB_TM = 512
PB_N = POOL_WIDTH + MEM_WIDTH + LANES


def _in_proj_b_kernel(x_ref, wum_ref, wk_ref, o_ref):
    xb = x_ref[...].astype(bf16)
    n_um = POOL_WIDTH + MEM_WIDTH
    o_ref[:, :n_um] = lax.dot_general(xb, wum_ref[...], _NT, preferred_element_type=f32)
    o_ref[:, n_um:] = lax.dot_general(xb, wk_ref[...], _NT, preferred_element_type=f32)


def _in_proj_b(x, wt, row_um, row_kiwi):
    n_um = POOL_WIDTH + MEM_WIDTH
    return pl.pallas_call(
        _in_proj_b_kernel,
        grid=(TOKENS // PB_TM,),
        in_specs=[
            pl.BlockSpec((PB_TM, D_MODEL), lambda i: (i, 0)),
            pl.BlockSpec((pl.Element(n_um), pl.Element(D_MODEL)), lambda i: (row_um, 0)),
            pl.BlockSpec((pl.Element(LANES), pl.Element(D_MODEL)), lambda i: (row_kiwi, 0)),
        ],
        out_specs=pl.BlockSpec((PB_TM, PB_N), lambda i: (i, 0)),
        out_shape=jax.ShapeDtypeStruct((TOKENS, PB_N), f32),
        compiler_params=_params("parallel"),
        name="in_proj_b",
    )(x, wt, wt)


DSA_TQ = 512
DSA_TK = 512
DSA_NCH = SEQ // DSA_TK
DSA_ONES = 16
KEY_NEG_FLT_MAX = -(2 ** 31) + (1 << 23)


def _key_to_f32(key):
    return lax.bitcast_convert_type(key ^ ((key >> 31) & jnp.int32(0x7FFFFFFF)), f32)


def _dsa_kernel(q_ref, k_ref, vt_ref, qi_ref, kk_ref, wt_ref, kf_ref, qf_ref, o_ref,
                sc_scr, sch_scr, acc_scr, qa_scr):
    i = pl.program_id(1)
    q0 = i * DSA_TQ
    nk = (q0 + DSA_TQ - 1) // DSA_TK + 1

    qpos = q0 + lax.broadcasted_iota(i32, (DSA_TK, DSA_TQ), 1)
    kofs = lax.broadcasted_iota(i32, (DSA_TK, DSA_TQ), 0)
    contract_last = (((1,), (1,)), ((), ()))

    def rows8(x):
        return x.reshape(DSA_TK // 8, 8, DSA_TQ)

    wt = wt_ref[...] * (IDX_DIM ** -0.5 * IDX_HEADS ** -0.5)

    def score_chunk(c, carry):
        r0 = pl.multiple_of(c * DSA_TK, DSA_TK)
        kk0 = kk_ref[0, pl.ds(r0, DSA_TK), :]
        kk1 = kk_ref[1, pl.ds(r0, DSA_TK), :]
        acc = jnp.zeros((DSA_TK, DSA_TQ), f32)
        for p in range(IDX_HEADS // 2):
            slab = qi_ref[:, p * LANES:(p + 1) * LANES]
            l0 = lax.dot_general(kk0, slab, contract_last, preferred_element_type=f32)
            l1 = lax.dot_general(kk1, slab, contract_last, preferred_element_type=f32)
            acc = acc + (jnp.maximum(l0, 0.0) * wt[2 * p:2 * p + 1]
                         + jnp.maximum(l1, 0.0) * wt[2 * p + 1:2 * p + 2])
        sc = jnp.where(kofs + r0 <= qpos, acc, -jnp.inf)
        sc_scr[c] = sc
        sch_scr[c] = sc.astype(bf16)
        return carry

    lax.fori_loop(0, nk, score_chunk, 0)

    def count(pred):
        n_acc = 8

        def body(c, accs):
            m = rows8(pred(c))
            accs = list(accs)
            for r in range(DSA_TK // 8):
                accs[r % n_acc] = jnp.where(m[r], accs[r % n_acc] + 1, accs[r % n_acc])
            return tuple(accs)
        accs = lax.fori_loop(0, nk, body, (jnp.zeros((8, DSA_TQ), i32),) * n_acc)
        acc = functools.reduce(lambda a, b: a + b, accs)
        return jnp.sum(acc, axis=0, keepdims=True)

    def count_ge(cand):
        return count(lambda c: sc_scr[c] >= cand)

    def count_ge_coarse(cand):
        n_acc = 2
        one = jnp.ones((), bf16)
        zero = jnp.zeros((), bf16)

        def body(c, accs):
            accs = list(accs)
            for r in range(DSA_TK // BF16_ROWS):
                rows = sch_scr[c, r * BF16_ROWS:(r + 1) * BF16_ROWS, :]
                accs[r % n_acc] = accs[r % n_acc] + jnp.where(rows >= cand, one, zero)
            return tuple(accs)
        accs = lax.fori_loop(0, nk, body, (jnp.zeros((BF16_ROWS, DSA_TQ), bf16),) * n_acc)
        acc = functools.reduce(lambda a, b: a + b, accs)
        return jnp.sum(acc.astype(f32), axis=0, keepdims=True)

    def coarse_cand(key):
        bits = lax.bitcast_convert_type(_key_to_f32(key), i32) & jnp.int32(-65536)
        return lax.bitcast_convert_type(bits, f32).astype(bf16)

    key = jnp.where(count_ge_coarse(jnp.zeros((1, DSA_TQ), bf16)) >= TOPK,
                    jnp.int32(0), jnp.int32(INT_MIN))

    def coarse_step(b, key):
        cand = key | jnp.left_shift(jnp.int32(1), 30 - b)
        return jnp.where(count_ge_coarse(coarse_cand(cand)) >= TOPK, cand, key)

    key = lax.fori_loop(0, 15, coarse_step, key)

    window_bits = 18
    lo = jnp.maximum(key, jnp.int32(INT_MIN + (1 << 17))) - jnp.int32((1 << 15) + 2)

    def fine_step(b, off):
        cand = off | jnp.left_shift(jnp.int32(1), (window_bits - 1) - b)
        return jnp.where(count_ge(_key_to_f32(lo + cand)) >= TOPK, cand, off)

    key = lo + lax.fori_loop(0, window_bits, fine_step, jnp.zeros((1, DSA_TQ), i32))
    thr = _key_to_f32(jnp.maximum(key, jnp.int32(KEY_NEG_FLT_MAX)))

    n_ge = count_ge(thr)

    @pl.when(jnp.max(n_ge) > TOPK)
    def _():
        need = TOPK - count(lambda c: sc_scr[c] > thr)

        def count_eq_below(pos):
            return count(lambda c: (sc_scr[c] == thr) & (kofs + c * DSA_TK < pos))

        def pos_step(b, r):
            cand = r | jnp.left_shift(jnp.int32(1), (SEQ.bit_length() - 2) - b)
            return jnp.where(count_eq_below(cand) < need, cand, r)
        r = lax.fori_loop(0, SEQ.bit_length() - 1, pos_step, jnp.zeros((1, DSA_TQ), i32))
        r = jnp.where(n_ge > TOPK, r, jnp.int32(SEQ))

        def drop(c, carry):
            sc = sc_scr[c]
            sc_scr[c] = jnp.where((sc == thr) & (kofs + c * DSA_TK > r), -jnp.inf, sc)
            return carry
        lax.fori_loop(0, nk, drop, 0)

    acc_scr[...] = jnp.zeros(acc_scr.shape, f32)
    c1 = HEAD_DIM ** -0.5 * LOG2E
    ones = jnp.ones((DSA_ONES, DSA_TK), bf16)

    for h in range(ATT_HEADS):
        qa_scr[h, :, :HEAD_DIM] = q_ref[:, h * HEAD_DIM:(h + 1) * HEAD_DIM]
        qa_scr[h, :, HEAD_DIM:] = jnp.broadcast_to(qf_ref[h, 0:1, :], (DSA_TQ, LANES))

    def attn_chunk(c, ms):
        r0 = pl.multiple_of(c * DSA_TK, DSA_TK)
        madd = jnp.where(sc_scr[c] >= thr, 0.0, -jnp.inf)
        kf = kf_ref[pl.ds(r0, DSA_TK), :]

        def qk(h):
            hd = slice(h * HEAD_DIM, (h + 1) * HEAD_DIM)
            k_aug = jnp.concatenate([k_ref[pl.ds(r0, DSA_TK), hd], kf], axis=1)
            return lax.dot_general(k_aug, qa_scr[h], contract_last, preferred_element_type=f32)

        new_ms = []
        s_next = qk(0)
        for h in range(ATT_HEADS):
            hd = slice(h * HEAD_DIM, (h + 1) * HEAD_DIM)
            s = s_next
            if h + 1 < ATT_HEADS:
                s_next = qk(h + 1)
            t = s * c1 + madd
            m_old = ms[h]
            m_new = jnp.maximum(m_old, jnp.max(t, axis=0, keepdims=True))
            alpha = jnp.exp2(m_old - m_new)
            p = jnp.exp2(t - m_new).astype(bf16)
            vt1 = jnp.concatenate([vt_ref[c, hd, :], ones], axis=0)
            acc_scr[h] = alpha * acc_scr[h] + jnp.dot(vt1, p, preferred_element_type=f32)
            new_ms.append(m_new)
        return tuple(new_ms)

    m0 = jnp.full((1, DSA_TQ), -1e30, f32)
    lax.fori_loop(0, nk, attn_chunk, (m0,) * ATT_HEADS)

    for h in range(ATT_HEADS):
        o_ref[h * HEAD_DIM:(h + 1) * HEAD_DIM, :] = (
            acc_scr[h, :HEAD_DIM, :] / acc_scr[h, HEAD_DIM:HEAD_DIM + 1, :]).astype(o_ref.dtype)


def _alibi_features():
    assert ATT_HEADS == 8 and HEAD_DIM == 128 and SEQ <= 64 * 64
    pieces, rest = [], math.sqrt(2.0)
    for _ in range(7):
        p = float(np.asarray(rest, np.float32).astype(jnp.bfloat16).astype(np.float64))
        pieces.append(p)
        rest -= p
    qf = np.zeros((ATT_HEADS, BF16_ROWS, LANES), np.float32)
    for h in range(ATT_HEADS):
        for i, p in enumerate(pieces):
            qf[h, :, 2 * i] = qf[h, :, 2 * i + 1] = p * 2.0 ** (2 - h)
    pos = np.arange(SEQ)
    kf = np.zeros((SEQ, LANES), np.float32)
    for i in range(len(pieces)):
        kf[:, 2 * i] = pos - pos % 64
        kf[:, 2 * i + 1] = pos % 64
    return jnp.asarray(kf, bf16), jnp.asarray(qf, bf16)


def _dsa(za, vt, kk, wt_idx):
    nq = SEQ // DSA_TQ
    kf, qf = _alibi_features()
    once = pl.Buffered(1)
    return pl.pallas_call(
        _dsa_kernel,
        grid=(BATCH, nq),
        in_specs=[
            pl.BlockSpec((DSA_TQ, ATT_WIDTH), lambda b, i: (b * nq + i, 0)),
            pl.BlockSpec((SEQ, ATT_WIDTH), lambda b, i: (b, 1), pipeline_mode=once),
            pl.BlockSpec((None, DSA_NCH, ATT_WIDTH, DSA_TK), lambda b, i: (b, 0, 0, 0),
                         pipeline_mode=once),
            pl.BlockSpec((DSA_TQ, IDX_HEADS * IDX_DIM), lambda b, i: (b * nq + i, 3)),
            pl.BlockSpec((2, SEQ, LANES), lambda b, i: (0, b, 0), pipeline_mode=once),
            pl.BlockSpec((IDX_HEADS, DSA_TQ), lambda b, i: (0, b * nq + i)),
            pl.BlockSpec((SEQ, LANES), lambda b, i: (0, 0), pipeline_mode=once),
            pl.BlockSpec((ATT_HEADS, BF16_ROWS, LANES), lambda b, i: (0, 0, 0)),
        ],
        out_specs=pl.BlockSpec((None, ATT_WIDTH, DSA_TQ), lambda b, i: (b, 0, i)),
        out_shape=jax.ShapeDtypeStruct((BATCH, ATT_WIDTH, SEQ), bf16),
        scratch_shapes=[
            pltpu.VMEM((DSA_NCH, DSA_TK, DSA_TQ), f32),
            pltpu.VMEM((DSA_NCH, DSA_TK, DSA_TQ), bf16),
            pltpu.VMEM((ATT_HEADS, HEAD_DIM + DSA_ONES, DSA_TQ), f32),
            pltpu.VMEM((ATT_HEADS, DSA_TQ, HEAD_DIM + LANES), bf16),
        ],
        compiler_params=_params("arbitrary", "arbitrary"),
        name="dsa",
    )(za, za, vt, za, kk, wt_idx, kf, qf)


def _pool_kernel(u_ref, wp_ref, ps_ref, o_ref):
    row = lax.broadcasted_iota(i32, (SEQ, POOL_GROUP), 0)
    for g, win in enumerate(POOL_WINDOWS):
        x = u_ref[:, g * POOL_GROUP:(g + 1) * POOL_GROUP]
        s = x
        k = 1
        while k < win:
            s = s + jnp.where(row >= k, pltpu.roll(s, k, axis=0), 0.0)
            k *= 2
        cnt = jnp.minimum(row + 1, win).astype(f32)
        pooled = (s / cnt - x).astype(bf16)
        mixed = jnp.dot(pooled, wp_ref[g], preferred_element_type=f32)
        o_ref[:, g * POOL_GROUP:(g + 1) * POOL_GROUP] = (
            mixed * ps_ref[:, g * POOL_GROUP:(g + 1) * POOL_GROUP]).astype(o_ref.dtype)


def _pool(zb, w_pool, pool_scale):
    return pl.pallas_call(
        _pool_kernel,
        grid=(BATCH,),
        in_specs=[
            pl.BlockSpec((SEQ, POOL_WIDTH), lambda b: (b, 0)),
            pl.BlockSpec((N_POOL, POOL_GROUP, POOL_GROUP), lambda b: (0, 0, 0)),
            pl.BlockSpec((1, POOL_WIDTH), lambda b: (0, 0)),
        ],
        out_specs=pl.BlockSpec((SEQ, POOL_WIDTH), lambda b: (b, 0)),
        out_shape=jax.ShapeDtypeStruct((TOKENS, POOL_WIDTH), bf16),
        compiler_params=_params("parallel"),
        name="pool",
    )(zb, w_pool, pool_scale)


MEM_TQ = 512


def _mem_attn_kernel(q_ref, k_ref, v_ref, o_ref):
    for h in range(MEM_HEADS):
        sl = slice(h * HEAD_DIM, (h + 1) * HEAD_DIM)
        s = lax.dot_general(q_ref[:, sl].astype(bf16), k_ref[:, sl], (((1,), (1,)), ((), ())),
                            preferred_element_type=f32) * (HEAD_DIM ** -0.5)
        p = jnp.exp(s - jnp.max(s, axis=1, keepdims=True))
        l = jnp.sum(p, axis=1, keepdims=True)
        o = jnp.dot(p.astype(bf16), v_ref[:, sl], preferred_element_type=f32)
        o_ref[:, sl] = (o / l).astype(o_ref.dtype)


def _mem_attn(zb, qcol, kv):
    nq = SEQ // MEM_TQ
    return pl.pallas_call(
        _mem_attn_kernel,
        grid=(BATCH, nq),
        in_specs=[
            pl.BlockSpec((MEM_TQ, MEM_WIDTH), lambda b, i: (b * nq + i, qcol)),
            pl.BlockSpec((MEM_LEN, MEM_WIDTH), lambda b, i: (b, 0)),
            pl.BlockSpec((MEM_LEN, MEM_WIDTH), lambda b, i: (b, 1)),
        ],
        out_specs=pl.BlockSpec((MEM_TQ, MEM_WIDTH), lambda b, i: (b * nq + i, 0)),
        out_shape=jax.ShapeDtypeStruct((TOKENS, MEM_WIDTH), bf16),
        compiler_params=_params("parallel", "parallel"),
        name="mem_attn",
    )(zb, kv, kv)


CMB_TM = 512
CMB_TC = 512


def _combine_ln_kernel(h_ref, a_ref, p_ref, m_ref, wg0_ref, wg1_ref, wg2_ref,
                       bg0_ref, bg1_ref, bg2_ref, wa_ref, wp_ref, wm_ref, wo_ref,
                       g_ref, b_ref, o_ref, hb_ref, acc_ref):
    i = pl.program_id(0)
    j = pl.program_id(1)
    n_tiles = pl.num_programs(0) - 1
    slot = i % 2
    ln_rows = CMB_TM // (D_MODEL // CMB_TC)

    @pl.when((i == 0) & (j == 0))
    def _():
        acc_ref[1] = jnp.zeros(acc_ref.shape[1:], f32)

    @pl.when((i < n_tiles) & (j == 0))
    def _():
        h = h_ref[...]
        hb_ref[...] = h.astype(bf16)
        acc_ref[slot] = ALPHA * h

    def layer_norm_slab():
        rows = pl.ds(pl.multiple_of(j * ln_rows, ln_rows), ln_rows)
        o_ref[rows, :] = _layer_norm(acc_ref[1 - slot, rows, :], g_ref[...], b_ref[...])

    def matmuls():
        hb = hb_ref[...]

        def gate(wg_ref, bg_ref):
            logits = lax.dot_general(hb, wg_ref[...], _NT, preferred_element_type=f32)
            return jax.nn.sigmoid(logits + bg_ref[...])

        y = gate(wg0_ref, bg0_ref) * jnp.dot(a_ref[...], wa_ref[...], preferred_element_type=f32)
        y = y + gate(wg1_ref, bg1_ref) * jnp.dot(p_ref[...], wp_ref[...], preferred_element_type=f32)
        y = y + gate(wg2_ref, bg2_ref) * jnp.dot(m_ref[...], wm_ref[...], preferred_element_type=f32)
        acc_ref[slot] += jnp.dot(y.astype(bf16), wo_ref[...], preferred_element_type=f32)

    @pl.when(i < n_tiles)
    def _():
        layer_norm_slab()
        matmuls()

    @pl.when(i == n_tiles)
    def _():
        layer_norm_slab()


def _combine_ln(h, a, p, m, wt_gate, row_gate, b_gate, w_a, w_p, w_m, w_out, g, b):
    nc = D_MODEL // CMB_TC
    gate_rows = pl.Element(CMB_TC)
    all_cols = pl.Element(D_MODEL)
    assert row_gate % BF16_ROWS == 0

    def gate_row(blk):
        return pl.multiple_of(row_gate + blk * CMB_TC, BF16_ROWS)

    ni = TOKENS // CMB_TM

    def cj(i, j):
        return jnp.where(i < ni, j, nc - 1)

    row = lambda i, j: (jnp.minimum(i, ni - 1), 0)
    col = lambda i, j: (0, cj(i, j))
    return pl.pallas_call(
        _combine_ln_kernel,
        grid=(ni + 1, nc),
        in_specs=[
            pl.BlockSpec((CMB_TM, D_MODEL), row),
            pl.BlockSpec((CMB_TM, ATT_WIDTH), row),
            pl.BlockSpec((CMB_TM, POOL_WIDTH), row),
            pl.BlockSpec((CMB_TM, MEM_WIDTH), row),
            pl.BlockSpec((gate_rows, all_cols), lambda i, j: (gate_row(cj(i, j)), 0)),
            pl.BlockSpec((gate_rows, all_cols), lambda i, j: (gate_row(cj(i, j) + nc), 0)),
            pl.BlockSpec((gate_rows, all_cols), lambda i, j: (gate_row(cj(i, j) + 2 * nc), 0)),
            pl.BlockSpec((1, CMB_TC), lambda i, j: (0, cj(i, j))),
            pl.BlockSpec((1, CMB_TC), lambda i, j: (0, cj(i, j) + nc)),
            pl.BlockSpec((1, CMB_TC), lambda i, j: (0, cj(i, j) + 2 * nc)),
            pl.BlockSpec((ATT_WIDTH, CMB_TC), col),
            pl.BlockSpec((POOL_WIDTH, CMB_TC), col),
            pl.BlockSpec((MEM_WIDTH, CMB_TC), col),
            pl.BlockSpec((CMB_TC, D_MODEL), lambda i, j: (cj(i, j), 0)),
            pl.BlockSpec((1, D_MODEL), lambda i, j: (0, 0)),
            pl.BlockSpec((1, D_MODEL), lambda i, j: (0, 0)),
        ],
        out_specs=pl.BlockSpec((CMB_TM, D_MODEL), lambda i, j: (jnp.maximum(i - 1, 0), 0)),
        out_shape=jax.ShapeDtypeStruct((TOKENS, D_MODEL), f32),
        scratch_shapes=[pltpu.VMEM((CMB_TM, D_MODEL), bf16),
                        pltpu.VMEM((2, CMB_TM, D_MODEL), f32)],
        compiler_params=_params("arbitrary", "arbitrary"),
        name="combine_ln",
    )(h, a, p, m, wt_gate, wt_gate, wt_gate, b_gate, b_gate, b_gate, w_a, w_p, w_m, w_out, g, b)


def kernel(x, mem, w_ffn1_up, w_ffn1_down, ln1_g, ln1_b, w_in, b_gate, w_mem_kv, w_pool,
           pool_scale, w_br_att, w_br_pool, w_br_mem, w_out, ln2_g, ln2_b, w_ffn2_up,
           w_ffn2_down, ln3_g, ln3_b):
    h = x.reshape(TOKENS, D_MODEL)
    memf = mem.reshape(BATCH * MEM_LEN, D_MODEL)
    for l in range(DEPTH):
        c_qi = 3 * ATT_WIDTH + IDX_HEADS * IDX_DIM
        c_wi = c_qi + IDX_DIM + IDX_HEADS
        c_qm = c_wi + POOL_WIDTH + MEM_WIDTH

        h, (wt, w2_up, w2_down) = _ffn_ln(
            h, w_ffn1_up[l].astype(bf16), w_ffn1_down[l].astype(bf16), ln1_g[l][None], ln1_b[l][None],
            casts=((w_in[l].T, True), (w_ffn2_up[l], False), (w_ffn2_down[l], True)))

        za = _matmul(h, wt, bf16, 1024, 1024, "in_proj_a", w_is_transposed=True, n=c_qi)
        zb = _in_proj_b(h, wt, c_wi, c_qi)
        kv = _matmul(memf, w_mem_kv[l].astype(bf16), bf16, BATCH * MEM_LEN, 512, "mem_kv")

        c_ki = POOL_WIDTH + MEM_WIDTH
        ki = zb[:, c_ki:c_ki + IDX_DIM].astype(bf16)
        zk = jnp.zeros_like(ki)
        kk = jnp.stack([jnp.concatenate([ki, zk], axis=1), jnp.concatenate([zk, ki], axis=1)])
        wt_idx = zb[:, c_ki + IDX_DIM:c_ki + IDX_DIM + IDX_HEADS].T
        vt = za[:, 2 * ATT_WIDTH:3 * ATT_WIDTH].reshape(BATCH, DSA_NCH, DSA_TK, ATT_WIDTH)
        vt = vt.transpose(0, 1, 3, 2)

        a = _dsa(za, vt, kk, wt_idx).transpose(0, 2, 1).reshape(TOKENS, ATT_WIDTH)
        p = _pool(zb, w_pool[l].astype(bf16), pool_scale[l][None])
        m = _mem_attn(zb, POOL_WIDTH // MEM_WIDTH, kv)

        h = _combine_ln(h, a, p, m, wt, c_qm, b_gate[l][None], w_br_att[l].astype(bf16),
                        w_br_pool[l].astype(bf16), w_br_mem[l].astype(bf16),
                        w_out[l].astype(bf16), ln2_g[l][None], ln2_b[l][None])

        h, _ = _ffn_ln(h, w2_up, w2_down, ln3_g[l][None], ln3_b[l][None])
    return h.reshape(BATCH, SEQ, D_MODEL)
```

```python
import functools
import math

import jax
import jax.numpy as jnp
import numpy as np
from jax import lax
from jax.experimental import pallas as pl
from jax.experimental.pallas import tpu as pltpu

f32 = jnp.float32
bf16 = jnp.bfloat16
i32 = jnp.int32

D_MODEL = 2048
BATCH = 2
SEQ = 4096
DEPTH = 1
MEM_LEN = 256
ATT_HEADS = 8
HEAD_DIM = 128
ATT_WIDTH = ATT_HEADS * HEAD_DIM
IDX_HEADS = 16
IDX_DIM = 64
TOPK = min(256, SEQ // 4)
POOL_WINDOWS = (2, 4, 8, 16)
N_POOL = len(POOL_WINDOWS)
POOL_GROUP = 128
POOL_WIDTH = N_POOL * POOL_GROUP
MEM_HEADS = 4
MEM_WIDTH = MEM_HEADS * HEAD_DIM
N_BRANCH = 3
D_FF = 5632
ALPHA = (2 * DEPTH) ** 0.25
LN_EPS = 1e-5
TOKENS = BATCH * SEQ

LANES = 128
BF16_ROWS = 16
VMEM_LIMIT = 60 * 1024 * 1024

INT_MIN = -(2 ** 31)
LOG2E = math.log2(math.e)


def _params(*sem):
    return pltpu.CompilerParams(dimension_semantics=sem, vmem_limit_bytes=VMEM_LIMIT)


def _layer_norm(y, g, b):
    mu = jnp.mean(y, axis=-1, keepdims=True)
    d = y - mu
    var = jnp.mean(d * d, axis=-1, keepdims=True)
    return d * lax.rsqrt(var + LN_EPS) * g + b


FFN_TM = 512
FFN_TF = 512


FFN_LN_STEPS = 8
FFN_LN_ROWS = FFN_TM // FFN_LN_STEPS


def _ffn_ln_kernel(x_ref, wa_ref, wu_ref, wd_ref, g_ref, b_ref, *rest, n_cast, has_z0):
    if has_z0:
        z0_ref, rest = rest[0], rest[1:]
    cast_in = rest[:n_cast]
    o_ref = rest[n_cast]
    cast_out = rest[n_cast + 1:2 * n_cast + 1]
    xb_ref, acc_ref = rest[2 * n_cast + 1:]
    i = pl.program_id(0)
    j = pl.program_id(1)
    n_tiles = pl.num_programs(0) - 1
    slot = i % 2
    computed = (i < n_tiles) & (i >= 1) if has_z0 else (i < n_tiles)

    for src, dst in zip(cast_in, cast_out):
        dst[...] = src[...].astype(dst.dtype)

    @pl.when((i == 0) & (j == 0))
    def _():
        if has_z0:
            acc_ref[0] = z0_ref[...]
        else:
            acc_ref[1] = jnp.zeros(acc_ref.shape[1:], f32)

    @pl.when(computed & (j == 0))
    def _():
        x = x_ref[...]
        xb_ref[...] = x.astype(bf16)
        acc_ref[slot] = (2.0 * ALPHA) * x

    def matmuls():
        xb = xb_ref[...]
        a = jnp.dot(xb, wa_ref[...], preferred_element_type=f32)
        u = jnp.dot(xb, wu_ref[...], preferred_element_type=f32)
        act = (a * jax.nn.sigmoid(a) * u).astype(bf16)
        acc_ref[slot] += jnp.dot(act, wd_ref[...], preferred_element_type=f32)

    def layer_norm_slab():
        rows = pl.ds(pl.multiple_of(j * FFN_LN_ROWS, FFN_LN_ROWS), FFN_LN_ROWS)
        z = acc_ref[1 - slot, rows, :]
        mu = jnp.mean(z, axis=-1, keepdims=True)
        d = z - mu
        var = jnp.mean(d * d, axis=-1, keepdims=True)
        o_ref[rows, :] = d * lax.rsqrt(var + 4.0 * LN_EPS) * g_ref[...] + b_ref[...]

    @pl.when(computed & (j < FFN_LN_STEPS))
    def _():
        layer_norm_slab()
        matmuls()

    @pl.when(computed & (j >= FFN_LN_STEPS))
    def _():
        matmuls()

    @pl.when((i == n_tiles) & (j < FFN_LN_STEPS))
    def _():
        layer_norm_slab()


def _cast_block_spec(shape, rows_on_j, ni, nj):
    r, c = shape
    n_r, n_c = (nj, ni) if rows_on_j else (ni, nj)
    br = -(-r // n_r)
    br = -(-br // BF16_ROWS) * BF16_ROWS
    assert c % n_c == 0 and (c // n_c) % LANES == 0
    def clamp(i, j):
        return jnp.minimum(i, ni - 1), jnp.where(i < ni, j, nj - 1)

    if rows_on_j:
        index_map = lambda i, j: clamp(i, j)[::-1]
    else:
        index_map = lambda i, j: clamp(i, j)
    return pl.BlockSpec((br, c // n_c), index_map)


FFN_HEAD_TF = 256


def _ffn_head_kernel(x_ref, wa_ref, wu_ref, wd_ref, z_ref, wa_out, wu_out, wd_out, xb_ref):
    j = pl.program_id(0)

    @pl.when(j == 0)
    def _():
        x = x_ref[...]
        xb_ref[...] = x.astype(bf16)
        z_ref[...] = (2.0 * ALPHA) * x

    wa = wa_ref[...].astype(bf16)
    wu = wu_ref[...].astype(bf16)
    wd = wd_ref[...].astype(bf16)
    wa_out[...] = wa
    wu_out[...] = wu
    wd_out[...] = wd
    xb = xb_ref[...]
    a = jnp.dot(xb, wa, preferred_element_type=f32)
    u = jnp.dot(xb, wu, preferred_element_type=f32)
    act = (a * jax.nn.sigmoid(a) * u).astype(bf16)
    z_ref[...] += jnp.dot(act, wd, preferred_element_type=f32)


def _ffn_head(x, w_up, w_down):
    nf = D_FF // FFN_HEAD_TF
    return pl.pallas_call(
        _ffn_head_kernel,
        grid=(nf,),
        in_specs=[
            pl.BlockSpec((FFN_TM, D_MODEL), lambda j: (0, 0)),
            pl.BlockSpec((D_MODEL, FFN_HEAD_TF), lambda j: (0, j)),
            pl.BlockSpec((D_MODEL, FFN_HEAD_TF), lambda j: (0, j + nf)),
            pl.BlockSpec((FFN_HEAD_TF, D_MODEL), lambda j: (j, 0)),
        ],
        out_specs=[
            pl.BlockSpec((FFN_TM, D_MODEL), lambda j: (0, 0)),
            pl.BlockSpec((D_MODEL, FFN_HEAD_TF), lambda j: (0, j)),
            pl.BlockSpec((D_MODEL, FFN_HEAD_TF), lambda j: (0, j)),
            pl.BlockSpec((FFN_HEAD_TF, D_MODEL), lambda j: (j, 0)),
        ],
        out_shape=[
            jax.ShapeDtypeStruct((FFN_TM, D_MODEL), f32),
            jax.ShapeDtypeStruct((D_MODEL, D_FF), bf16),
            jax.ShapeDtypeStruct((D_MODEL, D_FF), bf16),
            jax.ShapeDtypeStruct((D_FF, D_MODEL), bf16),
        ],
        scratch_shapes=[pltpu.VMEM((FFN_TM, D_MODEL), bf16)],
        compiler_params=_params("arbitrary"),
        name="ffn_head",
    )(x, w_up, w_up, w_down)


def _ffn_ln(x, w_a, w_u, w_down, g, b, casts=(), z0=None):
    ni = TOKENS // FFN_TM
    nf = D_FF // FFN_TF
    assert FFN_LN_STEPS <= nf
    u0 = (w_u.shape[1] - D_FF) // FFN_TF
    cast_specs = [_cast_block_spec(w.shape, rows_on_j, ni, nf) for w, rows_on_j in casts]
    z0_specs = [] if z0 is None else [pl.BlockSpec((FFN_TM, D_MODEL), lambda i, j: (0, 0))]
    z0_args = [] if z0 is None else [z0]

    def wj(i, j):
        return jnp.where(i < ni, j, nf - 1)

    outs = pl.pallas_call(
        functools.partial(_ffn_ln_kernel, n_cast=len(casts), has_z0=z0 is not None),
        grid=(ni + 1, nf),
        in_specs=[
            pl.BlockSpec((FFN_TM, D_MODEL), lambda i, j: (jnp.minimum(i, ni - 1), 0)),
            pl.BlockSpec((D_MODEL, FFN_TF), lambda i, j: (0, wj(i, j))),
            pl.BlockSpec((D_MODEL, FFN_TF), lambda i, j: (0, wj(i, j) + u0)),
            pl.BlockSpec((FFN_TF, D_MODEL), lambda i, j: (wj(i, j), 0)),
            pl.BlockSpec((1, D_MODEL), lambda i, j: (0, 0)),
            pl.BlockSpec((1, D_MODEL), lambda i, j: (0, 0)),
        ] + z0_specs + cast_specs,
        out_specs=[pl.BlockSpec((FFN_TM, D_MODEL), lambda i, j: (jnp.maximum(i - 1, 0), 0))]
        + cast_specs,
        out_shape=[jax.ShapeDtypeStruct((TOKENS, D_MODEL), f32)]
        + [jax.ShapeDtypeStruct(w.shape, bf16) for w, _ in casts],
        scratch_shapes=[pltpu.VMEM((FFN_TM, D_MODEL), bf16),
                        pltpu.VMEM((2, FFN_TM, D_MODEL), f32)],
        compiler_params=_params("arbitrary", "arbitrary"),
        name="ffn_ln",
    )(x, w_a, w_u, w_down, g, b, *z0_args, *[w for w, _ in casts])
    return outs[0], outs[1:]


_NT = (((1,), (1,)), ((), ()))


def _matmul_kernel(x_ref, w_ref, o_ref, xb_ref, *, w_is_transposed):
    @pl.when(pl.program_id(1) == 0)
    def _():
        xb_ref[...] = x_ref[...].astype(bf16)

    if w_is_transposed:
        y = lax.dot_general(xb_ref[...], w_ref[...], _NT, preferred_element_type=f32)
    else:
        y = jnp.dot(xb_ref[...], w_ref[...], preferred_element_type=f32)
    o_ref[...] = y.astype(o_ref.dtype)


def _matmul(x, w, out_dtype, tm, tn, name, w_is_transposed=False, n=None):
    m, k = x.shape
    if n is None:
        n = w.shape[0] if w_is_transposed else w.shape[1]
    assert n % tn == 0 and m % tm == 0
    if w_is_transposed:
        w_spec = pl.BlockSpec((tn, k), lambda i, j: (j, 0))
    else:
        w_spec = pl.BlockSpec((k, tn), lambda i, j: (0, j))
    return pl.pallas_call(
        functools.partial(_matmul_kernel, w_is_transposed=w_is_transposed),
        grid=(m // tm, n // tn),
        in_specs=[pl.BlockSpec((tm, k), lambda i, j: (i, 0)), w_spec],
        out_specs=pl.BlockSpec((tm, tn), lambda i, j: (i, j)),
        out_shape=jax.ShapeDtypeStruct((m, n), out_dtype),
        scratch_shapes=[pltpu.VMEM((tm, k), bf16)],
        compiler_params=_params("parallel", "arbitrary"),
        name=name,
    )(x, w)


PB_TM = 512
PB_N = POOL_WIDTH + MEM_WIDTH + LANES


def _in_proj_b_kernel(x_ref, wum_ref, wk_ref, o_ref):
    xb = x_ref[...].astype(bf16)
    n_um = POOL_WIDTH + MEM_WIDTH
    o_ref[:, :n_um] = lax.dot_general(xb, wum_ref[...], _NT, preferred_element_type=f32)
    o_ref[:, n_um:] = lax.dot_general(xb, wk_ref[...], _NT, preferred_element_type=f32)


def _in_proj_b(x, wt, row_um, row_kiwi):
    n_um = POOL_WIDTH + MEM_WIDTH
    return pl.pallas_call(
        _in_proj_b_kernel,
        grid=(TOKENS // PB_TM,),
        in_specs=[
            pl.BlockSpec((PB_TM, D_MODEL), lambda i: (i, 0)),
            pl.BlockSpec((pl.Element(n_um), pl.Element(D_MODEL)), lambda i: (row_um, 0)),
            pl.BlockSpec((pl.Element(LANES), pl.Element(D_MODEL)), lambda i: (row_kiwi, 0)),
        ],
        out_specs=pl.BlockSpec((PB_TM, PB_N), lambda i: (i, 0)),
        out_shape=jax.ShapeDtypeStruct((TOKENS, PB_N), f32),
        compiler_params=_params("parallel"),
        name="in_proj_b",
    )(x, wt, wt)


DSA_TQ = 512
DSA_TK = 512
DSA_NCH = SEQ // DSA_TK
DSA_ONES = 16
KEY_NEG_FLT_MAX = -(2 ** 31) + (1 << 23)


def _key_to_f32(key):
    return lax.bitcast_convert_type(key ^ ((key >> 31) & jnp.int32(0x7FFFFFFF)), f32)


def _dsa_kernel(q_ref, k_ref, vt_ref, qi_ref, kk_ref, wt_ref, kf_ref, qf_ref, o_ref,
                sc_scr, sch_scr, acc_scr, qa_scr):
    i = pl.program_id(1)
    q0 = i * DSA_TQ
    nk = (q0 + DSA_TQ - 1) // DSA_TK + 1

    qpos = q0 + lax.broadcasted_iota(i32, (DSA_TK, DSA_TQ), 1)
    kofs = lax.broadcasted_iota(i32, (DSA_TK, DSA_TQ), 0)
    contract_last = (((1,), (1,)), ((), ()))

    def rows8(x):
        return x.reshape(DSA_TK // 8, 8, DSA_TQ)

    wt = wt_ref[...] * (IDX_DIM ** -0.5 * IDX_HEADS ** -0.5)

    def score_chunk(c, carry):
        r0 = pl.multiple_of(c * DSA_TK, DSA_TK)
        kk0 = kk_ref[0, pl.ds(r0, DSA_TK), :]
        kk1 = kk_ref[1, pl.ds(r0, DSA_TK), :]
        acc = jnp.zeros((DSA_TK, DSA_TQ), f32)
        for p in range(IDX_HEADS // 2):
            slab = qi_ref[:, p * LANES:(p + 1) * LANES]
            l0 = lax.dot_general(kk0, slab, contract_last, preferred_element_type=f32)
            l1 = lax.dot_general(kk1, slab, contract_last, preferred_element_type=f32)
            acc = acc + (jnp.maximum(l0, 0.0) * wt[2 * p:2 * p + 1]
                         + jnp.maximum(l1, 0.0) * wt[2 * p + 1:2 * p + 2])
        sc = jnp.where(kofs + r0 <= qpos, acc, -jnp.inf)
        sc_scr[c] = sc
        sch_scr[c] = sc.astype(bf16)
        return carry

    lax.fori_loop(0, nk, score_chunk, 0)

    def count(pred):
        n_acc = 8

        def body(c, accs):
            m = rows8(pred(c))
            accs = list(accs)
            for r in range(DSA_TK // 8):
                accs[r % n_acc] = jnp.where(m[r], accs[r % n_acc] + 1, accs[r % n_acc])
            return tuple(accs)
        accs = lax.fori_loop(0, nk, body, (jnp.zeros((8, DSA_TQ), i32),) * n_acc)
        acc = functools.reduce(lambda a, b: a + b, accs)
        return jnp.sum(acc, axis=0, keepdims=True)

    def count_ge(cand):
        return count(lambda c: sc_scr[c] >= cand)

    def count_ge_coarse(cand):
        n_acc = 2
        one = jnp.ones((), bf16)
        zero = jnp.zeros((), bf16)

        def body(c, accs):
            accs = list(accs)
            for r in range(DSA_TK // BF16_ROWS):
                rows = sch_scr[c, r * BF16_ROWS:(r + 1) * BF16_ROWS, :]
                accs[r % n_acc] = accs[r % n_acc] + jnp.where(rows >= cand, one, zero)
            return tuple(accs)
        accs = lax.fori_loop(0, nk, body, (jnp.zeros((BF16_ROWS, DSA_TQ), bf16),) * n_acc)
        acc = functools.reduce(lambda a, b: a + b, accs)
        return jnp.sum(acc.astype(f32), axis=0, keepdims=True)

    def coarse_cand(key):
        bits = lax.bitcast_convert_type(_key_to_f32(key), i32) & jnp.int32(-65536)
        return lax.bitcast_convert_type(bits, f32).astype(bf16)

    key = jnp.where(count_ge_coarse(jnp.zeros((1, DSA_TQ), bf16)) >= TOPK,
                    jnp.int32(0), jnp.int32(INT_MIN))

    def coarse_step(b, key):
        cand = key | jnp.left_shift(jnp.int32(1), 30 - b)
        return jnp.where(count_ge_coarse(coarse_cand(cand)) >= TOPK, cand, key)

    key = lax.fori_loop(0, 15, coarse_step, key)

    window_bits = 18
    lo = jnp.maximum(key, jnp.int32(INT_MIN + (1 << 17))) - jnp.int32((1 << 15) + 2)

    def fine_step(b, off):
        cand = off | jnp.left_shift(jnp.int32(1), (window_bits - 1) - b)
        return jnp.where(count_ge(_key_to_f32(lo + cand)) >= TOPK, cand, off)

    key = lo + lax.fori_loop(0, window_bits, fine_step, jnp.zeros((1, DSA_TQ), i32))
    thr = _key_to_f32(jnp.maximum(key, jnp.int32(KEY_NEG_FLT_MAX)))

    n_ge = count_ge(thr)

    @pl.when(jnp.max(n_ge) > TOPK)
    def _():
        need = TOPK - count(lambda c: sc_scr[c] > thr)

        def count_eq_below(pos):
            return count(lambda c: (sc_scr[c] == thr) & (kofs + c * DSA_TK < pos))

        def pos_step(b, r):
            cand = r | jnp.left_shift(jnp.int32(1), (SEQ.bit_length() - 2) - b)
            return jnp.where(count_eq_below(cand) < need, cand, r)
        r = lax.fori_loop(0, SEQ.bit_length() - 1, pos_step, jnp.zeros((1, DSA_TQ), i32))
        r = jnp.where(n_ge > TOPK, r, jnp.int32(SEQ))

        def drop(c, carry):
            sc = sc_scr[c]
            sc_scr[c] = jnp.where((sc == thr) & (kofs + c * DSA_TK > r), -jnp.inf, sc)
            return carry
        lax.fori_loop(0, nk, drop, 0)

    acc_scr[...] = jnp.zeros(acc_scr.shape, f32)
    c1 = HEAD_DIM ** -0.5 * LOG2E
    ones = jnp.ones((DSA_ONES, DSA_TK), bf16)

    for h in range(ATT_HEADS):
        qa_scr[h, :, :HEAD_DIM] = q_ref[:, h * HEAD_DIM:(h + 1) * HEAD_DIM]
        qa_scr[h, :, HEAD_DIM:] = jnp.broadcast_to(qf_ref[h, 0:1, :], (DSA_TQ, LANES))

    def attn_chunk(c, ms):
        r0 = pl.multiple_of(c * DSA_TK, DSA_TK)
        madd = jnp.where(sc_scr[c] >= thr, 0.0, -jnp.inf)
        kf = kf_ref[pl.ds(r0, DSA_TK), :]

        def qk(h):
            hd = slice(h * HEAD_DIM, (h + 1) * HEAD_DIM)
            k_aug = jnp.concatenate([k_ref[pl.ds(r0, DSA_TK), hd], kf], axis=1)
            return lax.dot_general(k_aug, qa_scr[h], contract_last, preferred_element_type=f32)

        def pv(h, alpha, p):
            hd = slice(h * HEAD_DIM, (h + 1) * HEAD_DIM)
            vt1 = jnp.concatenate([vt_ref[c, hd, :], ones], axis=0)
            acc_scr[h] = alpha * acc_scr[h] + jnp.dot(vt1, p, preferred_element_type=f32)

        new_ms = []
        s_next = qk(0)
        pending = None
        for h in range(ATT_HEADS):
            s = s_next
            if h + 1 < ATT_HEADS:
                s_next = qk(h + 1)
            if pending is not None:
                pv(*pending)
            t = s * c1 + madd
            m_old = ms[h]
            m_new = jnp.maximum(m_old, jnp.max(t, axis=0, keepdims=True))
            alpha = jnp.exp2(m_old - m_new)
            p = jnp.exp2(t - m_new).astype(bf16)
            pending = (h, alpha, p)
            new_ms.append(m_new)
        pv(*pending)
        return tuple(new_ms)

    m0 = jnp.full((1, DSA_TQ), -1e30, f32)
    lax.fori_loop(0, nk, attn_chunk, (m0,) * ATT_HEADS)

    for h in range(ATT_HEADS):
        o_ref[h * HEAD_DIM:(h + 1) * HEAD_DIM, :] = (
            acc_scr[h, :HEAD_DIM, :] / acc_scr[h, HEAD_DIM:HEAD_DIM + 1, :]).astype(o_ref.dtype)


def _alibi_features():
    assert ATT_HEADS == 8 and HEAD_DIM == 128 and SEQ <= 64 * 64
    pieces, rest = [], math.sqrt(2.0)
    for _ in range(7):
        p = float(np.asarray(rest, np.float32).astype(jnp.bfloat16).astype(np.float64))
        pieces.append(p)
        rest -= p
    qf = np.zeros((ATT_HEADS, BF16_ROWS, LANES), np.float32)
    for h in range(ATT_HEADS):
        for i, p in enumerate(pieces):
            qf[h, :, 2 * i] = qf[h, :, 2 * i + 1] = p * 2.0 ** (2 - h)
    pos = np.arange(SEQ)
    kf = np.zeros((SEQ, LANES), np.float32)
    for i in range(len(pieces)):
        kf[:, 2 * i] = pos - pos % 64
        kf[:, 2 * i + 1] = pos % 64
    return jnp.asarray(kf, bf16), jnp.asarray(qf, bf16)


def _dsa(za, vt, kk, wt_idx):
    nq = SEQ // DSA_TQ
    kf, qf = _alibi_features()
    once = pl.Buffered(1)
    return pl.pallas_call(
        _dsa_kernel,
        grid=(BATCH, nq),
        in_specs=[
            pl.BlockSpec((DSA_TQ, ATT_WIDTH), lambda b, i: (b * nq + i, 0)),
            pl.BlockSpec((SEQ, ATT_WIDTH), lambda b, i: (b, 1), pipeline_mode=once),
            pl.BlockSpec((None, DSA_NCH, ATT_WIDTH, DSA_TK), lambda b, i: (b, 0, 0, 0),
                         pipeline_mode=once),
            pl.BlockSpec((DSA_TQ, IDX_HEADS * IDX_DIM), lambda b, i: (b * nq + i, 3)),
            pl.BlockSpec((2, SEQ, LANES), lambda b, i: (0, b, 0), pipeline_mode=once),
            pl.BlockSpec((IDX_HEADS, DSA_TQ), lambda b, i: (0, b * nq + i)),
            pl.BlockSpec((SEQ, LANES), lambda b, i: (0, 0), pipeline_mode=once),
            pl.BlockSpec((ATT_HEADS, BF16_ROWS, LANES), lambda b, i: (0, 0, 0)),
        ],
        out_specs=pl.BlockSpec((None, ATT_WIDTH, DSA_TQ), lambda b, i: (b, 0, i)),
        out_shape=jax.ShapeDtypeStruct((BATCH, ATT_WIDTH, SEQ), bf16),
        scratch_shapes=[
            pltpu.VMEM((DSA_NCH, DSA_TK, DSA_TQ), f32),
            pltpu.VMEM((DSA_NCH, DSA_TK, DSA_TQ), bf16),
            pltpu.VMEM((ATT_HEADS, HEAD_DIM + DSA_ONES, DSA_TQ), f32),
            pltpu.VMEM((ATT_HEADS, DSA_TQ, HEAD_DIM + LANES), bf16),
        ],
        compiler_params=_params("arbitrary", "arbitrary"),
        name="dsa",
    )(za, za, vt, za, kk, wt_idx, kf, qf)


def _pool_kernel(u_ref, wp_ref, ps_ref, o_ref):
    row = lax.broadcasted_iota(i32, (SEQ, POOL_GROUP), 0)
    for g, win in enumerate(POOL_WINDOWS):
        x = u_ref[:, g * POOL_GROUP:(g + 1) * POOL_GROUP]
        s = x
        k = 1
        while k < win:
            s = s + jnp.where(row >= k, pltpu.roll(s, k, axis=0), 0.0)
            k *= 2
        cnt = jnp.minimum(row + 1, win).astype(f32)
        pooled = (s / cnt - x).astype(bf16)
        mixed = jnp.dot(pooled, wp_ref[g], preferred_element_type=f32)
        o_ref[:, g * POOL_GROUP:(g + 1) * POOL_GROUP] = (
            mixed * ps_ref[:, g * POOL_GROUP:(g + 1) * POOL_GROUP]).astype(o_ref.dtype)


def _pool(zb, w_pool, pool_scale):
    return pl.pallas_call(
        _pool_kernel,
        grid=(BATCH,),
        in_specs=[
            pl.BlockSpec((SEQ, POOL_WIDTH), lambda b: (b, 0)),
            pl.BlockSpec((N_POOL, POOL_GROUP, POOL_GROUP), lambda b: (0, 0, 0)),
            pl.BlockSpec((1, POOL_WIDTH), lambda b: (0, 0)),
        ],
        out_specs=pl.BlockSpec((SEQ, POOL_WIDTH), lambda b: (b, 0)),
        out_shape=jax.ShapeDtypeStruct((TOKENS, POOL_WIDTH), bf16),
        compiler_params=_params("parallel"),
        name="pool",
    )(zb, w_pool, pool_scale)


MEM_TQ = 512


def _mem_attn_kernel(q_ref, k_ref, v_ref, o_ref):
    for h in range(MEM_HEADS):
        sl = slice(h * HEAD_DIM, (h + 1) * HEAD_DIM)
        s = lax.dot_general(q_ref[:, sl].astype(bf16), k_ref[:, sl], (((1,), (1,)), ((), ())),
                            preferred_element_type=f32) * (HEAD_DIM ** -0.5)
        p = jnp.exp(s - jnp.max(s, axis=1, keepdims=True))
        l = jnp.sum(p, axis=1, keepdims=True)
        o = jnp.dot(p.astype(bf16), v_ref[:, sl], preferred_element_type=f32)
        o_ref[:, sl] = (o / l).astype(o_ref.dtype)


def _mem_attn(zb, qcol, kv):
    nq = SEQ // MEM_TQ
    return pl.pallas_call(
        _mem_attn_kernel,
        grid=(BATCH, nq),
        in_specs=[
            pl.BlockSpec((MEM_TQ, MEM_WIDTH), lambda b, i: (b * nq + i, qcol)),
            pl.BlockSpec((MEM_LEN, MEM_WIDTH), lambda b, i: (b, 0)),
            pl.BlockSpec((MEM_LEN, MEM_WIDTH), lambda b, i: (b, 1)),
        ],
        out_specs=pl.BlockSpec((MEM_TQ, MEM_WIDTH), lambda b, i: (b * nq + i, 0)),
        out_shape=jax.ShapeDtypeStruct((TOKENS, MEM_WIDTH), bf16),
        compiler_params=_params("parallel", "parallel"),
        name="mem_attn",
    )(zb, kv, kv)


CMB_TM = 512
CMB_TC = 512


def _combine_ln_kernel(h_ref, a_ref, p_ref, m_ref, wg0_ref, wg1_ref, wg2_ref,
                       bg0_ref, bg1_ref, bg2_ref, wa_ref, wp_ref, wm_ref, wo_ref,
                       g_ref, b_ref, o_ref, hb_ref, acc_ref):
    i = pl.program_id(0)
    j = pl.program_id(1)
    n_tiles = pl.num_programs(0) - 1
    slot = i % 2
    ln_rows = CMB_TM // (D_MODEL // CMB_TC)

    @pl.when((i == 0) & (j == 0))
    def _():
        acc_ref[1] = jnp.zeros(acc_ref.shape[1:], f32)

    @pl.when((i < n_tiles) & (j == 0))
    def _():
        h = h_ref[...]
        hb_ref[...] = h.astype(bf16)
        acc_ref[slot] = ALPHA * h

    def layer_norm_slab():
        rows = pl.ds(pl.multiple_of(j * ln_rows, ln_rows), ln_rows)
        o_ref[rows, :] = _layer_norm(acc_ref[1 - slot, rows, :], g_ref[...], b_ref[...])

    def matmuls():
        hb = hb_ref[...]

        def gate(wg_ref, bg_ref):
            logits = lax.dot_general(hb, wg_ref[...], _NT, preferred_element_type=f32)
            return jax.nn.sigmoid(logits + bg_ref[...])

        y = gate(wg0_ref, bg0_ref) * jnp.dot(a_ref[...], wa_ref[...], preferred_element_type=f32)
        y = y + gate(wg1_ref, bg1_ref) * jnp.dot(p_ref[...], wp_ref[...], preferred_element_type=f32)
        y = y + gate(wg2_ref, bg2_ref) * jnp.dot(m_ref[...], wm_ref[...], preferred_element_type=f32)
        acc_ref[slot] += jnp.dot(y.astype(bf16), wo_ref[...], preferred_element_type=f32)

    @pl.when(i < n_tiles)
    def _():
        layer_norm_slab()
        matmuls()

    @pl.when(i == n_tiles)
    def _():
        layer_norm_slab()


def _combine_ln(h, a, p, m, wt_gate, row_gate, b_gate, w_a, w_p, w_m, w_out, g, b):
    nc = D_MODEL // CMB_TC
    gate_rows = pl.Element(CMB_TC)
    all_cols = pl.Element(D_MODEL)
    assert row_gate % BF16_ROWS == 0

    def gate_row(blk):
        return pl.multiple_of(row_gate + blk * CMB_TC, BF16_ROWS)

    ni = TOKENS // CMB_TM

    def cj(i, j):
        return jnp.where(i < ni, j, nc - 1)

    row = lambda i, j: (jnp.minimum(i, ni - 1), 0)
    col = lambda i, j: (0, cj(i, j))
    return pl.pallas_call(
        _combine_ln_kernel,
        grid=(ni + 1, nc),
        in_specs=[
            pl.BlockSpec((CMB_TM, D_MODEL), row),
            pl.BlockSpec((CMB_TM, ATT_WIDTH), row),
            pl.BlockSpec((CMB_TM, POOL_WIDTH), row),
            pl.BlockSpec((CMB_TM, MEM_WIDTH), row),
            pl.BlockSpec((gate_rows, all_cols), lambda i, j: (gate_row(cj(i, j)), 0)),
            pl.BlockSpec((gate_rows, all_cols), lambda i, j: (gate_row(cj(i, j) + nc), 0)),
            pl.BlockSpec((gate_rows, all_cols), lambda i, j: (gate_row(cj(i, j) + 2 * nc), 0)),
            pl.BlockSpec((1, CMB_TC), lambda i, j: (0, cj(i, j))),
            pl.BlockSpec((1, CMB_TC), lambda i, j: (0, cj(i, j) + nc)),
            pl.BlockSpec((1, CMB_TC), lambda i, j: (0, cj(i, j) + 2 * nc)),
            pl.BlockSpec((ATT_WIDTH, CMB_TC), col),
            pl.BlockSpec((POOL_WIDTH, CMB_TC), col),
            pl.BlockSpec((MEM_WIDTH, CMB_TC), col),
            pl.BlockSpec((CMB_TC, D_MODEL), lambda i, j: (cj(i, j), 0)),
            pl.BlockSpec((1, D_MODEL), lambda i, j: (0, 0)),
            pl.BlockSpec((1, D_MODEL), lambda i, j: (0, 0)),
        ],
        out_specs=pl.BlockSpec((CMB_TM, D_MODEL), lambda i, j: (jnp.maximum(i - 1, 0), 0)),
        out_shape=jax.ShapeDtypeStruct((TOKENS, D_MODEL), f32),
        scratch_shapes=[pltpu.VMEM((CMB_TM, D_MODEL), bf16),
                        pltpu.VMEM((2, CMB_TM, D_MODEL), f32)],
        compiler_params=_params("arbitrary", "arbitrary"),
        name="combine_ln",
    )(h, a, p, m, wt_gate, wt_gate, wt_gate, b_gate, b_gate, b_gate, w_a, w_p, w_m, w_out, g, b)


def kernel(x, mem, w_ffn1_up, w_ffn1_down, ln1_g, ln1_b, w_in, b_gate, w_mem_kv, w_pool,
           pool_scale, w_br_att, w_br_pool, w_br_mem, w_out, ln2_g, ln2_b, w_ffn2_up,
           w_ffn2_down, ln3_g, ln3_b):
    h = x.reshape(TOKENS, D_MODEL)
    memf = mem.reshape(BATCH * MEM_LEN, D_MODEL)
    for l in range(DEPTH):
        c_qi = 3 * ATT_WIDTH + IDX_HEADS * IDX_DIM
        c_wi = c_qi + IDX_DIM + IDX_HEADS
        c_qm = c_wi + POOL_WIDTH + MEM_WIDTH

        z0, w1_a, w1_u, w1_down = _ffn_head(h, w_ffn1_up[l], w_ffn1_down[l])
        h, (wt, w2_up, w2_down) = _ffn_ln(
            h, w1_a, w1_u, w1_down, ln1_g[l][None], ln1_b[l][None], z0=z0,
            casts=((w_in[l].T, True), (w_ffn2_up[l], False), (w_ffn2_down[l], True)))

        za = _matmul(h, wt, bf16, 1024, 1024, "in_proj_a", w_is_transposed=True, n=c_qi)
        zb = _in_proj_b(h, wt, c_wi, c_qi)
        kv = _matmul(memf, w_mem_kv[l].astype(bf16), bf16, BATCH * MEM_LEN, 512, "mem_kv")

        c_ki = POOL_WIDTH + MEM_WIDTH
        ki = zb[:, c_ki:c_ki + IDX_DIM].astype(bf16)
        zk = jnp.zeros_like(ki)
        kk = jnp.stack([jnp.concatenate([ki, zk], axis=1), jnp.concatenate([zk, ki], axis=1)])
        wt_idx = zb[:, c_ki + IDX_DIM:c_ki + IDX_DIM + IDX_HEADS].T
        vt = za[:, 2 * ATT_WIDTH:3 * ATT_WIDTH].reshape(BATCH, DSA_NCH, DSA_TK, ATT_WIDTH)
        vt = vt.transpose(0, 1, 3, 2)

        a = _dsa(za, vt, kk, wt_idx).transpose(0, 2, 1).reshape(TOKENS, ATT_WIDTH)
        p = _pool(zb, w_pool[l].astype(bf16), pool_scale[l][None])
        m = _mem_attn(zb, POOL_WIDTH // MEM_WIDTH, kv)

        h = _combine_ln(h, a, p, m, wt, c_qm, b_gate[l][None], w_br_att[l].astype(bf16),
                        w_br_pool[l].astype(bf16), w_br_mem[l].astype(bf16),
                        w_out[l].astype(bf16), ln2_g[l][None], ln2_b[l][None])

        h, _ = _ffn_ln(h, w2_up, w2_up, w2_down, ln3_g[l][None], ln3_b[l][None])
    return h.reshape(BATCH, SEQ, D_MODEL)
```

```python
import functools
import math

import jax
import jax.numpy as jnp
import numpy as np
from jax import lax
from jax.experimental import pallas as pl
from jax.experimental.pallas import tpu as pltpu

f32 = jnp.float32
bf16 = jnp.bfloat16
i32 = jnp.int32

D_MODEL = 2048
BATCH = 2
SEQ = 4096
DEPTH = 1
MEM_LEN = 256
ATT_HEADS = 8
HEAD_DIM = 128
ATT_WIDTH = ATT_HEADS * HEAD_DIM
IDX_HEADS = 16
IDX_DIM = 64
TOPK = min(256, SEQ // 4)
POOL_WINDOWS = (2, 4, 8, 16)
N_POOL = len(POOL_WINDOWS)
POOL_GROUP = 128
POOL_WIDTH = N_POOL * POOL_GROUP
MEM_HEADS = 4
MEM_WIDTH = MEM_HEADS * HEAD_DIM
N_BRANCH = 3
D_FF = 5632
ALPHA = (2 * DEPTH) ** 0.25
LN_EPS = 1e-5
TOKENS = BATCH * SEQ

LANES = 128
BF16_ROWS = 16
VMEM_LIMIT = 60 * 1024 * 1024

INT_MIN = -(2 ** 31)
LOG2E = math.log2(math.e)


def _params(*sem):
    return pltpu.CompilerParams(dimension_semantics=sem, vmem_limit_bytes=VMEM_LIMIT)


def _layer_norm(y, g, b):
    mu = jnp.mean(y, axis=-1, keepdims=True)
    d = y - mu
    var = jnp.mean(d * d, axis=-1, keepdims=True)
    return d * lax.rsqrt(var + LN_EPS) * g + b


FFN_TM = 512
FFN_TF = 512


FFN_LN_STEPS = 8
FFN_LN_ROWS = FFN_TM // FFN_LN_STEPS


def _ffn_ln_kernel(x_ref, wa_ref, wu_ref, wd_ref, g_ref, b_ref, *rest, n_cast, has_z0):
    if has_z0:
        z0_ref, rest = rest[0], rest[1:]
    cast_in = rest[:n_cast]
    o_ref = rest[n_cast]
    cast_out = rest[n_cast + 1:2 * n_cast + 1]
    xb_ref, acc_ref = rest[2 * n_cast + 1:]
    i = pl.program_id(0)
    j = pl.program_id(1)
    n_tiles = pl.num_programs(0) - 1
    slot = i % 2
    computed = (i < n_tiles) & (i >= 1) if has_z0 else (i < n_tiles)

    for src, dst in zip(cast_in, cast_out):
        dst[...] = src[...].astype(dst.dtype)

    @pl.when((i == 0) & (j == 0))
    def _():
        if has_z0:
            acc_ref[0] = z0_ref[...]
        else:
            acc_ref[1] = jnp.zeros(acc_ref.shape[1:], f32)

    @pl.when(computed & (j == 0))
    def _():
        x = x_ref[...]
        xb_ref[...] = x.astype(bf16)
        acc_ref[slot] = (2.0 * ALPHA) * x

    def matmuls():
        xb = xb_ref[...]
        a = jnp.dot(xb, wa_ref[...], preferred_element_type=f32)
        u = jnp.dot(xb, wu_ref[...], preferred_element_type=f32)
        act = (a * jax.nn.sigmoid(a) * u).astype(bf16)
        acc_ref[slot] += jnp.dot(act, wd_ref[...], preferred_element_type=f32)

    def layer_norm_slab():
        rows = pl.ds(pl.multiple_of(j * FFN_LN_ROWS, FFN_LN_ROWS), FFN_LN_ROWS)
        z = acc_ref[1 - slot, rows, :]
        mu = jnp.mean(z, axis=-1, keepdims=True)
        d = z - mu
        var = jnp.mean(d * d, axis=-1, keepdims=True)
        o_ref[rows, :] = d * lax.rsqrt(var + 4.0 * LN_EPS) * g_ref[...] + b_ref[...]

    @pl.when(computed & (j < FFN_LN_STEPS))
    def _():
        layer_norm_slab()
        matmuls()

    @pl.when(computed & (j >= FFN_LN_STEPS))
    def _():
        matmuls()

    @pl.when((i == n_tiles) & (j < FFN_LN_STEPS))
    def _():
        layer_norm_slab()


def _cast_block_spec(shape, rows_on_j, ni, nj):
    r, c = shape
    if rows_on_j is None:
        br = -(-r // (ni * nj))
        br = -(-br // BF16_ROWS) * BF16_ROWS
        last_block = -(-r // br) - 1
        return pl.BlockSpec((br, c), lambda i, j: (
            jnp.minimum(jnp.minimum(i, ni - 1) * nj + jnp.where(i < ni, j, nj - 1), last_block), 0))
    n_r, n_c = (nj, ni) if rows_on_j else (ni, nj)
    br = -(-r // n_r)
    br = -(-br // BF16_ROWS) * BF16_ROWS
    assert c % n_c == 0 and (c // n_c) % LANES == 0
    def clamp(i, j):
        return jnp.minimum(i, ni - 1), jnp.where(i < ni, j, nj - 1)

    if rows_on_j:
        index_map = lambda i, j: clamp(i, j)[::-1]
    else:
        index_map = lambda i, j: clamp(i, j)
    return pl.BlockSpec((br, c // n_c), index_map)


FFN_HEAD_TF = 256


def _ffn_head_kernel(x_ref, wa_ref, wu_ref, wd_ref, z_ref, wa_out, wu_out, wd_out, xb_ref):
    j = pl.program_id(0)

    @pl.when(j == 0)
    def _():
        x = x_ref[...]
        xb_ref[...] = x.astype(bf16)
        z_ref[...] = (2.0 * ALPHA) * x

    wa = wa_ref[...].astype(bf16)
    wu = wu_ref[...].astype(bf16)
    wd = wd_ref[...].astype(bf16)
    wa_out[...] = wa
    wu_out[...] = wu
    wd_out[...] = wd
    xb = xb_ref[...]
    a = jnp.dot(xb, wa, preferred_element_type=f32)
    u = jnp.dot(xb, wu, preferred_element_type=f32)
    act = (a * jax.nn.sigmoid(a) * u).astype(bf16)
    z_ref[...] += jnp.dot(act, wd, preferred_element_type=f32)


def _ffn_head(x, w_up, w_down):
    nf = D_FF // FFN_HEAD_TF
    return pl.pallas_call(
        _ffn_head_kernel,
        grid=(nf,),
        in_specs=[
            pl.BlockSpec((FFN_TM, D_MODEL), lambda j: (0, 0)),
            pl.BlockSpec((D_MODEL, FFN_HEAD_TF), lambda j: (0, j)),
            pl.BlockSpec((D_MODEL, FFN_HEAD_TF), lambda j: (0, j + nf)),
            pl.BlockSpec((FFN_HEAD_TF, D_MODEL), lambda j: (j, 0)),
        ],
        out_specs=[
            pl.BlockSpec((FFN_TM, D_MODEL), lambda j: (0, 0)),
            pl.BlockSpec((D_MODEL, FFN_HEAD_TF), lambda j: (0, j)),
            pl.BlockSpec((D_MODEL, FFN_HEAD_TF), lambda j: (0, j)),
            pl.BlockSpec((FFN_HEAD_TF, D_MODEL), lambda j: (j, 0)),
        ],
        out_shape=[
            jax.ShapeDtypeStruct((FFN_TM, D_MODEL), f32),
            jax.ShapeDtypeStruct((D_MODEL, D_FF), bf16),
            jax.ShapeDtypeStruct((D_MODEL, D_FF), bf16),
            jax.ShapeDtypeStruct((D_FF, D_MODEL), bf16),
        ],
        scratch_shapes=[pltpu.VMEM((FFN_TM, D_MODEL), bf16)],
        compiler_params=_params("arbitrary"),
        name="ffn_head",
    )(x, w_up, w_up, w_down)


def _ffn_ln(x, w_a, w_u, w_down, g, b, casts=(), z0=None):
    ni = TOKENS // FFN_TM
    nf = D_FF // FFN_TF
    assert FFN_LN_STEPS <= nf
    u0 = (w_u.shape[1] - D_FF) // FFN_TF
    cast_specs = [_cast_block_spec(w.shape, rows_on_j, ni, nf) for w, rows_on_j in casts]
    z0_specs = [] if z0 is None else [pl.BlockSpec((FFN_TM, D_MODEL), lambda i, j: (0, 0))]
    z0_args = [] if z0 is None else [z0]

    def wj(i, j):
        return jnp.where(i < ni, j, nf - 1)

    outs = pl.pallas_call(
        functools.partial(_ffn_ln_kernel, n_cast=len(casts), has_z0=z0 is not None),
        grid=(ni + 1, nf),
        in_specs=[
            pl.BlockSpec((FFN_TM, D_MODEL), lambda i, j: (jnp.minimum(i, ni - 1), 0)),
            pl.BlockSpec((D_MODEL, FFN_TF), lambda i, j: (0, wj(i, j))),
            pl.BlockSpec((D_MODEL, FFN_TF), lambda i, j: (0, wj(i, j) + u0)),
            pl.BlockSpec((FFN_TF, D_MODEL), lambda i, j: (wj(i, j), 0)),
            pl.BlockSpec((1, D_MODEL), lambda i, j: (0, 0)),
            pl.BlockSpec((1, D_MODEL), lambda i, j: (0, 0)),
        ] + z0_specs + cast_specs,
        out_specs=[pl.BlockSpec((FFN_TM, D_MODEL), lambda i, j: (jnp.maximum(i - 1, 0), 0))]
        + cast_specs,
        out_shape=[jax.ShapeDtypeStruct((TOKENS, D_MODEL), f32)]
        + [jax.ShapeDtypeStruct(w.shape, bf16) for w, _ in casts],
        scratch_shapes=[pltpu.VMEM((FFN_TM, D_MODEL), bf16),
                        pltpu.VMEM((2, FFN_TM, D_MODEL), f32)],
        compiler_params=_params("arbitrary", "arbitrary"),
        name="ffn_ln",
    )(x, w_a, w_u, w_down, g, b, *z0_args, *[w for w, _ in casts])
    return outs[0], outs[1:]


_NT = (((1,), (1,)), ((), ()))


def _matmul_kernel(x_ref, w_ref, o_ref, xb_ref, *, w_is_transposed):
    @pl.when(pl.program_id(1) == 0)
    def _():
        xb_ref[...] = x_ref[...].astype(bf16)

    if w_is_transposed:
        y = lax.dot_general(xb_ref[...], w_ref[...], _NT, preferred_element_type=f32)
    else:
        y = jnp.dot(xb_ref[...], w_ref[...], preferred_element_type=f32)
    o_ref[...] = y.astype(o_ref.dtype)


def _matmul(x, w, out_dtype, tm, tn, name, w_is_transposed=False, n=None):
    m, k = x.shape
    if n is None:
        n = w.shape[0] if w_is_transposed else w.shape[1]
    assert n % tn == 0 and m % tm == 0
    if w_is_transposed:
        w_spec = pl.BlockSpec((tn, k), lambda i, j: (j, 0))
    else:
        w_spec = pl.BlockSpec((k, tn), lambda i, j: (0, j))
    return pl.pallas_call(
        functools.partial(_matmul_kernel, w_is_transposed=w_is_transposed),
        grid=(m // tm, n // tn),
        in_specs=[pl.BlockSpec((tm, k), lambda i, j: (i, 0)), w_spec],
        out_specs=pl.BlockSpec((tm, tn), lambda i, j: (i, j)),
        out_shape=jax.ShapeDtypeStruct((m, n), out_dtype),
        scratch_shapes=[pltpu.VMEM((tm, k), bf16)],
        compiler_params=_params("parallel", "arbitrary"),
        name=name,
    )(x, w)


PB_TM = 512
PB_N = POOL_WIDTH + MEM_WIDTH + LANES


def _in_proj_b_kernel(x_ref, wum_ref, wk_ref, o_ref):
    xb = x_ref[...].astype(bf16)
    n_um = POOL_WIDTH + MEM_WIDTH
    o_ref[:, :n_um] = lax.dot_general(xb, wum_ref[...], _NT, preferred_element_type=f32)
    o_ref[:, n_um:] = lax.dot_general(xb, wk_ref[...], _NT, preferred_element_type=f32)


def _in_proj_b(x, wt, row_um, row_kiwi):
    n_um = POOL_WIDTH + MEM_WIDTH
    return pl.pallas_call(
        _in_proj_b_kernel,
        grid=(TOKENS // PB_TM,),
        in_specs=[
            pl.BlockSpec((PB_TM, D_MODEL), lambda i: (i, 0)),
            pl.BlockSpec((pl.Element(n_um), pl.Element(D_MODEL)), lambda i: (row_um, 0)),
            pl.BlockSpec((pl.Element(LANES), pl.Element(D_MODEL)), lambda i: (row_kiwi, 0)),
        ],
        out_specs=pl.BlockSpec((PB_TM, PB_N), lambda i: (i, 0)),
        out_shape=jax.ShapeDtypeStruct((TOKENS, PB_N), f32),
        compiler_params=_params("parallel"),
        name="in_proj_b",
    )(x, wt, wt)


DSA_TQ = 512
DSA_TK = 512
DSA_NCH = SEQ // DSA_TK
DSA_ONES = 16
KEY_NEG_FLT_MAX = -(2 ** 31) + (1 << 23)


def _key_to_f32(key):
    return lax.bitcast_convert_type(key ^ ((key >> 31) & jnp.int32(0x7FFFFFFF)), f32)


def _dsa_kernel(q_ref, k_ref, vt_ref, qi_ref, kk_ref, wt_ref, kf_ref, qf_ref, o_ref,
                sc_scr, sch_scr, acc_scr, qa_scr):
    i = pl.program_id(1)
    q0 = i * DSA_TQ
    nk = (q0 + DSA_TQ - 1) // DSA_TK + 1

    qpos = q0 + lax.broadcasted_iota(i32, (DSA_TK, DSA_TQ), 1)
    kofs = lax.broadcasted_iota(i32, (DSA_TK, DSA_TQ), 0)
    contract_last = (((1,), (1,)), ((), ()))

    def rows8(x):
        return x.reshape(DSA_TK // 8, 8, DSA_TQ)

    wt = wt_ref[...] * (IDX_DIM ** -0.5 * IDX_HEADS ** -0.5)

    def score_chunk(c, carry):
        r0 = pl.multiple_of(c * DSA_TK, DSA_TK)
        kk0 = kk_ref[0, pl.ds(r0, DSA_TK), :]
        kk1 = kk_ref[1, pl.ds(r0, DSA_TK), :]
        acc = jnp.zeros((DSA_TK, DSA_TQ), f32)
        for p in range(IDX_HEADS // 2):
            slab = qi_ref[:, p * LANES:(p + 1) * LANES]
            l0 = lax.dot_general(kk0, slab, contract_last, preferred_element_type=f32)
            l1 = lax.dot_general(kk1, slab, contract_last, preferred_element_type=f32)
            acc = acc + (jnp.maximum(l0, 0.0) * wt[2 * p:2 * p + 1]
                         + jnp.maximum(l1, 0.0) * wt[2 * p + 1:2 * p + 2])
        sc = jnp.where(kofs + r0 <= qpos, acc, -jnp.inf)
        sc_scr[c] = sc
        sch_scr[c] = sc.astype(bf16)
        return carry

    lax.fori_loop(0, nk, score_chunk, 0)

    def count(pred):
        n_acc = 8

        def body(c, accs):
            m = rows8(pred(c))
            accs = list(accs)
            for r in range(DSA_TK // 8):
                accs[r % n_acc] = jnp.where(m[r], accs[r % n_acc] + 1, accs[r % n_acc])
            return tuple(accs)
        accs = lax.fori_loop(0, nk, body, (jnp.zeros((8, DSA_TQ), i32),) * n_acc)
        acc = functools.reduce(lambda a, b: a + b, accs)
        return jnp.sum(acc, axis=0, keepdims=True)

    def count_ge(cand):
        return count(lambda c: sc_scr[c] >= cand)

    def count_ge_coarse(cand):
        n_acc = 2
        one = jnp.ones((), bf16)
        zero = jnp.zeros((), bf16)

        def body(c, accs):
            accs = list(accs)
            for r in range(DSA_TK // BF16_ROWS):
                rows = sch_scr[c, r * BF16_ROWS:(r + 1) * BF16_ROWS, :]
                accs[r % n_acc] = accs[r % n_acc] + jnp.where(rows >= cand, one, zero)
            return tuple(accs)
        accs = lax.fori_loop(0, nk, body, (jnp.zeros((BF16_ROWS, DSA_TQ), bf16),) * n_acc)
        acc = functools.reduce(lambda a, b: a + b, accs)
        return jnp.sum(acc.astype(f32), axis=0, keepdims=True)

    def coarse_cand(key):
        bits = lax.bitcast_convert_type(_key_to_f32(key), i32) & jnp.int32(-65536)
        return lax.bitcast_convert_type(bits, f32).astype(bf16)

    key = jnp.where(count_ge_coarse(jnp.zeros((1, DSA_TQ), bf16)) >= TOPK,
                    jnp.int32(0), jnp.int32(INT_MIN))

    def coarse_step(b, key):
        cand = key | jnp.left_shift(jnp.int32(1), 30 - b)
        return jnp.where(count_ge_coarse(coarse_cand(cand)) >= TOPK, cand, key)

    key = lax.fori_loop(0, 15, coarse_step, key)

    window_bits = 18
    lo = jnp.maximum(key, jnp.int32(INT_MIN + (1 << 17))) - jnp.int32((1 << 15) + 2)

    def fine_step(b, off):
        cand = off | jnp.left_shift(jnp.int32(1), (window_bits - 1) - b)
        return jnp.where(count_ge(_key_to_f32(lo + cand)) >= TOPK, cand, off)

    key = lo + lax.fori_loop(0, window_bits, fine_step, jnp.zeros((1, DSA_TQ), i32))
    thr = _key_to_f32(jnp.maximum(key, jnp.int32(KEY_NEG_FLT_MAX)))

    n_ge = count_ge(thr)

    @pl.when(jnp.max(n_ge) > TOPK)
    def _():
        need = TOPK - count(lambda c: sc_scr[c] > thr)

        def count_eq_below(pos):
            return count(lambda c: (sc_scr[c] == thr) & (kofs + c * DSA_TK < pos))

        def pos_step(b, r):
            cand = r | jnp.left_shift(jnp.int32(1), (SEQ.bit_length() - 2) - b)
            return jnp.where(count_eq_below(cand) < need, cand, r)
        r = lax.fori_loop(0, SEQ.bit_length() - 1, pos_step, jnp.zeros((1, DSA_TQ), i32))
        r = jnp.where(n_ge > TOPK, r, jnp.int32(SEQ))

        def drop(c, carry):
            sc = sc_scr[c]
            sc_scr[c] = jnp.where((sc == thr) & (kofs + c * DSA_TK > r), -jnp.inf, sc)
            return carry
        lax.fori_loop(0, nk, drop, 0)

    acc_scr[...] = jnp.zeros(acc_scr.shape, f32)
    c1 = HEAD_DIM ** -0.5 * LOG2E
    ones = jnp.ones((DSA_ONES, DSA_TK), bf16)

    for h in range(ATT_HEADS):
        qa_scr[h, :, :HEAD_DIM] = q_ref[:, h * HEAD_DIM:(h + 1) * HEAD_DIM]
        qa_scr[h, :, HEAD_DIM:] = jnp.broadcast_to(qf_ref[h, 0:1, :], (DSA_TQ, LANES))

    def attn_chunk(c, ms):
        r0 = pl.multiple_of(c * DSA_TK, DSA_TK)
        madd = jnp.where(sc_scr[c] >= thr, 0.0, -jnp.inf)
        kf = kf_ref[pl.ds(r0, DSA_TK), :]

        def qk(h):
            hd = slice(h * HEAD_DIM, (h + 1) * HEAD_DIM)
            k_aug = jnp.concatenate([k_ref[pl.ds(r0, DSA_TK), hd], kf], axis=1)
            return lax.dot_general(k_aug, qa_scr[h], contract_last, preferred_element_type=f32)

        def pv(h, alpha, p):
            hd = slice(h * HEAD_DIM, (h + 1) * HEAD_DIM)
            vt1 = jnp.concatenate([vt_ref[c, hd, :], ones], axis=0)
            acc_scr[h] = alpha * acc_scr[h] + jnp.dot(vt1, p, preferred_element_type=f32)

        new_ms = []
        s_next = qk(0)
        pending = None
        for h in range(ATT_HEADS):
            s = s_next
            if h + 1 < ATT_HEADS:
                s_next = qk(h + 1)
            if pending is not None:
                pv(*pending)
            t = s * c1 + madd
            m_old = ms[h]
            m_new = jnp.maximum(m_old, jnp.max(t, axis=0, keepdims=True))
            alpha = jnp.exp2(m_old - m_new)
            p = jnp.exp2(t - m_new).astype(bf16)
            pending = (h, alpha, p)
            new_ms.append(m_new)
        pv(*pending)
        return tuple(new_ms)

    m0 = jnp.full((1, DSA_TQ), -1e30, f32)
    lax.fori_loop(0, nk, attn_chunk, (m0,) * ATT_HEADS)

    for h in range(ATT_HEADS):
        o_ref[h * HEAD_DIM:(h + 1) * HEAD_DIM, :] = (
            acc_scr[h, :HEAD_DIM, :] / acc_scr[h, HEAD_DIM:HEAD_DIM + 1, :]).astype(o_ref.dtype)


def _alibi_features():
    assert ATT_HEADS == 8 and HEAD_DIM == 128 and SEQ <= 64 * 64
    pieces, rest = [], math.sqrt(2.0)
    for _ in range(7):
        p = float(np.asarray(rest, np.float32).astype(jnp.bfloat16).astype(np.float64))
        pieces.append(p)
        rest -= p
    qf = np.zeros((ATT_HEADS, BF16_ROWS, LANES), np.float32)
    for h in range(ATT_HEADS):
        for i, p in enumerate(pieces):
            qf[h, :, 2 * i] = qf[h, :, 2 * i + 1] = p * 2.0 ** (2 - h)
    pos = np.arange(SEQ)
    kf = np.zeros((SEQ, LANES), np.float32)
    for i in range(len(pieces)):
        kf[:, 2 * i] = pos - pos % 64
        kf[:, 2 * i + 1] = pos % 64
    return jnp.asarray(kf, bf16), jnp.asarray(qf, bf16)


def _dsa(za, vt, kk, wt_idx):
    nq = SEQ // DSA_TQ
    kf, qf = _alibi_features()
    once = pl.Buffered(1)
    return pl.pallas_call(
        _dsa_kernel,
        grid=(BATCH, nq),
        in_specs=[
            pl.BlockSpec((DSA_TQ, ATT_WIDTH), lambda b, i: (b * nq + i, 0)),
            pl.BlockSpec((SEQ, ATT_WIDTH), lambda b, i: (b, 1), pipeline_mode=once),
            pl.BlockSpec((None, DSA_NCH, ATT_WIDTH, DSA_TK), lambda b, i: (b, 0, 0, 0),
                         pipeline_mode=once),
            pl.BlockSpec((DSA_TQ, IDX_HEADS * IDX_DIM), lambda b, i: (b * nq + i, 3)),
            pl.BlockSpec((2, SEQ, LANES), lambda b, i: (0, b, 0), pipeline_mode=once),
            pl.BlockSpec((IDX_HEADS, DSA_TQ), lambda b, i: (0, b * nq + i)),
            pl.BlockSpec((SEQ, LANES), lambda b, i: (0, 0), pipeline_mode=once),
            pl.BlockSpec((ATT_HEADS, BF16_ROWS, LANES), lambda b, i: (0, 0, 0)),
        ],
        out_specs=pl.BlockSpec((None, ATT_WIDTH, DSA_TQ), lambda b, i: (b, 0, i)),
        out_shape=jax.ShapeDtypeStruct((BATCH, ATT_WIDTH, SEQ), bf16),
        scratch_shapes=[
            pltpu.VMEM((DSA_NCH, DSA_TK, DSA_TQ), f32),
            pltpu.VMEM((DSA_NCH, DSA_TK, DSA_TQ), bf16),
            pltpu.VMEM((ATT_HEADS, HEAD_DIM + DSA_ONES, DSA_TQ), f32),
            pltpu.VMEM((ATT_HEADS, DSA_TQ, HEAD_DIM + LANES), bf16),
        ],
        compiler_params=_params("arbitrary", "arbitrary"),
        name="dsa",
    )(za, za, vt, za, kk, wt_idx, kf, qf)


def _pool_kernel(u_ref, wp_ref, ps_ref, o_ref):
    row = lax.broadcasted_iota(i32, (SEQ, POOL_GROUP), 0)
    for g, win in enumerate(POOL_WINDOWS):
        x = u_ref[:, g * POOL_GROUP:(g + 1) * POOL_GROUP]
        s = x
        k = 1
        while k < win:
            s = s + jnp.where(row >= k, pltpu.roll(s, k, axis=0), 0.0)
            k *= 2
        cnt = jnp.minimum(row + 1, win).astype(f32)
        pooled = (s / cnt - x).astype(bf16)
        mixed = jnp.dot(pooled, wp_ref[g], preferred_element_type=f32)
        o_ref[:, g * POOL_GROUP:(g + 1) * POOL_GROUP] = (
            mixed * ps_ref[:, g * POOL_GROUP:(g + 1) * POOL_GROUP]).astype(o_ref.dtype)


def _pool(zb, w_pool, pool_scale):
    return pl.pallas_call(
        _pool_kernel,
        grid=(BATCH,),
        in_specs=[
            pl.BlockSpec((SEQ, POOL_WIDTH), lambda b: (b, 0)),
            pl.BlockSpec((N_POOL, POOL_GROUP, POOL_GROUP), lambda b: (0, 0, 0)),
            pl.BlockSpec((1, POOL_WIDTH), lambda b: (0, 0)),
        ],
        out_specs=pl.BlockSpec((SEQ, POOL_WIDTH), lambda b: (b, 0)),
        out_shape=jax.ShapeDtypeStruct((TOKENS, POOL_WIDTH), bf16),
        compiler_params=_params("parallel"),
        name="pool",
    )(zb, w_pool, pool_scale)


MEM_TQ = 512


def _mem_attn_kernel(q_ref, k_ref, v_ref, o_ref):
    for h in range(MEM_HEADS):
        sl = slice(h * HEAD_DIM, (h + 1) * HEAD_DIM)
        s = lax.dot_general(q_ref[:, sl].astype(bf16), k_ref[:, sl], (((1,), (1,)), ((), ())),
                            preferred_element_type=f32) * (HEAD_DIM ** -0.5)
        p = jnp.exp(s - jnp.max(s, axis=1, keepdims=True))
        l = jnp.sum(p, axis=1, keepdims=True)
        o = jnp.dot(p.astype(bf16), v_ref[:, sl], preferred_element_type=f32)
        o_ref[:, sl] = (o / l).astype(o_ref.dtype)


def _mem_attn(zb, qcol, kv):
    nq = SEQ // MEM_TQ
    return pl.pallas_call(
        _mem_attn_kernel,
        grid=(BATCH, nq),
        in_specs=[
            pl.BlockSpec((MEM_TQ, MEM_WIDTH), lambda b, i: (b * nq + i, qcol)),
            pl.BlockSpec((MEM_LEN, MEM_WIDTH), lambda b, i: (b, 0)),
            pl.BlockSpec((MEM_LEN, MEM_WIDTH), lambda b, i: (b, 1)),
        ],
        out_specs=pl.BlockSpec((MEM_TQ, MEM_WIDTH), lambda b, i: (b * nq + i, 0)),
        out_shape=jax.ShapeDtypeStruct((TOKENS, MEM_WIDTH), bf16),
        compiler_params=_params("parallel", "parallel"),
        name="mem_attn",
    )(zb, kv, kv)


CMB_TM = 512
CMB_TC = 512


def _combine_ln_kernel(h_ref, a_ref, p_ref, m_ref, wg0_ref, wg1_ref, wg2_ref,
                       bg0_ref, bg1_ref, bg2_ref, wa_ref, wp_ref, wm_ref, wo_ref,
                       g_ref, b_ref, o_ref, hb_ref, acc_ref):
    i = pl.program_id(0)
    j = pl.program_id(1)
    n_tiles = pl.num_programs(0) - 1
    slot = i % 2
    ln_rows = CMB_TM // (D_MODEL // CMB_TC)

    @pl.when((i == 0) & (j == 0))
    def _():
        acc_ref[1] = jnp.zeros(acc_ref.shape[1:], f32)

    @pl.when((i < n_tiles) & (j == 0))
    def _():
        h = h_ref[...]
        hb_ref[...] = h.astype(bf16)
        acc_ref[slot] = ALPHA * h

    def layer_norm_slab():
        rows = pl.ds(pl.multiple_of(j * ln_rows, ln_rows), ln_rows)
        o_ref[rows, :] = _layer_norm(acc_ref[1 - slot, rows, :], g_ref[...], b_ref[...])

    def matmuls():
        hb = hb_ref[...]

        def gate(wg_ref, bg_ref):
            logits = lax.dot_general(hb, wg_ref[...], _NT, preferred_element_type=f32)
            return jax.nn.sigmoid(logits + bg_ref[...])

        y = gate(wg0_ref, bg0_ref) * jnp.dot(a_ref[...], wa_ref[...], preferred_element_type=f32)
        y = y + gate(wg1_ref, bg1_ref) * jnp.dot(p_ref[...], wp_ref[...], preferred_element_type=f32)
        y = y + gate(wg2_ref, bg2_ref) * jnp.dot(m_ref[...], wm_ref[...], preferred_element_type=f32)
        acc_ref[slot] += jnp.dot(y.astype(bf16), wo_ref[...], preferred_element_type=f32)

    @pl.when(i < n_tiles)
    def _():
        layer_norm_slab()
        matmuls()

    @pl.when(i == n_tiles)
    def _():
        layer_norm_slab()


def _combine_ln(h, a, p, m, wt_gate, row_gate, b_gate, w_a, w_p, w_m, w_out, g, b):
    nc = D_MODEL // CMB_TC
    gate_rows = pl.Element(CMB_TC)
    all_cols = pl.Element(D_MODEL)
    assert row_gate % BF16_ROWS == 0

    def gate_row(blk):
        return pl.multiple_of(row_gate + blk * CMB_TC, BF16_ROWS)

    ni = TOKENS // CMB_TM

    def cj(i, j):
        return jnp.where(i < ni, j, nc - 1)

    row = lambda i, j: (jnp.minimum(i, ni - 1), 0)
    col = lambda i, j: (0, cj(i, j))
    return pl.pallas_call(
        _combine_ln_kernel,
        grid=(ni + 1, nc),
        in_specs=[
            pl.BlockSpec((CMB_TM, D_MODEL), row),
            pl.BlockSpec((CMB_TM, ATT_WIDTH), row),
            pl.BlockSpec((CMB_TM, POOL_WIDTH), row),
            pl.BlockSpec((CMB_TM, MEM_WIDTH), row),
            pl.BlockSpec((gate_rows, all_cols), lambda i, j: (gate_row(cj(i, j)), 0)),
            pl.BlockSpec((gate_rows, all_cols), lambda i, j: (gate_row(cj(i, j) + nc), 0)),
            pl.BlockSpec((gate_rows, all_cols), lambda i, j: (gate_row(cj(i, j) + 2 * nc), 0)),
            pl.BlockSpec((1, CMB_TC), lambda i, j: (0, cj(i, j))),
            pl.BlockSpec((1, CMB_TC), lambda i, j: (0, cj(i, j) + nc)),
            pl.BlockSpec((1, CMB_TC), lambda i, j: (0, cj(i, j) + 2 * nc)),
            pl.BlockSpec((ATT_WIDTH, CMB_TC), col),
            pl.BlockSpec((POOL_WIDTH, CMB_TC), col),
            pl.BlockSpec((MEM_WIDTH, CMB_TC), col),
            pl.BlockSpec((CMB_TC, D_MODEL), lambda i, j: (cj(i, j), 0)),
            pl.BlockSpec((1, D_MODEL), lambda i, j: (0, 0)),
            pl.BlockSpec((1, D_MODEL), lambda i, j: (0, 0)),
        ],
        out_specs=pl.BlockSpec((CMB_TM, D_MODEL), lambda i, j: (jnp.maximum(i - 1, 0), 0)),
        out_shape=jax.ShapeDtypeStruct((TOKENS, D_MODEL), f32),
        scratch_shapes=[pltpu.VMEM((CMB_TM, D_MODEL), bf16),
                        pltpu.VMEM((2, CMB_TM, D_MODEL), f32)],
        compiler_params=_params("arbitrary", "arbitrary"),
        name="combine_ln",
    )(h, a, p, m, wt_gate, wt_gate, wt_gate, b_gate, b_gate, b_gate, w_a, w_p, w_m, w_out, g, b)


def kernel(x, mem, w_ffn1_up, w_ffn1_down, ln1_g, ln1_b, w_in, b_gate, w_mem_kv, w_pool,
           pool_scale, w_br_att, w_br_pool, w_br_mem, w_out, ln2_g, ln2_b, w_ffn2_up,
           w_ffn2_down, ln3_g, ln3_b):
    h = x.reshape(TOKENS, D_MODEL)
    memf = mem.reshape(BATCH * MEM_LEN, D_MODEL)
    for l in range(DEPTH):
        c_qi = 3 * ATT_WIDTH + IDX_HEADS * IDX_DIM
        c_wi = c_qi + IDX_DIM + IDX_HEADS
        c_qm = c_wi + POOL_WIDTH + MEM_WIDTH

        z0, w1_a, w1_u, w1_down = _ffn_head(h, w_ffn1_up[l], w_ffn1_down[l])
        h, (wt, w2_up, w2_down) = _ffn_ln(
            h, w1_a, w1_u, w1_down, ln1_g[l][None], ln1_b[l][None], z0=z0,
            casts=((w_in[l].T, None), (w_ffn2_up[l], False), (w_ffn2_down[l], None)))

        za = _matmul(h, wt, bf16, 1024, 1024, "in_proj_a", w_is_transposed=True, n=c_qi)
        zb = _in_proj_b(h, wt, c_wi, c_qi)
        kv = _matmul(memf, w_mem_kv[l].astype(bf16), bf16, BATCH * MEM_LEN, 512, "mem_kv")

        c_ki = POOL_WIDTH + MEM_WIDTH
        ki = zb[:, c_ki:c_ki + IDX_DIM].astype(bf16)
        zk = jnp.zeros_like(ki)
        kk = jnp.stack([jnp.concatenate([ki, zk], axis=1), jnp.concatenate([zk, ki], axis=1)])
        wt_idx = zb[:, c_ki + IDX_DIM:c_ki + IDX_DIM + IDX_HEADS].T
        vt = za[:, 2 * ATT_WIDTH:3 * ATT_WIDTH].reshape(BATCH, DSA_NCH, DSA_TK, ATT_WIDTH)
        vt = vt.transpose(0, 1, 3, 2)

        a = _dsa(za, vt, kk, wt_idx).transpose(0, 2, 1).reshape(TOKENS, ATT_WIDTH)
        p = _pool(zb, w_pool[l].astype(bf16), pool_scale[l][None])
        m = _mem_attn(zb, POOL_WIDTH // MEM_WIDTH, kv)

        h = _combine_ln(h, a, p, m, wt, c_qm, b_gate[l][None], w_br_att[l].astype(bf16),
                        w_br_pool[l].astype(bf16), w_br_mem[l].astype(bf16),
                        w_out[l].astype(bf16), ln2_g[l][None], ln2_b[l][None])

        h, _ = _ffn_ln(h, w2_up, w2_up, w2_down, ln3_g[l][None], ln3_b[l][None])
    return h.reshape(BATCH, SEQ, D_MODEL)
```

```python
import functools
import math

import jax
import jax.numpy as jnp
import numpy as np
from jax import lax
from jax.experimental import pallas as pl
from jax.experimental.pallas import tpu as pltpu

f32 = jnp.float32
bf16 = jnp.bfloat16
i32 = jnp.int32

D_MODEL = 2048
BATCH = 2
SEQ = 4096
DEPTH = 1
MEM_LEN = 256
ATT_HEADS = 8
HEAD_DIM = 128
ATT_WIDTH = ATT_HEADS * HEAD_DIM
IDX_HEADS = 16
IDX_DIM = 64
TOPK = min(256, SEQ // 4)
POOL_WINDOWS = (2, 4, 8, 16)
N_POOL = len(POOL_WINDOWS)
POOL_GROUP = 128
POOL_WIDTH = N_POOL * POOL_GROUP
MEM_HEADS = 4
MEM_WIDTH = MEM_HEADS * HEAD_DIM
N_BRANCH = 3
D_FF = 5632
ALPHA = (2 * DEPTH) ** 0.25
LN_EPS = 1e-5
TOKENS = BATCH * SEQ

LANES = 128
BF16_ROWS = 16
VMEM_LIMIT = 60 * 1024 * 1024

INT_MIN = -(2 ** 31)
LOG2E = math.log2(math.e)


def _params(*sem):
    return pltpu.CompilerParams(dimension_semantics=sem, vmem_limit_bytes=VMEM_LIMIT)


def _layer_norm(y, g, b):
    mu = jnp.mean(y, axis=-1, keepdims=True)
    d = y - mu
    var = jnp.mean(d * d, axis=-1, keepdims=True)
    return d * lax.rsqrt(var + LN_EPS) * g + b


FFN_TM = 512
FFN_TF = 512


FFN_LN_STEPS = 8
FFN_LN_ROWS = FFN_TM // FFN_LN_STEPS


def _ffn_ln_kernel(x_ref, wa_ref, wu_ref, wd_ref, g_ref, b_ref, *rest, n_cast, has_z0):
    if has_z0:
        z0_ref, rest = rest[0], rest[1:]
    cast_in = rest[:n_cast]
    o_ref = rest[n_cast]
    cast_out = rest[n_cast + 1:2 * n_cast + 1]
    xb_ref, acc_ref = rest[2 * n_cast + 1:]
    i = pl.program_id(0)
    j = pl.program_id(1)
    n_tiles = pl.num_programs(0) - 1
    slot = i % 2
    computed = (i < n_tiles) & (i >= 1) if has_z0 else (i < n_tiles)

    for src, dst in zip(cast_in, cast_out):
        dst[...] = src[...].astype(dst.dtype)

    @pl.when((i == 0) & (j == 0))
    def _():
        if has_z0:
            acc_ref[0] = z0_ref[...]
        else:
            acc_ref[1] = jnp.zeros(acc_ref.shape[1:], f32)

    @pl.when(computed & (j == 0))
    def _():
        x = x_ref[...]
        xb_ref[...] = x.astype(bf16)
        acc_ref[slot] = (2.0 * ALPHA) * x

    def matmuls():
        xb = xb_ref[...]
        a = jnp.dot(xb, wa_ref[...], preferred_element_type=f32)
        u = jnp.dot(xb, wu_ref[...], preferred_element_type=f32)
        act = (a * jax.nn.sigmoid(a) * u).astype(bf16)
        acc_ref[slot] += jnp.dot(act, wd_ref[...], preferred_element_type=f32)

    def layer_norm_slab():
        rows = pl.ds(pl.multiple_of(j * FFN_LN_ROWS, FFN_LN_ROWS), FFN_LN_ROWS)
        z = acc_ref[1 - slot, rows, :]
        mu = jnp.mean(z, axis=-1, keepdims=True)
        d = z - mu
        var = jnp.mean(d * d, axis=-1, keepdims=True)
        o_ref[rows, :] = d * lax.rsqrt(var + 4.0 * LN_EPS) * g_ref[...] + b_ref[...]

    @pl.when(computed & (j < FFN_LN_STEPS))
    def _():
        layer_norm_slab()
        matmuls()

    @pl.when(computed & (j >= FFN_LN_STEPS))
    def _():
        matmuls()

    @pl.when((i == n_tiles) & (j < FFN_LN_STEPS))
    def _():
        layer_norm_slab()


def _cast_block_spec(shape, rows_on_j, ni, nj):
    r, c = shape
    if rows_on_j is None:
        br = -(-r // (ni * nj))
        br = -(-br // BF16_ROWS) * BF16_ROWS
        last_block = -(-r // br) - 1
        return pl.BlockSpec((br, c), lambda i, j: (
            jnp.minimum(jnp.minimum(i, ni - 1) * nj + jnp.where(i < ni, j, nj - 1), last_block), 0))
    n_r, n_c = (nj, ni) if rows_on_j else (ni, nj)
    br = -(-r // n_r)
    br = -(-br // BF16_ROWS) * BF16_ROWS
    assert c % n_c == 0 and (c // n_c) % LANES == 0
    def clamp(i, j):
        return jnp.minimum(i, ni - 1), jnp.where(i < ni, j, nj - 1)

    if rows_on_j:
        index_map = lambda i, j: clamp(i, j)[::-1]
    else:
        index_map = lambda i, j: clamp(i, j)
    return pl.BlockSpec((br, c // n_c), index_map)


FFN_HEAD_TF = 256


def _ffn_head_kernel(x_ref, wa_ref, wu_ref, wd_ref, z_ref, wa_out, wu_out, wd_out, xb_ref):
    j = pl.program_id(0)

    @pl.when(j == 0)
    def _():
        x = x_ref[...]
        xb_ref[...] = x.astype(bf16)
        z_ref[...] = (2.0 * ALPHA) * x

    wa = wa_ref[...].astype(bf16)
    wu = wu_ref[...].astype(bf16)
    wd = wd_ref[...].astype(bf16)
    wa_out[...] = wa
    wu_out[...] = wu
    wd_out[...] = wd
    xb = xb_ref[...]
    a = jnp.dot(xb, wa, preferred_element_type=f32)
    u = jnp.dot(xb, wu, preferred_element_type=f32)
    act = (a * jax.nn.sigmoid(a) * u).astype(bf16)
    z_ref[...] += jnp.dot(act, wd, preferred_element_type=f32)


def _ffn_head(x, w_up, w_down):
    nf = D_FF // FFN_HEAD_TF
    return pl.pallas_call(
        _ffn_head_kernel,
        grid=(nf,),
        in_specs=[
            pl.BlockSpec((FFN_TM, D_MODEL), lambda j: (0, 0)),
            pl.BlockSpec((D_MODEL, FFN_HEAD_TF), lambda j: (0, j)),
            pl.BlockSpec((D_MODEL, FFN_HEAD_TF), lambda j: (0, j + nf)),
            pl.BlockSpec((FFN_HEAD_TF, D_MODEL), lambda j: (j, 0)),
        ],
        out_specs=[
            pl.BlockSpec((FFN_TM, D_MODEL), lambda j: (0, 0)),
            pl.BlockSpec((D_MODEL, FFN_HEAD_TF), lambda j: (0, j)),
            pl.BlockSpec((D_MODEL, FFN_HEAD_TF), lambda j: (0, j)),
            pl.BlockSpec((FFN_HEAD_TF, D_MODEL), lambda j: (j, 0)),
        ],
        out_shape=[
            jax.ShapeDtypeStruct((FFN_TM, D_MODEL), f32),
            jax.ShapeDtypeStruct((D_MODEL, D_FF), bf16),
            jax.ShapeDtypeStruct((D_MODEL, D_FF), bf16),
            jax.ShapeDtypeStruct((D_FF, D_MODEL), bf16),
        ],
        scratch_shapes=[pltpu.VMEM((FFN_TM, D_MODEL), bf16)],
        compiler_params=_params("arbitrary"),
        name="ffn_head",
    )(x, w_up, w_up, w_down)


def _ffn_ln(x, w_a, w_u, w_down, g, b, casts=(), z0=None):
    ni = TOKENS // FFN_TM
    nf = D_FF // FFN_TF
    assert FFN_LN_STEPS <= nf
    u0 = (w_u.shape[1] - D_FF) // FFN_TF
    cast_specs = [_cast_block_spec(w.shape, rows_on_j, ni, nf) for w, rows_on_j in casts]
    z0_specs = [] if z0 is None else [pl.BlockSpec((FFN_TM, D_MODEL), lambda i, j: (0, 0))]
    z0_args = [] if z0 is None else [z0]

    def wj(i, j):
        j = jnp.where(i < ni, j, nf - 1)
        return j if z0 is None else jnp.where(i >= 1, j, 0)

    outs = pl.pallas_call(
        functools.partial(_ffn_ln_kernel, n_cast=len(casts), has_z0=z0 is not None),
        grid=(ni + 1, nf),
        in_specs=[
            pl.BlockSpec((FFN_TM, D_MODEL), lambda i, j: (jnp.minimum(i, ni - 1), 0)),
            pl.BlockSpec((D_MODEL, FFN_TF), lambda i, j: (0, wj(i, j))),
            pl.BlockSpec((D_MODEL, FFN_TF), lambda i, j: (0, wj(i, j) + u0)),
            pl.BlockSpec((FFN_TF, D_MODEL), lambda i, j: (wj(i, j), 0)),
            pl.BlockSpec((1, D_MODEL), lambda i, j: (0, 0)),
            pl.BlockSpec((1, D_MODEL), lambda i, j: (0, 0)),
        ] + z0_specs + cast_specs,
        out_specs=[pl.BlockSpec((FFN_TM, D_MODEL), lambda i, j: (jnp.maximum(i - 1, 0), 0))]
        + cast_specs,
        out_shape=[jax.ShapeDtypeStruct((TOKENS, D_MODEL), f32)]
        + [jax.ShapeDtypeStruct(w.shape, bf16) for w, _ in casts],
        scratch_shapes=[pltpu.VMEM((FFN_TM, D_MODEL), bf16),
                        pltpu.VMEM((2, FFN_TM, D_MODEL), f32)],
        compiler_params=_params("arbitrary", "arbitrary"),
        name="ffn_ln",
    )(x, w_a, w_u, w_down, g, b, *z0_args, *[w for w, _ in casts])
    return outs[0], outs[1:]


_NT = (((1,), (1,)), ((), ()))


def _matmul_kernel(x_ref, w_ref, o_ref, xb_ref, *, w_is_transposed):
    @pl.when(pl.program_id(1) == 0)
    def _():
        xb_ref[...] = x_ref[...].astype(bf16)

    if w_is_transposed:
        y = lax.dot_general(xb_ref[...], w_ref[...], _NT, preferred_element_type=f32)
    else:
        y = jnp.dot(xb_ref[...], w_ref[...], preferred_element_type=f32)
    o_ref[...] = y.astype(o_ref.dtype)


def _matmul(x, w, out_dtype, tm, tn, name, w_is_transposed=False, n=None):
    m, k = x.shape
    if n is None:
        n = w.shape[0] if w_is_transposed else w.shape[1]
    assert n % tn == 0 and m % tm == 0
    if w_is_transposed:
        w_spec = pl.BlockSpec((tn, k), lambda i, j: (j, 0))
    else:
        w_spec = pl.BlockSpec((k, tn), lambda i, j: (0, j))
    return pl.pallas_call(
        functools.partial(_matmul_kernel, w_is_transposed=w_is_transposed),
        grid=(m // tm, n // tn),
        in_specs=[pl.BlockSpec((tm, k), lambda i, j: (i, 0)), w_spec],
        out_specs=pl.BlockSpec((tm, tn), lambda i, j: (i, j)),
        out_shape=jax.ShapeDtypeStruct((m, n), out_dtype),
        scratch_shapes=[pltpu.VMEM((tm, k), bf16)],
        compiler_params=_params("parallel", "arbitrary"),
        name=name,
    )(x, w)


PB_TM = 512
PB_N = POOL_WIDTH + MEM_WIDTH + LANES


def _in_proj_b_kernel(x_ref, wum_ref, wk_ref, o_ref):
    xb = x_ref[...].astype(bf16)
    n_um = POOL_WIDTH + MEM_WIDTH
    o_ref[:, :n_um] = lax.dot_general(xb, wum_ref[...], _NT, preferred_element_type=f32)
    o_ref[:, n_um:] = lax.dot_general(xb, wk_ref[...], _NT, preferred_element_type=f32)


def _in_proj_b(x, wt, row_um, row_kiwi):
    n_um = POOL_WIDTH + MEM_WIDTH
    return pl.pallas_call(
        _in_proj_b_kernel,
        grid=(TOKENS // PB_TM,),
        in_specs=[
            pl.BlockSpec((PB_TM, D_MODEL), lambda i: (i, 0)),
            pl.BlockSpec((pl.Element(n_um), pl.Element(D_MODEL)), lambda i: (row_um, 0)),
            pl.BlockSpec((pl.Element(LANES), pl.Element(D_MODEL)), lambda i: (row_kiwi, 0)),
        ],
        out_specs=pl.BlockSpec((PB_TM, PB_N), lambda i: (i, 0)),
        out_shape=jax.ShapeDtypeStruct((TOKENS, PB_N), f32),
        compiler_params=_params("parallel"),
        name="in_proj_b",
    )(x, wt, wt)


DSA_TQ = 512
DSA_TK = 512
DSA_NCH = SEQ // DSA_TK
DSA_ONES = 16
KEY_NEG_FLT_MAX = -(2 ** 31) + (1 << 23)


def _key_to_f32(key):
    return lax.bitcast_convert_type(key ^ ((key >> 31) & jnp.int32(0x7FFFFFFF)), f32)


def _dsa_kernel(q_ref, k_ref, vt_ref, qi_ref, kk_ref, wt_ref, kf_ref, qf_ref, o_ref,
                sc_scr, sch_scr, acc_scr, qa_scr):
    i = pl.program_id(1)
    q0 = i * DSA_TQ
    nk = (q0 + DSA_TQ - 1) // DSA_TK + 1

    qpos = q0 + lax.broadcasted_iota(i32, (DSA_TK, DSA_TQ), 1)
    kofs = lax.broadcasted_iota(i32, (DSA_TK, DSA_TQ), 0)
    contract_last = (((1,), (1,)), ((), ()))

    def rows8(x):
        return x.reshape(DSA_TK // 8, 8, DSA_TQ)

    wt = wt_ref[...] * (IDX_DIM ** -0.5 * IDX_HEADS ** -0.5)

    def score_chunk(c, carry):
        r0 = pl.multiple_of(c * DSA_TK, DSA_TK)
        kk0 = kk_ref[0, pl.ds(r0, DSA_TK), :]
        kk1 = kk_ref[1, pl.ds(r0, DSA_TK), :]
        acc = jnp.zeros((DSA_TK, DSA_TQ), f32)
        for p in range(IDX_HEADS // 2):
            slab = qi_ref[:, p * LANES:(p + 1) * LANES]
            l0 = lax.dot_general(kk0, slab, contract_last, preferred_element_type=f32)
            l1 = lax.dot_general(kk1, slab, contract_last, preferred_element_type=f32)
            acc = acc + (jnp.maximum(l0, 0.0) * wt[2 * p:2 * p + 1]
                         + jnp.maximum(l1, 0.0) * wt[2 * p + 1:2 * p + 2])
        sc = jnp.where(kofs + r0 <= qpos, acc, -jnp.inf)
        sc_scr[c] = sc
        sch_scr[c] = sc.astype(bf16)
        return carry

    lax.fori_loop(0, nk, score_chunk, 0)

    def count(pred):
        n_acc = 8

        def body(c, accs):
            m = rows8(pred(c))
            accs = list(accs)
            for r in range(DSA_TK // 8):
                accs[r % n_acc] = jnp.where(m[r], accs[r % n_acc] + 1, accs[r % n_acc])
            return tuple(accs)
        accs = lax.fori_loop(0, nk, body, (jnp.zeros((8, DSA_TQ), i32),) * n_acc)
        acc = functools.reduce(lambda a, b: a + b, accs)
        return jnp.sum(acc, axis=0, keepdims=True)

    def count_ge(cand):
        return count(lambda c: sc_scr[c] >= cand)

    def count_ge_coarse(cand):
        n_acc = 2
        one = jnp.ones((), bf16)
        zero = jnp.zeros((), bf16)

        def body(c, accs):
            accs = list(accs)
            for r in range(DSA_TK // BF16_ROWS):
                rows = sch_scr[c, r * BF16_ROWS:(r + 1) * BF16_ROWS, :]
                accs[r % n_acc] = accs[r % n_acc] + jnp.where(rows >= cand, one, zero)
            return tuple(accs)
        accs = lax.fori_loop(0, nk, body, (jnp.zeros((BF16_ROWS, DSA_TQ), bf16),) * n_acc)
        acc = functools.reduce(lambda a, b: a + b, accs)
        return jnp.sum(acc.astype(f32), axis=0, keepdims=True)

    def coarse_cand(key):
        bits = lax.bitcast_convert_type(_key_to_f32(key), i32) & jnp.int32(-65536)
        return lax.bitcast_convert_type(bits, f32).astype(bf16)

    key = jnp.where(count_ge_coarse(jnp.zeros((1, DSA_TQ), bf16)) >= TOPK,
                    jnp.int32(0), jnp.int32(INT_MIN))

    def coarse_step(b, key):
        cand = key | jnp.left_shift(jnp.int32(1), 30 - b)
        return jnp.where(count_ge_coarse(coarse_cand(cand)) >= TOPK, cand, key)

    key = lax.fori_loop(0, 15, coarse_step, key)

    window_bits = 18
    lo = jnp.maximum(key, jnp.int32(INT_MIN + (1 << 17))) - jnp.int32((1 << 15) + 2)

    def fine_step(b, off):
        cand = off | jnp.left_shift(jnp.int32(1), (window_bits - 1) - b)
        return jnp.where(count_ge(_key_to_f32(lo + cand)) >= TOPK, cand, off)

    key = lo + lax.fori_loop(0, window_bits, fine_step, jnp.zeros((1, DSA_TQ), i32))
    thr = _key_to_f32(jnp.maximum(key, jnp.int32(KEY_NEG_FLT_MAX)))

    n_ge = count_ge(thr)

    @pl.when(jnp.max(n_ge) > TOPK)
    def _():
        need = TOPK - count(lambda c: sc_scr[c] > thr)

        def count_eq_below(pos):
            return count(lambda c: (sc_scr[c] == thr) & (kofs + c * DSA_TK < pos))

        def pos_step(b, r):
            cand = r | jnp.left_shift(jnp.int32(1), (SEQ.bit_length() - 2) - b)
            return jnp.where(count_eq_below(cand) < need, cand, r)
        r = lax.fori_loop(0, SEQ.bit_length() - 1, pos_step, jnp.zeros((1, DSA_TQ), i32))
        r = jnp.where(n_ge > TOPK, r, jnp.int32(SEQ))

        def drop(c, carry):
            sc = sc_scr[c]
            sc_scr[c] = jnp.where((sc == thr) & (kofs + c * DSA_TK > r), -jnp.inf, sc)
            return carry
        lax.fori_loop(0, nk, drop, 0)

    acc_scr[...] = jnp.zeros(acc_scr.shape, f32)
    c1 = HEAD_DIM ** -0.5 * LOG2E
    ones = jnp.ones((DSA_ONES, DSA_TK), bf16)

    for h in range(ATT_HEADS):
        qa_scr[h, :, :HEAD_DIM] = q_ref[:, h * HEAD_DIM:(h + 1) * HEAD_DIM]
        qa_scr[h, :, HEAD_DIM:] = jnp.broadcast_to(qf_ref[h, 0:1, :], (DSA_TQ, LANES))

    def attn_chunk(c, ms):
        r0 = pl.multiple_of(c * DSA_TK, DSA_TK)
        madd = jnp.where(sc_scr[c] >= thr, 0.0, -jnp.inf)
        kf = kf_ref[pl.ds(r0, DSA_TK), :]

        def qk(h):
            hd = slice(h * HEAD_DIM, (h + 1) * HEAD_DIM)
            k_aug = jnp.concatenate([k_ref[pl.ds(r0, DSA_TK), hd], kf], axis=1)
            return lax.dot_general(k_aug, qa_scr[h], contract_last, preferred_element_type=f32)

        def pv(h, alpha, p):
            hd = slice(h * HEAD_DIM, (h + 1) * HEAD_DIM)
            vt1 = jnp.concatenate([vt_ref[c, hd, :], ones], axis=0)
            acc_scr[h] = alpha * acc_scr[h] + jnp.dot(vt1, p, preferred_element_type=f32)

        new_ms = []
        s_next = qk(0)
        pending = None
        for h in range(ATT_HEADS):
            s = s_next
            if h + 1 < ATT_HEADS:
                s_next = qk(h + 1)
            if pending is not None:
                pv(*pending)
            t = s * c1 + madd
            m_old = ms[h]
            m_new = jnp.maximum(m_old, jnp.max(t, axis=0, keepdims=True))
            alpha = jnp.exp2(m_old - m_new)
            p = jnp.exp2(t - m_new).astype(bf16)
            pending = (h, alpha, p)
            new_ms.append(m_new)
        pv(*pending)
        return tuple(new_ms)

    m0 = jnp.full((1, DSA_TQ), -1e30, f32)
    lax.fori_loop(0, nk, attn_chunk, (m0,) * ATT_HEADS)

    for h in range(ATT_HEADS):
        o_ref[h * HEAD_DIM:(h + 1) * HEAD_DIM, :] = (
            acc_scr[h, :HEAD_DIM, :] / acc_scr[h, HEAD_DIM:HEAD_DIM + 1, :]).astype(o_ref.dtype)


def _alibi_features():
    assert ATT_HEADS == 8 and HEAD_DIM == 128 and SEQ <= 64 * 64
    pieces, rest = [], math.sqrt(2.0)
    for _ in range(7):
        p = float(np.asarray(rest, np.float32).astype(jnp.bfloat16).astype(np.float64))
        pieces.append(p)
        rest -= p
    qf = np.zeros((ATT_HEADS, BF16_ROWS, LANES), np.float32)
    for h in range(ATT_HEADS):
        for i, p in enumerate(pieces):
            qf[h, :, 2 * i] = qf[h, :, 2 * i + 1] = p * 2.0 ** (2 - h)
    pos = np.arange(SEQ)
    kf = np.zeros((SEQ, LANES), np.float32)
    for i in range(len(pieces)):
        kf[:, 2 * i] = pos - pos % 64
        kf[:, 2 * i + 1] = pos % 64
    return jnp.asarray(kf, bf16), jnp.asarray(qf, bf16)


def _dsa(za, vt, kk, wt_idx):
    nq = SEQ // DSA_TQ
    kf, qf = _alibi_features()
    once = pl.Buffered(1)
    return pl.pallas_call(
        _dsa_kernel,
        grid=(BATCH, nq),
        in_specs=[
            pl.BlockSpec((DSA_TQ, ATT_WIDTH), lambda b, i: (b * nq + i, 0)),
            pl.BlockSpec((SEQ, ATT_WIDTH), lambda b, i: (b, 1), pipeline_mode=once),
            pl.BlockSpec((None, DSA_NCH, ATT_WIDTH, DSA_TK), lambda b, i: (b, 0, 0, 0),
                         pipeline_mode=once),
            pl.BlockSpec((DSA_TQ, IDX_HEADS * IDX_DIM), lambda b, i: (b * nq + i, 3)),
            pl.BlockSpec((2, SEQ, LANES), lambda b, i: (0, b, 0), pipeline_mode=once),
            pl.BlockSpec((IDX_HEADS, DSA_TQ), lambda b, i: (0, b * nq + i)),
            pl.BlockSpec((SEQ, LANES), lambda b, i: (0, 0), pipeline_mode=once),
            pl.BlockSpec((ATT_HEADS, BF16_ROWS, LANES), lambda b, i: (0, 0, 0)),
        ],
        out_specs=pl.BlockSpec((None, ATT_WIDTH, DSA_TQ), lambda b, i: (b, 0, i)),
        out_shape=jax.ShapeDtypeStruct((BATCH, ATT_WIDTH, SEQ), bf16),
        scratch_shapes=[
            pltpu.VMEM((DSA_NCH, DSA_TK, DSA_TQ), f32),
            pltpu.VMEM((DSA_NCH, DSA_TK, DSA_TQ), bf16),
            pltpu.VMEM((ATT_HEADS, HEAD_DIM + DSA_ONES, DSA_TQ), f32),
            pltpu.VMEM((ATT_HEADS, DSA_TQ, HEAD_DIM + LANES), bf16),
        ],
        compiler_params=_params("arbitrary", "arbitrary"),
        name="dsa",
    )(za, za, vt, za, kk, wt_idx, kf, qf)


def _pool_kernel(u_ref, wp_ref, ps_ref, o_ref):
    row = lax.broadcasted_iota(i32, (SEQ, POOL_GROUP), 0)
    for g, win in enumerate(POOL_WINDOWS):
        x = u_ref[:, g * POOL_GROUP:(g + 1) * POOL_GROUP]
        s = x
        k = 1
        while k < win:
            s = s + jnp.where(row >= k, pltpu.roll(s, k, axis=0), 0.0)
            k *= 2
        cnt = jnp.minimum(row + 1, win).astype(f32)
        pooled = (s / cnt - x).astype(bf16)
        mixed = jnp.dot(pooled, wp_ref[g], preferred_element_type=f32)
        o_ref[:, g * POOL_GROUP:(g + 1) * POOL_GROUP] = (
            mixed * ps_ref[:, g * POOL_GROUP:(g + 1) * POOL_GROUP]).astype(o_ref.dtype)


def _pool(zb, w_pool, pool_scale):
    return pl.pallas_call(
        _pool_kernel,
        grid=(BATCH,),
        in_specs=[
            pl.BlockSpec((SEQ, POOL_WIDTH), lambda b: (b, 0)),
            pl.BlockSpec((N_POOL, POOL_GROUP, POOL_GROUP), lambda b: (0, 0, 0)),
            pl.BlockSpec((1, POOL_WIDTH), lambda b: (0, 0)),
        ],
        out_specs=pl.BlockSpec((SEQ, POOL_WIDTH), lambda b: (b, 0)),
        out_shape=jax.ShapeDtypeStruct((TOKENS, POOL_WIDTH), bf16),
        compiler_params=_params("parallel"),
        name="pool",
    )(zb, w_pool, pool_scale)


MEM_TQ = 512


def _mem_attn_kernel(q_ref, k_ref, v_ref, o_ref):
    for h in range(MEM_HEADS):
        sl = slice(h * HEAD_DIM, (h + 1) * HEAD_DIM)
        s = lax.dot_general(q_ref[:, sl].astype(bf16), k_ref[:, sl], (((1,), (1,)), ((), ())),
                            preferred_element_type=f32) * (HEAD_DIM ** -0.5)
        p = jnp.exp(s - jnp.max(s, axis=1, keepdims=True))
        l = jnp.sum(p, axis=1, keepdims=True)
        o = jnp.dot(p.astype(bf16), v_ref[:, sl], preferred_element_type=f32)
        o_ref[:, sl] = (o / l).astype(o_ref.dtype)


def _mem_attn(zb, qcol, kv):
    nq = SEQ // MEM_TQ
    return pl.pallas_call(
        _mem_attn_kernel,
        grid=(BATCH, nq),
        in_specs=[
            pl.BlockSpec((MEM_TQ, MEM_WIDTH), lambda b, i: (b * nq + i, qcol)),
            pl.BlockSpec((MEM_LEN, MEM_WIDTH), lambda b, i: (b, 0)),
            pl.BlockSpec((MEM_LEN, MEM_WIDTH), lambda b, i: (b, 1)),
        ],
        out_specs=pl.BlockSpec((MEM_TQ, MEM_WIDTH), lambda b, i: (b * nq + i, 0)),
        out_shape=jax.ShapeDtypeStruct((TOKENS, MEM_WIDTH), bf16),
        compiler_params=_params("parallel", "parallel"),
        name="mem_attn",
    )(zb, kv, kv)


CMB_TM = 512
CMB_TC = 512


def _combine_ln_kernel(h_ref, a_ref, p_ref, m_ref, wg0_ref, wg1_ref, wg2_ref,
                       bg0_ref, bg1_ref, bg2_ref, wa_ref, wp_ref, wm_ref, wo_ref,
                       g_ref, b_ref, o_ref, hb_ref, acc_ref):
    i = pl.program_id(0)
    j = pl.program_id(1)
    n_tiles = pl.num_programs(0) - 1
    slot = i % 2
    ln_rows = CMB_TM // (D_MODEL // CMB_TC)

    @pl.when((i == 0) & (j == 0))
    def _():
        acc_ref[1] = jnp.zeros(acc_ref.shape[1:], f32)

    @pl.when((i < n_tiles) & (j == 0))
    def _():
        h = h_ref[...]
        hb_ref[...] = h.astype(bf16)
        acc_ref[slot] = ALPHA * h

    def layer_norm_slab():
        rows = pl.ds(pl.multiple_of(j * ln_rows, ln_rows), ln_rows)
        o_ref[rows, :] = _layer_norm(acc_ref[1 - slot, rows, :], g_ref[...], b_ref[...])

    def matmuls():
        hb = hb_ref[...]

        def gate(wg_ref, bg_ref):
            logits = lax.dot_general(hb, wg_ref[...], _NT, preferred_element_type=f32)
            return jax.nn.sigmoid(logits + bg_ref[...])

        y = gate(wg0_ref, bg0_ref) * jnp.dot(a_ref[...], wa_ref[...], preferred_element_type=f32)
        y = y + gate(wg1_ref, bg1_ref) * jnp.dot(p_ref[...], wp_ref[...], preferred_element_type=f32)
        y = y + gate(wg2_ref, bg2_ref) * jnp.dot(m_ref[...], wm_ref[...], preferred_element_type=f32)
        acc_ref[slot] += jnp.dot(y.astype(bf16), wo_ref[...], preferred_element_type=f32)

    @pl.when(i < n_tiles)
    def _():
        layer_norm_slab()
        matmuls()

    @pl.when(i == n_tiles)
    def _():
        layer_norm_slab()


def _combine_ln(h, a, p, m, wt_gate, row_gate, b_gate, w_a, w_p, w_m, w_out, g, b):
    nc = D_MODEL // CMB_TC
    gate_rows = pl.Element(CMB_TC)
    all_cols = pl.Element(D_MODEL)
    assert row_gate % BF16_ROWS == 0

    def gate_row(blk):
        return pl.multiple_of(row_gate + blk * CMB_TC, BF16_ROWS)

    ni = TOKENS // CMB_TM

    def cj(i, j):
        return jnp.where(i < ni, j, nc - 1)

    row = lambda i, j: (jnp.minimum(i, ni - 1), 0)
    col = lambda i, j: (0, cj(i, j))
    return pl.pallas_call(
        _combine_ln_kernel,
        grid=(ni + 1, nc),
        in_specs=[
            pl.BlockSpec((CMB_TM, D_MODEL), row),
            pl.BlockSpec((CMB_TM, ATT_WIDTH), row),
            pl.BlockSpec((CMB_TM, POOL_WIDTH), row),
            pl.BlockSpec((CMB_TM, MEM_WIDTH), row),
            pl.BlockSpec((gate_rows, all_cols), lambda i, j: (gate_row(cj(i, j)), 0)),
            pl.BlockSpec((gate_rows, all_cols), lambda i, j: (gate_row(cj(i, j) + nc), 0)),
            pl.BlockSpec((gate_rows, all_cols), lambda i, j: (gate_row(cj(i, j) + 2 * nc), 0)),
            pl.BlockSpec((1, CMB_TC), lambda i, j: (0, cj(i, j))),
            pl.BlockSpec((1, CMB_TC), lambda i, j: (0, cj(i, j) + nc)),
            pl.BlockSpec((1, CMB_TC), lambda i, j: (0, cj(i, j) + 2 * nc)),
            pl.BlockSpec((ATT_WIDTH, CMB_TC), col),
            pl.BlockSpec((POOL_WIDTH, CMB_TC), col),
            pl.BlockSpec((MEM_WIDTH, CMB_TC), col),
            pl.BlockSpec((CMB_TC, D_MODEL), lambda i, j: (cj(i, j), 0)),
            pl.BlockSpec((1, D_MODEL), lambda i, j: (0, 0)),
            pl.BlockSpec((1, D_MODEL), lambda i, j: (0, 0)),
        ],
        out_specs=pl.BlockSpec((CMB_TM, D_MODEL), lambda i, j: (jnp.maximum(i - 1, 0), 0)),
        out_shape=jax.ShapeDtypeStruct((TOKENS, D_MODEL), f32),
        scratch_shapes=[pltpu.VMEM((CMB_TM, D_MODEL), bf16),
                        pltpu.VMEM((2, CMB_TM, D_MODEL), f32)],
        compiler_params=_params("arbitrary", "arbitrary"),
        name="combine_ln",
    )(h, a, p, m, wt_gate, wt_gate, wt_gate, b_gate, b_gate, b_gate, w_a, w_p, w_m, w_out, g, b)


def kernel(x, mem, w_ffn1_up, w_ffn1_down, ln1_g, ln1_b, w_in, b_gate, w_mem_kv, w_pool,
           pool_scale, w_br_att, w_br_pool, w_br_mem, w_out, ln2_g, ln2_b, w_ffn2_up,
           w_ffn2_down, ln3_g, ln3_b):
    h = x.reshape(TOKENS, D_MODEL)
    memf = mem.reshape(BATCH * MEM_LEN, D_MODEL)
    for l in range(DEPTH):
        c_qi = 3 * ATT_WIDTH + IDX_HEADS * IDX_DIM
        c_wi = c_qi + IDX_DIM + IDX_HEADS
        c_qm = c_wi + POOL_WIDTH + MEM_WIDTH

        z0, w1_a, w1_u, w1_down = _ffn_head(h, w_ffn1_up[l], w_ffn1_down[l])
        h, (wt, w2_up, w2_down) = _ffn_ln(
            h, w1_a, w1_u, w1_down, ln1_g[l][None], ln1_b[l][None], z0=z0,
            casts=((w_in[l].T, None), (w_ffn2_up[l], False), (w_ffn2_down[l], None)))

        za = _matmul(h, wt, bf16, 1024, 1024, "in_proj_a", w_is_transposed=True, n=c_qi)
        zb = _in_proj_b(h, wt, c_wi, c_qi)
        kv = _matmul(memf, w_mem_kv[l].astype(bf16), bf16, BATCH * MEM_LEN, 512, "mem_kv")

        c_ki = POOL_WIDTH + MEM_WIDTH
        ki = zb[:, c_ki:c_ki + IDX_DIM].astype(bf16)
        zk = jnp.zeros_like(ki)
        kk = jnp.stack([jnp.concatenate([ki, zk], axis=1), jnp.concatenate([zk, ki], axis=1)])
        wt_idx = zb[:, c_ki + IDX_DIM:c_ki + IDX_DIM + IDX_HEADS].T
        vt = za[:, 2 * ATT_WIDTH:3 * ATT_WIDTH].reshape(BATCH, DSA_NCH, DSA_TK, ATT_WIDTH)
        vt = vt.transpose(0, 1, 3, 2)

        a = _dsa(za, vt, kk, wt_idx).transpose(0, 2, 1).reshape(TOKENS, ATT_WIDTH)
        p = _pool(zb, w_pool[l].astype(bf16), pool_scale[l][None])
        m = _mem_attn(zb, POOL_WIDTH // MEM_WIDTH, kv)

        h = _combine_ln(h, a, p, m, wt, c_qm, b_gate[l][None], w_br_att[l].astype(bf16),
                        w_br_pool[l].astype(bf16), w_br_mem[l].astype(bf16),
                        w_out[l].astype(bf16), ln2_g[l][None], ln2_b[l][None])

        h, _ = _ffn_ln(h, w2_up, w2_up, w2_down, ln3_g[l][None], ln3_b[l][None])
    return h.reshape(BATCH, SEQ, D_MODEL)
```

```python
import functools
import math

import jax
import jax.numpy as jnp
import numpy as np
from jax import lax
from jax.experimental import pallas as pl
from jax.experimental.pallas import tpu as pltpu

f32 = jnp.float32
bf16 = jnp.bfloat16
i32 = jnp.int32

D_MODEL = 2048
BATCH = 2
SEQ = 4096
DEPTH = 1
MEM_LEN = 256
ATT_HEADS = 8
HEAD_DIM = 128
ATT_WIDTH = ATT_HEADS * HEAD_DIM
IDX_HEADS = 16
IDX_DIM = 64
TOPK = min(256, SEQ // 4)
POOL_WINDOWS = (2, 4, 8, 16)
N_POOL = len(POOL_WINDOWS)
POOL_GROUP = 128
POOL_WIDTH = N_POOL * POOL_GROUP
MEM_HEADS = 4
MEM_WIDTH = MEM_HEADS * HEAD_DIM
N_BRANCH = 3
D_FF = 5632
ALPHA = (2 * DEPTH) ** 0.25
LN_EPS = 1e-5
TOKENS = BATCH * SEQ

LANES = 128
BF16_ROWS = 16
VMEM_LIMIT = 60 * 1024 * 1024

INT_MIN = -(2 ** 31)
LOG2E = math.log2(math.e)


def _params(*sem):
    return pltpu.CompilerParams(dimension_semantics=sem, vmem_limit_bytes=VMEM_LIMIT)


def _layer_norm(y, g, b):
    mu = jnp.mean(y, axis=-1, keepdims=True)
    d = y - mu
    var = jnp.mean(d * d, axis=-1, keepdims=True)
    return d * lax.rsqrt(var + LN_EPS) * g + b


FFN_TM = 512
FFN_TF = 512


FFN_LN_STEPS = 8
FFN_LN_ROWS = FFN_TM // FFN_LN_STEPS


def _ffn_ln_kernel(x_ref, wa_ref, wu_ref, wd_ref, g_ref, b_ref, *rest, n_cast, has_z0):
    if has_z0:
        z0_ref, rest = rest[0], rest[1:]
    cast_in = rest[:n_cast]
    o_ref = rest[n_cast]
    cast_out = rest[n_cast + 1:2 * n_cast + 1]
    xb_ref, acc_ref = rest[2 * n_cast + 1:]
    i = pl.program_id(0)
    j = pl.program_id(1)
    n_tiles = pl.num_programs(0) - 1
    slot = (i + 1) % 2 if has_z0 else i % 2
    computed = i < n_tiles

    for src, dst in zip(cast_in, cast_out):
        dst[...] = src[...].astype(dst.dtype)

    @pl.when((i == 0) & (j == 0))
    def _():
        if has_z0:
            acc_ref[0] = z0_ref[...]
        else:
            acc_ref[1] = jnp.zeros(acc_ref.shape[1:], f32)

    @pl.when(computed & (j == 0))
    def _():
        x = x_ref[...]
        xb_ref[...] = x.astype(bf16)
        acc_ref[slot] = (2.0 * ALPHA) * x

    def matmuls():
        xb = xb_ref[...]
        a = jnp.dot(xb, wa_ref[...], preferred_element_type=f32)
        u = jnp.dot(xb, wu_ref[...], preferred_element_type=f32)
        act = (a * jax.nn.sigmoid(a) * u).astype(bf16)
        acc_ref[slot] += jnp.dot(act, wd_ref[...], preferred_element_type=f32)

    def layer_norm_slab():
        rows = pl.ds(pl.multiple_of(j * FFN_LN_ROWS, FFN_LN_ROWS), FFN_LN_ROWS)
        z = acc_ref[1 - slot, rows, :]
        mu = jnp.mean(z, axis=-1, keepdims=True)
        d = z - mu
        var = jnp.mean(d * d, axis=-1, keepdims=True)
        o_ref[rows, :] = d * lax.rsqrt(var + 4.0 * LN_EPS) * g_ref[...] + b_ref[...]

    @pl.when(computed & (j < FFN_LN_STEPS))
    def _():
        layer_norm_slab()
        matmuls()

    @pl.when(computed & (j >= FFN_LN_STEPS))
    def _():
        matmuls()

    @pl.when((i == n_tiles) & (j < FFN_LN_STEPS))
    def _():
        layer_norm_slab()


def _cast_block_spec(shape, rows_on_j, ni, nj):
    r, c = shape
    if rows_on_j is None:
        br = -(-r // (ni * nj))
        br = -(-br // BF16_ROWS) * BF16_ROWS
        last_block = -(-r // br) - 1
        return pl.BlockSpec((br, c), lambda i, j: (
            jnp.minimum(jnp.minimum(i, ni - 1) * nj + jnp.where(i < ni, j, nj - 1), last_block), 0))
    n_r, n_c = (nj, ni) if rows_on_j else (ni, nj)
    br = -(-r // n_r)
    br = -(-br // BF16_ROWS) * BF16_ROWS
    assert c % n_c == 0 and (c // n_c) % LANES == 0
    def clamp(i, j):
        return jnp.minimum(i, ni - 1), jnp.where(i < ni, j, nj - 1)

    if rows_on_j:
        index_map = lambda i, j: clamp(i, j)[::-1]
    else:
        index_map = lambda i, j: clamp(i, j)
    return pl.BlockSpec((br, c // n_c), index_map)


FFN_HEAD_TF = 256


def _ffn_head_kernel(x_ref, wa_ref, wu_ref, wd_ref, z_ref, wa_out, wu_out, wd_out, xb_ref):
    j = pl.program_id(0)

    @pl.when(j == 0)
    def _():
        x = x_ref[...]
        xb_ref[...] = x.astype(bf16)
        z_ref[...] = (2.0 * ALPHA) * x

    wa = wa_ref[...].astype(bf16)
    wu = wu_ref[...].astype(bf16)
    wd = wd_ref[...].astype(bf16)
    wa_out[...] = wa
    wu_out[...] = wu
    wd_out[...] = wd
    xb = xb_ref[...]
    a = jnp.dot(xb, wa, preferred_element_type=f32)
    u = jnp.dot(xb, wu, preferred_element_type=f32)
    act = (a * jax.nn.sigmoid(a) * u).astype(bf16)
    z_ref[...] += jnp.dot(act, wd, preferred_element_type=f32)


def _ffn_head(x, w_up, w_down):
    nf = D_FF // FFN_HEAD_TF
    return pl.pallas_call(
        _ffn_head_kernel,
        grid=(nf,),
        in_specs=[
            pl.BlockSpec((FFN_TM, D_MODEL), lambda j: (0, 0)),
            pl.BlockSpec((D_MODEL, FFN_HEAD_TF), lambda j: (0, j)),
            pl.BlockSpec((D_MODEL, FFN_HEAD_TF), lambda j: (0, j + nf)),
            pl.BlockSpec((FFN_HEAD_TF, D_MODEL), lambda j: (j, 0)),
        ],
        out_specs=[
            pl.BlockSpec((FFN_TM, D_MODEL), lambda j: (0, 0)),
            pl.BlockSpec((D_MODEL, FFN_HEAD_TF), lambda j: (0, j)),
            pl.BlockSpec((D_MODEL, FFN_HEAD_TF), lambda j: (0, j)),
            pl.BlockSpec((FFN_HEAD_TF, D_MODEL), lambda j: (j, 0)),
        ],
        out_shape=[
            jax.ShapeDtypeStruct((FFN_TM, D_MODEL), f32),
            jax.ShapeDtypeStruct((D_MODEL, D_FF), bf16),
            jax.ShapeDtypeStruct((D_MODEL, D_FF), bf16),
            jax.ShapeDtypeStruct((D_FF, D_MODEL), bf16),
        ],
        scratch_shapes=[pltpu.VMEM((FFN_TM, D_MODEL), bf16)],
        compiler_params=_params("arbitrary"),
        name="ffn_head",
    )(x, w_up, w_up, w_down)


def _ffn_ln(x, w_a, w_u, w_down, g, b, casts=(), z0=None):
    n_tok_tiles = TOKENS // FFN_TM
    first = 0 if z0 is None else 1
    ni = n_tok_tiles - first
    nf = D_FF // FFN_TF
    assert FFN_LN_STEPS <= nf
    u0 = (w_u.shape[1] - D_FF) // FFN_TF
    cast_specs = [_cast_block_spec(w.shape, rows_on_j, ni, nf) for w, rows_on_j in casts]
    z0_specs = [] if z0 is None else [pl.BlockSpec((FFN_TM, D_MODEL), lambda i, j: (0, 0))]
    z0_args = [] if z0 is None else [z0]

    def wj(i, j):
        return jnp.where(i < ni, j, nf - 1)

    outs = pl.pallas_call(
        functools.partial(_ffn_ln_kernel, n_cast=len(casts), has_z0=z0 is not None),
        grid=(ni + 1, nf),
        in_specs=[
            pl.BlockSpec((FFN_TM, D_MODEL), lambda i, j: (jnp.minimum(i + first, n_tok_tiles - 1), 0)),
            pl.BlockSpec((D_MODEL, FFN_TF), lambda i, j: (0, wj(i, j))),
            pl.BlockSpec((D_MODEL, FFN_TF), lambda i, j: (0, wj(i, j) + u0)),
            pl.BlockSpec((FFN_TF, D_MODEL), lambda i, j: (wj(i, j), 0)),
            pl.BlockSpec((1, D_MODEL), lambda i, j: (0, 0)),
            pl.BlockSpec((1, D_MODEL), lambda i, j: (0, 0)),
        ] + z0_specs + cast_specs,
        out_specs=[pl.BlockSpec((FFN_TM, D_MODEL), lambda i, j: (jnp.maximum(i + first - 1, 0), 0))]
        + cast_specs,
        out_shape=[jax.ShapeDtypeStruct((TOKENS, D_MODEL), f32)]
        + [jax.ShapeDtypeStruct(w.shape, bf16) for w, _ in casts],
        scratch_shapes=[pltpu.VMEM((FFN_TM, D_MODEL), bf16),
                        pltpu.VMEM((2, FFN_TM, D_MODEL), f32)],
        compiler_params=_params("arbitrary", "arbitrary"),
        name="ffn_ln",
    )(x, w_a, w_u, w_down, g, b, *z0_args, *[w for w, _ in casts])
    return outs[0], outs[1:]


_NT = (((1,), (1,)), ((), ()))


def _matmul_kernel(x_ref, w_ref, o_ref, xb_ref, *, w_is_transposed):
    @pl.when(pl.program_id(1) == 0)
    def _():
        xb_ref[...] = x_ref[...].astype(bf16)

    if w_is_transposed:
        y = lax.dot_general(xb_ref[...], w_ref[...], _NT, preferred_element_type=f32)
    else:
        y = jnp.dot(xb_ref[...], w_ref[...], preferred_element_type=f32)
    o_ref[...] = y.astype(o_ref.dtype)


def _matmul(x, w, out_dtype, tm, tn, name, w_is_transposed=False, n=None):
    m, k = x.shape
    if n is None:
        n = w.shape[0] if w_is_transposed else w.shape[1]
    assert n % tn == 0 and m % tm == 0
    if w_is_transposed:
        w_spec = pl.BlockSpec((tn, k), lambda i, j: (j, 0))
    else:
        w_spec = pl.BlockSpec((k, tn), lambda i, j: (0, j))
    return pl.pallas_call(
        functools.partial(_matmul_kernel, w_is_transposed=w_is_transposed),
        grid=(m // tm, n // tn),
        in_specs=[pl.BlockSpec((tm, k), lambda i, j: (i, 0)), w_spec],
        out_specs=pl.BlockSpec((tm, tn), lambda i, j: (i, j)),
        out_shape=jax.ShapeDtypeStruct((m, n), out_dtype),
        scratch_shapes=[pltpu.VMEM((tm, k), bf16)],
        compiler_params=_params("parallel", "arbitrary"),
        name=name,
    )(x, w)


PB_TM = 512
PB_N = POOL_WIDTH + MEM_WIDTH + LANES


def _in_proj_b_kernel(x_ref, wum_ref, wk_ref, o_ref):
    xb = x_ref[...].astype(bf16)
    n_um = POOL_WIDTH + MEM_WIDTH
    o_ref[:, :n_um] = lax.dot_general(xb, wum_ref[...], _NT, preferred_element_type=f32)
    o_ref[:, n_um:] = lax.dot_general(xb, wk_ref[...], _NT, preferred_element_type=f32)


def _in_proj_b(x, wt, row_um, row_kiwi):
    n_um = POOL_WIDTH + MEM_WIDTH
    return pl.pallas_call(
        _in_proj_b_kernel,
        grid=(TOKENS // PB_TM,),
        in_specs=[
            pl.BlockSpec((PB_TM, D_MODEL), lambda i: (i, 0)),
            pl.BlockSpec((pl.Element(n_um), pl.Element(D_MODEL)), lambda i: (row_um, 0)),
            pl.BlockSpec((pl.Element(LANES), pl.Element(D_MODEL)), lambda i: (row_kiwi, 0)),
        ],
        out_specs=pl.BlockSpec((PB_TM, PB_N), lambda i: (i, 0)),
        out_shape=jax.ShapeDtypeStruct((TOKENS, PB_N), f32),
        compiler_params=_params("parallel"),
        name="in_proj_b",
    )(x, wt, wt)


DSA_TQ = 512
DSA_TK = 512
DSA_NCH = SEQ // DSA_TK
DSA_ONES = 16
KEY_NEG_FLT_MAX = -(2 ** 31) + (1 << 23)


def _key_to_f32(key):
    return lax.bitcast_convert_type(key ^ ((key >> 31) & jnp.int32(0x7FFFFFFF)), f32)


def _dsa_kernel(q_ref, k_ref, vt_ref, qi_ref, kk_ref, wt_ref, kf_ref, qf_ref, o_ref,
                sc_scr, sch_scr, acc_scr, qa_scr):
    i = pl.program_id(1)
    q0 = i * DSA_TQ
    nk = (q0 + DSA_TQ - 1) // DSA_TK + 1

    qpos = q0 + lax.broadcasted_iota(i32, (DSA_TK, DSA_TQ), 1)
    kofs = lax.broadcasted_iota(i32, (DSA_TK, DSA_TQ), 0)
    contract_last = (((1,), (1,)), ((), ()))

    def rows8(x):
        return x.reshape(DSA_TK // 8, 8, DSA_TQ)

    wt = wt_ref[...] * (IDX_DIM ** -0.5 * IDX_HEADS ** -0.5)

    def score_chunk(c, carry):
        r0 = pl.multiple_of(c * DSA_TK, DSA_TK)
        kk0 = kk_ref[0, pl.ds(r0, DSA_TK), :]
        kk1 = kk_ref[1, pl.ds(r0, DSA_TK), :]
        acc = jnp.zeros((DSA_TK, DSA_TQ), f32)
        for p in range(IDX_HEADS // 2):
            slab = qi_ref[:, p * LANES:(p + 1) * LANES]
            l0 = lax.dot_general(kk0, slab, contract_last, preferred_element_type=f32)
            l1 = lax.dot_general(kk1, slab, contract_last, preferred_element_type=f32)
            acc = acc + (jnp.maximum(l0, 0.0) * wt[2 * p:2 * p + 1]
                         + jnp.maximum(l1, 0.0) * wt[2 * p + 1:2 * p + 2])
        sc = jnp.where(kofs + r0 <= qpos, acc, -jnp.inf)
        sc_scr[c] = sc
        sch_scr[c] = sc.astype(bf16)
        return carry

    lax.fori_loop(0, nk, score_chunk, 0)

    def count(pred):
        n_acc = 8

        def body(c, accs):
            m = rows8(pred(c))
            accs = list(accs)
            for r in range(DSA_TK // 8):
                accs[r % n_acc] = jnp.where(m[r], accs[r % n_acc] + 1, accs[r % n_acc])
            return tuple(accs)
        accs = lax.fori_loop(0, nk, body, (jnp.zeros((8, DSA_TQ), i32),) * n_acc)
        acc = functools.reduce(lambda a, b: a + b, accs)
        return jnp.sum(acc, axis=0, keepdims=True)

    def count_ge(cand):
        return count(lambda c: sc_scr[c] >= cand)

    def count_ge_coarse(cand):
        n_acc = 2
        one = jnp.ones((), bf16)
        zero = jnp.zeros((), bf16)

        def body(c, accs):
            accs = list(accs)
            for r in range(DSA_TK // BF16_ROWS):
                rows = sch_scr[c, r * BF16_ROWS:(r + 1) * BF16_ROWS, :]
                accs[r % n_acc] = accs[r % n_acc] + jnp.where(rows >= cand, one, zero)
            return tuple(accs)
        accs = lax.fori_loop(0, nk, body, (jnp.zeros((BF16_ROWS, DSA_TQ), bf16),) * n_acc)
        acc = functools.reduce(lambda a, b: a + b, accs)
        return jnp.sum(acc.astype(f32), axis=0, keepdims=True)

    def coarse_cand(key):
        bits = lax.bitcast_convert_type(_key_to_f32(key), i32) & jnp.int32(-65536)
        return lax.bitcast_convert_type(bits, f32).astype(bf16)

    key = jnp.where(count_ge_coarse(jnp.zeros((1, DSA_TQ), bf16)) >= TOPK,
                    jnp.int32(0), jnp.int32(INT_MIN))

    def coarse_step(b, key):
        cand = key | jnp.left_shift(jnp.int32(1), 30 - b)
        return jnp.where(count_ge_coarse(coarse_cand(cand)) >= TOPK, cand, key)

    key = lax.fori_loop(0, 15, coarse_step, key)

    window_bits = 18
    lo = jnp.maximum(key, jnp.int32(INT_MIN + (1 << 17))) - jnp.int32((1 << 15) + 2)

    def fine_step(b, off):
        cand = off | jnp.left_shift(jnp.int32(1), (window_bits - 1) - b)
        return jnp.where(count_ge(_key_to_f32(lo + cand)) >= TOPK, cand, off)

    key = lo + lax.fori_loop(0, window_bits, fine_step, jnp.zeros((1, DSA_TQ), i32))
    thr = _key_to_f32(jnp.maximum(key, jnp.int32(KEY_NEG_FLT_MAX)))

    n_ge = count_ge(thr)

    @pl.when(jnp.max(n_ge) > TOPK)
    def _():
        need = TOPK - count(lambda c: sc_scr[c] > thr)

        def count_eq_below(pos):
            return count(lambda c: (sc_scr[c] == thr) & (kofs + c * DSA_TK < pos))

        def pos_step(b, r):
            cand = r | jnp.left_shift(jnp.int32(1), (SEQ.bit_length() - 2) - b)
            return jnp.where(count_eq_below(cand) < need, cand, r)
        r = lax.fori_loop(0, SEQ.bit_length() - 1, pos_step, jnp.zeros((1, DSA_TQ), i32))
        r = jnp.where(n_ge > TOPK, r, jnp.int32(SEQ))

        def drop(c, carry):
            sc = sc_scr[c]
            sc_scr[c] = jnp.where((sc == thr) & (kofs + c * DSA_TK > r), -jnp.inf, sc)
            return carry
        lax.fori_loop(0, nk, drop, 0)

    acc_scr[...] = jnp.zeros(acc_scr.shape, f32)
    c1 = HEAD_DIM ** -0.5 * LOG2E
    ones = jnp.ones((DSA_ONES, DSA_TK), bf16)

    for h in range(ATT_HEADS):
        qa_scr[h, :, :HEAD_DIM] = q_ref[:, h * HEAD_DIM:(h + 1) * HEAD_DIM]
        qa_scr[h, :, HEAD_DIM:] = jnp.broadcast_to(qf_ref[h, 0:1, :], (DSA_TQ, LANES))

    def attn_chunk(c, ms):
        r0 = pl.multiple_of(c * DSA_TK, DSA_TK)
        madd = jnp.where(sc_scr[c] >= thr, 0.0, -jnp.inf)
        kf = kf_ref[pl.ds(r0, DSA_TK), :]

        def qk(h):
            hd = slice(h * HEAD_DIM, (h + 1) * HEAD_DIM)
            k_aug = jnp.concatenate([k_ref[pl.ds(r0, DSA_TK), hd], kf], axis=1)
            return lax.dot_general(k_aug, qa_scr[h], contract_last, preferred_element_type=f32)

        def pv(h, alpha, p):
            hd = slice(h * HEAD_DIM, (h + 1) * HEAD_DIM)
            vt1 = jnp.concatenate([vt_ref[c, hd, :], ones], axis=0)
            acc_scr[h] = alpha * acc_scr[h] + jnp.dot(vt1, p, preferred_element_type=f32)

        new_ms = []
        s_next = qk(0)
        pending = None
        for h in range(ATT_HEADS):
            s = s_next
            if h + 1 < ATT_HEADS:
                s_next = qk(h + 1)
            if pending is not None:
                pv(*pending)
            t = s * c1 + madd
            m_old = ms[h]
            m_new = jnp.maximum(m_old, jnp.max(t, axis=0, keepdims=True))
            alpha = jnp.exp2(m_old - m_new)
            p = jnp.exp2(t - m_new).astype(bf16)
            pending = (h, alpha, p)
            new_ms.append(m_new)
        pv(*pending)
        return tuple(new_ms)

    m0 = jnp.full((1, DSA_TQ), -1e30, f32)
    lax.fori_loop(0, nk, attn_chunk, (m0,) * ATT_HEADS)

    for h in range(ATT_HEADS):
        o_ref[h * HEAD_DIM:(h + 1) * HEAD_DIM, :] = (
            acc_scr[h, :HEAD_DIM, :] / acc_scr[h, HEAD_DIM:HEAD_DIM + 1, :]).astype(o_ref.dtype)


def _alibi_features():
    assert ATT_HEADS == 8 and HEAD_DIM == 128 and SEQ <= 64 * 64
    pieces, rest = [], math.sqrt(2.0)
    for _ in range(7):
        p = float(np.asarray(rest, np.float32).astype(jnp.bfloat16).astype(np.float64))
        pieces.append(p)
        rest -= p
    qf = np.zeros((ATT_HEADS, BF16_ROWS, LANES), np.float32)
    for h in range(ATT_HEADS):
        for i, p in enumerate(pieces):
            qf[h, :, 2 * i] = qf[h, :, 2 * i + 1] = p * 2.0 ** (2 - h)
    pos = np.arange(SEQ)
    kf = np.zeros((SEQ, LANES), np.float32)
    for i in range(len(pieces)):
        kf[:, 2 * i] = pos - pos % 64
        kf[:, 2 * i + 1] = pos % 64
    return jnp.asarray(kf, bf16), jnp.asarray(qf, bf16)


def _dsa(za, vt, kk, wt_idx):
    nq = SEQ // DSA_TQ
    kf, qf = _alibi_features()
    once = pl.Buffered(1)
    return pl.pallas_call(
        _dsa_kernel,
        grid=(BATCH, nq),
        in_specs=[
            pl.BlockSpec((DSA_TQ, ATT_WIDTH), lambda b, i: (b * nq + i, 0)),
            pl.BlockSpec((SEQ, ATT_WIDTH), lambda b, i: (b, 1), pipeline_mode=once),
            pl.BlockSpec((None, DSA_NCH, ATT_WIDTH, DSA_TK), lambda b, i: (b, 0, 0, 0),
                         pipeline_mode=once),
            pl.BlockSpec((DSA_TQ, IDX_HEADS * IDX_DIM), lambda b, i: (b * nq + i, 3)),
            pl.BlockSpec((2, SEQ, LANES), lambda b, i: (0, b, 0), pipeline_mode=once),
            pl.BlockSpec((IDX_HEADS, DSA_TQ), lambda b, i: (0, b * nq + i)),
            pl.BlockSpec((SEQ, LANES), lambda b, i: (0, 0), pipeline_mode=once),
            pl.BlockSpec((ATT_HEADS, BF16_ROWS, LANES), lambda b, i: (0, 0, 0)),
        ],
        out_specs=pl.BlockSpec((None, ATT_WIDTH, DSA_TQ), lambda b, i: (b, 0, i)),
        out_shape=jax.ShapeDtypeStruct((BATCH, ATT_WIDTH, SEQ), bf16),
        scratch_shapes=[
            pltpu.VMEM((DSA_NCH, DSA_TK, DSA_TQ), f32),
            pltpu.VMEM((DSA_NCH, DSA_TK, DSA_TQ), bf16),
            pltpu.VMEM((ATT_HEADS, HEAD_DIM + DSA_ONES, DSA_TQ), f32),
            pltpu.VMEM((ATT_HEADS, DSA_TQ, HEAD_DIM + LANES), bf16),
        ],
        compiler_params=_params("arbitrary", "arbitrary"),
        name="dsa",
    )(za, za, vt, za, kk, wt_idx, kf, qf)


def _pool_kernel(u_ref, wp_ref, ps_ref, o_ref):
    row = lax.broadcasted_iota(i32, (SEQ, POOL_GROUP), 0)
    for g, win in enumerate(POOL_WINDOWS):
        x = u_ref[:, g * POOL_GROUP:(g + 1) * POOL_GROUP]
        s = x
        k = 1
        while k < win:
            s = s + jnp.where(row >= k, pltpu.roll(s, k, axis=0), 0.0)
            k *= 2
        cnt = jnp.minimum(row + 1, win).astype(f32)
        pooled = (s / cnt - x).astype(bf16)
        mixed = jnp.dot(pooled, wp_ref[g], preferred_element_type=f32)
        o_ref[:, g * POOL_GROUP:(g + 1) * POOL_GROUP] = (
            mixed * ps_ref[:, g * POOL_GROUP:(g + 1) * POOL_GROUP]).astype(o_ref.dtype)


def _pool(zb, w_pool, pool_scale):
    return pl.pallas_call(
        _pool_kernel,
        grid=(BATCH,),
        in_specs=[
            pl.BlockSpec((SEQ, POOL_WIDTH), lambda b: (b, 0)),
            pl.BlockSpec((N_POOL, POOL_GROUP, POOL_GROUP), lambda b: (0, 0, 0)),
            pl.BlockSpec((1, POOL_WIDTH), lambda b: (0, 0)),
        ],
        out_specs=pl.BlockSpec((SEQ, POOL_WIDTH), lambda b: (b, 0)),
        out_shape=jax.ShapeDtypeStruct((TOKENS, POOL_WIDTH), bf16),
        compiler_params=_params("parallel"),
        name="pool",
    )(zb, w_pool, pool_scale)


MEM_TQ = 512


def _mem_attn_kernel(q_ref, k_ref, v_ref, o_ref):
    for h in range(MEM_HEADS):
        sl = slice(h * HEAD_DIM, (h + 1) * HEAD_DIM)
        s = lax.dot_general(q_ref[:, sl].astype(bf16), k_ref[:, sl], (((1,), (1,)), ((), ())),
                            preferred_element_type=f32) * (HEAD_DIM ** -0.5)
        p = jnp.exp(s - jnp.max(s, axis=1, keepdims=True))
        l = jnp.sum(p, axis=1, keepdims=True)
        o = jnp.dot(p.astype(bf16), v_ref[:, sl], preferred_element_type=f32)
        o_ref[:, sl] = (o / l).astype(o_ref.dtype)


def _mem_attn(zb, qcol, kv):
    nq = SEQ // MEM_TQ
    return pl.pallas_call(
        _mem_attn_kernel,
        grid=(BATCH, nq),
        in_specs=[
            pl.BlockSpec((MEM_TQ, MEM_WIDTH), lambda b, i: (b * nq + i, qcol)),
            pl.BlockSpec((MEM_LEN, MEM_WIDTH), lambda b, i: (b, 0)),
            pl.BlockSpec((MEM_LEN, MEM_WIDTH), lambda b, i: (b, 1)),
        ],
        out_specs=pl.BlockSpec((MEM_TQ, MEM_WIDTH), lambda b, i: (b * nq + i, 0)),
        out_shape=jax.ShapeDtypeStruct((TOKENS, MEM_WIDTH), bf16),
        compiler_params=_params("parallel", "parallel"),
        name="mem_attn",
    )(zb, kv, kv)


CMB_TM = 512
CMB_TC = 512


def _combine_ln_kernel(h_ref, a_ref, p_ref, m_ref, wg0_ref, wg1_ref, wg2_ref,
                       bg0_ref, bg1_ref, bg2_ref, wa_ref, wp_ref, wm_ref, wo_ref,
                       g_ref, b_ref, o_ref, hb_ref, acc_ref):
    i = pl.program_id(0)
    j = pl.program_id(1)
    n_tiles = pl.num_programs(0) - 1
    slot = i % 2
    ln_rows = CMB_TM // (D_MODEL // CMB_TC)

    @pl.when((i == 0) & (j == 0))
    def _():
        acc_ref[1] = jnp.zeros(acc_ref.shape[1:], f32)

    @pl.when((i < n_tiles) & (j == 0))
    def _():
        h = h_ref[...]
        hb_ref[...] = h.astype(bf16)
        acc_ref[slot] = ALPHA * h

    def layer_norm_slab():
        rows = pl.ds(pl.multiple_of(j * ln_rows, ln_rows), ln_rows)
        o_ref[rows, :] = _layer_norm(acc_ref[1 - slot, rows, :], g_ref[...], b_ref[...])

    def matmuls():
        hb = hb_ref[...]

        def gate(wg_ref, bg_ref):
            logits = lax.dot_general(hb, wg_ref[...], _NT, preferred_element_type=f32)
            return jax.nn.sigmoid(logits + bg_ref[...])

        y = gate(wg0_ref, bg0_ref) * jnp.dot(a_ref[...], wa_ref[...], preferred_element_type=f32)
        y = y + gate(wg1_ref, bg1_ref) * jnp.dot(p_ref[...], wp_ref[...], preferred_element_type=f32)
        y = y + gate(wg2_ref, bg2_ref) * jnp.dot(m_ref[...], wm_ref[...], preferred_element_type=f32)
        acc_ref[slot] += jnp.dot(y.astype(bf16), wo_ref[...], preferred_element_type=f32)

    @pl.when(i < n_tiles)
    def _():
        layer_norm_slab()
        matmuls()

    @pl.when(i == n_tiles)
    def _():
        layer_norm_slab()


def _combine_ln(h, a, p, m, wt_gate, row_gate, b_gate, w_a, w_p, w_m, w_out, g, b):
    nc = D_MODEL // CMB_TC
    gate_rows = pl.Element(CMB_TC)
    all_cols = pl.Element(D_MODEL)
    assert row_gate % BF16_ROWS == 0

    def gate_row(blk):
        return pl.multiple_of(row_gate + blk * CMB_TC, BF16_ROWS)

    ni = TOKENS // CMB_TM

    def cj(i, j):
        return jnp.where(i < ni, j, nc - 1)

    row = lambda i, j: (jnp.minimum(i, ni - 1), 0)
    col = lambda i, j: (0, cj(i, j))
    return pl.pallas_call(
        _combine_ln_kernel,
        grid=(ni + 1, nc),
        in_specs=[
            pl.BlockSpec((CMB_TM, D_MODEL), row),
            pl.BlockSpec((CMB_TM, ATT_WIDTH), row),
            pl.BlockSpec((CMB_TM, POOL_WIDTH), row),
            pl.BlockSpec((CMB_TM, MEM_WIDTH), row),
            pl.BlockSpec((gate_rows, all_cols), lambda i, j: (gate_row(cj(i, j)), 0)),
            pl.BlockSpec((gate_rows, all_cols), lambda i, j: (gate_row(cj(i, j) + nc), 0)),
            pl.BlockSpec((gate_rows, all_cols), lambda i, j: (gate_row(cj(i, j) + 2 * nc), 0)),
            pl.BlockSpec((1, CMB_TC), lambda i, j: (0, cj(i, j))),
            pl.BlockSpec((1, CMB_TC), lambda i, j: (0, cj(i, j) + nc)),
            pl.BlockSpec((1, CMB_TC), lambda i, j: (0, cj(i, j) + 2 * nc)),
            pl.BlockSpec((ATT_WIDTH, CMB_TC), col),
            pl.BlockSpec((POOL_WIDTH, CMB_TC), col),
            pl.BlockSpec((MEM_WIDTH, CMB_TC), col),
            pl.BlockSpec((CMB_TC, D_MODEL), lambda i, j: (cj(i, j), 0)),
            pl.BlockSpec((1, D_MODEL), lambda i, j: (0, 0)),
            pl.BlockSpec((1, D_MODEL), lambda i, j: (0, 0)),
        ],
        out_specs=pl.BlockSpec((CMB_TM, D_MODEL), lambda i, j: (jnp.maximum(i - 1, 0), 0)),
        out_shape=jax.ShapeDtypeStruct((TOKENS, D_MODEL), f32),
        scratch_shapes=[pltpu.VMEM((CMB_TM, D_MODEL), bf16),
                        pltpu.VMEM((2, CMB_TM, D_MODEL), f32)],
        compiler_params=_params("arbitrary", "arbitrary"),
        name="combine_ln",
    )(h, a, p, m, wt_gate, wt_gate, wt_gate, b_gate, b_gate, b_gate, w_a, w_p, w_m, w_out, g, b)


def kernel(x, mem, w_ffn1_up, w_ffn1_down, ln1_g, ln1_b, w_in, b_gate, w_mem_kv, w_pool,
           pool_scale, w_br_att, w_br_pool, w_br_mem, w_out, ln2_g, ln2_b, w_ffn2_up,
           w_ffn2_down, ln3_g, ln3_b):
    h = x.reshape(TOKENS, D_MODEL)
    memf = mem.reshape(BATCH * MEM_LEN, D_MODEL)
    for l in range(DEPTH):
        c_qi = 3 * ATT_WIDTH + IDX_HEADS * IDX_DIM
        c_wi = c_qi + IDX_DIM + IDX_HEADS
        c_qm = c_wi + POOL_WIDTH + MEM_WIDTH

        z0, w1_a, w1_u, w1_down = _ffn_head(h, w_ffn1_up[l], w_ffn1_down[l])
        h, (wt, w2_up, w2_down) = _ffn_ln(
            h, w1_a, w1_u, w1_down, ln1_g[l][None], ln1_b[l][None], z0=z0,
            casts=((w_in[l].T, None), (w_ffn2_up[l], None), (w_ffn2_down[l], None)))

        za = _matmul(h, wt, bf16, 1024, 1024, "in_proj_a", w_is_transposed=True, n=c_qi)
        zb = _in_proj_b(h, wt, c_wi, c_qi)
        kv = _matmul(memf, w_mem_kv[l].astype(bf16), bf16, BATCH * MEM_LEN, 512, "mem_kv")

        c_ki = POOL_WIDTH + MEM_WIDTH
        ki = zb[:, c_ki:c_ki + IDX_DIM].astype(bf16)
        zk = jnp.zeros_like(ki)
        kk = jnp.stack([jnp.concatenate([ki, zk], axis=1), jnp.concatenate([zk, ki], axis=1)])
        wt_idx = zb[:, c_ki + IDX_DIM:c_ki + IDX_DIM + IDX_HEADS].T
        vt = za[:, 2 * ATT_WIDTH:3 * ATT_WIDTH].reshape(BATCH, DSA_NCH, DSA_TK, ATT_WIDTH)
        vt = vt.transpose(0, 1, 3, 2)

        a = _dsa(za, vt, kk, wt_idx).transpose(0, 2, 1).reshape(TOKENS, ATT_WIDTH)
        p = _pool(zb, w_pool[l].astype(bf16), pool_scale[l][None])
        m = _mem_attn(zb, POOL_WIDTH // MEM_WIDTH, kv)

        h = _combine_ln(h, a, p, m, wt, c_qm, b_gate[l][None], w_br_att[l].astype(bf16),
                        w_br_pool[l].astype(bf16), w_br_mem[l].astype(bf16),
                        w_out[l].astype(bf16), ln2_g[l][None], ln2_b[l][None])

        h, _ = _ffn_ln(h, w2_up, w2_up, w2_down, ln3_g[l][None], ln3_b[l][None])
    return h.reshape(BATCH, SEQ, D_MODEL)
```

```python
import functools
import math

import jax
import jax.numpy as jnp
import numpy as np
from jax import lax
from jax.experimental import pallas as pl
from jax.experimental.pallas import tpu as pltpu

f32 = jnp.float32
bf16 = jnp.bfloat16
i32 = jnp.int32

D_MODEL = 2048
BATCH = 2
SEQ = 4096
DEPTH = 1
MEM_LEN = 256
ATT_HEADS = 8
HEAD_DIM = 128
ATT_WIDTH = ATT_HEADS * HEAD_DIM
IDX_HEADS = 16
IDX_DIM = 64
TOPK = min(256, SEQ // 4)
POOL_WINDOWS = (2, 4, 8, 16)
N_POOL = len(POOL_WINDOWS)
POOL_GROUP = 128
POOL_WIDTH = N_POOL * POOL_GROUP
MEM_HEADS = 4
MEM_WIDTH = MEM_HEADS * HEAD_DIM
N_BRANCH = 3
D_FF = 5632
ALPHA = (2 * DEPTH) ** 0.25
LN_EPS = 1e-5
TOKENS = BATCH * SEQ

LANES = 128
BF16_ROWS = 16
VMEM_LIMIT = 60 * 1024 * 1024

INT_MIN = -(2 ** 31)
LOG2E = math.log2(math.e)


def _params(*sem):
    return pltpu.CompilerParams(dimension_semantics=sem, vmem_limit_bytes=VMEM_LIMIT)


def _layer_norm(y, g, b):
    mu = jnp.mean(y, axis=-1, keepdims=True)
    d = y - mu
    var = jnp.mean(d * d, axis=-1, keepdims=True)
    return d * lax.rsqrt(var + LN_EPS) * g + b


FFN_TM = 512
FFN_TF = 512


FFN_LN_STEPS = 8
FFN_LN_ROWS = FFN_TM // FFN_LN_STEPS


def _ffn_ln_kernel(x_ref, wa_ref, wu_ref, wd_ref, g_ref, b_ref, *rest, n_cast, has_z0):
    if has_z0:
        z0_ref, rest = rest[0], rest[1:]
    cast_in = rest[:n_cast]
    o_ref = rest[n_cast]
    cast_out = rest[n_cast + 1:2 * n_cast + 1]
    xb_ref, acc_ref = rest[2 * n_cast + 1:]
    i = pl.program_id(0)
    j = pl.program_id(1)
    n_tiles = pl.num_programs(0) - 1
    slot = (i + 1) % 2 if has_z0 else i % 2
    computed = i < n_tiles

    for src, dst in zip(cast_in, cast_out):
        dst[...] = src[...].astype(dst.dtype)

    @pl.when((i == 0) & (j == 0))
    def _():
        if has_z0:
            acc_ref[0] = z0_ref[...]
        else:
            acc_ref[1] = jnp.zeros(acc_ref.shape[1:], f32)

    @pl.when(computed & (j == 0))
    def _():
        x = x_ref[...]
        xb_ref[...] = x.astype(bf16)
        acc_ref[slot] = (2.0 * ALPHA) * x

    def matmuls():
        xb = xb_ref[...]
        a = jnp.dot(xb, wa_ref[...], preferred_element_type=f32)
        u = jnp.dot(xb, wu_ref[...], preferred_element_type=f32)
        act = (a * jax.nn.sigmoid(a) * u).astype(bf16)
        acc_ref[slot] += jnp.dot(act, wd_ref[...], preferred_element_type=f32)

    def layer_norm_slab():
        rows = pl.ds(pl.multiple_of(j * FFN_LN_ROWS, FFN_LN_ROWS), FFN_LN_ROWS)
        z = acc_ref[1 - slot, rows, :]
        mu = jnp.mean(z, axis=-1, keepdims=True)
        d = z - mu
        var = jnp.mean(d * d, axis=-1, keepdims=True)
        o_ref[rows, :] = d * lax.rsqrt(var + 4.0 * LN_EPS) * g_ref[...] + b_ref[...]

    @pl.when(computed & (j < FFN_LN_STEPS))
    def _():
        layer_norm_slab()
        matmuls()

    @pl.when(computed & (j >= FFN_LN_STEPS))
    def _():
        matmuls()

    @pl.when((i == n_tiles) & (j < FFN_LN_STEPS))
    def _():
        layer_norm_slab()


def _cast_block_spec(shape, rows_on_j, ni, nj):
    r, c = shape
    if rows_on_j is None:
        br = -(-r // (ni * nj))
        br = -(-br // BF16_ROWS) * BF16_ROWS
        last_block = -(-r // br) - 1
        return pl.BlockSpec((br, c), lambda i, j: (
            jnp.minimum(jnp.minimum(i, ni - 1) * nj + jnp.where(i < ni, j, nj - 1), last_block), 0))
    n_r, n_c = (nj, ni) if rows_on_j else (ni, nj)
    br = -(-r // n_r)
    br = -(-br // BF16_ROWS) * BF16_ROWS
    assert c % n_c == 0 and (c // n_c) % LANES == 0
    def clamp(i, j):
        return jnp.minimum(i, ni - 1), jnp.where(i < ni, j, nj - 1)

    if rows_on_j:
        index_map = lambda i, j: clamp(i, j)[::-1]
    else:
        index_map = lambda i, j: clamp(i, j)
    return pl.BlockSpec((br, c // n_c), index_map)


FFN_HEAD_TF = 256


def _ffn_head_kernel(x_ref, wa_ref, wu_ref, wd_ref, z_ref, wa_out, wu_out, wd_out, xb_ref):
    j = pl.program_id(0)

    @pl.when(j == 0)
    def _():
        x = x_ref[...]
        xb_ref[...] = x.astype(bf16)
        z_ref[...] = (2.0 * ALPHA) * x

    wa = wa_ref[...].astype(bf16)
    wu = wu_ref[...].astype(bf16)
    wd = wd_ref[...].astype(bf16)
    wa_out[...] = wa
    wu_out[...] = wu
    wd_out[...] = wd
    xb = xb_ref[...]
    a = jnp.dot(xb, wa, preferred_element_type=f32)
    u = jnp.dot(xb, wu, preferred_element_type=f32)
    act = (a * jax.nn.sigmoid(a) * u).astype(bf16)
    z_ref[...] += jnp.dot(act, wd, preferred_element_type=f32)


def _ffn_head(x, w_up, w_down):
    nf = D_FF // FFN_HEAD_TF
    return pl.pallas_call(
        _ffn_head_kernel,
        grid=(nf,),
        in_specs=[
            pl.BlockSpec((FFN_TM, D_MODEL), lambda j: (0, 0)),
            pl.BlockSpec((D_MODEL, FFN_HEAD_TF), lambda j: (0, j)),
            pl.BlockSpec((D_MODEL, FFN_HEAD_TF), lambda j: (0, j + nf)),
            pl.BlockSpec((FFN_HEAD_TF, D_MODEL), lambda j: (j, 0)),
        ],
        out_specs=[
            pl.BlockSpec((FFN_TM, D_MODEL), lambda j: (0, 0)),
            pl.BlockSpec((D_MODEL, FFN_HEAD_TF), lambda j: (0, j)),
            pl.BlockSpec((D_MODEL, FFN_HEAD_TF), lambda j: (0, j)),
            pl.BlockSpec((FFN_HEAD_TF, D_MODEL), lambda j: (j, 0)),
        ],
        out_shape=[
            jax.ShapeDtypeStruct((FFN_TM, D_MODEL), f32),
            jax.ShapeDtypeStruct((D_MODEL, D_FF), bf16),
            jax.ShapeDtypeStruct((D_MODEL, D_FF), bf16),
            jax.ShapeDtypeStruct((D_FF, D_MODEL), bf16),
        ],
        scratch_shapes=[pltpu.VMEM((FFN_TM, D_MODEL), bf16)],
        compiler_params=_params("arbitrary"),
        name="ffn_head",
    )(x, w_up, w_up, w_down)


def _ffn_ln(x, w_a, w_u, w_down, g, b, casts=(), z0=None):
    n_tok_tiles = TOKENS // FFN_TM
    first = 0 if z0 is None else 1
    ni = n_tok_tiles - first
    nf = D_FF // FFN_TF
    assert FFN_LN_STEPS <= nf
    u0 = (w_u.shape[1] - D_FF) // FFN_TF
    cast_specs = [_cast_block_spec(w.shape, rows_on_j, ni, nf) for w, rows_on_j in casts]
    z0_specs = [] if z0 is None else [pl.BlockSpec((FFN_TM, D_MODEL), lambda i, j: (0, 0))]
    z0_args = [] if z0 is None else [z0]

    def wj(i, j):
        return jnp.where(i < ni, j, nf - 1)

    outs = pl.pallas_call(
        functools.partial(_ffn_ln_kernel, n_cast=len(casts), has_z0=z0 is not None),
        grid=(ni + 1, nf),
        in_specs=[
            pl.BlockSpec((FFN_TM, D_MODEL), lambda i, j: (jnp.minimum(i + first, n_tok_tiles - 1), 0)),
            pl.BlockSpec((D_MODEL, FFN_TF), lambda i, j: (0, wj(i, j))),
            pl.BlockSpec((D_MODEL, FFN_TF), lambda i, j: (0, wj(i, j) + u0)),
            pl.BlockSpec((FFN_TF, D_MODEL), lambda i, j: (wj(i, j), 0)),
            pl.BlockSpec((1, D_MODEL), lambda i, j: (0, 0)),
            pl.BlockSpec((1, D_MODEL), lambda i, j: (0, 0)),
        ] + z0_specs + cast_specs,
        out_specs=[pl.BlockSpec((FFN_TM, D_MODEL), lambda i, j: (jnp.maximum(i + first - 1, 0), 0))]
        + cast_specs,
        out_shape=[jax.ShapeDtypeStruct((TOKENS, D_MODEL), f32)]
        + [jax.ShapeDtypeStruct(w.shape, bf16) for w, _ in casts],
        scratch_shapes=[pltpu.VMEM((FFN_TM, D_MODEL), bf16),
                        pltpu.VMEM((2, FFN_TM, D_MODEL), f32)],
        compiler_params=_params("arbitrary", "arbitrary"),
        name="ffn_ln",
    )(x, w_a, w_u, w_down, g, b, *z0_args, *[w for w, _ in casts])
    return outs[0], outs[1:]


_NT = (((1,), (1,)), ((), ()))


def _matmul_kernel(x_ref, w_ref, o_ref, xb_ref, *, w_is_transposed):
    @pl.when(pl.program_id(1) == 0)
    def _():
        xb_ref[...] = x_ref[...].astype(bf16)

    if w_is_transposed:
        y = lax.dot_general(xb_ref[...], w_ref[...], _NT, preferred_element_type=f32)
    else:
        y = jnp.dot(xb_ref[...], w_ref[...], preferred_element_type=f32)
    o_ref[...] = y.astype(o_ref.dtype)


def _matmul(x, w, out_dtype, tm, tn, name, w_is_transposed=False, n=None):
    m, k = x.shape
    if n is None:
        n = w.shape[0] if w_is_transposed else w.shape[1]
    assert n % tn == 0 and m % tm == 0
    if w_is_transposed:
        w_spec = pl.BlockSpec((tn, k), lambda i, j: (j, 0))
    else:
        w_spec = pl.BlockSpec((k, tn), lambda i, j: (0, j))
    return pl.pallas_call(
        functools.partial(_matmul_kernel, w_is_transposed=w_is_transposed),
        grid=(m // tm, n // tn),
        in_specs=[pl.BlockSpec((tm, k), lambda i, j: (i, 0)), w_spec],
        out_specs=pl.BlockSpec((tm, tn), lambda i, j: (i, j)),
        out_shape=jax.ShapeDtypeStruct((m, n), out_dtype),
        scratch_shapes=[pltpu.VMEM((tm, k), bf16)],
        compiler_params=_params("parallel", "arbitrary"),
        name=name,
    )(x, w)


PB_TM = 512
PB_N = POOL_WIDTH + MEM_WIDTH + LANES


def _in_proj_b_kernel(x_ref, wum_ref, wk_ref, o_ref):
    xb = x_ref[...].astype(bf16)
    n_um = POOL_WIDTH + MEM_WIDTH
    o_ref[:, :n_um] = lax.dot_general(xb, wum_ref[...], _NT, preferred_element_type=f32)
    o_ref[:, n_um:] = lax.dot_general(xb, wk_ref[...], _NT, preferred_element_type=f32)


def _in_proj_b(x, wt, row_um, row_kiwi):
    n_um = POOL_WIDTH + MEM_WIDTH
    return pl.pallas_call(
        _in_proj_b_kernel,
        grid=(TOKENS // PB_TM,),
        in_specs=[
            pl.BlockSpec((PB_TM, D_MODEL), lambda i: (i, 0)),
            pl.BlockSpec((pl.Element(n_um), pl.Element(D_MODEL)), lambda i: (row_um, 0)),
            pl.BlockSpec((pl.Element(LANES), pl.Element(D_MODEL)), lambda i: (row_kiwi, 0)),
        ],
        out_specs=pl.BlockSpec((PB_TM, PB_N), lambda i: (i, 0)),
        out_shape=jax.ShapeDtypeStruct((TOKENS, PB_N), f32),
        compiler_params=_params("parallel"),
        name="in_proj_b",
    )(x, wt, wt)


DSA_TQ = 512
DSA_TK = 512
DSA_NCH = SEQ // DSA_TK
DSA_ONES = 16
KEY_NEG_FLT_MAX = -(2 ** 31) + (1 << 23)


def _key_to_f32(key):
    return lax.bitcast_convert_type(key ^ ((key >> 31) & jnp.int32(0x7FFFFFFF)), f32)


def _dsa_kernel(q_ref, k_ref, vt_ref, qi_ref, kk_ref, wt_ref, kf_ref, qf_ref, o_ref,
                sc_scr, sch_scr, acc_scr, qa_scr):
    i = pl.program_id(1)
    q0 = i * DSA_TQ
    nk = (q0 + DSA_TQ - 1) // DSA_TK + 1

    qpos = q0 + lax.broadcasted_iota(i32, (DSA_TK, DSA_TQ), 1)
    kofs = lax.broadcasted_iota(i32, (DSA_TK, DSA_TQ), 0)
    contract_last = (((1,), (1,)), ((), ()))

    def rows8(x):
        return x.reshape(DSA_TK // 8, 8, DSA_TQ)

    wt = wt_ref[...] * (IDX_DIM ** -0.5 * IDX_HEADS ** -0.5)

    def score_chunk(c, carry):
        r0 = pl.multiple_of(c * DSA_TK, DSA_TK)
        kk0 = kk_ref[0, pl.ds(r0, DSA_TK), :]
        kk1 = kk_ref[1, pl.ds(r0, DSA_TK), :]
        acc = jnp.zeros((DSA_TK, DSA_TQ), f32)
        for p in range(IDX_HEADS // 2):
            slab = qi_ref[:, p * LANES:(p + 1) * LANES]
            l0 = lax.dot_general(kk0, slab, contract_last, preferred_element_type=f32)
            l1 = lax.dot_general(kk1, slab, contract_last, preferred_element_type=f32)
            acc = acc + (jnp.maximum(l0, 0.0) * wt[2 * p:2 * p + 1]
                         + jnp.maximum(l1, 0.0) * wt[2 * p + 1:2 * p + 2])
        sc = jnp.where(kofs + r0 <= qpos, acc, -jnp.inf)
        sc_scr[c] = sc
        sch_scr[c] = sc.astype(bf16)
        return carry

    lax.fori_loop(0, nk, score_chunk, 0)

    def count(pred):
        n_acc = 8

        def body(c, accs):
            m = rows8(pred(c))
            accs = list(accs)
            for r in range(DSA_TK // 8):
                accs[r % n_acc] = jnp.where(m[r], accs[r % n_acc] + 1, accs[r % n_acc])
            return tuple(accs)
        accs = lax.fori_loop(0, nk, body, (jnp.zeros((8, DSA_TQ), i32),) * n_acc)
        acc = functools.reduce(lambda a, b: a + b, accs)
        return jnp.sum(acc, axis=0, keepdims=True)

    def count_ge(cand):
        return count(lambda c: sc_scr[c] >= cand)

    def count_ge_coarse(cand):
        n_acc = 2
        one = jnp.ones((), bf16)
        zero = jnp.zeros((), bf16)

        def body(c, accs):
            accs = list(accs)
            for r in range(DSA_TK // BF16_ROWS):
                rows = sch_scr[c, r * BF16_ROWS:(r + 1) * BF16_ROWS, :]
                accs[r % n_acc] = accs[r % n_acc] + jnp.where(rows >= cand, one, zero)
            return tuple(accs)
        accs = lax.fori_loop(0, nk, body, (jnp.zeros((BF16_ROWS, DSA_TQ), bf16),) * n_acc)
        acc = functools.reduce(lambda a, b: a + b, accs)
        return jnp.sum(acc.astype(f32), axis=0, keepdims=True)

    def coarse_cand(key):
        bits = lax.bitcast_convert_type(_key_to_f32(key), i32) & jnp.int32(-65536)
        return lax.bitcast_convert_type(bits, f32).astype(bf16)

    key = jnp.where(count_ge_coarse(jnp.zeros((1, DSA_TQ), bf16)) >= TOPK,
                    jnp.int32(0), jnp.int32(INT_MIN))

    def coarse_step(b, key):
        cand = key | jnp.left_shift(jnp.int32(1), 30 - b)
        return jnp.where(count_ge_coarse(coarse_cand(cand)) >= TOPK, cand, key)

    key = lax.fori_loop(0, 15, coarse_step, key)

    window_bits = 18
    lo = jnp.maximum(key, jnp.int32(INT_MIN + (1 << 17))) - jnp.int32((1 << 15) + 2)

    def fine_step(b, off):
        cand = off | jnp.left_shift(jnp.int32(1), (window_bits - 1) - b)
        return jnp.where(count_ge(_key_to_f32(lo + cand)) >= TOPK, cand, off)

    key = lo + lax.fori_loop(0, window_bits, fine_step, jnp.zeros((1, DSA_TQ), i32))
    thr = _key_to_f32(jnp.maximum(key, jnp.int32(KEY_NEG_FLT_MAX)))

    n_ge = count_ge(thr)

    @pl.when(jnp.max(n_ge) > TOPK)
    def _():
        need = TOPK - count(lambda c: sc_scr[c] > thr)

        def count_eq_below(pos):
            return count(lambda c: (sc_scr[c] == thr) & (kofs + c * DSA_TK < pos))

        def pos_step(b, r):
            cand = r | jnp.left_shift(jnp.int32(1), (SEQ.bit_length() - 2) - b)
            return jnp.where(count_eq_below(cand) < need, cand, r)
        r = lax.fori_loop(0, SEQ.bit_length() - 1, pos_step, jnp.zeros((1, DSA_TQ), i32))
        r = jnp.where(n_ge > TOPK, r, jnp.int32(SEQ))

        def drop(c, carry):
            sc = sc_scr[c]
            sc_scr[c] = jnp.where((sc == thr) & (kofs + c * DSA_TK > r), -jnp.inf, sc)
            return carry
        lax.fori_loop(0, nk, drop, 0)

    acc_scr[...] = jnp.zeros(acc_scr.shape, f32)
    c1 = HEAD_DIM ** -0.5 * LOG2E
    ones = jnp.ones((DSA_ONES, DSA_TK), bf16)

    for h in range(ATT_HEADS):
        qa_scr[h, :, :HEAD_DIM] = q_ref[:, h * HEAD_DIM:(h + 1) * HEAD_DIM]
        qa_scr[h, :, HEAD_DIM:] = jnp.broadcast_to(qf_ref[h, 0:1, :], (DSA_TQ, LANES))

    def attn_chunk(c, ms):
        r0 = pl.multiple_of(c * DSA_TK, DSA_TK)
        madd = jnp.where(sc_scr[c] >= thr, 0.0, -jnp.inf)
        kf = kf_ref[pl.ds(r0, DSA_TK), :]

        def qk(h):
            hd = slice(h * HEAD_DIM, (h + 1) * HEAD_DIM)
            k_aug = jnp.concatenate([k_ref[pl.ds(r0, DSA_TK), hd], kf], axis=1)
            return lax.dot_general(k_aug, qa_scr[h], contract_last, preferred_element_type=f32)

        def pv(h, alpha, p):
            hd = slice(h * HEAD_DIM, (h + 1) * HEAD_DIM)
            vt1 = jnp.concatenate([vt_ref[c, hd, :], ones], axis=0)
            acc_scr[h] = alpha * acc_scr[h] + jnp.dot(vt1, p, preferred_element_type=f32)

        new_ms = []
        s_next = qk(0)
        pending = None
        for h in range(ATT_HEADS):
            s = s_next
            if h + 1 < ATT_HEADS:
                s_next = qk(h + 1)
            if pending is not None:
                pv(*pending)
            t = s * c1 + madd
            m_old = ms[h]
            m_new = jnp.maximum(m_old, jnp.max(t, axis=0, keepdims=True))
            alpha = jnp.exp2(m_old - m_new)
            p = jnp.exp2(t - m_new).astype(bf16)
            pending = (h, alpha, p)
            new_ms.append(m_new)
        pv(*pending)
        return tuple(new_ms)

    m0 = jnp.full((1, DSA_TQ), -1e30, f32)
    lax.fori_loop(0, nk, attn_chunk, (m0,) * ATT_HEADS)

    for h in range(ATT_HEADS):
        o_ref[h * HEAD_DIM:(h + 1) * HEAD_DIM, :] = (
            acc_scr[h, :HEAD_DIM, :] / acc_scr[h, HEAD_DIM:HEAD_DIM + 1, :]).astype(o_ref.dtype)


def _alibi_features():
    assert ATT_HEADS == 8 and HEAD_DIM == 128 and SEQ <= 64 * 64
    pieces, rest = [], math.sqrt(2.0)
    for _ in range(7):
        p = float(np.asarray(rest, np.float32).astype(jnp.bfloat16).astype(np.float64))
        pieces.append(p)
        rest -= p
    qf = np.zeros((ATT_HEADS, BF16_ROWS, LANES), np.float32)
    for h in range(ATT_HEADS):
        for i, p in enumerate(pieces):
            qf[h, :, 2 * i] = qf[h, :, 2 * i + 1] = p * 2.0 ** (2 - h)
    pos = np.arange(SEQ)
    kf = np.zeros((SEQ, LANES), np.float32)
    for i in range(len(pieces)):
        kf[:, 2 * i] = pos - pos % 64
        kf[:, 2 * i + 1] = pos % 64
    return jnp.asarray(kf, bf16), jnp.asarray(qf, bf16)


def _dsa(za, vt, kk, wt_idx):
    nq = SEQ // DSA_TQ
    kf, qf = _alibi_features()
    once = pl.Buffered(1)
    return pl.pallas_call(
        _dsa_kernel,
        grid=(BATCH, nq),
        in_specs=[
            pl.BlockSpec((DSA_TQ, ATT_WIDTH), lambda b, i: (b * nq + i, 0)),
            pl.BlockSpec((SEQ, ATT_WIDTH), lambda b, i: (b, 1), pipeline_mode=once),
            pl.BlockSpec((None, DSA_NCH, ATT_WIDTH, DSA_TK), lambda b, i: (b, 0, 0, 0),
                         pipeline_mode=once),
            pl.BlockSpec((DSA_TQ, IDX_HEADS * IDX_DIM), lambda b, i: (b * nq + i, 3)),
            pl.BlockSpec((2, SEQ, LANES), lambda b, i: (0, b, 0), pipeline_mode=once),
            pl.BlockSpec((IDX_HEADS, DSA_TQ), lambda b, i: (0, b * nq + i)),
            pl.BlockSpec((SEQ, LANES), lambda b, i: (0, 0), pipeline_mode=once),
            pl.BlockSpec((ATT_HEADS, BF16_ROWS, LANES), lambda b, i: (0, 0, 0)),
        ],
        out_specs=pl.BlockSpec((None, ATT_WIDTH, DSA_TQ), lambda b, i: (b, 0, i)),
        out_shape=jax.ShapeDtypeStruct((BATCH, ATT_WIDTH, SEQ), bf16),
        scratch_shapes=[
            pltpu.VMEM((DSA_NCH, DSA_TK, DSA_TQ), f32),
            pltpu.VMEM((DSA_NCH, DSA_TK, DSA_TQ), bf16),
            pltpu.VMEM((ATT_HEADS, HEAD_DIM + DSA_ONES, DSA_TQ), f32),
            pltpu.VMEM((ATT_HEADS, DSA_TQ, HEAD_DIM + LANES), bf16),
        ],
        compiler_params=_params("arbitrary", "arbitrary"),
        name="dsa",
    )(za, za, vt, za, kk, wt_idx, kf, qf)


def _pool_kernel(u_ref, wp_ref, ps_ref, o_ref):
    row = lax.broadcasted_iota(i32, (SEQ, POOL_GROUP), 0)
    for g, win in enumerate(POOL_WINDOWS):
        x = u_ref[:, g * POOL_GROUP:(g + 1) * POOL_GROUP]
        s = x
        k = 1
        while k < win:
            s = s + jnp.where(row >= k, pltpu.roll(s, k, axis=0), 0.0)
            k *= 2
        cnt = jnp.minimum(row + 1, win).astype(f32)
        pooled = (s / cnt - x).astype(bf16)
        mixed = jnp.dot(pooled, wp_ref[g], preferred_element_type=f32)
        o_ref[:, g * POOL_GROUP:(g + 1) * POOL_GROUP] = (
            mixed * ps_ref[:, g * POOL_GROUP:(g + 1) * POOL_GROUP]).astype(o_ref.dtype)


def _pool(zb, w_pool, pool_scale):
    return pl.pallas_call(
        _pool_kernel,
        grid=(BATCH,),
        in_specs=[
            pl.BlockSpec((SEQ, POOL_WIDTH), lambda b: (b, 0)),
            pl.BlockSpec((N_POOL, POOL_GROUP, POOL_GROUP), lambda b: (0, 0, 0)),
            pl.BlockSpec((1, POOL_WIDTH), lambda b: (0, 0)),
        ],
        out_specs=pl.BlockSpec((SEQ, POOL_WIDTH), lambda b: (b, 0)),
        out_shape=jax.ShapeDtypeStruct((TOKENS, POOL_WIDTH), bf16),
        compiler_params=_params("parallel"),
        name="pool",
    )(zb, w_pool, pool_scale)


MEM_TQ = 512


def _mem_attn_kernel(q_ref, k_ref, v_ref, o_ref):
    for h in range(MEM_HEADS):
        sl = slice(h * HEAD_DIM, (h + 1) * HEAD_DIM)
        s = lax.dot_general(q_ref[:, sl].astype(bf16), k_ref[:, sl], (((1,), (1,)), ((), ())),
                            preferred_element_type=f32) * (HEAD_DIM ** -0.5)
        p = jnp.exp(s - jnp.max(s, axis=1, keepdims=True))
        l = jnp.sum(p, axis=1, keepdims=True)
        o = jnp.dot(p.astype(bf16), v_ref[:, sl], preferred_element_type=f32)
        o_ref[:, sl] = (o / l).astype(o_ref.dtype)


def _mem_attn(zb, qcol, kv):
    nq = SEQ // MEM_TQ
    return pl.pallas_call(
        _mem_attn_kernel,
        grid=(BATCH, nq),
        in_specs=[
            pl.BlockSpec((MEM_TQ, MEM_WIDTH), lambda b, i: (b * nq + i, qcol)),
            pl.BlockSpec((MEM_LEN, MEM_WIDTH), lambda b, i: (b, 0)),
            pl.BlockSpec((MEM_LEN, MEM_WIDTH), lambda b, i: (b, 1)),
        ],
        out_specs=pl.BlockSpec((MEM_TQ, MEM_WIDTH), lambda b, i: (b * nq + i, 0)),
        out_shape=jax.ShapeDtypeStruct((TOKENS, MEM_WIDTH), bf16),
        compiler_params=_params("parallel", "parallel"),
        name="mem_attn",
    )(zb, kv, kv)


CMB_TM = 512
CMB_TC = 512


def _combine_ln_kernel(h_ref, a_ref, p_ref, m_ref, wg0_ref, wg1_ref, wg2_ref,
                       bg0_ref, bg1_ref, bg2_ref, wa_ref, wp_ref, wm_ref, wo_ref,
                       g_ref, b_ref, o_ref, hb_ref, acc_ref):
    i = pl.program_id(0)
    j = pl.program_id(1)
    n_tiles = pl.num_programs(0) - 1
    slot = i % 2
    ln_rows = CMB_TM // (D_MODEL // CMB_TC)

    @pl.when((i == 0) & (j == 0))
    def _():
        acc_ref[1] = jnp.zeros(acc_ref.shape[1:], f32)

    @pl.when((i < n_tiles) & (j == 0))
    def _():
        h = h_ref[...]
        hb_ref[...] = h.astype(bf16)
        acc_ref[slot] = ALPHA * h

    def layer_norm_slab():
        rows = pl.ds(pl.multiple_of(j * ln_rows, ln_rows), ln_rows)
        o_ref[rows, :] = _layer_norm(acc_ref[1 - slot, rows, :], g_ref[...], b_ref[...])

    def matmuls():
        hb = hb_ref[...]

        def gate(wg_ref, bg_ref):
            logits = lax.dot_general(hb, wg_ref[...], _NT, preferred_element_type=f32)
            return jax.nn.sigmoid(logits + bg_ref[...])

        y = gate(wg0_ref, bg0_ref) * jnp.dot(a_ref[...], wa_ref[...], preferred_element_type=f32)
        y = y + gate(wg1_ref, bg1_ref) * jnp.dot(p_ref[...], wp_ref[...], preferred_element_type=f32)
        y = y + gate(wg2_ref, bg2_ref) * jnp.dot(m_ref[...], wm_ref[...], preferred_element_type=f32)
        acc_ref[slot] += jnp.dot(y.astype(bf16), wo_ref[...], preferred_element_type=f32)

    @pl.when(i < n_tiles)
    def _():
        layer_norm_slab()
        matmuls()

    @pl.when(i == n_tiles)
    def _():
        layer_norm_slab()


def _combine_ln(h, a, p, m, wt_gate, row_gate, b_gate, w_a, w_p, w_m, w_out, g, b):
    nc = D_MODEL // CMB_TC
    gate_rows = pl.Element(CMB_TC)
    all_cols = pl.Element(D_MODEL)
    assert row_gate % BF16_ROWS == 0

    def gate_row(blk):
        return pl.multiple_of(row_gate + blk * CMB_TC, BF16_ROWS)

    ni = TOKENS // CMB_TM

    def cj(i, j):
        return jnp.where(i < ni, j, nc - 1)

    row = lambda i, j: (jnp.minimum(i, ni - 1), 0)
    col = lambda i, j: (0, cj(i, j))
    return pl.pallas_call(
        _combine_ln_kernel,
        grid=(ni + 1, nc),
        in_specs=[
            pl.BlockSpec((CMB_TM, D_MODEL), row),
            pl.BlockSpec((CMB_TM, ATT_WIDTH), row),
            pl.BlockSpec((CMB_TM, POOL_WIDTH), row),
            pl.BlockSpec((CMB_TM, MEM_WIDTH), row),
            pl.BlockSpec((gate_rows, all_cols), lambda i, j: (gate_row(cj(i, j)), 0)),
            pl.BlockSpec((gate_rows, all_cols), lambda i, j: (gate_row(cj(i, j) + nc), 0)),
            pl.BlockSpec((gate_rows, all_cols), lambda i, j: (gate_row(cj(i, j) + 2 * nc), 0)),
            pl.BlockSpec((1, CMB_TC), lambda i, j: (0, cj(i, j))),
            pl.BlockSpec((1, CMB_TC), lambda i, j: (0, cj(i, j) + nc)),
            pl.BlockSpec((1, CMB_TC), lambda i, j: (0, cj(i, j) + 2 * nc)),
            pl.BlockSpec((ATT_WIDTH, CMB_TC), col),
            pl.BlockSpec((POOL_WIDTH, CMB_TC), col),
            pl.BlockSpec((MEM_WIDTH, CMB_TC), col),
            pl.BlockSpec((CMB_TC, D_MODEL), lambda i, j: (cj(i, j), 0)),
            pl.BlockSpec((1, D_MODEL), lambda i, j: (0, 0)),
            pl.BlockSpec((1, D_MODEL), lambda i, j: (0, 0)),
        ],
        out_specs=pl.BlockSpec((CMB_TM, D_MODEL), lambda i, j: (jnp.maximum(i - 1, 0), 0)),
        out_shape=jax.ShapeDtypeStruct((TOKENS, D_MODEL), f32),
        scratch_shapes=[pltpu.VMEM((CMB_TM, D_MODEL), bf16),
                        pltpu.VMEM((2, CMB_TM, D_MODEL), f32)],
        compiler_params=_params("arbitrary", "arbitrary"),
        name="combine_ln",
    )(h, a, p, m, wt_gate, wt_gate, wt_gate, b_gate, b_gate, b_gate, w_a, w_p, w_m, w_out, g, b)


def kernel(x, mem, w_ffn1_up, w_ffn1_down, ln1_g, ln1_b, w_in, b_gate, w_mem_kv, w_pool,
           pool_scale, w_br_att, w_br_pool, w_br_mem, w_out, ln2_g, ln2_b, w_ffn2_up,
           w_ffn2_down, ln3_g, ln3_b):
    h = x.reshape(TOKENS, D_MODEL)
    memf = mem.reshape(BATCH * MEM_LEN, D_MODEL)
    for l in range(DEPTH):
        c_qi = 3 * ATT_WIDTH + IDX_HEADS * IDX_DIM
        c_wi = c_qi + IDX_DIM + IDX_HEADS
        c_qm = c_wi + POOL_WIDTH + MEM_WIDTH

        z0, w1_a, w1_u, w1_down = _ffn_head(h, w_ffn1_up[l], w_ffn1_down[l])
        later = (w_in[l].T, w_ffn2_up[l], w_ffn2_down[l], w_mem_kv[l], w_br_att[l], w_br_pool[l],
                 w_br_mem[l], w_out[l])
        h, (wt, w2_up, w2_down, wb_mem_kv, wb_att, wb_pool, wb_mem, wb_out) = _ffn_ln(
            h, w1_a, w1_u, w1_down, ln1_g[l][None], ln1_b[l][None], z0=z0,
            casts=tuple((w, None) for w in later))

        za = _matmul(h, wt, bf16, 1024, 1024, "in_proj_a", w_is_transposed=True, n=c_qi)
        zb = _in_proj_b(h, wt, c_wi, c_qi)
        kv = _matmul(memf, wb_mem_kv, bf16, BATCH * MEM_LEN, 512, "mem_kv")

        c_ki = POOL_WIDTH + MEM_WIDTH
        ki = zb[:, c_ki:c_ki + IDX_DIM].astype(bf16)
        zk = jnp.zeros_like(ki)
        kk = jnp.stack([jnp.concatenate([ki, zk], axis=1), jnp.concatenate([zk, ki], axis=1)])
        wt_idx = zb[:, c_ki + IDX_DIM:c_ki + IDX_DIM + IDX_HEADS].T
        vt = za[:, 2 * ATT_WIDTH:3 * ATT_WIDTH].reshape(BATCH, DSA_NCH, DSA_TK, ATT_WIDTH)
        vt = vt.transpose(0, 1, 3, 2)

        a = _dsa(za, vt, kk, wt_idx).transpose(0, 2, 1).reshape(TOKENS, ATT_WIDTH)
        p = _pool(zb, w_pool[l].astype(bf16), pool_scale[l][None])
        m = _mem_attn(zb, POOL_WIDTH // MEM_WIDTH, kv)

        h = _combine_ln(h, a, p, m, wt, c_qm, b_gate[l][None], wb_att, wb_pool, wb_mem, wb_out,
                        ln2_g[l][None], ln2_b[l][None])

        h, _ = _ffn_ln(h, w2_up, w2_up, w2_down, ln3_g[l][None], ln3_b[l][None])
    return h.reshape(BATCH, SEQ, D_MODEL)
```

```python
import functools
import math

import jax
import jax.numpy as jnp
import numpy as np
from jax import lax
from jax.experimental import pallas as pl
from jax.experimental.pallas import tpu as pltpu

f32 = jnp.float32
bf16 = jnp.bfloat16
i32 = jnp.int32

D_MODEL = 2048
BATCH = 2
SEQ = 4096
DEPTH = 1
MEM_LEN = 256
ATT_HEADS = 8
HEAD_DIM = 128
ATT_WIDTH = ATT_HEADS * HEAD_DIM
IDX_HEADS = 16
IDX_DIM = 64
TOPK = min(256, SEQ // 4)
POOL_WINDOWS = (2, 4, 8, 16)
N_POOL = len(POOL_WINDOWS)
POOL_GROUP = 128
POOL_WIDTH = N_POOL * POOL_GROUP
MEM_HEADS = 4
MEM_WIDTH = MEM_HEADS * HEAD_DIM
N_BRANCH = 3
D_FF = 5632
ALPHA = (2 * DEPTH) ** 0.25
LN_EPS = 1e-5
TOKENS = BATCH * SEQ

LANES = 128
BF16_ROWS = 16
VMEM_LIMIT = 60 * 1024 * 1024

INT_MIN = -(2 ** 31)
LOG2E = math.log2(math.e)


def _params(*sem):
    return pltpu.CompilerParams(dimension_semantics=sem, vmem_limit_bytes=VMEM_LIMIT)


def _layer_norm(y, g, b):
    mu = jnp.mean(y, axis=-1, keepdims=True)
    d = y - mu
    var = jnp.mean(d * d, axis=-1, keepdims=True)
    return d * lax.rsqrt(var + LN_EPS) * g + b


FFN_TM = 512
FFN_TF = 512


FFN_LN_STEPS = 8
FFN_LN_ROWS = FFN_TM // FFN_LN_STEPS


def _ffn_ln_kernel(x_ref, wa_ref, wu_ref, wd_ref, g_ref, b_ref, *rest, n_cast, has_z0):
    if has_z0:
        z0_ref, rest = rest[0], rest[1:]
    cast_in = rest[:n_cast]
    o_ref = rest[n_cast]
    cast_out = rest[n_cast + 1:2 * n_cast + 1]
    xb_ref, acc_ref = rest[2 * n_cast + 1:]
    i = pl.program_id(0)
    j = pl.program_id(1)
    n_tiles = pl.num_programs(0) - 1
    slot = (i + 1) % 2 if has_z0 else i % 2
    computed = i < n_tiles

    for src, dst in zip(cast_in, cast_out):
        dst[...] = src[...].astype(dst.dtype)

    @pl.when((i == 0) & (j == 0))
    def _():
        if has_z0:
            acc_ref[0] = z0_ref[...]
        else:
            acc_ref[1] = jnp.zeros(acc_ref.shape[1:], f32)

    @pl.when(computed & (j == 0))
    def _():
        x = x_ref[...]
        xb_ref[...] = x.astype(bf16)
        acc_ref[slot] = (2.0 * ALPHA) * x

    def matmuls():
        xb = xb_ref[...]
        a = jnp.dot(xb, wa_ref[...], preferred_element_type=f32)
        u = jnp.dot(xb, wu_ref[...], preferred_element_type=f32)
        act = (a * jax.nn.sigmoid(a) * u).astype(bf16)
        acc_ref[slot] += jnp.dot(act, wd_ref[...], preferred_element_type=f32)

    def layer_norm_slab():
        rows = pl.ds(pl.multiple_of(j * FFN_LN_ROWS, FFN_LN_ROWS), FFN_LN_ROWS)
        z = acc_ref[1 - slot, rows, :]
        mu = jnp.mean(z, axis=-1, keepdims=True)
        d = z - mu
        var = jnp.mean(d * d, axis=-1, keepdims=True)
        o_ref[rows, :] = d * lax.rsqrt(var + 4.0 * LN_EPS) * g_ref[...] + b_ref[...]

    @pl.when(computed & (j < FFN_LN_STEPS))
    def _():
        layer_norm_slab()
        matmuls()

    @pl.when(computed & (j >= FFN_LN_STEPS))
    def _():
        matmuls()

    @pl.when((i == n_tiles) & (j < FFN_LN_STEPS))
    def _():
        layer_norm_slab()


CAST_SMALL_ELEMS = 4 * 1024 * 1024
CAST_SMALL_BLOCKS = 32


def _cast_block_spec(shape, rows_on_j, ni, nj):
    r, c = shape
    if rows_on_j is None:
        steps = ni * nj
        stride = 1 if r * c > CAST_SMALL_ELEMS else -(-steps // CAST_SMALL_BLOCKS)
        br = -(-r // -(-steps // stride))
        br = -(-br // BF16_ROWS) * BF16_ROWS
        last_block = -(-r // br) - 1
        return pl.BlockSpec((br, c), lambda i, j: (jnp.minimum(
            (jnp.minimum(i, ni - 1) * nj + jnp.where(i < ni, j, nj - 1)) // stride, last_block), 0))
    n_r, n_c = (nj, ni) if rows_on_j else (ni, nj)
    br = -(-r // n_r)
    br = -(-br // BF16_ROWS) * BF16_ROWS
    assert c % n_c == 0 and (c // n_c) % LANES == 0
    def clamp(i, j):
        return jnp.minimum(i, ni - 1), jnp.where(i < ni, j, nj - 1)

    if rows_on_j:
        index_map = lambda i, j: clamp(i, j)[::-1]
    else:
        index_map = lambda i, j: clamp(i, j)
    return pl.BlockSpec((br, c // n_c), index_map)


FFN_HEAD_TF = 256


def _ffn_head_kernel(x_ref, wa_ref, wu_ref, wd_ref, z_ref, wa_out, wu_out, wd_out, xb_ref):
    j = pl.program_id(0)

    @pl.when(j == 0)
    def _():
        x = x_ref[...]
        xb_ref[...] = x.astype(bf16)
        z_ref[...] = (2.0 * ALPHA) * x

    wa = wa_ref[...].astype(bf16)
    wu = wu_ref[...].astype(bf16)
    wd = wd_ref[...].astype(bf16)
    wa_out[...] = wa
    wu_out[...] = wu
    wd_out[...] = wd
    xb = xb_ref[...]
    a = jnp.dot(xb, wa, preferred_element_type=f32)
    u = jnp.dot(xb, wu, preferred_element_type=f32)
    act = (a * jax.nn.sigmoid(a) * u).astype(bf16)
    z_ref[...] += jnp.dot(act, wd, preferred_element_type=f32)


def _ffn_head(x, w_up, w_down):
    nf = D_FF // FFN_HEAD_TF
    return pl.pallas_call(
        _ffn_head_kernel,
        grid=(nf,),
        in_specs=[
            pl.BlockSpec((FFN_TM, D_MODEL), lambda j: (0, 0)),
            pl.BlockSpec((D_MODEL, FFN_HEAD_TF), lambda j: (0, j)),
            pl.BlockSpec((D_MODEL, FFN_HEAD_TF), lambda j: (0, j + nf)),
            pl.BlockSpec((FFN_HEAD_TF, D_MODEL), lambda j: (j, 0)),
        ],
        out_specs=[
            pl.BlockSpec((FFN_TM, D_MODEL), lambda j: (0, 0)),
            pl.BlockSpec((D_MODEL, FFN_HEAD_TF), lambda j: (0, j)),
            pl.BlockSpec((D_MODEL, FFN_HEAD_TF), lambda j: (0, j)),
            pl.BlockSpec((FFN_HEAD_TF, D_MODEL), lambda j: (j, 0)),
        ],
        out_shape=[
            jax.ShapeDtypeStruct((FFN_TM, D_MODEL), f32),
            jax.ShapeDtypeStruct((D_MODEL, D_FF), bf16),
            jax.ShapeDtypeStruct((D_MODEL, D_FF), bf16),
            jax.ShapeDtypeStruct((D_FF, D_MODEL), bf16),
        ],
        scratch_shapes=[pltpu.VMEM((FFN_TM, D_MODEL), bf16)],
        compiler_params=_params("arbitrary"),
        name="ffn_head",
    )(x, w_up, w_up, w_down)


def _ffn_ln(x, w_a, w_u, w_down, g, b, casts=(), z0=None):
    n_tok_tiles = TOKENS // FFN_TM
    first = 0 if z0 is None else 1
    ni = n_tok_tiles - first
    nf = D_FF // FFN_TF
    assert FFN_LN_STEPS <= nf
    u0 = (w_u.shape[1] - D_FF) // FFN_TF
    cast_specs = [_cast_block_spec(w.shape, rows_on_j, ni, nf) for w, rows_on_j in casts]
    z0_specs = [] if z0 is None else [pl.BlockSpec((FFN_TM, D_MODEL), lambda i, j: (0, 0))]
    z0_args = [] if z0 is None else [z0]

    def wj(i, j):
        return jnp.where(i < ni, j, nf - 1)

    outs = pl.pallas_call(
        functools.partial(_ffn_ln_kernel, n_cast=len(casts), has_z0=z0 is not None),
        grid=(ni + 1, nf),
        in_specs=[
            pl.BlockSpec((FFN_TM, D_MODEL), lambda i, j: (jnp.minimum(i + first, n_tok_tiles - 1), 0)),
            pl.BlockSpec((D_MODEL, FFN_TF), lambda i, j: (0, wj(i, j))),
            pl.BlockSpec((D_MODEL, FFN_TF), lambda i, j: (0, wj(i, j) + u0)),
            pl.BlockSpec((FFN_TF, D_MODEL), lambda i, j: (wj(i, j), 0)),
            pl.BlockSpec((1, D_MODEL), lambda i, j: (0, 0)),
            pl.BlockSpec((1, D_MODEL), lambda i, j: (0, 0)),
        ] + z0_specs + cast_specs,
        out_specs=[pl.BlockSpec((FFN_TM, D_MODEL), lambda i, j: (jnp.maximum(i + first - 1, 0), 0))]
        + cast_specs,
        out_shape=[jax.ShapeDtypeStruct((TOKENS, D_MODEL), f32)]
        + [jax.ShapeDtypeStruct(w.shape, bf16) for w, _ in casts],
        scratch_shapes=[pltpu.VMEM((FFN_TM, D_MODEL), bf16),
                        pltpu.VMEM((2, FFN_TM, D_MODEL), f32)],
        compiler_params=_params("arbitrary", "arbitrary"),
        name="ffn_ln",
    )(x, w_a, w_u, w_down, g, b, *z0_args, *[w for w, _ in casts])
    return outs[0], outs[1:]


_NT = (((1,), (1,)), ((), ()))


def _matmul_kernel(x_ref, w_ref, o_ref, *rest, w_is_transposed, also_block):
    xb_ref = rest[-1]

    @pl.when(pl.program_id(1) == 0)
    def _():
        xb_ref[...] = x_ref[...].astype(bf16)

    if w_is_transposed:
        y = lax.dot_general(xb_ref[...], w_ref[...], _NT, preferred_element_type=f32)
    else:
        y = jnp.dot(xb_ref[...], w_ref[...], preferred_element_type=f32)
    o_ref[...] = y.astype(o_ref.dtype)

    if also_block is not None:
        @pl.when(pl.program_id(1) == also_block)
        def _():
            rest[0][...] = y.astype(rest[0].dtype)


def _matmul(x, w, out_dtype, tm, tn, name, w_is_transposed=False, n=None, also_block=None):
    m, k = x.shape
    if n is None:
        n = w.shape[0] if w_is_transposed else w.shape[1]
    assert n % tn == 0 and m % tm == 0
    if w_is_transposed:
        w_spec = pl.BlockSpec((tn, k), lambda i, j: (j, 0))
    else:
        w_spec = pl.BlockSpec((k, tn), lambda i, j: (0, j))
    out_specs = [pl.BlockSpec((tm, tn), lambda i, j: (i, j))]
    out_shape = [jax.ShapeDtypeStruct((m, n), out_dtype)]
    if also_block is not None:
        out_specs.append(pl.BlockSpec((tm, tn), lambda i, j: (i, 0)))
        out_shape.append(jax.ShapeDtypeStruct((m, tn), out_dtype))
    outs = pl.pallas_call(
        functools.partial(_matmul_kernel, w_is_transposed=w_is_transposed, also_block=also_block),
        grid=(m // tm, n // tn),
        in_specs=[pl.BlockSpec((tm, k), lambda i, j: (i, 0)), w_spec],
        out_specs=out_specs,
        out_shape=out_shape,
        scratch_shapes=[pltpu.VMEM((tm, k), bf16)],
        compiler_params=_params("arbitrary", "arbitrary"),
        name=name,
    )(x, w)
    return outs[0] if also_block is None else outs


PB_TM = 512
PB_N = POOL_WIDTH + MEM_WIDTH + LANES


def _in_proj_b_kernel(x_ref, wum_ref, wk_ref, o_ref):
    xb = x_ref[...].astype(bf16)
    n_um = POOL_WIDTH + MEM_WIDTH
    o_ref[:, :n_um] = lax.dot_general(xb, wum_ref[...], _NT, preferred_element_type=f32)
    o_ref[:, n_um:] = lax.dot_general(xb, wk_ref[...], _NT, preferred_element_type=f32)


def _in_proj_b(x, wt, row_um, row_kiwi):
    n_um = POOL_WIDTH + MEM_WIDTH
    return pl.pallas_call(
        _in_proj_b_kernel,
        grid=(TOKENS // PB_TM,),
        in_specs=[
            pl.BlockSpec((PB_TM, D_MODEL), lambda i: (i, 0)),
            pl.BlockSpec((pl.Element(n_um), pl.Element(D_MODEL)), lambda i: (row_um, 0)),
            pl.BlockSpec((pl.Element(LANES), pl.Element(D_MODEL)), lambda i: (row_kiwi, 0)),
        ],
        out_specs=pl.BlockSpec((PB_TM, PB_N), lambda i: (i, 0)),
        out_shape=jax.ShapeDtypeStruct((TOKENS, PB_N), f32),
        compiler_params=_params("parallel"),
        name="in_proj_b",
    )(x, wt, wt)


DSA_TQ = 512
DSA_TK = 512
DSA_NCH = SEQ // DSA_TK
DSA_ONES = 16
KEY_NEG_FLT_MAX = -(2 ** 31) + (1 << 23)


def _key_to_f32(key):
    return lax.bitcast_convert_type(key ^ ((key >> 31) & jnp.int32(0x7FFFFFFF)), f32)


def _dsa_kernel(q_ref, k_ref, vt_ref, qi_ref, kk_ref, wt_ref, kf_ref, qf_ref, o_ref,
                sc_scr, sch_scr, acc_scr, qa_scr):
    i = pl.program_id(1)
    q0 = i * DSA_TQ
    nk = (q0 + DSA_TQ - 1) // DSA_TK + 1

    qpos = q0 + lax.broadcasted_iota(i32, (DSA_TK, DSA_TQ), 1)
    kofs = lax.broadcasted_iota(i32, (DSA_TK, DSA_TQ), 0)
    contract_last = (((1,), (1,)), ((), ()))

    def rows8(x):
        return x.reshape(DSA_TK // 8, 8, DSA_TQ)

    wt = wt_ref[...] * (IDX_DIM ** -0.5 * IDX_HEADS ** -0.5)

    def score_chunk(c, carry):
        r0 = pl.multiple_of(c * DSA_TK, DSA_TK)
        kk0 = kk_ref[0, pl.ds(r0, DSA_TK), :]
        kk1 = kk_ref[1, pl.ds(r0, DSA_TK), :]
        acc = jnp.zeros((DSA_TK, DSA_TQ), f32)
        for p in range(IDX_HEADS // 2):
            slab = qi_ref[:, p * LANES:(p + 1) * LANES]
            l0 = lax.dot_general(kk0, slab, contract_last, preferred_element_type=f32)
            l1 = lax.dot_general(kk1, slab, contract_last, preferred_element_type=f32)
            acc = acc + (jnp.maximum(l0, 0.0) * wt[2 * p:2 * p + 1]
                         + jnp.maximum(l1, 0.0) * wt[2 * p + 1:2 * p + 2])
        sc = jnp.where(kofs + r0 <= qpos, acc, -jnp.inf)
        sc_scr[c] = sc
        sch_scr[c] = sc.astype(bf16)
        return carry

    lax.fori_loop(0, nk, score_chunk, 0)

    def count(pred):
        n_acc = 8

        def body(c, accs):
            m = rows8(pred(c))
            accs = list(accs)
            for r in range(DSA_TK // 8):
                accs[r % n_acc] = jnp.where(m[r], accs[r % n_acc] + 1, accs[r % n_acc])
            return tuple(accs)
        accs = lax.fori_loop(0, nk, body, (jnp.zeros((8, DSA_TQ), i32),) * n_acc)
        acc = functools.reduce(lambda a, b: a + b, accs)
        return jnp.sum(acc, axis=0, keepdims=True)

    def count_ge(cand):
        return count(lambda c: sc_scr[c] >= cand)

    def count_ge_coarse(cand):
        n_acc = 2
        one = jnp.ones((), bf16)
        zero = jnp.zeros((), bf16)

        def body(c, accs):
            accs = list(accs)
            for r in range(DSA_TK // BF16_ROWS):
                rows = sch_scr[c, r * BF16_ROWS:(r + 1) * BF16_ROWS, :]
                accs[r % n_acc] = accs[r % n_acc] + jnp.where(rows >= cand, one, zero)
            return tuple(accs)
        accs = lax.fori_loop(0, nk, body, (jnp.zeros((BF16_ROWS, DSA_TQ), bf16),) * n_acc)
        acc = functools.reduce(lambda a, b: a + b, accs)
        return jnp.sum(acc.astype(f32), axis=0, keepdims=True)

    def coarse_cand(key):
        bits = lax.bitcast_convert_type(_key_to_f32(key), i32) & jnp.int32(-65536)
        return lax.bitcast_convert_type(bits, f32).astype(bf16)

    key = jnp.where(count_ge_coarse(jnp.zeros((1, DSA_TQ), bf16)) >= TOPK,
                    jnp.int32(0), jnp.int32(INT_MIN))

    def coarse_step(b, key):
        cand = key | jnp.left_shift(jnp.int32(1), 30 - b)
        return jnp.where(count_ge_coarse(coarse_cand(cand)) >= TOPK, cand, key)

    key = lax.fori_loop(0, 15, coarse_step, key)

    window_bits = 18
    lo = jnp.maximum(key, jnp.int32(INT_MIN + (1 << 17))) - jnp.int32((1 << 15) + 2)

    def fine_step(b, off):
        cand = off | jnp.left_shift(jnp.int32(1), (window_bits - 1) - b)
        return jnp.where(count_ge(_key_to_f32(lo + cand)) >= TOPK, cand, off)

    key = lo + lax.fori_loop(0, window_bits, fine_step, jnp.zeros((1, DSA_TQ), i32))
    thr = _key_to_f32(jnp.maximum(key, jnp.int32(KEY_NEG_FLT_MAX)))

    n_ge = count_ge(thr)

    @pl.when(jnp.max(n_ge) > TOPK)
    def _():
        need = TOPK - count(lambda c: sc_scr[c] > thr)

        def count_eq_below(pos):
            return count(lambda c: (sc_scr[c] == thr) & (kofs + c * DSA_TK < pos))

        def pos_step(b, r):
            cand = r | jnp.left_shift(jnp.int32(1), (SEQ.bit_length() - 2) - b)
            return jnp.where(count_eq_below(cand) < need, cand, r)
        r = lax.fori_loop(0, SEQ.bit_length() - 1, pos_step, jnp.zeros((1, DSA_TQ), i32))
        r = jnp.where(n_ge > TOPK, r, jnp.int32(SEQ))

        def drop(c, carry):
            sc = sc_scr[c]
            sc_scr[c] = jnp.where((sc == thr) & (kofs + c * DSA_TK > r), -jnp.inf, sc)
            return carry
        lax.fori_loop(0, nk, drop, 0)

    acc_scr[...] = jnp.zeros(acc_scr.shape, f32)
    c1 = HEAD_DIM ** -0.5 * LOG2E
    ones = jnp.ones((DSA_ONES, DSA_TK), bf16)

    for h in range(ATT_HEADS):
        qa_scr[h, :, :HEAD_DIM] = q_ref[:, h * HEAD_DIM:(h + 1) * HEAD_DIM]
        qa_scr[h, :, HEAD_DIM:] = jnp.broadcast_to(qf_ref[h, 0:1, :], (DSA_TQ, LANES))

    def attn_chunk(c, ms):
        r0 = pl.multiple_of(c * DSA_TK, DSA_TK)
        madd = jnp.where(sc_scr[c] >= thr, 0.0, -jnp.inf)
        kf = kf_ref[pl.ds(r0, DSA_TK), :]

        def qk(h):
            hd = slice(h * HEAD_DIM, (h + 1) * HEAD_DIM)
            k_aug = jnp.concatenate([k_ref[pl.ds(r0, DSA_TK), hd], kf], axis=1)
            return lax.dot_general(k_aug, qa_scr[h], contract_last, preferred_element_type=f32)

        def pv(h, alpha, p):
            hd = slice(h * HEAD_DIM, (h + 1) * HEAD_DIM)
            vt1 = jnp.concatenate([vt_ref[c, hd, :], ones], axis=0)
            acc_scr[h] = alpha * acc_scr[h] + jnp.dot(vt1, p, preferred_element_type=f32)

        new_ms = []
        s_next = qk(0)
        pending = None
        for h in range(ATT_HEADS):
            s = s_next
            if h + 1 < ATT_HEADS:
                s_next = qk(h + 1)
            if pending is not None:
                pv(*pending)
            t = s * c1 + madd
            m_old = ms[h]
            m_new = jnp.maximum(m_old, jnp.max(t, axis=0, keepdims=True))
            alpha = jnp.exp2(m_old - m_new)
            p = jnp.exp2(t - m_new).astype(bf16)
            pending = (h, alpha, p)
            new_ms.append(m_new)
        pv(*pending)
        return tuple(new_ms)

    m0 = jnp.full((1, DSA_TQ), -1e30, f32)
    lax.fori_loop(0, nk, attn_chunk, (m0,) * ATT_HEADS)

    for h in range(ATT_HEADS):
        o_ref[h * HEAD_DIM:(h + 1) * HEAD_DIM, :] = (
            acc_scr[h, :HEAD_DIM, :] / acc_scr[h, HEAD_DIM:HEAD_DIM + 1, :]).astype(o_ref.dtype)


def _alibi_features():
    assert ATT_HEADS == 8 and HEAD_DIM == 128 and SEQ <= 64 * 64
    pieces, rest = [], math.sqrt(2.0)
    for _ in range(7):
        p = float(np.asarray(rest, np.float32).astype(jnp.bfloat16).astype(np.float64))
        pieces.append(p)
        rest -= p
    qf = np.zeros((ATT_HEADS, BF16_ROWS, LANES), np.float32)
    for h in range(ATT_HEADS):
        for i, p in enumerate(pieces):
            qf[h, :, 2 * i] = qf[h, :, 2 * i + 1] = p * 2.0 ** (2 - h)
    pos = np.arange(SEQ)
    kf = np.zeros((SEQ, LANES), np.float32)
    for i in range(len(pieces)):
        kf[:, 2 * i] = pos - pos % 64
        kf[:, 2 * i + 1] = pos % 64
    return jnp.asarray(kf, bf16), jnp.asarray(qf, bf16)


def _dsa(za, vt, kk, wt_idx):
    nq = SEQ // DSA_TQ
    kf, qf = _alibi_features()
    once = pl.Buffered(1)
    return pl.pallas_call(
        _dsa_kernel,
        grid=(BATCH, nq),
        in_specs=[
            pl.BlockSpec((DSA_TQ, ATT_WIDTH), lambda b, i: (b * nq + i, 0)),
            pl.BlockSpec((SEQ, ATT_WIDTH), lambda b, i: (b, 1), pipeline_mode=once),
            pl.BlockSpec((None, DSA_NCH, ATT_WIDTH, DSA_TK), lambda b, i: (b, 0, 0, 0),
                         pipeline_mode=once),
            pl.BlockSpec((DSA_TQ, IDX_HEADS * IDX_DIM), lambda b, i: (b * nq + i, 3)),
            pl.BlockSpec((2, SEQ, LANES), lambda b, i: (0, b, 0), pipeline_mode=once),
            pl.BlockSpec((IDX_HEADS, DSA_TQ), lambda b, i: (0, b * nq + i)),
            pl.BlockSpec((SEQ, LANES), lambda b, i: (0, 0), pipeline_mode=once),
            pl.BlockSpec((ATT_HEADS, BF16_ROWS, LANES), lambda b, i: (0, 0, 0)),
        ],
        out_specs=pl.BlockSpec((None, ATT_WIDTH, DSA_TQ), lambda b, i: (b, 0, i)),
        out_shape=jax.ShapeDtypeStruct((BATCH, ATT_WIDTH, SEQ), bf16),
        scratch_shapes=[
            pltpu.VMEM((DSA_NCH, DSA_TK, DSA_TQ), f32),
            pltpu.VMEM((DSA_NCH, DSA_TK, DSA_TQ), bf16),
            pltpu.VMEM((ATT_HEADS, HEAD_DIM + DSA_ONES, DSA_TQ), f32),
            pltpu.VMEM((ATT_HEADS, DSA_TQ, HEAD_DIM + LANES), bf16),
        ],
        compiler_params=_params("arbitrary", "arbitrary"),
        name="dsa",
    )(za, za, vt, za, kk, wt_idx, kf, qf)


def _pool_kernel(u_ref, wp_ref, ps_ref, o_ref):
    row = lax.broadcasted_iota(i32, (SEQ, POOL_GROUP), 0)
    for g, win in enumerate(POOL_WINDOWS):
        x = u_ref[:, g * POOL_GROUP:(g + 1) * POOL_GROUP]
        s = x
        k = 1
        while k < win:
            s = s + jnp.where(row >= k, pltpu.roll(s, k, axis=0), 0.0)
            k *= 2
        cnt = jnp.minimum(row + 1, win).astype(f32)
        pooled = (s / cnt - x).astype(bf16)
        mixed = jnp.dot(pooled, wp_ref[g], preferred_element_type=f32)
        o_ref[:, g * POOL_GROUP:(g + 1) * POOL_GROUP] = (
            mixed * ps_ref[:, g * POOL_GROUP:(g + 1) * POOL_GROUP]).astype(o_ref.dtype)


def _pool(zb, w_pool, pool_scale):
    return pl.pallas_call(
        _pool_kernel,
        grid=(BATCH,),
        in_specs=[
            pl.BlockSpec((SEQ, POOL_WIDTH), lambda b: (b, 0)),
            pl.BlockSpec((N_POOL, POOL_GROUP, POOL_GROUP), lambda b: (0, 0, 0)),
            pl.BlockSpec((1, POOL_WIDTH), lambda b: (0, 0)),
        ],
        out_specs=pl.BlockSpec((SEQ, POOL_WIDTH), lambda b: (b, 0)),
        out_shape=jax.ShapeDtypeStruct((TOKENS, POOL_WIDTH), bf16),
        compiler_params=_params("parallel"),
        name="pool",
    )(zb, w_pool, pool_scale)


MEM_TQ = 512


def _mem_attn_kernel(q_ref, k_ref, v_ref, o_ref):
    for h in range(MEM_HEADS):
        sl = slice(h * HEAD_DIM, (h + 1) * HEAD_DIM)
        s = lax.dot_general(q_ref[:, sl].astype(bf16), k_ref[:, sl], (((1,), (1,)), ((), ())),
                            preferred_element_type=f32) * (HEAD_DIM ** -0.5)
        p = jnp.exp(s - jnp.max(s, axis=1, keepdims=True))
        l = jnp.sum(p, axis=1, keepdims=True)
        o = jnp.dot(p.astype(bf16), v_ref[:, sl], preferred_element_type=f32)
        o_ref[:, sl] = (o / l).astype(o_ref.dtype)


def _mem_attn(zb, qcol, kv):
    nq = SEQ // MEM_TQ
    return pl.pallas_call(
        _mem_attn_kernel,
        grid=(BATCH, nq),
        in_specs=[
            pl.BlockSpec((MEM_TQ, MEM_WIDTH), lambda b, i: (b * nq + i, qcol)),
            pl.BlockSpec((MEM_LEN, MEM_WIDTH), lambda b, i: (b, 0)),
            pl.BlockSpec((MEM_LEN, MEM_WIDTH), lambda b, i: (b, 1)),
        ],
        out_specs=pl.BlockSpec((MEM_TQ, MEM_WIDTH), lambda b, i: (b * nq + i, 0)),
        out_shape=jax.ShapeDtypeStruct((TOKENS, MEM_WIDTH), bf16),
        compiler_params=_params("parallel", "parallel"),
        name="mem_attn",
    )(zb, kv, kv)


CMB_TM = 512
CMB_TC = 512


def _combine_ln_kernel(h_ref, a_ref, p_ref, m_ref, wg0_ref, wg1_ref, wg2_ref,
                       bg0_ref, bg1_ref, bg2_ref, wa_ref, wp_ref, wm_ref, wo_ref,
                       g_ref, b_ref, o_ref, hb_ref, acc_ref):
    i = pl.program_id(0)
    j = pl.program_id(1)
    n_tiles = pl.num_programs(0) - 1
    slot = i % 2
    ln_rows = CMB_TM // (D_MODEL // CMB_TC)

    @pl.when((i == 0) & (j == 0))
    def _():
        acc_ref[1] = jnp.zeros(acc_ref.shape[1:], f32)

    @pl.when((i < n_tiles) & (j == 0))
    def _():
        h = h_ref[...]
        hb_ref[...] = h.astype(bf16)
        acc_ref[slot] = ALPHA * h

    def layer_norm_slab():
        rows = pl.ds(pl.multiple_of(j * ln_rows, ln_rows), ln_rows)
        o_ref[rows, :] = _layer_norm(acc_ref[1 - slot, rows, :], g_ref[...], b_ref[...])

    def matmuls():
        hb = hb_ref[...]

        def gate(wg_ref, bg_ref):
            logits = lax.dot_general(hb, wg_ref[...], _NT, preferred_element_type=f32)
            return jax.nn.sigmoid(logits + bg_ref[...])

        y = gate(wg0_ref, bg0_ref) * jnp.dot(a_ref[...], wa_ref[...], preferred_element_type=f32)
        y = y + gate(wg1_ref, bg1_ref) * jnp.dot(p_ref[...], wp_ref[...], preferred_element_type=f32)
        y = y + gate(wg2_ref, bg2_ref) * jnp.dot(m_ref[...], wm_ref[...], preferred_element_type=f32)
        acc_ref[slot] += jnp.dot(y.astype(bf16), wo_ref[...], preferred_element_type=f32)

    @pl.when(i < n_tiles)
    def _():
        layer_norm_slab()
        matmuls()

    @pl.when(i == n_tiles)
    def _():
        layer_norm_slab()


def _combine_ln(h, a, p, m, wt_gate, row_gate, b_gate, w_a, w_p, w_m, w_out, g, b):
    nc = D_MODEL // CMB_TC
    gate_rows = pl.Element(CMB_TC)
    all_cols = pl.Element(D_MODEL)
    assert row_gate % BF16_ROWS == 0

    def gate_row(blk):
        return pl.multiple_of(row_gate + blk * CMB_TC, BF16_ROWS)

    ni = TOKENS // CMB_TM

    def cj(i, j):
        return jnp.where(i < ni, j, nc - 1)

    row = lambda i, j: (jnp.minimum(i, ni - 1), 0)
    col = lambda i, j: (0, cj(i, j))
    return pl.pallas_call(
        _combine_ln_kernel,
        grid=(ni + 1, nc),
        in_specs=[
            pl.BlockSpec((CMB_TM, D_MODEL), row),
            pl.BlockSpec((CMB_TM, ATT_WIDTH), row),
            pl.BlockSpec((CMB_TM, POOL_WIDTH), row),
            pl.BlockSpec((CMB_TM, MEM_WIDTH), row),
            pl.BlockSpec((gate_rows, all_cols), lambda i, j: (gate_row(cj(i, j)), 0)),
            pl.BlockSpec((gate_rows, all_cols), lambda i, j: (gate_row(cj(i, j) + nc), 0)),
            pl.BlockSpec((gate_rows, all_cols), lambda i, j: (gate_row(cj(i, j) + 2 * nc), 0)),
            pl.BlockSpec((1, CMB_TC), lambda i, j: (0, cj(i, j))),
            pl.BlockSpec((1, CMB_TC), lambda i, j: (0, cj(i, j) + nc)),
            pl.BlockSpec((1, CMB_TC), lambda i, j: (0, cj(i, j) + 2 * nc)),
            pl.BlockSpec((ATT_WIDTH, CMB_TC), col),
            pl.BlockSpec((POOL_WIDTH, CMB_TC), col),
            pl.BlockSpec((MEM_WIDTH, CMB_TC), col),
            pl.BlockSpec((CMB_TC, D_MODEL), lambda i, j: (cj(i, j), 0)),
            pl.BlockSpec((1, D_MODEL), lambda i, j: (0, 0)),
            pl.BlockSpec((1, D_MODEL), lambda i, j: (0, 0)),
        ],
        out_specs=pl.BlockSpec((CMB_TM, D_MODEL), lambda i, j: (jnp.maximum(i - 1, 0), 0)),
        out_shape=jax.ShapeDtypeStruct((TOKENS, D_MODEL), f32),
        scratch_shapes=[pltpu.VMEM((CMB_TM, D_MODEL), bf16),
                        pltpu.VMEM((2, CMB_TM, D_MODEL), f32)],
        compiler_params=_params("arbitrary", "arbitrary"),
        name="combine_ln",
    )(h, a, p, m, wt_gate, wt_gate, wt_gate, b_gate, b_gate, b_gate, w_a, w_p, w_m, w_out, g, b)


def kernel(x, mem, w_ffn1_up, w_ffn1_down, ln1_g, ln1_b, w_in, b_gate, w_mem_kv, w_pool,
           pool_scale, w_br_att, w_br_pool, w_br_mem, w_out, ln2_g, ln2_b, w_ffn2_up,
           w_ffn2_down, ln3_g, ln3_b):
    h = x.reshape(TOKENS, D_MODEL)
    memf = mem.reshape(BATCH * MEM_LEN, D_MODEL)
    for l in range(DEPTH):
        c_qi = 3 * ATT_WIDTH + IDX_HEADS * IDX_DIM
        c_wi = c_qi + IDX_DIM + IDX_HEADS
        c_qm = c_wi + POOL_WIDTH + MEM_WIDTH

        z0, w1_a, w1_u, w1_down = _ffn_head(h, w_ffn1_up[l], w_ffn1_down[l])
        later = (w_in[l].T, w_ffn2_up[l], w_ffn2_down[l], w_mem_kv[l], w_br_att[l], w_br_pool[l],
                 w_br_mem[l], w_out[l])
        h, (wt, w2_up, w2_down, wb_mem_kv, wb_att, wb_pool, wb_mem, wb_out) = _ffn_ln(
            h, w1_a, w1_u, w1_down, ln1_g[l][None], ln1_b[l][None], z0=z0,
            casts=tuple((w, None) for w in later))

        za, v = _matmul(h, wt, bf16, 1024, ATT_WIDTH, "in_proj_a", w_is_transposed=True, n=c_qi,
                        also_block=2)
        zb = _in_proj_b(h, wt, c_wi, c_qi)
        kv = _matmul(memf, wb_mem_kv, bf16, BATCH * MEM_LEN, 512, "mem_kv")

        c_ki = POOL_WIDTH + MEM_WIDTH
        ki = zb[:, c_ki:c_ki + IDX_DIM].astype(bf16)
        zk = jnp.zeros_like(ki)
        kk = jnp.stack([jnp.concatenate([ki, zk], axis=1), jnp.concatenate([zk, ki], axis=1)])
        wt_idx = zb[:, c_ki + IDX_DIM:c_ki + IDX_DIM + IDX_HEADS].T
        vt = v.reshape(BATCH, DSA_NCH, DSA_TK, ATT_WIDTH).transpose(0, 1, 3, 2)

        a = _dsa(za, vt, kk, wt_idx).transpose(0, 2, 1).reshape(TOKENS, ATT_WIDTH)
        p = _pool(zb, w_pool[l].astype(bf16), pool_scale[l][None])
        m = _mem_attn(zb, POOL_WIDTH // MEM_WIDTH, kv)

        h = _combine_ln(h, a, p, m, wt, c_qm, b_gate[l][None], wb_att, wb_pool, wb_mem, wb_out,
                        ln2_g[l][None], ln2_b[l][None])

        h, _ = _ffn_ln(h, w2_up, w2_up, w2_down, ln3_g[l][None], ln3_b[l][None])
    return h.reshape(BATCH, SEQ, D_MODEL)
```

```python
import functools
import math

import jax
import jax.numpy as jnp
import numpy as np
from jax import lax
from jax.experimental import pallas as pl
from jax.experimental.pallas import tpu as pltpu

f32 = jnp.float32
bf16 = jnp.bfloat16
i32 = jnp.int32

D_MODEL = 2048
BATCH = 2
SEQ = 4096
DEPTH = 1
MEM_LEN = 256
ATT_HEADS = 8
HEAD_DIM = 128
ATT_WIDTH = ATT_HEADS * HEAD_DIM
IDX_HEADS = 16
IDX_DIM = 64
TOPK = min(256, SEQ // 4)
POOL_WINDOWS = (2, 4, 8, 16)
N_POOL = len(POOL_WINDOWS)
POOL_GROUP = 128
POOL_WIDTH = N_POOL * POOL_GROUP
MEM_HEADS = 4
MEM_WIDTH = MEM_HEADS * HEAD_DIM
N_BRANCH = 3
D_FF = 5632
ALPHA = (2 * DEPTH) ** 0.25
LN_EPS = 1e-5
TOKENS = BATCH * SEQ

LANES = 128
BF16_ROWS = 16
VMEM_LIMIT = 60 * 1024 * 1024

INT_MIN = -(2 ** 31)
LOG2E = math.log2(math.e)


def _params(*sem):
    return pltpu.CompilerParams(dimension_semantics=sem, vmem_limit_bytes=VMEM_LIMIT)


def _layer_norm(y, g, b):
    mu = jnp.mean(y, axis=-1, keepdims=True)
    d = y - mu
    var = jnp.mean(d * d, axis=-1, keepdims=True)
    return d * lax.rsqrt(var + LN_EPS) * g + b


FFN_TM = 512
FFN_TF = 512


FFN_LN_STEPS = 8
FFN_LN_ROWS = FFN_TM // FFN_LN_STEPS


def _ffn_ln_kernel(x_ref, wa_ref, wu_ref, wd_ref, g_ref, b_ref, *rest, n_cast, has_z0):
    if has_z0:
        z0_ref, rest = rest[0], rest[1:]
    cast_in = rest[:n_cast]
    o_ref = rest[n_cast]
    cast_out = rest[n_cast + 1:2 * n_cast + 1]
    xb_ref, acc_ref = rest[2 * n_cast + 1:]
    i = pl.program_id(0)
    j = pl.program_id(1)
    n_tiles = pl.num_programs(0) - 1
    slot = (i + 1) % 2 if has_z0 else i % 2
    computed = i < n_tiles

    for src, dst in zip(cast_in, cast_out):
        dst[...] = src[...].astype(dst.dtype)

    @pl.when((i == 0) & (j == 0))
    def _():
        if has_z0:
            acc_ref[0] = z0_ref[...]
        else:
            acc_ref[1] = jnp.zeros(acc_ref.shape[1:], f32)

    @pl.when(computed & (j == 0))
    def _():
        x = x_ref[...]
        xb_ref[...] = x.astype(bf16)
        acc_ref[slot] = (2.0 * ALPHA) * x

    def matmuls():
        xb = xb_ref[...]
        a = jnp.dot(xb, wa_ref[...], preferred_element_type=f32)
        u = jnp.dot(xb, wu_ref[...], preferred_element_type=f32)
        act = (a * jax.nn.sigmoid(a) * u).astype(bf16)
        acc_ref[slot] += jnp.dot(act, wd_ref[...], preferred_element_type=f32)

    def layer_norm_slab():
        rows = pl.ds(pl.multiple_of(j * FFN_LN_ROWS, FFN_LN_ROWS), FFN_LN_ROWS)
        z = acc_ref[1 - slot, rows, :]
        mu = jnp.mean(z, axis=-1, keepdims=True)
        d = z - mu
        var = jnp.mean(d * d, axis=-1, keepdims=True)
        o_ref[rows, :] = d * lax.rsqrt(var + 4.0 * LN_EPS) * g_ref[...] + b_ref[...]

    @pl.when(computed & (j < FFN_LN_STEPS))
    def _():
        layer_norm_slab()
        matmuls()

    @pl.when(computed & (j >= FFN_LN_STEPS))
    def _():
        matmuls()

    @pl.when((i == n_tiles) & (j < FFN_LN_STEPS))
    def _():
        layer_norm_slab()


def _cast_block_spec(shape, rows_on_j, ni, nj):
    r, c = shape
    if rows_on_j is None:
        br = -(-r // (ni * nj))
        br = -(-br // BF16_ROWS) * BF16_ROWS
        last_block = -(-r // br) - 1
        return pl.BlockSpec((br, c), lambda i, j: (
            jnp.minimum(jnp.minimum(i, ni - 1) * nj + jnp.where(i < ni, j, nj - 1), last_block), 0))
    n_r, n_c = (nj, ni) if rows_on_j else (ni, nj)
    br = -(-r // n_r)
    br = -(-br // BF16_ROWS) * BF16_ROWS
    assert c % n_c == 0 and (c // n_c) % LANES == 0
    def clamp(i, j):
        return jnp.minimum(i, ni - 1), jnp.where(i < ni, j, nj - 1)

    if rows_on_j:
        index_map = lambda i, j: clamp(i, j)[::-1]
    else:
        index_map = lambda i, j: clamp(i, j)
    return pl.BlockSpec((br, c // n_c), index_map)


FFN_HEAD_TF = 256


def _ffn_head_kernel(x_ref, wa_ref, wu_ref, wd_ref, z_ref, wa_out, wu_out, wd_out, xb_ref):
    j = pl.program_id(0)

    @pl.when(j == 0)
    def _():
        x = x_ref[...]
        xb_ref[...] = x.astype(bf16)
        z_ref[...] = (2.0 * ALPHA) * x

    wa = wa_ref[...].astype(bf16)
    wu = wu_ref[...].astype(bf16)
    wd = wd_ref[...].astype(bf16)
    wa_out[...] = wa
    wu_out[...] = wu
    wd_out[...] = wd
    xb = xb_ref[...]
    a = jnp.dot(xb, wa, preferred_element_type=f32)
    u = jnp.dot(xb, wu, preferred_element_type=f32)
    act = (a * jax.nn.sigmoid(a) * u).astype(bf16)
    z_ref[...] += jnp.dot(act, wd, preferred_element_type=f32)


def _ffn_head(x, w_up, w_down):
    nf = D_FF // FFN_HEAD_TF
    return pl.pallas_call(
        _ffn_head_kernel,
        grid=(nf,),
        in_specs=[
            pl.BlockSpec((FFN_TM, D_MODEL), lambda j: (0, 0)),
            pl.BlockSpec((D_MODEL, FFN_HEAD_TF), lambda j: (0, j)),
            pl.BlockSpec((D_MODEL, FFN_HEAD_TF), lambda j: (0, j + nf)),
            pl.BlockSpec((FFN_HEAD_TF, D_MODEL), lambda j: (j, 0)),
        ],
        out_specs=[
            pl.BlockSpec((FFN_TM, D_MODEL), lambda j: (0, 0)),
            pl.BlockSpec((D_MODEL, FFN_HEAD_TF), lambda j: (0, j)),
            pl.BlockSpec((D_MODEL, FFN_HEAD_TF), lambda j: (0, j)),
            pl.BlockSpec((FFN_HEAD_TF, D_MODEL), lambda j: (j, 0)),
        ],
        out_shape=[
            jax.ShapeDtypeStruct((FFN_TM, D_MODEL), f32),
            jax.ShapeDtypeStruct((D_MODEL, D_FF), bf16),
            jax.ShapeDtypeStruct((D_MODEL, D_FF), bf16),
            jax.ShapeDtypeStruct((D_FF, D_MODEL), bf16),
        ],
        scratch_shapes=[pltpu.VMEM((FFN_TM, D_MODEL), bf16)],
        compiler_params=_params("arbitrary"),
        name="ffn_head",
    )(x, w_up, w_up, w_down)


def _ffn_ln(x, w_a, w_u, w_down, g, b, casts=(), z0=None):
    n_tok_tiles = TOKENS // FFN_TM
    first = 0 if z0 is None else 1
    ni = n_tok_tiles - first
    nf = D_FF // FFN_TF
    assert FFN_LN_STEPS <= nf
    u0 = (w_u.shape[1] - D_FF) // FFN_TF
    cast_specs = [_cast_block_spec(w.shape, rows_on_j, ni, nf) for w, rows_on_j in casts]
    z0_specs = [] if z0 is None else [pl.BlockSpec((FFN_TM, D_MODEL), lambda i, j: (0, 0))]
    z0_args = [] if z0 is None else [z0]

    def wj(i, j):
        return jnp.where(i < ni, j, nf - 1)

    outs = pl.pallas_call(
        functools.partial(_ffn_ln_kernel, n_cast=len(casts), has_z0=z0 is not None),
        grid=(ni + 1, nf),
        in_specs=[
            pl.BlockSpec((FFN_TM, D_MODEL), lambda i, j: (jnp.minimum(i + first, n_tok_tiles - 1), 0)),
            pl.BlockSpec((D_MODEL, FFN_TF), lambda i, j: (0, wj(i, j))),
            pl.BlockSpec((D_MODEL, FFN_TF), lambda i, j: (0, wj(i, j) + u0)),
            pl.BlockSpec((FFN_TF, D_MODEL), lambda i, j: (wj(i, j), 0)),
            pl.BlockSpec((1, D_MODEL), lambda i, j: (0, 0)),
            pl.BlockSpec((1, D_MODEL), lambda i, j: (0, 0)),
        ] + z0_specs + cast_specs,
        out_specs=[pl.BlockSpec((FFN_TM, D_MODEL), lambda i, j: (jnp.maximum(i + first - 1, 0), 0))]
        + cast_specs,
        out_shape=[jax.ShapeDtypeStruct((TOKENS, D_MODEL), f32)]
        + [jax.ShapeDtypeStruct(w.shape, bf16) for w, _ in casts],
        scratch_shapes=[pltpu.VMEM((FFN_TM, D_MODEL), bf16),
                        pltpu.VMEM((2, FFN_TM, D_MODEL), f32)],
        compiler_params=_params("arbitrary", "arbitrary"),
        name="ffn_ln",
    )(x, w_a, w_u, w_down, g, b, *z0_args, *[w for w, _ in casts])
    return outs[0], outs[1:]


_NT = (((1,), (1,)), ((), ()))


def _matmul_kernel(x_ref, w_ref, o_ref, *rest, w_is_transposed, also_block):
    xb_ref = rest[-1]

    @pl.when(pl.program_id(1) == 0)
    def _():
        xb_ref[...] = x_ref[...].astype(bf16)

    if w_is_transposed:
        y = lax.dot_general(xb_ref[...], w_ref[...], _NT, preferred_element_type=f32)
    else:
        y = jnp.dot(xb_ref[...], w_ref[...], preferred_element_type=f32)
    o_ref[...] = y.astype(o_ref.dtype)

    if also_block is not None:
        @pl.when(pl.program_id(1) == also_block)
        def _():
            rest[0][...] = y.astype(rest[0].dtype)


def _matmul(x, w, out_dtype, tm, tn, name, w_is_transposed=False, n=None, also_block=None):
    m, k = x.shape
    if n is None:
        n = w.shape[0] if w_is_transposed else w.shape[1]
    assert n % tn == 0 and m % tm == 0
    if w_is_transposed:
        w_spec = pl.BlockSpec((tn, k), lambda i, j: (j, 0))
    else:
        w_spec = pl.BlockSpec((k, tn), lambda i, j: (0, j))
    out_specs = [pl.BlockSpec((tm, tn), lambda i, j: (i, j))]
    out_shape = [jax.ShapeDtypeStruct((m, n), out_dtype)]
    if also_block is not None:
        out_specs.append(pl.BlockSpec((tm, tn), lambda i, j: (i, 0)))
        out_shape.append(jax.ShapeDtypeStruct((m, tn), out_dtype))
    outs = pl.pallas_call(
        functools.partial(_matmul_kernel, w_is_transposed=w_is_transposed, also_block=also_block),
        grid=(m // tm, n // tn),
        in_specs=[pl.BlockSpec((tm, k), lambda i, j: (i, 0)), w_spec],
        out_specs=out_specs,
        out_shape=out_shape,
        scratch_shapes=[pltpu.VMEM((tm, k), bf16)],
        compiler_params=_params("arbitrary", "arbitrary"),
        name=name,
    )(x, w)
    return outs[0] if also_block is None else outs


PB_TM = 512
PB_N = POOL_WIDTH + MEM_WIDTH + LANES


def _in_proj_b_kernel(x_ref, wum_ref, wk_ref, o_ref):
    xb = x_ref[...].astype(bf16)
    n_um = POOL_WIDTH + MEM_WIDTH
    o_ref[:, :n_um] = lax.dot_general(xb, wum_ref[...], _NT, preferred_element_type=f32)
    o_ref[:, n_um:] = lax.dot_general(xb, wk_ref[...], _NT, preferred_element_type=f32)


def _in_proj_b(x, wt, row_um, row_kiwi):
    n_um = POOL_WIDTH + MEM_WIDTH
    return pl.pallas_call(
        _in_proj_b_kernel,
        grid=(TOKENS // PB_TM,),
        in_specs=[
            pl.BlockSpec((PB_TM, D_MODEL), lambda i: (i, 0)),
            pl.BlockSpec((pl.Element(n_um), pl.Element(D_MODEL)), lambda i: (row_um, 0)),
            pl.BlockSpec((pl.Element(LANES), pl.Element(D_MODEL)), lambda i: (row_kiwi, 0)),
        ],
        out_specs=pl.BlockSpec((PB_TM, PB_N), lambda i: (i, 0)),
        out_shape=jax.ShapeDtypeStruct((TOKENS, PB_N), f32),
        compiler_params=_params("parallel"),
        name="in_proj_b",
    )(x, wt, wt)


DSA_TQ = 512
DSA_TK = 512
DSA_NCH = SEQ // DSA_TK
DSA_ONES = 16
KEY_NEG_FLT_MAX = -(2 ** 31) + (1 << 23)


def _key_to_f32(key):
    return lax.bitcast_convert_type(key ^ ((key >> 31) & jnp.int32(0x7FFFFFFF)), f32)


def _dsa_kernel(q_ref, k_ref, vt_ref, qi_ref, kk_ref, wt_ref, kf_ref, qf_ref, o_ref,
                sc_scr, sch_scr, acc_scr, qa_scr):
    i = pl.program_id(1)
    q0 = i * DSA_TQ
    nk = (q0 + DSA_TQ - 1) // DSA_TK + 1

    qpos = q0 + lax.broadcasted_iota(i32, (DSA_TK, DSA_TQ), 1)
    kofs = lax.broadcasted_iota(i32, (DSA_TK, DSA_TQ), 0)
    contract_last = (((1,), (1,)), ((), ()))

    def rows8(x):
        return x.reshape(DSA_TK // 8, 8, DSA_TQ)

    wt = wt_ref[...] * (IDX_DIM ** -0.5 * IDX_HEADS ** -0.5)

    def score_chunk(c, carry):
        r0 = pl.multiple_of(c * DSA_TK, DSA_TK)
        kk0 = kk_ref[0, pl.ds(r0, DSA_TK), :]
        kk1 = kk_ref[1, pl.ds(r0, DSA_TK), :]
        acc = jnp.zeros((DSA_TK, DSA_TQ), f32)
        for p in range(IDX_HEADS // 2):
            slab = qi_ref[:, p * LANES:(p + 1) * LANES]
            l0 = lax.dot_general(kk0, slab, contract_last, preferred_element_type=f32)
            l1 = lax.dot_general(kk1, slab, contract_last, preferred_element_type=f32)
            acc = acc + (jnp.maximum(l0, 0.0) * wt[2 * p:2 * p + 1]
                         + jnp.maximum(l1, 0.0) * wt[2 * p + 1:2 * p + 2])
        sc = jnp.where(kofs + r0 <= qpos, acc, -jnp.inf)
        sc_scr[c] = sc
        sch_scr[c] = sc.astype(bf16)
        return carry

    lax.fori_loop(0, nk, score_chunk, 0)

    def count(pred):
        n_acc = 8

        def body(c, accs):
            m = rows8(pred(c))
            accs = list(accs)
            for r in range(DSA_TK // 8):
                accs[r % n_acc] = jnp.where(m[r], accs[r % n_acc] + 1, accs[r % n_acc])
            return tuple(accs)
        accs = lax.fori_loop(0, nk, body, (jnp.zeros((8, DSA_TQ), i32),) * n_acc)
        acc = functools.reduce(lambda a, b: a + b, accs)
        return jnp.sum(acc, axis=0, keepdims=True)

    def count_ge(cand):
        return count(lambda c: sc_scr[c] >= cand)

    def count_ge_coarse(cand):
        n_acc = 2
        one = jnp.ones((), bf16)
        zero = jnp.zeros((), bf16)

        def body(c, accs):
            accs = list(accs)
            for r in range(DSA_TK // BF16_ROWS):
                rows = sch_scr[c, r * BF16_ROWS:(r + 1) * BF16_ROWS, :]
                accs[r % n_acc] = accs[r % n_acc] + jnp.where(rows >= cand, one, zero)
            return tuple(accs)
        accs = lax.fori_loop(0, nk, body, (jnp.zeros((BF16_ROWS, DSA_TQ), bf16),) * n_acc)
        acc = functools.reduce(lambda a, b: a + b, accs)
        return jnp.sum(acc.astype(f32), axis=0, keepdims=True)

    def coarse_cand(key):
        bits = lax.bitcast_convert_type(_key_to_f32(key), i32) & jnp.int32(-65536)
        return lax.bitcast_convert_type(bits, f32).astype(bf16)

    key = jnp.where(count_ge_coarse(jnp.zeros((1, DSA_TQ), bf16)) >= TOPK,
                    jnp.int32(0), jnp.int32(INT_MIN))

    def coarse_step(b, key):
        cand = key | jnp.left_shift(jnp.int32(1), 30 - b)
        return jnp.where(count_ge_coarse(coarse_cand(cand)) >= TOPK, cand, key)

    key = lax.fori_loop(0, 15, coarse_step, key)

    window_bits = 18
    lo = jnp.maximum(key, jnp.int32(INT_MIN + (1 << 17))) - jnp.int32((1 << 15) + 2)

    def fine_step(b, off):
        cand = off | jnp.left_shift(jnp.int32(1), (window_bits - 1) - b)
        return jnp.where(count_ge(_key_to_f32(lo + cand)) >= TOPK, cand, off)

    key = lo + lax.fori_loop(0, window_bits, fine_step, jnp.zeros((1, DSA_TQ), i32))
    thr = _key_to_f32(jnp.maximum(key, jnp.int32(KEY_NEG_FLT_MAX)))

    n_ge = count_ge(thr)

    @pl.when(jnp.max(n_ge) > TOPK)
    def _():
        need = TOPK - count(lambda c: sc_scr[c] > thr)

        def count_eq_below(pos):
            return count(lambda c: (sc_scr[c] == thr) & (kofs + c * DSA_TK < pos))

        def pos_step(b, r):
            cand = r | jnp.left_shift(jnp.int32(1), (SEQ.bit_length() - 2) - b)
            return jnp.where(count_eq_below(cand) < need, cand, r)
        r = lax.fori_loop(0, SEQ.bit_length() - 1, pos_step, jnp.zeros((1, DSA_TQ), i32))
        r = jnp.where(n_ge > TOPK, r, jnp.int32(SEQ))

        def drop(c, carry):
            sc = sc_scr[c]
            sc_scr[c] = jnp.where((sc == thr) & (kofs + c * DSA_TK > r), -jnp.inf, sc)
            return carry
        lax.fori_loop(0, nk, drop, 0)

    acc_scr[...] = jnp.zeros(acc_scr.shape, f32)
    c1 = HEAD_DIM ** -0.5 * LOG2E
    ones = jnp.ones((DSA_ONES, DSA_TK), bf16)

    for h in range(ATT_HEADS):
        qa_scr[h, :, :HEAD_DIM] = q_ref[:, h * HEAD_DIM:(h + 1) * HEAD_DIM]
        qa_scr[h, :, HEAD_DIM:] = jnp.broadcast_to(qf_ref[h, 0:1, :], (DSA_TQ, LANES))

    def attn_chunk(c, ms):
        r0 = pl.multiple_of(c * DSA_TK, DSA_TK)
        madd = jnp.where(sc_scr[c] >= thr, 0.0, -jnp.inf)
        kf = kf_ref[pl.ds(r0, DSA_TK), :]

        def qk(h):
            hd = slice(h * HEAD_DIM, (h + 1) * HEAD_DIM)
            k_aug = jnp.concatenate([k_ref[pl.ds(r0, DSA_TK), hd], kf], axis=1)
            return lax.dot_general(k_aug, qa_scr[h], contract_last, preferred_element_type=f32)

        def pv(h, alpha, p):
            hd = slice(h * HEAD_DIM, (h + 1) * HEAD_DIM)
            vt1 = jnp.concatenate([vt_ref[c, hd, :], ones], axis=0)
            acc_scr[h] = alpha * acc_scr[h] + jnp.dot(vt1, p, preferred_element_type=f32)

        new_ms = []
        s_next = qk(0)
        pending = None
        for h in range(ATT_HEADS):
            s = s_next
            if h + 1 < ATT_HEADS:
                s_next = qk(h + 1)
            if pending is not None:
                pv(*pending)
            t = s * c1 + madd
            m_old = ms[h]
            m_new = jnp.maximum(m_old, jnp.max(t, axis=0, keepdims=True))
            alpha = jnp.exp2(m_old - m_new)
            p = jnp.exp2(t - m_new).astype(bf16)
            pending = (h, alpha, p)
            new_ms.append(m_new)
        pv(*pending)
        return tuple(new_ms)

    m0 = jnp.full((1, DSA_TQ), -1e30, f32)
    lax.fori_loop(0, nk, attn_chunk, (m0,) * ATT_HEADS)

    for h in range(ATT_HEADS):
        o_ref[h * HEAD_DIM:(h + 1) * HEAD_DIM, :] = (
            acc_scr[h, :HEAD_DIM, :] / acc_scr[h, HEAD_DIM:HEAD_DIM + 1, :]).astype(o_ref.dtype)


def _alibi_features():
    assert ATT_HEADS == 8 and HEAD_DIM == 128 and SEQ <= 64 * 64
    pieces, rest = [], math.sqrt(2.0)
    for _ in range(7):
        p = float(np.asarray(rest, np.float32).astype(jnp.bfloat16).astype(np.float64))
        pieces.append(p)
        rest -= p
    qf = np.zeros((ATT_HEADS, BF16_ROWS, LANES), np.float32)
    for h in range(ATT_HEADS):
        for i, p in enumerate(pieces):
            qf[h, :, 2 * i] = qf[h, :, 2 * i + 1] = p * 2.0 ** (2 - h)
    pos = np.arange(SEQ)
    kf = np.zeros((SEQ, LANES), np.float32)
    for i in range(len(pieces)):
        kf[:, 2 * i] = pos - pos % 64
        kf[:, 2 * i + 1] = pos % 64
    return jnp.asarray(kf, bf16), jnp.asarray(qf, bf16)


def _dsa(za, vt, kk, wt_idx):
    nq = SEQ // DSA_TQ
    kf, qf = _alibi_features()
    once = pl.Buffered(1)
    return pl.pallas_call(
        _dsa_kernel,
        grid=(BATCH, nq),
        in_specs=[
            pl.BlockSpec((DSA_TQ, ATT_WIDTH), lambda b, i: (b * nq + i, 0)),
            pl.BlockSpec((SEQ, ATT_WIDTH), lambda b, i: (b, 1), pipeline_mode=once),
            pl.BlockSpec((None, DSA_NCH, ATT_WIDTH, DSA_TK), lambda b, i: (b, 0, 0, 0),
                         pipeline_mode=once),
            pl.BlockSpec((DSA_TQ, IDX_HEADS * IDX_DIM), lambda b, i: (b * nq + i, 3)),
            pl.BlockSpec((2, SEQ, LANES), lambda b, i: (0, b, 0), pipeline_mode=once),
            pl.BlockSpec((IDX_HEADS, DSA_TQ), lambda b, i: (0, b * nq + i)),
            pl.BlockSpec((SEQ, LANES), lambda b, i: (0, 0), pipeline_mode=once),
            pl.BlockSpec((ATT_HEADS, BF16_ROWS, LANES), lambda b, i: (0, 0, 0)),
        ],
        out_specs=pl.BlockSpec((None, ATT_WIDTH, DSA_TQ), lambda b, i: (b, 0, i)),
        out_shape=jax.ShapeDtypeStruct((BATCH, ATT_WIDTH, SEQ), bf16),
        scratch_shapes=[
            pltpu.VMEM((DSA_NCH, DSA_TK, DSA_TQ), f32),
            pltpu.VMEM((DSA_NCH, DSA_TK, DSA_TQ), bf16),
            pltpu.VMEM((ATT_HEADS, HEAD_DIM + DSA_ONES, DSA_TQ), f32),
            pltpu.VMEM((ATT_HEADS, DSA_TQ, HEAD_DIM + LANES), bf16),
        ],
        compiler_params=_params("arbitrary", "arbitrary"),
        name="dsa",
    )(za, za, vt, za, kk, wt_idx, kf, qf)


def _pool_kernel(u_ref, wp_ref, ps_ref, o_ref):
    row = lax.broadcasted_iota(i32, (SEQ, POOL_GROUP), 0)
    for g, win in enumerate(POOL_WINDOWS):
        x = u_ref[:, g * POOL_GROUP:(g + 1) * POOL_GROUP]
        s = x
        k = 1
        while k < win:
            s = s + jnp.where(row >= k, pltpu.roll(s, k, axis=0), 0.0)
            k *= 2
        cnt = jnp.minimum(row + 1, win).astype(f32)
        pooled = (s / cnt - x).astype(bf16)
        mixed = jnp.dot(pooled, wp_ref[g], preferred_element_type=f32)
        o_ref[:, g * POOL_GROUP:(g + 1) * POOL_GROUP] = (
            mixed * ps_ref[:, g * POOL_GROUP:(g + 1) * POOL_GROUP]).astype(o_ref.dtype)


def _pool(zb, w_pool, pool_scale):
    return pl.pallas_call(
        _pool_kernel,
        grid=(BATCH,),
        in_specs=[
            pl.BlockSpec((SEQ, POOL_WIDTH), lambda b: (b, 0)),
            pl.BlockSpec((N_POOL, POOL_GROUP, POOL_GROUP), lambda b: (0, 0, 0)),
            pl.BlockSpec((1, POOL_WIDTH), lambda b: (0, 0)),
        ],
        out_specs=pl.BlockSpec((SEQ, POOL_WIDTH), lambda b: (b, 0)),
        out_shape=jax.ShapeDtypeStruct((TOKENS, POOL_WIDTH), bf16),
        compiler_params=_params("parallel"),
        name="pool",
    )(zb, w_pool, pool_scale)


MEM_TQ = 512


def _mem_attn_kernel(q_ref, k_ref, v_ref, o_ref):
    for h in range(MEM_HEADS):
        sl = slice(h * HEAD_DIM, (h + 1) * HEAD_DIM)
        s = lax.dot_general(q_ref[:, sl].astype(bf16), k_ref[:, sl], (((1,), (1,)), ((), ())),
                            preferred_element_type=f32) * (HEAD_DIM ** -0.5)
        p = jnp.exp(s - jnp.max(s, axis=1, keepdims=True))
        l = jnp.sum(p, axis=1, keepdims=True)
        o = jnp.dot(p.astype(bf16), v_ref[:, sl], preferred_element_type=f32)
        o_ref[:, sl] = (o / l).astype(o_ref.dtype)


def _mem_attn(zb, qcol, kv):
    nq = SEQ // MEM_TQ
    return pl.pallas_call(
        _mem_attn_kernel,
        grid=(BATCH, nq),
        in_specs=[
            pl.BlockSpec((MEM_TQ, MEM_WIDTH), lambda b, i: (b * nq + i, qcol)),
            pl.BlockSpec((MEM_LEN, MEM_WIDTH), lambda b, i: (b, 0)),
            pl.BlockSpec((MEM_LEN, MEM_WIDTH), lambda b, i: (b, 1)),
        ],
        out_specs=pl.BlockSpec((MEM_TQ, MEM_WIDTH), lambda b, i: (b * nq + i, 0)),
        out_shape=jax.ShapeDtypeStruct((TOKENS, MEM_WIDTH), bf16),
        compiler_params=_params("parallel", "parallel"),
        name="mem_attn",
    )(zb, kv, kv)


CMB_TM = 512
CMB_TC = 512


def _combine_ln_kernel(h_ref, a_ref, p_ref, m_ref, wg0_ref, wg1_ref, wg2_ref,
                       bg0_ref, bg1_ref, bg2_ref, wa_ref, wp_ref, wm_ref, wo_ref,
                       g_ref, b_ref, o_ref, hb_ref, acc_ref):
    i = pl.program_id(0)
    j = pl.program_id(1)
    n_tiles = pl.num_programs(0) - 1
    slot = i % 2
    ln_rows = CMB_TM // (D_MODEL // CMB_TC)

    @pl.when((i == 0) & (j == 0))
    def _():
        acc_ref[1] = jnp.zeros(acc_ref.shape[1:], f32)

    @pl.when((i < n_tiles) & (j == 0))
    def _():
        h = h_ref[...]
        hb_ref[...] = h.astype(bf16)
        acc_ref[slot] = ALPHA * h

    def layer_norm_slab():
        rows = pl.ds(pl.multiple_of(j * ln_rows, ln_rows), ln_rows)
        o_ref[rows, :] = _layer_norm(acc_ref[1 - slot, rows, :], g_ref[...], b_ref[...])

    def matmuls():
        hb = hb_ref[...]

        def gate(wg_ref, bg_ref):
            logits = lax.dot_general(hb, wg_ref[...], _NT, preferred_element_type=f32)
            return jax.nn.sigmoid(logits + bg_ref[...])

        y = gate(wg0_ref, bg0_ref) * jnp.dot(a_ref[...], wa_ref[...], preferred_element_type=f32)
        y = y + gate(wg1_ref, bg1_ref) * jnp.dot(p_ref[...], wp_ref[...], preferred_element_type=f32)
        y = y + gate(wg2_ref, bg2_ref) * jnp.dot(m_ref[...], wm_ref[...], preferred_element_type=f32)
        acc_ref[slot] += jnp.dot(y.astype(bf16), wo_ref[...], preferred_element_type=f32)

    @pl.when(i < n_tiles)
    def _():
        layer_norm_slab()
        matmuls()

    @pl.when(i == n_tiles)
    def _():
        layer_norm_slab()


def _combine_ln(h, a, p, m, wt_gate, row_gate, b_gate, w_a, w_p, w_m, w_out, g, b):
    nc = D_MODEL // CMB_TC
    gate_rows = pl.Element(CMB_TC)
    all_cols = pl.Element(D_MODEL)
    assert row_gate % BF16_ROWS == 0

    def gate_row(blk):
        return pl.multiple_of(row_gate + blk * CMB_TC, BF16_ROWS)

    ni = TOKENS // CMB_TM

    def cj(i, j):
        return jnp.where(i < ni, j, nc - 1)

    row = lambda i, j: (jnp.minimum(i, ni - 1), 0)
    col = lambda i, j: (0, cj(i, j))
    return pl.pallas_call(
        _combine_ln_kernel,
        grid=(ni + 1, nc),
        in_specs=[
            pl.BlockSpec((CMB_TM, D_MODEL), row),
            pl.BlockSpec((CMB_TM, ATT_WIDTH), row),
            pl.BlockSpec((CMB_TM, POOL_WIDTH), row),
            pl.BlockSpec((CMB_TM, MEM_WIDTH), row),
            pl.BlockSpec((gate_rows, all_cols), lambda i, j: (gate_row(cj(i, j)), 0)),
            pl.BlockSpec((gate_rows, all_cols), lambda i, j: (gate_row(cj(i, j) + nc), 0)),
            pl.BlockSpec((gate_rows, all_cols), lambda i, j: (gate_row(cj(i, j) + 2 * nc), 0)),
            pl.BlockSpec((1, CMB_TC), lambda i, j: (0, cj(i, j))),
            pl.BlockSpec((1, CMB_TC), lambda i, j: (0, cj(i, j) + nc)),
            pl.BlockSpec((1, CMB_TC), lambda i, j: (0, cj(i, j) + 2 * nc)),
            pl.BlockSpec((ATT_WIDTH, CMB_TC), col),
            pl.BlockSpec((POOL_WIDTH, CMB_TC), col),
            pl.BlockSpec((MEM_WIDTH, CMB_TC), col),
            pl.BlockSpec((CMB_TC, D_MODEL), lambda i, j: (cj(i, j), 0)),
            pl.BlockSpec((1, D_MODEL), lambda i, j: (0, 0)),
            pl.BlockSpec((1, D_MODEL), lambda i, j: (0, 0)),
        ],
        out_specs=pl.BlockSpec((CMB_TM, D_MODEL), lambda i, j: (jnp.maximum(i - 1, 0), 0)),
        out_shape=jax.ShapeDtypeStruct((TOKENS, D_MODEL), f32),
        scratch_shapes=[pltpu.VMEM((CMB_TM, D_MODEL), bf16),
                        pltpu.VMEM((2, CMB_TM, D_MODEL), f32)],
        compiler_params=_params("arbitrary", "arbitrary"),
        name="combine_ln",
    )(h, a, p, m, wt_gate, wt_gate, wt_gate, b_gate, b_gate, b_gate, w_a, w_p, w_m, w_out, g, b)


def kernel(x, mem, w_ffn1_up, w_ffn1_down, ln1_g, ln1_b, w_in, b_gate, w_mem_kv, w_pool,
           pool_scale, w_br_att, w_br_pool, w_br_mem, w_out, ln2_g, ln2_b, w_ffn2_up,
           w_ffn2_down, ln3_g, ln3_b):
    h = x.reshape(TOKENS, D_MODEL)
    memf = mem.reshape(BATCH * MEM_LEN, D_MODEL)
    for l in range(DEPTH):
        c_qi = 3 * ATT_WIDTH + IDX_HEADS * IDX_DIM
        c_wi = c_qi + IDX_DIM + IDX_HEADS
        c_qm = c_wi + POOL_WIDTH + MEM_WIDTH

        z0, w1_a, w1_u, w1_down = _ffn_head(h, w_ffn1_up[l], w_ffn1_down[l])
        later = (w_in[l].T, w_ffn2_up[l], w_ffn2_down[l], w_mem_kv[l], w_br_att[l], w_br_pool[l],
                 w_br_mem[l], w_out[l])
        h, (wt, w2_up, w2_down, wb_mem_kv, wb_att, wb_pool, wb_mem, wb_out) = _ffn_ln(
            h, w1_a, w1_u, w1_down, ln1_g[l][None], ln1_b[l][None], z0=z0,
            casts=tuple((w, None) for w in later))

        za, v = _matmul(h, wt, bf16, 1024, ATT_WIDTH, "in_proj_a", w_is_transposed=True, n=c_qi,
                        also_block=2)
        zb = _in_proj_b(h, wt, c_wi, c_qi)
        kv = _matmul(memf, wb_mem_kv, bf16, BATCH * MEM_LEN, 512, "mem_kv")

        c_ki = POOL_WIDTH + MEM_WIDTH
        ki = zb[:, c_ki:c_ki + IDX_DIM].astype(bf16)
        zk = jnp.zeros_like(ki)
        kk = jnp.stack([jnp.concatenate([ki, zk], axis=1), jnp.concatenate([zk, ki], axis=1)])
        wt_idx = zb[:, c_ki + IDX_DIM:c_ki + IDX_DIM + IDX_HEADS].T
        vt = v.reshape(BATCH, DSA_NCH, DSA_TK, ATT_WIDTH).transpose(0, 1, 3, 2)

        a = _dsa(za, vt, kk, wt_idx).transpose(0, 2, 1).reshape(TOKENS, ATT_WIDTH)
        p = _pool(zb, w_pool[l].astype(bf16), pool_scale[l][None])
        m = _mem_attn(zb, POOL_WIDTH // MEM_WIDTH, kv)

        h = _combine_ln(h, a, p, m, wt, c_qm, b_gate[l][None], wb_att, wb_pool, wb_mem, wb_out,
                        ln2_g[l][None], ln2_b[l][None])

        h, _ = _ffn_ln(h, w2_up, w2_up, w2_down, ln3_g[l][None], ln3_b[l][None])
    return h.reshape(BATCH, SEQ, D_MODEL)
```

```python
import functools
import math

import jax
import jax.numpy as jnp
import numpy as np
from jax import lax
from jax.experimental import pallas as pl
from jax.experimental.pallas import tpu as pltpu

f32 = jnp.float32
bf16 = jnp.bfloat16
i32 = jnp.int32

D_MODEL = 2048
BATCH = 2
SEQ = 4096
DEPTH = 1
MEM_LEN = 256
ATT_HEADS = 8
HEAD_DIM = 128
ATT_WIDTH = ATT_HEADS * HEAD_DIM
IDX_HEADS = 16
IDX_DIM = 64
TOPK = min(256, SEQ // 4)
POOL_WINDOWS = (2, 4, 8, 16)
N_POOL = len(POOL_WINDOWS)
POOL_GROUP = 128
POOL_WIDTH = N_POOL * POOL_GROUP
MEM_HEADS = 4
MEM_WIDTH = MEM_HEADS * HEAD_DIM
N_BRANCH = 3
D_FF = 5632
ALPHA = (2 * DEPTH) ** 0.25
LN_EPS = 1e-5
TOKENS = BATCH * SEQ

LANES = 128
BF16_ROWS = 16
VMEM_LIMIT = 60 * 1024 * 1024

INT_MIN = -(2 ** 31)
LOG2E = math.log2(math.e)


def _params(*sem):
    return pltpu.CompilerParams(dimension_semantics=sem, vmem_limit_bytes=VMEM_LIMIT)


def _layer_norm(y, g, b):
    mu = jnp.mean(y, axis=-1, keepdims=True)
    d = y - mu
    var = jnp.mean(d * d, axis=-1, keepdims=True)
    return d * lax.rsqrt(var + LN_EPS) * g + b


FFN_TM = 512
FFN_TF = 512


FFN_LN_STEPS = 8
FFN_LN_ROWS = FFN_TM // FFN_LN_STEPS


def _ffn_ln_kernel(x_ref, wa_ref, wu_ref, wd_ref, g_ref, b_ref, *rest, n_cast, has_z0):
    if has_z0:
        z0_ref, rest = rest[0], rest[1:]
    cast_in = rest[:n_cast]
    o_ref = rest[n_cast]
    cast_out = rest[n_cast + 1:2 * n_cast + 1]
    xb_ref, acc_ref = rest[2 * n_cast + 1:]
    i = pl.program_id(0)
    j = pl.program_id(1)
    n_tiles = pl.num_programs(0) - 1
    slot = (i + 1) % 2 if has_z0 else i % 2
    computed = i < n_tiles

    for src, dst in zip(cast_in, cast_out):
        dst[...] = src[...].astype(dst.dtype)

    @pl.when((i == 0) & (j == 0))
    def _():
        if has_z0:
            acc_ref[0] = z0_ref[...]
        else:
            acc_ref[1] = jnp.zeros(acc_ref.shape[1:], f32)

    @pl.when(computed & (j == 0))
    def _():
        x = x_ref[...]
        xb_ref[...] = x.astype(bf16)
        acc_ref[slot] = (2.0 * ALPHA) * x

    def matmuls():
        xb = xb_ref[...]
        a = jnp.dot(xb, wa_ref[...], preferred_element_type=f32)
        u = jnp.dot(xb, wu_ref[...], preferred_element_type=f32)
        act = (a * jax.nn.sigmoid(a) * u).astype(bf16)
        acc_ref[slot] += jnp.dot(act, wd_ref[...], preferred_element_type=f32)

    def layer_norm_slab():
        rows = pl.ds(pl.multiple_of(j * FFN_LN_ROWS, FFN_LN_ROWS), FFN_LN_ROWS)
        z = acc_ref[1 - slot, rows, :]
        mu = jnp.mean(z, axis=-1, keepdims=True)
        d = z - mu
        var = jnp.mean(d * d, axis=-1, keepdims=True)
        o_ref[rows, :] = d * lax.rsqrt(var + 4.0 * LN_EPS) * g_ref[...] + b_ref[...]

    @pl.when(computed & (j < FFN_LN_STEPS))
    def _():
        layer_norm_slab()
        matmuls()

    @pl.when(computed & (j >= FFN_LN_STEPS))
    def _():
        matmuls()

    @pl.when((i == n_tiles) & (j < FFN_LN_STEPS))
    def _():
        layer_norm_slab()


def _cast_block_spec(shape, ni, nj):
    r, c = shape
    br = -(-r // (ni * nj))
    br = -(-br // BF16_ROWS) * BF16_ROWS
    last_block = -(-r // br) - 1
    return pl.BlockSpec((br, c), lambda i, j: (
        jnp.minimum(jnp.minimum(i, ni - 1) * nj + jnp.where(i < ni, j, nj - 1), last_block), 0))


FFN_HEAD_TF = 256


def _ffn_head_kernel(x_ref, wa_ref, wu_ref, wd_ref, z_ref, wa_out, wu_out, wd_out, xb_ref):
    j = pl.program_id(0)

    @pl.when(j == 0)
    def _():
        x = x_ref[...]
        xb_ref[...] = x.astype(bf16)
        z_ref[...] = (2.0 * ALPHA) * x

    wa = wa_ref[...].astype(bf16)
    wu = wu_ref[...].astype(bf16)
    wd = wd_ref[...].astype(bf16)
    wa_out[...] = wa
    wu_out[...] = wu
    wd_out[...] = wd
    xb = xb_ref[...]
    a = jnp.dot(xb, wa, preferred_element_type=f32)
    u = jnp.dot(xb, wu, preferred_element_type=f32)
    act = (a * jax.nn.sigmoid(a) * u).astype(bf16)
    z_ref[...] += jnp.dot(act, wd, preferred_element_type=f32)


def _ffn_head(x, w_up, w_down):
    nf = D_FF // FFN_HEAD_TF
    return pl.pallas_call(
        _ffn_head_kernel,
        grid=(nf,),
        in_specs=[
            pl.BlockSpec((FFN_TM, D_MODEL), lambda j: (0, 0)),
            pl.BlockSpec((D_MODEL, FFN_HEAD_TF), lambda j: (0, j)),
            pl.BlockSpec((D_MODEL, FFN_HEAD_TF), lambda j: (0, j + nf)),
            pl.BlockSpec((FFN_HEAD_TF, D_MODEL), lambda j: (j, 0)),
        ],
        out_specs=[
            pl.BlockSpec((FFN_TM, D_MODEL), lambda j: (0, 0)),
            pl.BlockSpec((D_MODEL, FFN_HEAD_TF), lambda j: (0, j)),
            pl.BlockSpec((D_MODEL, FFN_HEAD_TF), lambda j: (0, j)),
            pl.BlockSpec((FFN_HEAD_TF, D_MODEL), lambda j: (j, 0)),
        ],
        out_shape=[
            jax.ShapeDtypeStruct((FFN_TM, D_MODEL), f32),
            jax.ShapeDtypeStruct((D_MODEL, D_FF), bf16),
            jax.ShapeDtypeStruct((D_MODEL, D_FF), bf16),
            jax.ShapeDtypeStruct((D_FF, D_MODEL), bf16),
        ],
        scratch_shapes=[pltpu.VMEM((FFN_TM, D_MODEL), bf16)],
        compiler_params=_params("arbitrary"),
        name="ffn_head",
    )(x, w_up, w_up, w_down)


def _ffn_ln(x, w_a, w_u, w_down, g, b, casts=(), z0=None):
    n_tok_tiles = TOKENS // FFN_TM
    first = 0 if z0 is None else 1
    ni = n_tok_tiles - first
    nf = D_FF // FFN_TF
    assert FFN_LN_STEPS <= nf
    u0 = (w_u.shape[1] - D_FF) // FFN_TF
    cast_specs = [_cast_block_spec(w.shape, ni, nf) for w in casts]
    z0_specs = [] if z0 is None else [pl.BlockSpec((FFN_TM, D_MODEL), lambda i, j: (0, 0))]
    z0_args = [] if z0 is None else [z0]

    def wj(i, j):
        return jnp.where(i < ni, j, nf - 1)

    outs = pl.pallas_call(
        functools.partial(_ffn_ln_kernel, n_cast=len(casts), has_z0=z0 is not None),
        grid=(ni + 1, nf),
        in_specs=[
            pl.BlockSpec((FFN_TM, D_MODEL), lambda i, j: (jnp.minimum(i + first, n_tok_tiles - 1), 0)),
            pl.BlockSpec((D_MODEL, FFN_TF), lambda i, j: (0, wj(i, j))),
            pl.BlockSpec((D_MODEL, FFN_TF), lambda i, j: (0, wj(i, j) + u0)),
            pl.BlockSpec((FFN_TF, D_MODEL), lambda i, j: (wj(i, j), 0)),
            pl.BlockSpec((1, D_MODEL), lambda i, j: (0, 0)),
            pl.BlockSpec((1, D_MODEL), lambda i, j: (0, 0)),
        ] + z0_specs + cast_specs,
        out_specs=[pl.BlockSpec((FFN_TM, D_MODEL), lambda i, j: (jnp.maximum(i + first - 1, 0), 0))]
        + cast_specs,
        out_shape=[jax.ShapeDtypeStruct((TOKENS, D_MODEL), f32)]
        + [jax.ShapeDtypeStruct(w.shape, bf16) for w in casts],
        scratch_shapes=[pltpu.VMEM((FFN_TM, D_MODEL), bf16),
                        pltpu.VMEM((2, FFN_TM, D_MODEL), f32)],
        compiler_params=_params("arbitrary", "arbitrary"),
        name="ffn_ln",
    )(x, w_a, w_u, w_down, g, b, *z0_args, *casts)
    return outs[0], outs[1:]


_NT = (((1,), (1,)), ((), ()))


def _matmul_kernel(x_ref, w_ref, o_ref, *rest, w_is_transposed, also_block):
    xb_ref = rest[-1]

    @pl.when(pl.program_id(1) == 0)
    def _():
        xb_ref[...] = x_ref[...].astype(bf16)

    if w_is_transposed:
        y = lax.dot_general(xb_ref[...], w_ref[...], _NT, preferred_element_type=f32)
    else:
        y = jnp.dot(xb_ref[...], w_ref[...], preferred_element_type=f32)
    o_ref[...] = y.astype(o_ref.dtype)

    if also_block is not None:
        @pl.when(pl.program_id(1) == also_block)
        def _():
            rest[0][...] = y.astype(rest[0].dtype)


def _matmul(x, w, out_dtype, tm, tn, name, w_is_transposed=False, n=None, also_block=None):
    m, k = x.shape
    if n is None:
        n = w.shape[0] if w_is_transposed else w.shape[1]
    assert n % tn == 0 and m % tm == 0
    if w_is_transposed:
        w_spec = pl.BlockSpec((tn, k), lambda i, j: (j, 0))
    else:
        w_spec = pl.BlockSpec((k, tn), lambda i, j: (0, j))
    out_specs = [pl.BlockSpec((tm, tn), lambda i, j: (i, j))]
    out_shape = [jax.ShapeDtypeStruct((m, n), out_dtype)]
    if also_block is not None:
        out_specs.append(pl.BlockSpec((tm, tn), lambda i, j: (i, 0)))
        out_shape.append(jax.ShapeDtypeStruct((m, tn), out_dtype))
    outs = pl.pallas_call(
        functools.partial(_matmul_kernel, w_is_transposed=w_is_transposed, also_block=also_block),
        grid=(m // tm, n // tn),
        in_specs=[pl.BlockSpec((tm, k), lambda i, j: (i, 0)), w_spec],
        out_specs=out_specs,
        out_shape=out_shape,
        scratch_shapes=[pltpu.VMEM((tm, k), bf16)],
        compiler_params=_params("arbitrary", "arbitrary"),
        name=name,
    )(x, w)
    return outs[0] if also_block is None else outs


PB_TM = 512
PB_N = POOL_WIDTH + MEM_WIDTH + LANES


def _in_proj_b_kernel(x_ref, wum_ref, wk_ref, o_ref):
    xb = x_ref[...].astype(bf16)
    n_um = POOL_WIDTH + MEM_WIDTH
    o_ref[:, :n_um] = lax.dot_general(xb, wum_ref[...], _NT, preferred_element_type=f32)
    o_ref[:, n_um:] = lax.dot_general(xb, wk_ref[...], _NT, preferred_element_type=f32)


def _in_proj_b(x, wt, row_um, row_kiwi):
    n_um = POOL_WIDTH + MEM_WIDTH
    return pl.pallas_call(
        _in_proj_b_kernel,
        grid=(TOKENS // PB_TM,),
        in_specs=[
            pl.BlockSpec((PB_TM, D_MODEL), lambda i: (i, 0)),
            pl.BlockSpec((pl.Element(n_um), pl.Element(D_MODEL)), lambda i: (row_um, 0)),
            pl.BlockSpec((pl.Element(LANES), pl.Element(D_MODEL)), lambda i: (row_kiwi, 0)),
        ],
        out_specs=pl.BlockSpec((PB_TM, PB_N), lambda i: (i, 0)),
        out_shape=jax.ShapeDtypeStruct((TOKENS, PB_N), f32),
        compiler_params=_params("parallel"),
        name="in_proj_b",
    )(x, wt, wt)


DSA_TQ = 512
DSA_TK = 512
DSA_NCH = SEQ // DSA_TK
DSA_ONES = 16
KEY_NEG_FLT_MAX = -(2 ** 31) + (1 << 23)


def _key_to_f32(key):
    return lax.bitcast_convert_type(key ^ ((key >> 31) & jnp.int32(0x7FFFFFFF)), f32)


def _dsa_kernel(q_ref, k_ref, vt_ref, qi_ref, kk_ref, wt_ref, kf_ref, qf_ref, o_ref,
                sc_scr, sch_scr, acc_scr, qa_scr):
    i = pl.program_id(1)
    q0 = i * DSA_TQ
    nk = (q0 + DSA_TQ - 1) // DSA_TK + 1

    qpos = q0 + lax.broadcasted_iota(i32, (DSA_TK, DSA_TQ), 1)
    kofs = lax.broadcasted_iota(i32, (DSA_TK, DSA_TQ), 0)
    contract_last = (((1,), (1,)), ((), ()))

    def rows8(x):
        return x.reshape(DSA_TK // 8, 8, DSA_TQ)

    wt = wt_ref[...] * (IDX_DIM ** -0.5 * IDX_HEADS ** -0.5)

    def score_chunk(c, carry):
        r0 = pl.multiple_of(c * DSA_TK, DSA_TK)
        kk0 = kk_ref[0, pl.ds(r0, DSA_TK), :]
        kk1 = kk_ref[1, pl.ds(r0, DSA_TK), :]
        acc = jnp.zeros((DSA_TK, DSA_TQ), f32)
        for p in range(IDX_HEADS // 2):
            slab = qi_ref[:, p * LANES:(p + 1) * LANES]
            l0 = lax.dot_general(kk0, slab, contract_last, preferred_element_type=f32)
            l1 = lax.dot_general(kk1, slab, contract_last, preferred_element_type=f32)
            acc = acc + (jnp.maximum(l0, 0.0) * wt[2 * p:2 * p + 1]
                         + jnp.maximum(l1, 0.0) * wt[2 * p + 1:2 * p + 2])
        sc = jnp.where(kofs + r0 <= qpos, acc, -jnp.inf)
        sc_scr[c] = sc
        sch_scr[c] = sc.astype(bf16)
        return carry

    lax.fori_loop(0, nk, score_chunk, 0)

    def count(pred):
        n_acc = 8

        def body(c, accs):
            m = rows8(pred(c))
            accs = list(accs)
            for r in range(DSA_TK // 8):
                accs[r % n_acc] = jnp.where(m[r], accs[r % n_acc] + 1, accs[r % n_acc])
            return tuple(accs)
        accs = lax.fori_loop(0, nk, body, (jnp.zeros((8, DSA_TQ), i32),) * n_acc)
        acc = functools.reduce(lambda a, b: a + b, accs)
        return jnp.sum(acc, axis=0, keepdims=True)

    def count_ge(cand):
        return count(lambda c: sc_scr[c] >= cand)

    def count_ge_coarse(cand):
        n_acc = 2
        one = jnp.ones((), bf16)
        zero = jnp.zeros((), bf16)

        def body(c, accs):
            accs = list(accs)
            for r in range(DSA_TK // BF16_ROWS):
                rows = sch_scr[c, r * BF16_ROWS:(r + 1) * BF16_ROWS, :]
                accs[r % n_acc] = accs[r % n_acc] + jnp.where(rows >= cand, one, zero)
            return tuple(accs)
        accs = lax.fori_loop(0, nk, body, (jnp.zeros((BF16_ROWS, DSA_TQ), bf16),) * n_acc)
        acc = functools.reduce(lambda a, b: a + b, accs)
        return jnp.sum(acc.astype(f32), axis=0, keepdims=True)

    def coarse_cand(key):
        bits = lax.bitcast_convert_type(_key_to_f32(key), i32) & jnp.int32(-65536)
        return lax.bitcast_convert_type(bits, f32).astype(bf16)

    key = jnp.where(count_ge_coarse(jnp.zeros((1, DSA_TQ), bf16)) >= TOPK,
                    jnp.int32(0), jnp.int32(INT_MIN))

    def coarse_step(b, key):
        cand = key | jnp.left_shift(jnp.int32(1), 30 - b)
        return jnp.where(count_ge_coarse(coarse_cand(cand)) >= TOPK, cand, key)

    key = lax.fori_loop(0, 15, coarse_step, key)

    window_bits = 18
    lo = jnp.maximum(key, jnp.int32(INT_MIN + (1 << 17))) - jnp.int32((1 << 15) + 2)

    def fine_step(b, off):
        cand = off | jnp.left_shift(jnp.int32(1), (window_bits - 1) - b)
        return jnp.where(count_ge(_key_to_f32(lo + cand)) >= TOPK, cand, off)

    key = lo + lax.fori_loop(0, window_bits, fine_step, jnp.zeros((1, DSA_TQ), i32))
    thr = _key_to_f32(jnp.maximum(key, jnp.int32(KEY_NEG_FLT_MAX)))

    n_ge = count_ge(thr)

    @pl.when(jnp.max(n_ge) > TOPK)
    def _():
        need = TOPK - count(lambda c: sc_scr[c] > thr)

        def count_eq_below(pos):
            return count(lambda c: (sc_scr[c] == thr) & (kofs + c * DSA_TK < pos))

        def pos_step(b, r):
            cand = r | jnp.left_shift(jnp.int32(1), (SEQ.bit_length() - 2) - b)
            return jnp.where(count_eq_below(cand) < need, cand, r)
        r = lax.fori_loop(0, SEQ.bit_length() - 1, pos_step, jnp.zeros((1, DSA_TQ), i32))
        r = jnp.where(n_ge > TOPK, r, jnp.int32(SEQ))

        def drop(c, carry):
            sc = sc_scr[c]
            sc_scr[c] = jnp.where((sc == thr) & (kofs + c * DSA_TK > r), -jnp.inf, sc)
            return carry
        lax.fori_loop(0, nk, drop, 0)

    acc_scr[...] = jnp.zeros(acc_scr.shape, f32)
    c1 = HEAD_DIM ** -0.5 * LOG2E
    ones = jnp.ones((DSA_ONES, DSA_TK), bf16)

    for h in range(ATT_HEADS):
        qa_scr[h, :, :HEAD_DIM] = q_ref[:, h * HEAD_DIM:(h + 1) * HEAD_DIM]
        qa_scr[h, :, HEAD_DIM:] = jnp.broadcast_to(qf_ref[h, 0:1, :], (DSA_TQ, LANES))

    def attn_chunk(c, ms):
        r0 = pl.multiple_of(c * DSA_TK, DSA_TK)
        madd = jnp.where(sc_scr[c] >= thr, 0.0, -jnp.inf)
        kf = kf_ref[pl.ds(r0, DSA_TK), :]

        def qk(h):
            hd = slice(h * HEAD_DIM, (h + 1) * HEAD_DIM)
            k_aug = jnp.concatenate([k_ref[pl.ds(r0, DSA_TK), hd], kf], axis=1)
            return lax.dot_general(k_aug, qa_scr[h], contract_last, preferred_element_type=f32)

        def pv(h, alpha, p):
            hd = slice(h * HEAD_DIM, (h + 1) * HEAD_DIM)
            vt1 = jnp.concatenate([vt_ref[c, hd, :], ones], axis=0)
            acc_scr[h] = alpha * acc_scr[h] + jnp.dot(vt1, p, preferred_element_type=f32)

        new_ms = []
        s_next = qk(0)
        pending = None
        for h in range(ATT_HEADS):
            s = s_next
            if h + 1 < ATT_HEADS:
                s_next = qk(h + 1)
            if pending is not None:
                pv(*pending)
            t = s * c1 + madd
            m_old = ms[h]
            m_new = jnp.maximum(m_old, jnp.max(t, axis=0, keepdims=True))
            alpha = jnp.exp2(m_old - m_new)
            p = jnp.exp2(t - m_new).astype(bf16)
            pending = (h, alpha, p)
            new_ms.append(m_new)
        pv(*pending)
        return tuple(new_ms)

    m0 = jnp.full((1, DSA_TQ), -1e30, f32)
    lax.fori_loop(0, nk, attn_chunk, (m0,) * ATT_HEADS)

    for h in range(ATT_HEADS):
        o_ref[h * HEAD_DIM:(h + 1) * HEAD_DIM, :] = (
            acc_scr[h, :HEAD_DIM, :] / acc_scr[h, HEAD_DIM:HEAD_DIM + 1, :]).astype(o_ref.dtype)


def _alibi_features():
    assert ATT_HEADS == 8 and HEAD_DIM == 128 and SEQ <= 64 * 64
    pieces, rest = [], math.sqrt(2.0)
    for _ in range(7):
        p = float(np.asarray(rest, np.float32).astype(jnp.bfloat16).astype(np.float64))
        pieces.append(p)
        rest -= p
    qf = np.zeros((ATT_HEADS, BF16_ROWS, LANES), np.float32)
    for h in range(ATT_HEADS):
        for i, p in enumerate(pieces):
            qf[h, :, 2 * i] = qf[h, :, 2 * i + 1] = p * 2.0 ** (2 - h)
    pos = np.arange(SEQ)
    kf = np.zeros((SEQ, LANES), np.float32)
    for i in range(len(pieces)):
        kf[:, 2 * i] = pos - pos % 64
        kf[:, 2 * i + 1] = pos % 64
    return jnp.asarray(kf, bf16), jnp.asarray(qf, bf16)


def _dsa(za, vt, kk, wt_idx):
    nq = SEQ // DSA_TQ
    assert SEQ // BF16_ROWS <= 256
    kf, qf = _alibi_features()
    once = pl.Buffered(1)
    return pl.pallas_call(
        _dsa_kernel,
        grid=(BATCH, nq),
        in_specs=[
            pl.BlockSpec((DSA_TQ, ATT_WIDTH), lambda b, i: (b * nq + i, 0)),
            pl.BlockSpec((SEQ, ATT_WIDTH), lambda b, i: (b, 1), pipeline_mode=once),
            pl.BlockSpec((None, DSA_NCH, ATT_WIDTH, DSA_TK), lambda b, i: (b, 0, 0, 0),
                         pipeline_mode=once),
            pl.BlockSpec((DSA_TQ, IDX_HEADS * IDX_DIM), lambda b, i: (b * nq + i, 3)),
            pl.BlockSpec((2, SEQ, LANES), lambda b, i: (0, b, 0), pipeline_mode=once),
            pl.BlockSpec((IDX_HEADS, DSA_TQ), lambda b, i: (0, b * nq + i)),
            pl.BlockSpec((SEQ, LANES), lambda b, i: (0, 0), pipeline_mode=once),
            pl.BlockSpec((ATT_HEADS, BF16_ROWS, LANES), lambda b, i: (0, 0, 0)),
        ],
        out_specs=pl.BlockSpec((None, ATT_WIDTH, DSA_TQ), lambda b, i: (b, 0, i)),
        out_shape=jax.ShapeDtypeStruct((BATCH, ATT_WIDTH, SEQ), bf16),
        scratch_shapes=[
            pltpu.VMEM((DSA_NCH, DSA_TK, DSA_TQ), f32),
            pltpu.VMEM((DSA_NCH, DSA_TK, DSA_TQ), bf16),
            pltpu.VMEM((ATT_HEADS, HEAD_DIM + DSA_ONES, DSA_TQ), f32),
            pltpu.VMEM((ATT_HEADS, DSA_TQ, HEAD_DIM + LANES), bf16),
        ],
        compiler_params=_params("arbitrary", "arbitrary"),
        name="dsa",
    )(za, za, vt, za, kk, wt_idx, kf, qf)


def _pool_kernel(u_ref, wp_ref, ps_ref, o_ref):
    row = lax.broadcasted_iota(i32, (SEQ, POOL_GROUP), 0)
    for g, win in enumerate(POOL_WINDOWS):
        x = u_ref[:, g * POOL_GROUP:(g + 1) * POOL_GROUP]
        s = x
        k = 1
        while k < win:
            s = s + jnp.where(row >= k, pltpu.roll(s, k, axis=0), 0.0)
            k *= 2
        cnt = jnp.minimum(row + 1, win).astype(f32)
        pooled = (s / cnt - x).astype(bf16)
        mixed = jnp.dot(pooled, wp_ref[g], preferred_element_type=f32)
        o_ref[:, g * POOL_GROUP:(g + 1) * POOL_GROUP] = (
            mixed * ps_ref[:, g * POOL_GROUP:(g + 1) * POOL_GROUP]).astype(o_ref.dtype)


def _pool(zb, w_pool, pool_scale):
    return pl.pallas_call(
        _pool_kernel,
        grid=(BATCH,),
        in_specs=[
            pl.BlockSpec((SEQ, POOL_WIDTH), lambda b: (b, 0)),
            pl.BlockSpec((N_POOL, POOL_GROUP, POOL_GROUP), lambda b: (0, 0, 0)),
            pl.BlockSpec((1, POOL_WIDTH), lambda b: (0, 0)),
        ],
        out_specs=pl.BlockSpec((SEQ, POOL_WIDTH), lambda b: (b, 0)),
        out_shape=jax.ShapeDtypeStruct((TOKENS, POOL_WIDTH), bf16),
        compiler_params=_params("parallel"),
        name="pool",
    )(zb, w_pool, pool_scale)


MEM_TQ = 512


def _mem_attn_kernel(q_ref, k_ref, v_ref, o_ref):
    for h in range(MEM_HEADS):
        sl = slice(h * HEAD_DIM, (h + 1) * HEAD_DIM)
        s = lax.dot_general(q_ref[:, sl].astype(bf16), k_ref[:, sl], (((1,), (1,)), ((), ())),
                            preferred_element_type=f32) * (HEAD_DIM ** -0.5)
        p = jnp.exp(s - jnp.max(s, axis=1, keepdims=True))
        l = jnp.sum(p, axis=1, keepdims=True)
        o = jnp.dot(p.astype(bf16), v_ref[:, sl], preferred_element_type=f32)
        o_ref[:, sl] = (o / l).astype(o_ref.dtype)


def _mem_attn(zb, qcol, kv):
    nq = SEQ // MEM_TQ
    return pl.pallas_call(
        _mem_attn_kernel,
        grid=(BATCH, nq),
        in_specs=[
            pl.BlockSpec((MEM_TQ, MEM_WIDTH), lambda b, i: (b * nq + i, qcol)),
            pl.BlockSpec((MEM_LEN, MEM_WIDTH), lambda b, i: (b, 0)),
            pl.BlockSpec((MEM_LEN, MEM_WIDTH), lambda b, i: (b, 1)),
        ],
        out_specs=pl.BlockSpec((MEM_TQ, MEM_WIDTH), lambda b, i: (b * nq + i, 0)),
        out_shape=jax.ShapeDtypeStruct((TOKENS, MEM_WIDTH), bf16),
        compiler_params=_params("parallel", "parallel"),
        name="mem_attn",
    )(zb, kv, kv)


CMB_TM = 512
CMB_TC = 512


def _combine_ln_kernel(h_ref, a_ref, p_ref, m_ref, wg0_ref, wg1_ref, wg2_ref,
                       bg0_ref, bg1_ref, bg2_ref, wa_ref, wp_ref, wm_ref, wo_ref,
                       g_ref, b_ref, o_ref, hb_ref, acc_ref):
    i = pl.program_id(0)
    j = pl.program_id(1)
    n_tiles = pl.num_programs(0) - 1
    slot = i % 2
    ln_rows = CMB_TM // (D_MODEL // CMB_TC)

    @pl.when((i == 0) & (j == 0))
    def _():
        acc_ref[1] = jnp.zeros(acc_ref.shape[1:], f32)

    @pl.when((i < n_tiles) & (j == 0))
    def _():
        h = h_ref[...]
        hb_ref[...] = h.astype(bf16)
        acc_ref[slot] = ALPHA * h

    def layer_norm_slab():
        rows = pl.ds(pl.multiple_of(j * ln_rows, ln_rows), ln_rows)
        o_ref[rows, :] = _layer_norm(acc_ref[1 - slot, rows, :], g_ref[...], b_ref[...])

    def matmuls():
        hb = hb_ref[...]

        def gate(wg_ref, bg_ref):
            logits = lax.dot_general(hb, wg_ref[...], _NT, preferred_element_type=f32)
            return jax.nn.sigmoid(logits + bg_ref[...])

        y = gate(wg0_ref, bg0_ref) * jnp.dot(a_ref[...], wa_ref[...], preferred_element_type=f32)
        y = y + gate(wg1_ref, bg1_ref) * jnp.dot(p_ref[...], wp_ref[...], preferred_element_type=f32)
        y = y + gate(wg2_ref, bg2_ref) * jnp.dot(m_ref[...], wm_ref[...], preferred_element_type=f32)
        acc_ref[slot] += jnp.dot(y.astype(bf16), wo_ref[...], preferred_element_type=f32)

    @pl.when(i < n_tiles)
    def _():
        layer_norm_slab()
        matmuls()

    @pl.when(i == n_tiles)
    def _():
        layer_norm_slab()


def _combine_ln(h, a, p, m, wt_gate, row_gate, b_gate, w_a, w_p, w_m, w_out, g, b):
    nc = D_MODEL // CMB_TC
    gate_rows = pl.Element(CMB_TC)
    all_cols = pl.Element(D_MODEL)
    assert row_gate % BF16_ROWS == 0

    def gate_row(blk):
        return pl.multiple_of(row_gate + blk * CMB_TC, BF16_ROWS)

    ni = TOKENS // CMB_TM

    def cj(i, j):
        return jnp.where(i < ni, j, nc - 1)

    row = lambda i, j: (jnp.minimum(i, ni - 1), 0)
    col = lambda i, j: (0, cj(i, j))
    return pl.pallas_call(
        _combine_ln_kernel,
        grid=(ni + 1, nc),
        in_specs=[
            pl.BlockSpec((CMB_TM, D_MODEL), row),
            pl.BlockSpec((CMB_TM, ATT_WIDTH), row),
            pl.BlockSpec((CMB_TM, POOL_WIDTH), row),
            pl.BlockSpec((CMB_TM, MEM_WIDTH), row),
            pl.BlockSpec((gate_rows, all_cols), lambda i, j: (gate_row(cj(i, j)), 0)),
            pl.BlockSpec((gate_rows, all_cols), lambda i, j: (gate_row(cj(i, j) + nc), 0)),
            pl.BlockSpec((gate_rows, all_cols), lambda i, j: (gate_row(cj(i, j) + 2 * nc), 0)),
            pl.BlockSpec((1, CMB_TC), lambda i, j: (0, cj(i, j))),
            pl.BlockSpec((1, CMB_TC), lambda i, j: (0, cj(i, j) + nc)),
            pl.BlockSpec((1, CMB_TC), lambda i, j: (0, cj(i, j) + 2 * nc)),
            pl.BlockSpec((ATT_WIDTH, CMB_TC), col),
            pl.BlockSpec((POOL_WIDTH, CMB_TC), col),
            pl.BlockSpec((MEM_WIDTH, CMB_TC), col),
            pl.BlockSpec((CMB_TC, D_MODEL), lambda i, j: (cj(i, j), 0)),
            pl.BlockSpec((1, D_MODEL), lambda i, j: (0, 0)),
            pl.BlockSpec((1, D_MODEL), lambda i, j: (0, 0)),
        ],
        out_specs=pl.BlockSpec((CMB_TM, D_MODEL), lambda i, j: (jnp.maximum(i - 1, 0), 0)),
        out_shape=jax.ShapeDtypeStruct((TOKENS, D_MODEL), f32),
        scratch_shapes=[pltpu.VMEM((CMB_TM, D_MODEL), bf16),
                        pltpu.VMEM((2, CMB_TM, D_MODEL), f32)],
        compiler_params=_params("arbitrary", "arbitrary"),
        name="combine_ln",
    )(h, a, p, m, wt_gate, wt_gate, wt_gate, b_gate, b_gate, b_gate, w_a, w_p, w_m, w_out, g, b)


def kernel(x, mem, w_ffn1_up, w_ffn1_down, ln1_g, ln1_b, w_in, b_gate, w_mem_kv, w_pool,
           pool_scale, w_br_att, w_br_pool, w_br_mem, w_out, ln2_g, ln2_b, w_ffn2_up,
           w_ffn2_down, ln3_g, ln3_b):
    h = x.reshape(TOKENS, D_MODEL)
    memf = mem.reshape(BATCH * MEM_LEN, D_MODEL)
    for l in range(DEPTH):
        c_qi = 3 * ATT_WIDTH + IDX_HEADS * IDX_DIM
        c_wi = c_qi + IDX_DIM + IDX_HEADS
        c_qm = c_wi + POOL_WIDTH + MEM_WIDTH

        z0, w1_a, w1_u, w1_down = _ffn_head(h, w_ffn1_up[l], w_ffn1_down[l])
        later = (w_in[l].T, w_ffn2_up[l], w_ffn2_down[l], w_mem_kv[l], w_br_att[l], w_br_pool[l],
                 w_br_mem[l], w_out[l])
        h, (wt, w2_up, w2_down, wb_mem_kv, wb_att, wb_pool, wb_mem, wb_out) = _ffn_ln(
            h, w1_a, w1_u, w1_down, ln1_g[l][None], ln1_b[l][None], z0=z0,
            casts=later)

        za, v = _matmul(h, wt, bf16, 1024, ATT_WIDTH, "in_proj_a", w_is_transposed=True, n=c_qi,
                        also_block=2)
        zb = _in_proj_b(h, wt, c_wi, c_qi)
        kv = _matmul(memf, wb_mem_kv, bf16, BATCH * MEM_LEN, 512, "mem_kv")

        c_ki = POOL_WIDTH + MEM_WIDTH
        ki = zb[:, c_ki:c_ki + IDX_DIM].astype(bf16)
        zk = jnp.zeros_like(ki)
        kk = jnp.stack([jnp.concatenate([ki, zk], axis=1), jnp.concatenate([zk, ki], axis=1)])
        wt_idx = zb[:, c_ki + IDX_DIM:c_ki + IDX_DIM + IDX_HEADS].T
        vt = v.reshape(BATCH, DSA_NCH, DSA_TK, ATT_WIDTH).transpose(0, 1, 3, 2)

        a = _dsa(za, vt, kk, wt_idx).transpose(0, 2, 1).reshape(TOKENS, ATT_WIDTH)
        p = _pool(zb, w_pool[l].astype(bf16), pool_scale[l][None])
        m = _mem_attn(zb, POOL_WIDTH // MEM_WIDTH, kv)

        h = _combine_ln(h, a, p, m, wt, c_qm, b_gate[l][None], wb_att, wb_pool, wb_mem, wb_out,
                        ln2_g[l][None], ln2_b[l][None])

        h, _ = _ffn_ln(h, w2_up, w2_up, w2_down, ln3_g[l][None], ln3_b[l][None])
    return h.reshape(BATCH, SEQ, D_MODEL)
```

```python
import functools
import math

import jax
import jax.numpy as jnp
import numpy as np
from jax import lax
from jax.experimental import pallas as pl
from jax.experimental.pallas import tpu as pltpu

f32 = jnp.float32
bf16 = jnp.bfloat16
i32 = jnp.int32

D_MODEL = 2048
BATCH = 2
SEQ = 4096
DEPTH = 1
MEM_LEN = 256
ATT_HEADS = 8
HEAD_DIM = 128
ATT_WIDTH = ATT_HEADS * HEAD_DIM
IDX_HEADS = 16
IDX_DIM = 64
TOPK = min(256, SEQ // 4)
POOL_WINDOWS = (2, 4, 8, 16)
N_POOL = len(POOL_WINDOWS)
POOL_GROUP = 128
POOL_WIDTH = N_POOL * POOL_GROUP
MEM_HEADS = 4
MEM_WIDTH = MEM_HEADS * HEAD_DIM
N_BRANCH = 3
D_FF = 5632
ALPHA = (2 * DEPTH) ** 0.25
LN_EPS = 1e-5
TOKENS = BATCH * SEQ

LANES = 128
BF16_ROWS = 16
VMEM_LIMIT = 60 * 1024 * 1024

INT_MIN = -(2 ** 31)
LOG2E = math.log2(math.e)


def _params(*sem):
    return pltpu.CompilerParams(dimension_semantics=sem, vmem_limit_bytes=VMEM_LIMIT)


def _layer_norm(y, g, b):
    mu = jnp.mean(y, axis=-1, keepdims=True)
    d = y - mu
    var = jnp.mean(d * d, axis=-1, keepdims=True)
    return d * lax.rsqrt(var + LN_EPS) * g + b


FFN_TM = 512
FFN_TF = 512


FFN_LN_STEPS = 8
FFN_LN_ROWS = FFN_TM // FFN_LN_STEPS


def _ffn_ln_kernel(x_ref, wa_ref, wu_ref, wd_ref, g_ref, b_ref, *rest, n_cast, has_z0):
    if has_z0:
        z0_ref, rest = rest[0], rest[1:]
    cast_in = rest[:n_cast]
    o_ref = rest[n_cast]
    cast_out = rest[n_cast + 1:2 * n_cast + 1]
    xb_ref, acc_ref = rest[2 * n_cast + 1:]
    i = pl.program_id(0)
    j = pl.program_id(1)
    n_tiles = pl.num_programs(0) - 1
    slot = (i + 1) % 2 if has_z0 else i % 2
    computed = i < n_tiles

    for src, dst in zip(cast_in, cast_out):
        dst[...] = src[...].astype(dst.dtype)

    @pl.when((i == 0) & (j == 0))
    def _():
        if has_z0:
            acc_ref[0] = z0_ref[...]
        else:
            acc_ref[1] = jnp.zeros(acc_ref.shape[1:], f32)

    @pl.when(computed & (j == 0))
    def _():
        x = x_ref[...]
        xb_ref[...] = x.astype(bf16)
        acc_ref[slot] = (2.0 * ALPHA) * x

    def matmuls():
        xb = xb_ref[...]
        a = jnp.dot(xb, wa_ref[...], preferred_element_type=f32)
        u = jnp.dot(xb, wu_ref[...], preferred_element_type=f32)
        act = (a * jax.nn.sigmoid(a) * u).astype(bf16)
        acc_ref[slot] += jnp.dot(act, wd_ref[...], preferred_element_type=f32)

    def layer_norm_slab():
        rows = pl.ds(pl.multiple_of(j * FFN_LN_ROWS, FFN_LN_ROWS), FFN_LN_ROWS)
        z = acc_ref[1 - slot, rows, :]
        mu = jnp.mean(z, axis=-1, keepdims=True)
        d = z - mu
        var = jnp.mean(d * d, axis=-1, keepdims=True)
        o_ref[rows, :] = d * lax.rsqrt(var + 4.0 * LN_EPS) * g_ref[...] + b_ref[...]

    @pl.when(computed & (j < FFN_LN_STEPS))
    def _():
        layer_norm_slab()
        matmuls()

    @pl.when(computed & (j >= FFN_LN_STEPS))
    def _():
        matmuls()

    @pl.when((i == n_tiles) & (j < FFN_LN_STEPS))
    def _():
        layer_norm_slab()


def _cast_block_spec(shape, ni, nj):
    r, c = shape
    br = -(-r // (ni * nj))
    br = -(-br // BF16_ROWS) * BF16_ROWS
    last_block = -(-r // br) - 1
    return pl.BlockSpec((br, c), lambda i, j: (
        jnp.minimum(jnp.minimum(i, ni - 1) * nj + jnp.where(i < ni, j, nj - 1), last_block), 0))


FFN_HEAD_TF = 256


def _ffn_head_kernel(x_ref, wa_ref, wu_ref, wd_ref, z_ref, wa_out, wu_out, wd_out, xb_ref):
    j = pl.program_id(0)

    @pl.when(j == 0)
    def _():
        x = x_ref[...]
        xb_ref[...] = x.astype(bf16)
        z_ref[...] = (2.0 * ALPHA) * x

    wa = wa_ref[...].astype(bf16)
    wu = wu_ref[...].astype(bf16)
    wd = wd_ref[...].astype(bf16)
    wa_out[...] = wa
    wu_out[...] = wu
    wd_out[...] = wd
    xb = xb_ref[...]
    a = jnp.dot(xb, wa, preferred_element_type=f32)
    u = jnp.dot(xb, wu, preferred_element_type=f32)
    act = (a * jax.nn.sigmoid(a) * u).astype(bf16)
    z_ref[...] += jnp.dot(act, wd, preferred_element_type=f32)


def _ffn_head(x, w_up, w_down):
    nf = D_FF // FFN_HEAD_TF
    return pl.pallas_call(
        _ffn_head_kernel,
        grid=(nf,),
        in_specs=[
            pl.BlockSpec((FFN_TM, D_MODEL), lambda j: (0, 0)),
            pl.BlockSpec((D_MODEL, FFN_HEAD_TF), lambda j: (0, j)),
            pl.BlockSpec((D_MODEL, FFN_HEAD_TF), lambda j: (0, j + nf)),
            pl.BlockSpec((FFN_HEAD_TF, D_MODEL), lambda j: (j, 0)),
        ],
        out_specs=[
            pl.BlockSpec((FFN_TM, D_MODEL), lambda j: (0, 0)),
            pl.BlockSpec((D_MODEL, FFN_HEAD_TF), lambda j: (0, j)),
            pl.BlockSpec((D_MODEL, FFN_HEAD_TF), lambda j: (0, j)),
            pl.BlockSpec((FFN_HEAD_TF, D_MODEL), lambda j: (j, 0)),
        ],
        out_shape=[
            jax.ShapeDtypeStruct((FFN_TM, D_MODEL), f32),
            jax.ShapeDtypeStruct((D_MODEL, D_FF), bf16),
            jax.ShapeDtypeStruct((D_MODEL, D_FF), bf16),
            jax.ShapeDtypeStruct((D_FF, D_MODEL), bf16),
        ],
        scratch_shapes=[pltpu.VMEM((FFN_TM, D_MODEL), bf16)],
        compiler_params=_params("arbitrary"),
        name="ffn_head",
    )(x, w_up, w_up, w_down)


def _ffn_ln(x, w_a, w_u, w_down, g, b, casts=(), z0=None):
    n_tok_tiles = TOKENS // FFN_TM
    first = 0 if z0 is None else 1
    ni = n_tok_tiles - first
    nf = D_FF // FFN_TF
    assert FFN_LN_STEPS <= nf
    u0 = (w_u.shape[1] - D_FF) // FFN_TF
    cast_specs = [_cast_block_spec(w.shape, ni, nf) for w in casts]
    z0_specs = [] if z0 is None else [pl.BlockSpec((FFN_TM, D_MODEL), lambda i, j: (0, 0))]
    z0_args = [] if z0 is None else [z0]

    def wj(i, j):
        return jnp.where(i < ni, j, nf - 1)

    outs = pl.pallas_call(
        functools.partial(_ffn_ln_kernel, n_cast=len(casts), has_z0=z0 is not None),
        grid=(ni + 1, nf),
        in_specs=[
            pl.BlockSpec((FFN_TM, D_MODEL), lambda i, j: (jnp.minimum(i + first, n_tok_tiles - 1), 0)),
            pl.BlockSpec((D_MODEL, FFN_TF), lambda i, j: (0, wj(i, j))),
            pl.BlockSpec((D_MODEL, FFN_TF), lambda i, j: (0, wj(i, j) + u0)),
            pl.BlockSpec((FFN_TF, D_MODEL), lambda i, j: (wj(i, j), 0)),
            pl.BlockSpec((1, D_MODEL), lambda i, j: (0, 0)),
            pl.BlockSpec((1, D_MODEL), lambda i, j: (0, 0)),
        ] + z0_specs + cast_specs,
        out_specs=[pl.BlockSpec((FFN_TM, D_MODEL), lambda i, j: (jnp.maximum(i + first - 1, 0), 0))]
        + cast_specs,
        out_shape=[jax.ShapeDtypeStruct((TOKENS, D_MODEL), f32)]
        + [jax.ShapeDtypeStruct(w.shape, bf16) for w in casts],
        scratch_shapes=[pltpu.VMEM((FFN_TM, D_MODEL), bf16),
                        pltpu.VMEM((2, FFN_TM, D_MODEL), f32)],
        compiler_params=_params("arbitrary", "arbitrary"),
        name="ffn_ln",
    )(x, w_a, w_u, w_down, g, b, *z0_args, *casts)
    return outs[0], outs[1:]


_NT = (((1,), (1,)), ((), ()))


def _matmul_kernel(x_ref, w_ref, o_ref, *rest, w_is_transposed, also_block):
    xb_ref = rest[-1]

    @pl.when(pl.program_id(1) == 0)
    def _():
        xb_ref[...] = x_ref[...].astype(bf16)

    if w_is_transposed:
        y = lax.dot_general(xb_ref[...], w_ref[...], _NT, preferred_element_type=f32)
    else:
        y = jnp.dot(xb_ref[...], w_ref[...], preferred_element_type=f32)
    o_ref[...] = y.astype(o_ref.dtype)

    if also_block is not None:
        @pl.when(pl.program_id(1) == also_block)
        def _():
            rest[0][...] = y.astype(rest[0].dtype)


def _matmul(x, w, out_dtype, tm, tn, name, w_is_transposed=False, n=None, also_block=None):
    m, k = x.shape
    if n is None:
        n = w.shape[0] if w_is_transposed else w.shape[1]
    assert n % tn == 0 and m % tm == 0
    if w_is_transposed:
        w_spec = pl.BlockSpec((tn, k), lambda i, j: (j, 0))
    else:
        w_spec = pl.BlockSpec((k, tn), lambda i, j: (0, j))
    out_specs = [pl.BlockSpec((tm, tn), lambda i, j: (i, j))]
    out_shape = [jax.ShapeDtypeStruct((m, n), out_dtype)]
    if also_block is not None:
        out_specs.append(pl.BlockSpec((tm, tn), lambda i, j: (i, 0)))
        out_shape.append(jax.ShapeDtypeStruct((m, tn), out_dtype))
    outs = pl.pallas_call(
        functools.partial(_matmul_kernel, w_is_transposed=w_is_transposed, also_block=also_block),
        grid=(m // tm, n // tn),
        in_specs=[pl.BlockSpec((tm, k), lambda i, j: (i, 0)), w_spec],
        out_specs=out_specs,
        out_shape=out_shape,
        scratch_shapes=[pltpu.VMEM((tm, k), bf16)],
        compiler_params=_params("arbitrary", "arbitrary"),
        name=name,
    )(x, w)
    return outs[0] if also_block is None else outs


PB_TM = 1024
PB_N = POOL_WIDTH + MEM_WIDTH + LANES


def _in_proj_b_kernel(x_ref, wum_ref, wk_ref, o_ref):
    xb = x_ref[...].astype(bf16)
    n_um = POOL_WIDTH + MEM_WIDTH
    o_ref[:, :n_um] = lax.dot_general(xb, wum_ref[...], _NT, preferred_element_type=f32)
    o_ref[:, n_um:] = lax.dot_general(xb, wk_ref[...], _NT, preferred_element_type=f32)


def _in_proj_b(x, wt, row_um, row_kiwi):
    n_um = POOL_WIDTH + MEM_WIDTH
    return pl.pallas_call(
        _in_proj_b_kernel,
        grid=(TOKENS // PB_TM,),
        in_specs=[
            pl.BlockSpec((PB_TM, D_MODEL), lambda i: (i, 0)),
            pl.BlockSpec((pl.Element(n_um), pl.Element(D_MODEL)), lambda i: (row_um, 0)),
            pl.BlockSpec((pl.Element(LANES), pl.Element(D_MODEL)), lambda i: (row_kiwi, 0)),
        ],
        out_specs=pl.BlockSpec((PB_TM, PB_N), lambda i: (i, 0)),
        out_shape=jax.ShapeDtypeStruct((TOKENS, PB_N), f32),
        compiler_params=_params("parallel"),
        name="in_proj_b",
    )(x, wt, wt)


DSA_TQ = 512
DSA_TK = 512
DSA_NCH = SEQ // DSA_TK
DSA_ONES = 16
KEY_NEG_FLT_MAX = -(2 ** 31) + (1 << 23)


def _key_to_f32(key):
    return lax.bitcast_convert_type(key ^ ((key >> 31) & jnp.int32(0x7FFFFFFF)), f32)


def _dsa_kernel(q_ref, k_ref, vt_ref, qi_ref, kk_ref, wt_ref, kf_ref, qf_ref, o_ref,
                sc_scr, sch_scr, acc_scr, qa_scr):
    i = pl.program_id(1)
    q0 = i * DSA_TQ
    nk = (q0 + DSA_TQ - 1) // DSA_TK + 1

    qpos = q0 + lax.broadcasted_iota(i32, (DSA_TK, DSA_TQ), 1)
    kofs = lax.broadcasted_iota(i32, (DSA_TK, DSA_TQ), 0)
    contract_last = (((1,), (1,)), ((), ()))

    def rows8(x):
        return x.reshape(DSA_TK // 8, 8, DSA_TQ)

    wt = wt_ref[...] * (IDX_DIM ** -0.5 * IDX_HEADS ** -0.5)

    def score_chunk(c, carry):
        r0 = pl.multiple_of(c * DSA_TK, DSA_TK)
        kk0 = kk_ref[0, pl.ds(r0, DSA_TK), :]
        kk1 = kk_ref[1, pl.ds(r0, DSA_TK), :]
        acc = jnp.zeros((DSA_TK, DSA_TQ), f32)
        for p in range(IDX_HEADS // 2):
            slab = qi_ref[:, p * LANES:(p + 1) * LANES]
            l0 = lax.dot_general(kk0, slab, contract_last, preferred_element_type=f32)
            l1 = lax.dot_general(kk1, slab, contract_last, preferred_element_type=f32)
            acc = acc + (jnp.maximum(l0, 0.0) * wt[2 * p:2 * p + 1]
                         + jnp.maximum(l1, 0.0) * wt[2 * p + 1:2 * p + 2])
        sc = jnp.where(kofs + r0 <= qpos, acc, -jnp.inf)
        sc_scr[c] = sc
        sch_scr[c] = sc.astype(bf16)
        return carry

    lax.fori_loop(0, nk, score_chunk, 0)

    def count(pred):
        n_acc = 8

        def body(c, accs):
            m = rows8(pred(c))
            accs = list(accs)
            for r in range(DSA_TK // 8):
                accs[r % n_acc] = jnp.where(m[r], accs[r % n_acc] + 1, accs[r % n_acc])
            return tuple(accs)
        accs = lax.fori_loop(0, nk, body, (jnp.zeros((8, DSA_TQ), i32),) * n_acc)
        acc = functools.reduce(lambda a, b: a + b, accs)
        return jnp.sum(acc, axis=0, keepdims=True)

    def count_ge(cand):
        return count(lambda c: sc_scr[c] >= cand)

    def count_ge_coarse(cand):
        n_acc = 2
        one = jnp.ones((), bf16)
        zero = jnp.zeros((), bf16)

        def body(c, accs):
            accs = list(accs)
            for r in range(DSA_TK // BF16_ROWS):
                rows = sch_scr[c, r * BF16_ROWS:(r + 1) * BF16_ROWS, :]
                accs[r % n_acc] = accs[r % n_acc] + jnp.where(rows >= cand, one, zero)
            return tuple(accs)
        accs = lax.fori_loop(0, nk, body, (jnp.zeros((BF16_ROWS, DSA_TQ), bf16),) * n_acc)
        acc = functools.reduce(lambda a, b: a + b, accs)
        return jnp.sum(acc.astype(f32), axis=0, keepdims=True)

    def coarse_cand(key):
        bits = lax.bitcast_convert_type(_key_to_f32(key), i32) & jnp.int32(-65536)
        return lax.bitcast_convert_type(bits, f32).astype(bf16)

    key = jnp.where(count_ge_coarse(jnp.zeros((1, DSA_TQ), bf16)) >= TOPK,
                    jnp.int32(0), jnp.int32(INT_MIN))

    def coarse_step(b, key):
        cand = key | jnp.left_shift(jnp.int32(1), 30 - b)
        return jnp.where(count_ge_coarse(coarse_cand(cand)) >= TOPK, cand, key)

    key = lax.fori_loop(0, 15, coarse_step, key)

    window_bits = 18
    lo = jnp.maximum(key, jnp.int32(INT_MIN + (1 << 17))) - jnp.int32((1 << 15) + 2)

    def fine_step(b, off):
        cand = off | jnp.left_shift(jnp.int32(1), (window_bits - 1) - b)
        return jnp.where(count_ge(_key_to_f32(lo + cand)) >= TOPK, cand, off)

    key = lo + lax.fori_loop(0, window_bits, fine_step, jnp.zeros((1, DSA_TQ), i32))
    thr = _key_to_f32(jnp.maximum(key, jnp.int32(KEY_NEG_FLT_MAX)))

    n_ge = count_ge(thr)

    @pl.when(jnp.max(n_ge) > TOPK)
    def _():
        need = TOPK - count(lambda c: sc_scr[c] > thr)

        def count_eq_below(pos):
            return count(lambda c: (sc_scr[c] == thr) & (kofs + c * DSA_TK < pos))

        def pos_step(b, r):
            cand = r | jnp.left_shift(jnp.int32(1), (SEQ.bit_length() - 2) - b)
            return jnp.where(count_eq_below(cand) < need, cand, r)
        r = lax.fori_loop(0, SEQ.bit_length() - 1, pos_step, jnp.zeros((1, DSA_TQ), i32))
        r = jnp.where(n_ge > TOPK, r, jnp.int32(SEQ))

        def drop(c, carry):
            sc = sc_scr[c]
            sc_scr[c] = jnp.where((sc == thr) & (kofs + c * DSA_TK > r), -jnp.inf, sc)
            return carry
        lax.fori_loop(0, nk, drop, 0)

    acc_scr[...] = jnp.zeros(acc_scr.shape, f32)
    c1 = HEAD_DIM ** -0.5 * LOG2E
    ones = jnp.ones((DSA_ONES, DSA_TK), bf16)

    for h in range(ATT_HEADS):
        qa_scr[h, :, :HEAD_DIM] = q_ref[:, h * HEAD_DIM:(h + 1) * HEAD_DIM]
        qa_scr[h, :, HEAD_DIM:] = jnp.broadcast_to(qf_ref[h, 0:1, :], (DSA_TQ, LANES))

    def attn_chunk(c, ms):
        r0 = pl.multiple_of(c * DSA_TK, DSA_TK)
        madd = jnp.where(sc_scr[c] >= thr, 0.0, -jnp.inf)
        kf = kf_ref[pl.ds(r0, DSA_TK), :]

        def qk(h):
            hd = slice(h * HEAD_DIM, (h + 1) * HEAD_DIM)
            k_aug = jnp.concatenate([k_ref[pl.ds(r0, DSA_TK), hd], kf], axis=1)
            return lax.dot_general(k_aug, qa_scr[h], contract_last, preferred_element_type=f32)

        def pv(h, alpha, p):
            hd = slice(h * HEAD_DIM, (h + 1) * HEAD_DIM)
            vt1 = jnp.concatenate([vt_ref[c, hd, :], ones], axis=0)
            acc_scr[h] = alpha * acc_scr[h] + jnp.dot(vt1, p, preferred_element_type=f32)

        new_ms = []
        s_next = qk(0)
        pending = None
        for h in range(ATT_HEADS):
            s = s_next
            if h + 1 < ATT_HEADS:
                s_next = qk(h + 1)
            if pending is not None:
                pv(*pending)
            t = s * c1 + madd
            m_old = ms[h]
            m_new = jnp.maximum(m_old, jnp.max(t, axis=0, keepdims=True))
            alpha = jnp.exp2(m_old - m_new)
            p = jnp.exp2(t - m_new).astype(bf16)
            pending = (h, alpha, p)
            new_ms.append(m_new)
        pv(*pending)
        return tuple(new_ms)

    m0 = jnp.full((1, DSA_TQ), -1e30, f32)
    lax.fori_loop(0, nk, attn_chunk, (m0,) * ATT_HEADS)

    for h in range(ATT_HEADS):
        o_ref[h * HEAD_DIM:(h + 1) * HEAD_DIM, :] = (
            acc_scr[h, :HEAD_DIM, :] / acc_scr[h, HEAD_DIM:HEAD_DIM + 1, :]).astype(o_ref.dtype)


def _alibi_features():
    assert ATT_HEADS == 8 and HEAD_DIM == 128 and SEQ <= 64 * 64
    pieces, rest = [], math.sqrt(2.0)
    for _ in range(7):
        p = float(np.asarray(rest, np.float32).astype(jnp.bfloat16).astype(np.float64))
        pieces.append(p)
        rest -= p
    qf = np.zeros((ATT_HEADS, BF16_ROWS, LANES), np.float32)
    for h in range(ATT_HEADS):
        for i, p in enumerate(pieces):
            qf[h, :, 2 * i] = qf[h, :, 2 * i + 1] = p * 2.0 ** (2 - h)
    pos = np.arange(SEQ)
    kf = np.zeros((SEQ, LANES), np.float32)
    for i in range(len(pieces)):
        kf[:, 2 * i] = pos - pos % 64
        kf[:, 2 * i + 1] = pos % 64
    return jnp.asarray(kf, bf16), jnp.asarray(qf, bf16)


def _dsa(za, vt, kk, wt_idx):
    nq = SEQ // DSA_TQ
    assert SEQ // BF16_ROWS <= 256
    kf, qf = _alibi_features()
    once = pl.Buffered(1)
    return pl.pallas_call(
        _dsa_kernel,
        grid=(BATCH, nq),
        in_specs=[
            pl.BlockSpec((DSA_TQ, ATT_WIDTH), lambda b, i: (b * nq + i, 0)),
            pl.BlockSpec((SEQ, ATT_WIDTH), lambda b, i: (b, 1), pipeline_mode=once),
            pl.BlockSpec((None, DSA_NCH, ATT_WIDTH, DSA_TK), lambda b, i: (b, 0, 0, 0),
                         pipeline_mode=once),
            pl.BlockSpec((DSA_TQ, IDX_HEADS * IDX_DIM), lambda b, i: (b * nq + i, 3)),
            pl.BlockSpec((2, SEQ, LANES), lambda b, i: (0, b, 0), pipeline_mode=once),
            pl.BlockSpec((IDX_HEADS, DSA_TQ), lambda b, i: (0, b * nq + i)),
            pl.BlockSpec((SEQ, LANES), lambda b, i: (0, 0), pipeline_mode=once),
            pl.BlockSpec((ATT_HEADS, BF16_ROWS, LANES), lambda b, i: (0, 0, 0)),
        ],
        out_specs=pl.BlockSpec((None, ATT_WIDTH, DSA_TQ), lambda b, i: (b, 0, i)),
        out_shape=jax.ShapeDtypeStruct((BATCH, ATT_WIDTH, SEQ), bf16),
        scratch_shapes=[
            pltpu.VMEM((DSA_NCH, DSA_TK, DSA_TQ), f32),
            pltpu.VMEM((DSA_NCH, DSA_TK, DSA_TQ), bf16),
            pltpu.VMEM((ATT_HEADS, HEAD_DIM + DSA_ONES, DSA_TQ), f32),
            pltpu.VMEM((ATT_HEADS, DSA_TQ, HEAD_DIM + LANES), bf16),
        ],
        compiler_params=_params("arbitrary", "arbitrary"),
        name="dsa",
    )(za, za, vt, za, kk, wt_idx, kf, qf)


def _pool_kernel(u_ref, wp_ref, ps_ref, o_ref):
    row = lax.broadcasted_iota(i32, (SEQ, POOL_GROUP), 0)
    for g, win in enumerate(POOL_WINDOWS):
        x = u_ref[:, g * POOL_GROUP:(g + 1) * POOL_GROUP]
        s = x
        k = 1
        while k < win:
            s = s + jnp.where(row >= k, pltpu.roll(s, k, axis=0), 0.0)
            k *= 2
        cnt = jnp.minimum(row + 1, win).astype(f32)
        pooled = (s / cnt - x).astype(bf16)
        mixed = jnp.dot(pooled, wp_ref[g], preferred_element_type=f32)
        o_ref[:, g * POOL_GROUP:(g + 1) * POOL_GROUP] = (
            mixed * ps_ref[:, g * POOL_GROUP:(g + 1) * POOL_GROUP]).astype(o_ref.dtype)


def _pool(zb, w_pool, pool_scale):
    return pl.pallas_call(
        _pool_kernel,
        grid=(BATCH,),
        in_specs=[
            pl.BlockSpec((SEQ, POOL_WIDTH), lambda b: (b, 0)),
            pl.BlockSpec((N_POOL, POOL_GROUP, POOL_GROUP), lambda b: (0, 0, 0)),
            pl.BlockSpec((1, POOL_WIDTH), lambda b: (0, 0)),
        ],
        out_specs=pl.BlockSpec((SEQ, POOL_WIDTH), lambda b: (b, 0)),
        out_shape=jax.ShapeDtypeStruct((TOKENS, POOL_WIDTH), bf16),
        compiler_params=_params("parallel"),
        name="pool",
    )(zb, w_pool, pool_scale)


MEM_TQ = 2048


def _mem_attn_kernel(q_ref, k_ref, v_ref, o_ref):
    for h in range(MEM_HEADS):
        sl = slice(h * HEAD_DIM, (h + 1) * HEAD_DIM)
        s = lax.dot_general(q_ref[:, sl].astype(bf16), k_ref[:, sl], (((1,), (1,)), ((), ())),
                            preferred_element_type=f32) * (HEAD_DIM ** -0.5)
        p = jnp.exp(s - jnp.max(s, axis=1, keepdims=True))
        l = jnp.sum(p, axis=1, keepdims=True)
        o = jnp.dot(p.astype(bf16), v_ref[:, sl], preferred_element_type=f32)
        o_ref[:, sl] = (o / l).astype(o_ref.dtype)


def _mem_attn(zb, qcol, kv):
    nq = SEQ // MEM_TQ
    return pl.pallas_call(
        _mem_attn_kernel,
        grid=(BATCH, nq),
        in_specs=[
            pl.BlockSpec((MEM_TQ, MEM_WIDTH), lambda b, i: (b * nq + i, qcol)),
            pl.BlockSpec((MEM_LEN, MEM_WIDTH), lambda b, i: (b, 0)),
            pl.BlockSpec((MEM_LEN, MEM_WIDTH), lambda b, i: (b, 1)),
        ],
        out_specs=pl.BlockSpec((MEM_TQ, MEM_WIDTH), lambda b, i: (b * nq + i, 0)),
        out_shape=jax.ShapeDtypeStruct((TOKENS, MEM_WIDTH), bf16),
        compiler_params=_params("parallel", "parallel"),
        name="mem_attn",
    )(zb, kv, kv)


CMB_TM = 512
CMB_TC = 512


def _combine_ln_kernel(h_ref, a_ref, p_ref, m_ref, wg0_ref, wg1_ref, wg2_ref,
                       bg0_ref, bg1_ref, bg2_ref, wa_ref, wp_ref, wm_ref, wo_ref,
                       g_ref, b_ref, o_ref, hb_ref, acc_ref):
    i = pl.program_id(0)
    j = pl.program_id(1)
    n_tiles = pl.num_programs(0) - 1
    slot = i % 2
    ln_rows = CMB_TM // (D_MODEL // CMB_TC)

    @pl.when((i == 0) & (j == 0))
    def _():
        acc_ref[1] = jnp.zeros(acc_ref.shape[1:], f32)

    @pl.when((i < n_tiles) & (j == 0))
    def _():
        h = h_ref[...]
        hb_ref[...] = h.astype(bf16)
        acc_ref[slot] = ALPHA * h

    def layer_norm_slab():
        rows = pl.ds(pl.multiple_of(j * ln_rows, ln_rows), ln_rows)
        o_ref[rows, :] = _layer_norm(acc_ref[1 - slot, rows, :], g_ref[...], b_ref[...])

    def matmuls():
        hb = hb_ref[...]

        def gate(wg_ref, bg_ref):
            logits = lax.dot_general(hb, wg_ref[...], _NT, preferred_element_type=f32)
            return jax.nn.sigmoid(logits + bg_ref[...])

        y = gate(wg0_ref, bg0_ref) * jnp.dot(a_ref[...], wa_ref[...], preferred_element_type=f32)
        y = y + gate(wg1_ref, bg1_ref) * jnp.dot(p_ref[...], wp_ref[...], preferred_element_type=f32)
        y = y + gate(wg2_ref, bg2_ref) * jnp.dot(m_ref[...], wm_ref[...], preferred_element_type=f32)
        acc_ref[slot] += jnp.dot(y.astype(bf16), wo_ref[...], preferred_element_type=f32)

    @pl.when(i < n_tiles)
    def _():
        layer_norm_slab()
        matmuls()

    @pl.when(i == n_tiles)
    def _():
        layer_norm_slab()


def _combine_ln(h, a, p, m, wt_gate, row_gate, b_gate, w_a, w_p, w_m, w_out, g, b):
    nc = D_MODEL // CMB_TC
    gate_rows = pl.Element(CMB_TC)
    all_cols = pl.Element(D_MODEL)
    assert row_gate % BF16_ROWS == 0

    def gate_row(blk):
        return pl.multiple_of(row_gate + blk * CMB_TC, BF16_ROWS)

    ni = TOKENS // CMB_TM

    def cj(i, j):
        return jnp.where(i < ni, j, nc - 1)

    row = lambda i, j: (jnp.minimum(i, ni - 1), 0)
    col = lambda i, j: (0, cj(i, j))
    return pl.pallas_call(
        _combine_ln_kernel,
        grid=(ni + 1, nc),
        in_specs=[
            pl.BlockSpec((CMB_TM, D_MODEL), row),
            pl.BlockSpec((CMB_TM, ATT_WIDTH), row),
            pl.BlockSpec((CMB_TM, POOL_WIDTH), row),
            pl.BlockSpec((CMB_TM, MEM_WIDTH), row),
            pl.BlockSpec((gate_rows, all_cols), lambda i, j: (gate_row(cj(i, j)), 0)),
            pl.BlockSpec((gate_rows, all_cols), lambda i, j: (gate_row(cj(i, j) + nc), 0)),
            pl.BlockSpec((gate_rows, all_cols), lambda i, j: (gate_row(cj(i, j) + 2 * nc), 0)),
            pl.BlockSpec((1, CMB_TC), lambda i, j: (0, cj(i, j))),
            pl.BlockSpec((1, CMB_TC), lambda i, j: (0, cj(i, j) + nc)),
            pl.BlockSpec((1, CMB_TC), lambda i, j: (0, cj(i, j) + 2 * nc)),
            pl.BlockSpec((ATT_WIDTH, CMB_TC), col),
            pl.BlockSpec((POOL_WIDTH, CMB_TC), col),
            pl.BlockSpec((MEM_WIDTH, CMB_TC), col),
            pl.BlockSpec((CMB_TC, D_MODEL), lambda i, j: (cj(i, j), 0)),
            pl.BlockSpec((1, D_MODEL), lambda i, j: (0, 0)),
            pl.BlockSpec((1, D_MODEL), lambda i, j: (0, 0)),
        ],
        out_specs=pl.BlockSpec((CMB_TM, D_MODEL), lambda i, j: (jnp.maximum(i - 1, 0), 0)),
        out_shape=jax.ShapeDtypeStruct((TOKENS, D_MODEL), f32),
        scratch_shapes=[pltpu.VMEM((CMB_TM, D_MODEL), bf16),
                        pltpu.VMEM((2, CMB_TM, D_MODEL), f32)],
        compiler_params=_params("arbitrary", "arbitrary"),
        name="combine_ln",
    )(h, a, p, m, wt_gate, wt_gate, wt_gate, b_gate, b_gate, b_gate, w_a, w_p, w_m, w_out, g, b)


def kernel(x, mem, w_ffn1_up, w_ffn1_down, ln1_g, ln1_b, w_in, b_gate, w_mem_kv, w_pool,
           pool_scale, w_br_att, w_br_pool, w_br_mem, w_out, ln2_g, ln2_b, w_ffn2_up,
           w_ffn2_down, ln3_g, ln3_b):
    h = x.reshape(TOKENS, D_MODEL)
    memf = mem.reshape(BATCH * MEM_LEN, D_MODEL)
    for l in range(DEPTH):
        c_qi = 3 * ATT_WIDTH + IDX_HEADS * IDX_DIM
        c_wi = c_qi + IDX_DIM + IDX_HEADS
        c_qm = c_wi + POOL_WIDTH + MEM_WIDTH

        z0, w1_a, w1_u, w1_down = _ffn_head(h, w_ffn1_up[l], w_ffn1_down[l])
        later = (w_in[l].T, w_ffn2_up[l], w_ffn2_down[l], w_mem_kv[l], w_br_att[l], w_br_pool[l],
                 w_br_mem[l], w_out[l])
        h, (wt, w2_up, w2_down, wb_mem_kv, wb_att, wb_pool, wb_mem, wb_out) = _ffn_ln(
            h, w1_a, w1_u, w1_down, ln1_g[l][None], ln1_b[l][None], z0=z0,
            casts=later)

        za, v = _matmul(h, wt, bf16, 1024, ATT_WIDTH, "in_proj_a", w_is_transposed=True, n=c_qi,
                        also_block=2)
        zb = _in_proj_b(h, wt, c_wi, c_qi)
        kv = _matmul(memf, wb_mem_kv, bf16, BATCH * MEM_LEN, 512, "mem_kv")

        c_ki = POOL_WIDTH + MEM_WIDTH
        ki = zb[:, c_ki:c_ki + IDX_DIM].astype(bf16)
        zk = jnp.zeros_like(ki)
        kk = jnp.stack([jnp.concatenate([ki, zk], axis=1), jnp.concatenate([zk, ki], axis=1)])
        wt_idx = zb[:, c_ki + IDX_DIM:c_ki + IDX_DIM + IDX_HEADS].T
        vt = v.reshape(BATCH, DSA_NCH, DSA_TK, ATT_WIDTH).transpose(0, 1, 3, 2)

        a = _dsa(za, vt, kk, wt_idx).transpose(0, 2, 1).reshape(TOKENS, ATT_WIDTH)
        p = _pool(zb, w_pool[l].astype(bf16), pool_scale[l][None])
        m = _mem_attn(zb, POOL_WIDTH // MEM_WIDTH, kv)

        h = _combine_ln(h, a, p, m, wt, c_qm, b_gate[l][None], wb_att, wb_pool, wb_mem, wb_out,
                        ln2_g[l][None], ln2_b[l][None])

        h, _ = _ffn_ln(h, w2_up, w2_up, w2_down, ln3_g[l][None], ln3_b[l][None])
    return h.reshape(BATCH, SEQ, D_MODEL)
```

```python
import functools
import math

import jax
import jax.numpy as jnp
import numpy as np
from jax import lax
from jax.experimental import pallas as pl
from jax.experimental.pallas import tpu as pltpu

f32 = jnp.float32
bf16 = jnp.bfloat16
i32 = jnp.int32

D_MODEL = 2048
BATCH = 2
SEQ = 4096
DEPTH = 1
MEM_LEN = 256
ATT_HEADS = 8
HEAD_DIM = 128
ATT_WIDTH = ATT_HEADS * HEAD_DIM
IDX_HEADS = 16
IDX_DIM = 64
TOPK = min(256, SEQ // 4)
POOL_WINDOWS = (2, 4, 8, 16)
N_POOL = len(POOL_WINDOWS)
POOL_GROUP = 128
POOL_WIDTH = N_POOL * POOL_GROUP
MEM_HEADS = 4
MEM_WIDTH = MEM_HEADS * HEAD_DIM
N_BRANCH = 3
D_FF = 5632
ALPHA = (2 * DEPTH) ** 0.25
LN_EPS = 1e-5
TOKENS = BATCH * SEQ

LANES = 128
BF16_ROWS = 16
VMEM_LIMIT = 60 * 1024 * 1024

INT_MIN = -(2 ** 31)
LOG2E = math.log2(math.e)


def _params(*sem):
    return pltpu.CompilerParams(dimension_semantics=sem, vmem_limit_bytes=VMEM_LIMIT)


def _layer_norm(y, g, b):
    mu = jnp.mean(y, axis=-1, keepdims=True)
    d = y - mu
    var = jnp.mean(d * d, axis=-1, keepdims=True)
    return d * lax.rsqrt(var + LN_EPS) * g + b


FFN_TM = 512
FFN_TF = 512


FFN_LN_STEPS = 8
FFN_LN_ROWS = FFN_TM // FFN_LN_STEPS


def _ffn_ln_kernel(x_ref, wa_ref, wu_ref, wd_ref, g_ref, b_ref, *rest, n_cast, has_z0):
    if has_z0:
        z0_ref, rest = rest[0], rest[1:]
    cast_in = rest[:n_cast]
    o_ref = rest[n_cast]
    cast_out = rest[n_cast + 1:2 * n_cast + 1]
    xb_ref, acc_ref = rest[2 * n_cast + 1:]
    i = pl.program_id(0)
    j = pl.program_id(1)
    n_tiles = pl.num_programs(0) - 1
    slot = (i + 1) % 2 if has_z0 else i % 2
    computed = i < n_tiles

    for src, dst in zip(cast_in, cast_out):
        dst[...] = src[...].astype(dst.dtype)

    @pl.when((i == 0) & (j == 0))
    def _():
        if has_z0:
            acc_ref[0] = z0_ref[...]
        else:
            acc_ref[1] = jnp.zeros(acc_ref.shape[1:], f32)

    @pl.when(computed & (j == 0))
    def _():
        x = x_ref[...]
        xb_ref[...] = x.astype(bf16)
        acc_ref[slot] = (2.0 * ALPHA) * x

    def matmuls():
        xb = xb_ref[...]
        a = jnp.dot(xb, wa_ref[...], preferred_element_type=f32)
        u = jnp.dot(xb, wu_ref[...], preferred_element_type=f32)
        act = (a * jax.nn.sigmoid(a) * u).astype(bf16)
        acc_ref[slot] += jnp.dot(act, wd_ref[...], preferred_element_type=f32)

    def layer_norm_slab():
        rows = pl.ds(pl.multiple_of(j * FFN_LN_ROWS, FFN_LN_ROWS), FFN_LN_ROWS)
        z = acc_ref[1 - slot, rows, :]
        mu = jnp.mean(z, axis=-1, keepdims=True)
        d = z - mu
        var = jnp.mean(d * d, axis=-1, keepdims=True)
        o_ref[rows, :] = d * lax.rsqrt(var + 4.0 * LN_EPS) * g_ref[...] + b_ref[...]

    @pl.when(computed & (j < FFN_LN_STEPS))
    def _():
        layer_norm_slab()
        matmuls()

    @pl.when(computed & (j >= FFN_LN_STEPS))
    def _():
        matmuls()

    @pl.when((i == n_tiles) & (j < FFN_LN_STEPS))
    def _():
        layer_norm_slab()


def _cast_block_spec(shape, ni, nj):
    r, c = shape
    br = -(-r // (ni * nj))
    br = -(-br // BF16_ROWS) * BF16_ROWS
    last_block = -(-r // br) - 1
    return pl.BlockSpec((br, c), lambda i, j: (
        jnp.minimum(jnp.minimum(i, ni - 1) * nj + jnp.where(i < ni, j, nj - 1), last_block), 0))


FFN_HEAD_TF = 256


def _ffn_head_kernel(x_ref, wa_ref, wu_ref, wd_ref, z_ref, wa_out, wu_out, wd_out, xb_ref):
    j = pl.program_id(0)

    @pl.when(j == 0)
    def _():
        x = x_ref[...]
        xb_ref[...] = x.astype(bf16)
        z_ref[...] = (2.0 * ALPHA) * x

    wa = wa_ref[...].astype(bf16)
    wu = wu_ref[...].astype(bf16)
    wd = wd_ref[...].astype(bf16)
    wa_out[...] = wa
    wu_out[...] = wu
    wd_out[...] = wd
    xb = xb_ref[...]
    a = jnp.dot(xb, wa, preferred_element_type=f32)
    u = jnp.dot(xb, wu, preferred_element_type=f32)
    act = (a * jax.nn.sigmoid(a) * u).astype(bf16)
    z_ref[...] += jnp.dot(act, wd, preferred_element_type=f32)


def _ffn_head(x, w_up, w_down):
    nf = D_FF // FFN_HEAD_TF
    return pl.pallas_call(
        _ffn_head_kernel,
        grid=(nf,),
        in_specs=[
            pl.BlockSpec((FFN_TM, D_MODEL), lambda j: (0, 0)),
            pl.BlockSpec((D_MODEL, FFN_HEAD_TF), lambda j: (0, j)),
            pl.BlockSpec((D_MODEL, FFN_HEAD_TF), lambda j: (0, j + nf)),
            pl.BlockSpec((FFN_HEAD_TF, D_MODEL), lambda j: (j, 0)),
        ],
        out_specs=[
            pl.BlockSpec((FFN_TM, D_MODEL), lambda j: (0, 0)),
            pl.BlockSpec((D_MODEL, FFN_HEAD_TF), lambda j: (0, j)),
            pl.BlockSpec((D_MODEL, FFN_HEAD_TF), lambda j: (0, j)),
            pl.BlockSpec((FFN_HEAD_TF, D_MODEL), lambda j: (j, 0)),
        ],
        out_shape=[
            jax.ShapeDtypeStruct((FFN_TM, D_MODEL), f32),
            jax.ShapeDtypeStruct((D_MODEL, D_FF), bf16),
            jax.ShapeDtypeStruct((D_MODEL, D_FF), bf16),
            jax.ShapeDtypeStruct((D_FF, D_MODEL), bf16),
        ],
        scratch_shapes=[pltpu.VMEM((FFN_TM, D_MODEL), bf16)],
        compiler_params=_params("arbitrary"),
        name="ffn_head",
    )(x, w_up, w_up, w_down)


def _ffn_ln(x, w_a, w_u, w_down, g, b, casts=(), z0=None):
    n_tok_tiles = TOKENS // FFN_TM
    first = 0 if z0 is None else 1
    ni = n_tok_tiles - first
    nf = D_FF // FFN_TF
    assert FFN_LN_STEPS <= nf
    u0 = (w_u.shape[1] - D_FF) // FFN_TF
    cast_specs = [_cast_block_spec(w.shape, ni, nf) for w in casts]
    z0_specs = [] if z0 is None else [pl.BlockSpec((FFN_TM, D_MODEL), lambda i, j: (0, 0))]
    z0_args = [] if z0 is None else [z0]

    def wj(i, j):
        return jnp.where(i < ni, j, nf - 1)

    outs = pl.pallas_call(
        functools.partial(_ffn_ln_kernel, n_cast=len(casts), has_z0=z0 is not None),
        grid=(ni + 1, nf),
        in_specs=[
            pl.BlockSpec((FFN_TM, D_MODEL), lambda i, j: (jnp.minimum(i + first, n_tok_tiles - 1), 0)),
            pl.BlockSpec((D_MODEL, FFN_TF), lambda i, j: (0, wj(i, j))),
            pl.BlockSpec((D_MODEL, FFN_TF), lambda i, j: (0, wj(i, j) + u0)),
            pl.BlockSpec((FFN_TF, D_MODEL), lambda i, j: (wj(i, j), 0)),
            pl.BlockSpec((1, D_MODEL), lambda i, j: (0, 0)),
            pl.BlockSpec((1, D_MODEL), lambda i, j: (0, 0)),
        ] + z0_specs + cast_specs,
        out_specs=[pl.BlockSpec((FFN_TM, D_MODEL), lambda i, j: (jnp.maximum(i + first - 1, 0), 0))]
        + cast_specs,
        out_shape=[jax.ShapeDtypeStruct((TOKENS, D_MODEL), f32)]
        + [jax.ShapeDtypeStruct(w.shape, bf16) for w in casts],
        scratch_shapes=[pltpu.VMEM((FFN_TM, D_MODEL), bf16),
                        pltpu.VMEM((2, FFN_TM, D_MODEL), f32)],
        compiler_params=_params("arbitrary", "arbitrary"),
        name="ffn_ln",
    )(x, w_a, w_u, w_down, g, b, *z0_args, *casts)
    return outs[0], outs[1:]


_NT = (((1,), (1,)), ((), ()))


def _matmul_kernel(x_ref, w_ref, o_ref, *rest, w_is_transposed, also_block):
    xb_ref = rest[-1]

    @pl.when(pl.program_id(1) == 0)
    def _():
        xb_ref[...] = x_ref[...].astype(bf16)

    if w_is_transposed:
        y = lax.dot_general(xb_ref[...], w_ref[...], _NT, preferred_element_type=f32)
    else:
        y = jnp.dot(xb_ref[...], w_ref[...], preferred_element_type=f32)
    o_ref[...] = y.astype(o_ref.dtype)

    if also_block is not None:
        @pl.when(pl.program_id(1) == also_block)
        def _():
            rest[0][...] = y.astype(rest[0].dtype)


def _matmul(x, w, out_dtype, tm, tn, name, w_is_transposed=False, n=None, also_block=None):
    m, k = x.shape
    if n is None:
        n = w.shape[0] if w_is_transposed else w.shape[1]
    assert n % tn == 0 and m % tm == 0
    if w_is_transposed:
        w_spec = pl.BlockSpec((tn, k), lambda i, j: (j, 0))
    else:
        w_spec = pl.BlockSpec((k, tn), lambda i, j: (0, j))
    out_specs = [pl.BlockSpec((tm, tn), lambda i, j: (i, j))]
    out_shape = [jax.ShapeDtypeStruct((m, n), out_dtype)]
    if also_block is not None:
        out_specs.append(pl.BlockSpec((tm, tn), lambda i, j: (i, 0)))
        out_shape.append(jax.ShapeDtypeStruct((m, tn), out_dtype))
    outs = pl.pallas_call(
        functools.partial(_matmul_kernel, w_is_transposed=w_is_transposed, also_block=also_block),
        grid=(m // tm, n // tn),
        in_specs=[pl.BlockSpec((tm, k), lambda i, j: (i, 0)), w_spec],
        out_specs=out_specs,
        out_shape=out_shape,
        scratch_shapes=[pltpu.VMEM((tm, k), bf16)],
        compiler_params=_params("arbitrary", "arbitrary"),
        name=name,
    )(x, w)
    return outs[0] if also_block is None else outs


PB_TM = 1024
PB_N = POOL_WIDTH + MEM_WIDTH + LANES


def _in_proj_b_kernel(x_ref, wum_ref, wk_ref, o_ref):
    xb = x_ref[...].astype(bf16)
    n_um = POOL_WIDTH + MEM_WIDTH
    o_ref[:, :n_um] = lax.dot_general(xb, wum_ref[...], _NT, preferred_element_type=f32)
    o_ref[:, n_um:] = lax.dot_general(xb, wk_ref[...], _NT, preferred_element_type=f32)


def _in_proj_b(x, wt, row_um, row_kiwi):
    n_um = POOL_WIDTH + MEM_WIDTH
    return pl.pallas_call(
        _in_proj_b_kernel,
        grid=(TOKENS // PB_TM,),
        in_specs=[
            pl.BlockSpec((PB_TM, D_MODEL), lambda i: (i, 0)),
            pl.BlockSpec((pl.Element(n_um), pl.Element(D_MODEL)), lambda i: (row_um, 0)),
            pl.BlockSpec((pl.Element(LANES), pl.Element(D_MODEL)), lambda i: (row_kiwi, 0)),
        ],
        out_specs=pl.BlockSpec((PB_TM, PB_N), lambda i: (i, 0)),
        out_shape=jax.ShapeDtypeStruct((TOKENS, PB_N), f32),
        compiler_params=_params("parallel"),
        name="in_proj_b",
    )(x, wt, wt)


DSA_TQ = 512
DSA_TK = 512
DSA_NCH = SEQ // DSA_TK
DSA_ONES = 16
KEY_NEG_FLT_MAX = -(2 ** 31) + (1 << 23)


def _key_to_f32(key):
    return lax.bitcast_convert_type(key ^ ((key >> 31) & jnp.int32(0x7FFFFFFF)), f32)


def _dsa_kernel(q_ref, k_ref, vt_ref, qi_ref, kk_ref, wt_ref, kf_ref, qf_ref, o_ref,
                sc_scr, sch_scr, acc_scr, qa_scr):
    i = pl.program_id(1)
    q0 = i * DSA_TQ
    nk = (q0 + DSA_TQ - 1) // DSA_TK + 1

    qpos = q0 + lax.broadcasted_iota(i32, (DSA_TK, DSA_TQ), 1)
    kofs = lax.broadcasted_iota(i32, (DSA_TK, DSA_TQ), 0)
    contract_last = (((1,), (1,)), ((), ()))

    def rows8(x):
        return x.reshape(DSA_TK // 8, 8, DSA_TQ)

    wt = wt_ref[...] * (IDX_DIM ** -0.5 * IDX_HEADS ** -0.5)

    def score_chunk(c, carry):
        r0 = pl.multiple_of(c * DSA_TK, DSA_TK)
        kk0 = kk_ref[0, pl.ds(r0, DSA_TK), :]
        kk1 = kk_ref[1, pl.ds(r0, DSA_TK), :]
        acc = jnp.zeros((DSA_TK, DSA_TQ), f32)
        for p in range(IDX_HEADS // 2):
            slab = qi_ref[:, p * LANES:(p + 1) * LANES]
            l0 = lax.dot_general(kk0, slab, contract_last, preferred_element_type=f32)
            l1 = lax.dot_general(kk1, slab, contract_last, preferred_element_type=f32)
            acc = acc + (jnp.maximum(l0, 0.0) * wt[2 * p:2 * p + 1]
                         + jnp.maximum(l1, 0.0) * wt[2 * p + 1:2 * p + 2])
        sc = jnp.where(kofs + r0 <= qpos, acc, -jnp.inf)
        sc_scr[c] = sc
        sch_scr[c] = sc.astype(bf16)
        return carry

    lax.fori_loop(0, nk, score_chunk, 0)

    def count(pred):
        n_acc = 8

        def body(c, accs):
            m = rows8(pred(c))
            accs = list(accs)
            for r in range(DSA_TK // 8):
                accs[r % n_acc] = jnp.where(m[r], accs[r % n_acc] + 1, accs[r % n_acc])
            return tuple(accs)
        accs = lax.fori_loop(0, nk, body, (jnp.zeros((8, DSA_TQ), i32),) * n_acc)
        acc = functools.reduce(lambda a, b: a + b, accs)
        return jnp.sum(acc, axis=0, keepdims=True)

    def count_ge(cand):
        return count(lambda c: sc_scr[c] >= cand)

    def count_ge_coarse(cand):
        n_acc = 2
        one = jnp.ones((), bf16)
        zero = jnp.zeros((), bf16)

        def body(c, accs):
            accs = list(accs)
            for r in range(DSA_TK // BF16_ROWS):
                rows = sch_scr[c, r * BF16_ROWS:(r + 1) * BF16_ROWS, :]
                accs[r % n_acc] = accs[r % n_acc] + jnp.where(rows >= cand, one, zero)
            return tuple(accs)
        accs = lax.fori_loop(0, nk, body, (jnp.zeros((BF16_ROWS, DSA_TQ), bf16),) * n_acc)
        acc = functools.reduce(lambda a, b: a + b, accs)
        return jnp.sum(acc.astype(f32), axis=0, keepdims=True)

    def coarse_cand(key):
        bits = lax.bitcast_convert_type(_key_to_f32(key), i32) & jnp.int32(-65536)
        return lax.bitcast_convert_type(bits, f32).astype(bf16)

    key = jnp.where(count_ge_coarse(jnp.zeros((1, DSA_TQ), bf16)) >= TOPK,
                    jnp.int32(0), jnp.int32(INT_MIN))

    def coarse_step(b, key):
        cand = key | jnp.left_shift(jnp.int32(1), 30 - b)
        return jnp.where(count_ge_coarse(coarse_cand(cand)) >= TOPK, cand, key)

    key = lax.fori_loop(0, 15, coarse_step, key)

    window_bits = 18
    lo = jnp.maximum(key, jnp.int32(INT_MIN + (1 << 17))) - jnp.int32((1 << 15) + 2)

    def fine_step(b, off):
        cand = off | jnp.left_shift(jnp.int32(1), (window_bits - 1) - b)
        return jnp.where(count_ge(_key_to_f32(lo + cand)) >= TOPK, cand, off)

    key = lo + lax.fori_loop(0, window_bits, fine_step, jnp.zeros((1, DSA_TQ), i32))
    thr = _key_to_f32(jnp.maximum(key, jnp.int32(KEY_NEG_FLT_MAX)))

    n_ge = count_ge(thr)

    @pl.when(jnp.max(n_ge) > TOPK)
    def _():
        need = TOPK - count(lambda c: sc_scr[c] > thr)

        def count_eq_below(pos):
            return count(lambda c: (sc_scr[c] == thr) & (kofs + c * DSA_TK < pos))

        def pos_step(b, r):
            cand = r | jnp.left_shift(jnp.int32(1), (SEQ.bit_length() - 2) - b)
            return jnp.where(count_eq_below(cand) < need, cand, r)
        r = lax.fori_loop(0, SEQ.bit_length() - 1, pos_step, jnp.zeros((1, DSA_TQ), i32))
        r = jnp.where(n_ge > TOPK, r, jnp.int32(SEQ))

        def drop(c, carry):
            sc = sc_scr[c]
            sc_scr[c] = jnp.where((sc == thr) & (kofs + c * DSA_TK > r), -jnp.inf, sc)
            return carry
        lax.fori_loop(0, nk, drop, 0)

    acc_scr[...] = jnp.zeros(acc_scr.shape, f32)
    c1 = HEAD_DIM ** -0.5 * LOG2E
    ones = jnp.ones((DSA_ONES, DSA_TK), bf16)

    for h in range(ATT_HEADS):
        qa_scr[h, :, :HEAD_DIM] = q_ref[:, h * HEAD_DIM:(h + 1) * HEAD_DIM]
        qa_scr[h, :, HEAD_DIM:] = jnp.broadcast_to(qf_ref[h, 0:1, :], (DSA_TQ, LANES))

    def attn_chunk(c, ms):
        r0 = pl.multiple_of(c * DSA_TK, DSA_TK)
        madd = jnp.where(sc_scr[c] >= thr, 0.0, -jnp.inf)
        kf = kf_ref[pl.ds(r0, DSA_TK), :]

        def qk(h):
            hd = slice(h * HEAD_DIM, (h + 1) * HEAD_DIM)
            k_aug = jnp.concatenate([k_ref[pl.ds(r0, DSA_TK), hd], kf], axis=1)
            return lax.dot_general(k_aug, qa_scr[h], contract_last, preferred_element_type=f32)

        def pv(h, alpha, p):
            hd = slice(h * HEAD_DIM, (h + 1) * HEAD_DIM)
            vt1 = jnp.concatenate([vt_ref[c, hd, :], ones], axis=0)
            acc_scr[h] = alpha * acc_scr[h] + jnp.dot(vt1, p, preferred_element_type=f32)

        new_ms = []
        s_next = qk(0)
        pending = None
        for h in range(ATT_HEADS):
            s = s_next
            if h + 1 < ATT_HEADS:
                s_next = qk(h + 1)
            if pending is not None:
                pv(*pending)
            t = s * c1 + madd
            m_old = ms[h]
            m_new = jnp.maximum(m_old, jnp.max(t, axis=0, keepdims=True))
            alpha = jnp.exp2(m_old - m_new)
            p = jnp.exp2(t - m_new).astype(bf16)
            pending = (h, alpha, p)
            new_ms.append(m_new)
        pv(*pending)
        return tuple(new_ms)

    m0 = jnp.full((1, DSA_TQ), -1e30, f32)
    lax.fori_loop(0, nk, attn_chunk, (m0,) * ATT_HEADS)

    for h in range(ATT_HEADS):
        o_ref[h * HEAD_DIM:(h + 1) * HEAD_DIM, :] = (
            acc_scr[h, :HEAD_DIM, :] / acc_scr[h, HEAD_DIM:HEAD_DIM + 1, :]).astype(o_ref.dtype)


def _alibi_features():
    assert ATT_HEADS == 8 and HEAD_DIM == 128 and SEQ <= 64 * 64
    pieces, rest = [], math.sqrt(2.0)
    for _ in range(7):
        p = float(np.asarray(rest, np.float32).astype(jnp.bfloat16).astype(np.float64))
        pieces.append(p)
        rest -= p
    qf = np.zeros((ATT_HEADS, BF16_ROWS, LANES), np.float32)
    for h in range(ATT_HEADS):
        for i, p in enumerate(pieces):
            qf[h, :, 2 * i] = qf[h, :, 2 * i + 1] = p * 2.0 ** (2 - h)
    pos = np.arange(SEQ)
    kf = np.zeros((SEQ, LANES), np.float32)
    for i in range(len(pieces)):
        kf[:, 2 * i] = pos - pos % 64
        kf[:, 2 * i + 1] = pos % 64
    return jnp.asarray(kf, bf16), jnp.asarray(qf, bf16)


def _dsa(za, vt, kk, wt_idx):
    nq = SEQ // DSA_TQ
    assert SEQ // BF16_ROWS <= 256
    kf, qf = _alibi_features()
    once = pl.Buffered(1)
    return pl.pallas_call(
        _dsa_kernel,
        grid=(BATCH, nq),
        in_specs=[
            pl.BlockSpec((DSA_TQ, ATT_WIDTH), lambda b, i: (b * nq + i, 0)),
            pl.BlockSpec((SEQ, ATT_WIDTH), lambda b, i: (b, 1), pipeline_mode=once),
            pl.BlockSpec((None, DSA_NCH, ATT_WIDTH, DSA_TK), lambda b, i: (b, 0, 0, 0),
                         pipeline_mode=once),
            pl.BlockSpec((DSA_TQ, IDX_HEADS * IDX_DIM), lambda b, i: (b * nq + i, 3)),
            pl.BlockSpec((2, SEQ, LANES), lambda b, i: (0, b, 0), pipeline_mode=once),
            pl.BlockSpec((IDX_HEADS, DSA_TQ), lambda b, i: (0, b * nq + i)),
            pl.BlockSpec((SEQ, LANES), lambda b, i: (0, 0), pipeline_mode=once),
            pl.BlockSpec((ATT_HEADS, BF16_ROWS, LANES), lambda b, i: (0, 0, 0)),
        ],
        out_specs=pl.BlockSpec((None, ATT_WIDTH, DSA_TQ), lambda b, i: (b, 0, i)),
        out_shape=jax.ShapeDtypeStruct((BATCH, ATT_WIDTH, SEQ), bf16),
        scratch_shapes=[
            pltpu.VMEM((DSA_NCH, DSA_TK, DSA_TQ), f32),
            pltpu.VMEM((DSA_NCH, DSA_TK, DSA_TQ), bf16),
            pltpu.VMEM((ATT_HEADS, HEAD_DIM + DSA_ONES, DSA_TQ), f32),
            pltpu.VMEM((ATT_HEADS, DSA_TQ, HEAD_DIM + LANES), bf16),
        ],
        compiler_params=_params("arbitrary", "arbitrary"),
        name="dsa",
    )(za, za, vt, za, kk, wt_idx, kf, qf)


def _pool_kernel(u_ref, wp_ref, ps_ref, o_ref):
    row = lax.broadcasted_iota(i32, (SEQ, POOL_GROUP), 0)
    for g, win in enumerate(POOL_WINDOWS):
        x = u_ref[:, g * POOL_GROUP:(g + 1) * POOL_GROUP]
        s = x
        k = 1
        while k < win:
            s = s + jnp.where(row >= k, pltpu.roll(s, k, axis=0), 0.0)
            k *= 2
        cnt = jnp.minimum(row + 1, win).astype(f32)
        pooled = (s / cnt - x).astype(bf16)
        mixed = jnp.dot(pooled, wp_ref[g], preferred_element_type=f32)
        o_ref[:, g * POOL_GROUP:(g + 1) * POOL_GROUP] = (
            mixed * ps_ref[:, g * POOL_GROUP:(g + 1) * POOL_GROUP]).astype(o_ref.dtype)


def _pool(zb, w_pool, pool_scale):
    return pl.pallas_call(
        _pool_kernel,
        grid=(BATCH,),
        in_specs=[
            pl.BlockSpec((SEQ, POOL_WIDTH), lambda b: (b, 0)),
            pl.BlockSpec((N_POOL, POOL_GROUP, POOL_GROUP), lambda b: (0, 0, 0)),
            pl.BlockSpec((1, POOL_WIDTH), lambda b: (0, 0)),
        ],
        out_specs=pl.BlockSpec((SEQ, POOL_WIDTH), lambda b: (b, 0)),
        out_shape=jax.ShapeDtypeStruct((TOKENS, POOL_WIDTH), bf16),
        compiler_params=_params("parallel"),
        name="pool",
    )(zb, w_pool, pool_scale)


MEM_TQ = 2048


def _mem_attn_kernel(q_ref, mem_ref, wkv_ref, o_ref, kv_scr):
    @pl.when(pl.program_id(1) == 0)
    def _():
        kv_scr[...] = jnp.dot(mem_ref[...].astype(bf16), wkv_ref[...],
                              preferred_element_type=f32).astype(bf16)

    for h in range(MEM_HEADS):
        sl = slice(h * HEAD_DIM, (h + 1) * HEAD_DIM)
        vl = slice(MEM_WIDTH + h * HEAD_DIM, MEM_WIDTH + (h + 1) * HEAD_DIM)
        s = lax.dot_general(q_ref[:, sl].astype(bf16), kv_scr[:, sl], (((1,), (1,)), ((), ())),
                            preferred_element_type=f32) * (HEAD_DIM ** -0.5)
        p = jnp.exp(s - jnp.max(s, axis=1, keepdims=True))
        l = jnp.sum(p, axis=1, keepdims=True)
        o = jnp.dot(p.astype(bf16), kv_scr[:, vl], preferred_element_type=f32)
        o_ref[:, sl] = (o / l).astype(o_ref.dtype)


def _mem_attn(zb, qcol, mem, w_kv):
    nq = SEQ // MEM_TQ
    return pl.pallas_call(
        _mem_attn_kernel,
        grid=(BATCH, nq),
        in_specs=[
            pl.BlockSpec((MEM_TQ, MEM_WIDTH), lambda b, i: (b * nq + i, qcol)),
            pl.BlockSpec((MEM_LEN, D_MODEL), lambda b, i: (b, 0)),
            pl.BlockSpec((D_MODEL, 2 * MEM_WIDTH), lambda b, i: (0, 0)),
        ],
        out_specs=pl.BlockSpec((MEM_TQ, MEM_WIDTH), lambda b, i: (b * nq + i, 0)),
        out_shape=jax.ShapeDtypeStruct((TOKENS, MEM_WIDTH), bf16),
        scratch_shapes=[pltpu.VMEM((MEM_LEN, 2 * MEM_WIDTH), bf16)],
        compiler_params=_params("arbitrary", "arbitrary"),
        name="mem_attn",
    )(zb, mem, w_kv)


CMB_TM = 512
CMB_TC = 512


def _combine_ln_kernel(h_ref, a_ref, p_ref, m_ref, wg0_ref, wg1_ref, wg2_ref,
                       bg0_ref, bg1_ref, bg2_ref, wa_ref, wp_ref, wm_ref, wo_ref,
                       g_ref, b_ref, o_ref, hb_ref, acc_ref):
    i = pl.program_id(0)
    j = pl.program_id(1)
    n_tiles = pl.num_programs(0) - 1
    slot = i % 2
    ln_rows = CMB_TM // (D_MODEL // CMB_TC)

    @pl.when((i == 0) & (j == 0))
    def _():
        acc_ref[1] = jnp.zeros(acc_ref.shape[1:], f32)

    @pl.when((i < n_tiles) & (j == 0))
    def _():
        h = h_ref[...]
        hb_ref[...] = h.astype(bf16)
        acc_ref[slot] = ALPHA * h

    def layer_norm_slab():
        rows = pl.ds(pl.multiple_of(j * ln_rows, ln_rows), ln_rows)
        o_ref[rows, :] = _layer_norm(acc_ref[1 - slot, rows, :], g_ref[...], b_ref[...])

    def matmuls():
        hb = hb_ref[...]

        def gate(wg_ref, bg_ref):
            logits = lax.dot_general(hb, wg_ref[...], _NT, preferred_element_type=f32)
            return jax.nn.sigmoid(logits + bg_ref[...])

        y = gate(wg0_ref, bg0_ref) * jnp.dot(a_ref[...], wa_ref[...], preferred_element_type=f32)
        y = y + gate(wg1_ref, bg1_ref) * jnp.dot(p_ref[...], wp_ref[...], preferred_element_type=f32)
        y = y + gate(wg2_ref, bg2_ref) * jnp.dot(m_ref[...], wm_ref[...], preferred_element_type=f32)
        acc_ref[slot] += jnp.dot(y.astype(bf16), wo_ref[...], preferred_element_type=f32)

    @pl.when(i < n_tiles)
    def _():
        layer_norm_slab()
        matmuls()

    @pl.when(i == n_tiles)
    def _():
        layer_norm_slab()


def _combine_ln(h, a, p, m, wt_gate, row_gate, b_gate, w_a, w_p, w_m, w_out, g, b):
    nc = D_MODEL // CMB_TC
    gate_rows = pl.Element(CMB_TC)
    all_cols = pl.Element(D_MODEL)
    assert row_gate % BF16_ROWS == 0

    def gate_row(blk):
        return pl.multiple_of(row_gate + blk * CMB_TC, BF16_ROWS)

    ni = TOKENS // CMB_TM

    def cj(i, j):
        return jnp.where(i < ni, j, nc - 1)

    row = lambda i, j: (jnp.minimum(i, ni - 1), 0)
    col = lambda i, j: (0, cj(i, j))
    return pl.pallas_call(
        _combine_ln_kernel,
        grid=(ni + 1, nc),
        in_specs=[
            pl.BlockSpec((CMB_TM, D_MODEL), row),
            pl.BlockSpec((CMB_TM, ATT_WIDTH), row),
            pl.BlockSpec((CMB_TM, POOL_WIDTH), row),
            pl.BlockSpec((CMB_TM, MEM_WIDTH), row),
            pl.BlockSpec((gate_rows, all_cols), lambda i, j: (gate_row(cj(i, j)), 0)),
            pl.BlockSpec((gate_rows, all_cols), lambda i, j: (gate_row(cj(i, j) + nc), 0)),
            pl.BlockSpec((gate_rows, all_cols), lambda i, j: (gate_row(cj(i, j) + 2 * nc), 0)),
            pl.BlockSpec((1, CMB_TC), lambda i, j: (0, cj(i, j))),
            pl.BlockSpec((1, CMB_TC), lambda i, j: (0, cj(i, j) + nc)),
            pl.BlockSpec((1, CMB_TC), lambda i, j: (0, cj(i, j) + 2 * nc)),
            pl.BlockSpec((ATT_WIDTH, CMB_TC), col),
            pl.BlockSpec((POOL_WIDTH, CMB_TC), col),
            pl.BlockSpec((MEM_WIDTH, CMB_TC), col),
            pl.BlockSpec((CMB_TC, D_MODEL), lambda i, j: (cj(i, j), 0)),
            pl.BlockSpec((1, D_MODEL), lambda i, j: (0, 0)),
            pl.BlockSpec((1, D_MODEL), lambda i, j: (0, 0)),
        ],
        out_specs=pl.BlockSpec((CMB_TM, D_MODEL), lambda i, j: (jnp.maximum(i - 1, 0), 0)),
        out_shape=jax.ShapeDtypeStruct((TOKENS, D_MODEL), f32),
        scratch_shapes=[pltpu.VMEM((CMB_TM, D_MODEL), bf16),
                        pltpu.VMEM((2, CMB_TM, D_MODEL), f32)],
        compiler_params=_params("arbitrary", "arbitrary"),
        name="combine_ln",
    )(h, a, p, m, wt_gate, wt_gate, wt_gate, b_gate, b_gate, b_gate, w_a, w_p, w_m, w_out, g, b)


def kernel(x, mem, w_ffn1_up, w_ffn1_down, ln1_g, ln1_b, w_in, b_gate, w_mem_kv, w_pool,
           pool_scale, w_br_att, w_br_pool, w_br_mem, w_out, ln2_g, ln2_b, w_ffn2_up,
           w_ffn2_down, ln3_g, ln3_b):
    h = x.reshape(TOKENS, D_MODEL)
    memf = mem.reshape(BATCH * MEM_LEN, D_MODEL)
    for l in range(DEPTH):
        c_qi = 3 * ATT_WIDTH + IDX_HEADS * IDX_DIM
        c_wi = c_qi + IDX_DIM + IDX_HEADS
        c_qm = c_wi + POOL_WIDTH + MEM_WIDTH

        z0, w1_a, w1_u, w1_down = _ffn_head(h, w_ffn1_up[l], w_ffn1_down[l])
        later = (w_in[l].T, w_ffn2_up[l], w_ffn2_down[l], w_mem_kv[l], w_br_att[l], w_br_pool[l],
                 w_br_mem[l], w_out[l])
        h, (wt, w2_up, w2_down, wb_mem_kv, wb_att, wb_pool, wb_mem, wb_out) = _ffn_ln(
            h, w1_a, w1_u, w1_down, ln1_g[l][None], ln1_b[l][None], z0=z0,
            casts=later)

        za, v = _matmul(h, wt, bf16, 1024, ATT_WIDTH, "in_proj_a", w_is_transposed=True, n=c_qi,
                        also_block=2)
        zb = _in_proj_b(h, wt, c_wi, c_qi)

        c_ki = POOL_WIDTH + MEM_WIDTH
        ki = zb[:, c_ki:c_ki + IDX_DIM].astype(bf16)
        zk = jnp.zeros_like(ki)
        kk = jnp.stack([jnp.concatenate([ki, zk], axis=1), jnp.concatenate([zk, ki], axis=1)])
        wt_idx = zb[:, c_ki + IDX_DIM:c_ki + IDX_DIM + IDX_HEADS].T
        vt = v.reshape(BATCH, DSA_NCH, DSA_TK, ATT_WIDTH).transpose(0, 1, 3, 2)

        a = _dsa(za, vt, kk, wt_idx).transpose(0, 2, 1).reshape(TOKENS, ATT_WIDTH)
        p = _pool(zb, w_pool[l].astype(bf16), pool_scale[l][None])
        m = _mem_attn(zb, POOL_WIDTH // MEM_WIDTH, memf, wb_mem_kv)

        h = _combine_ln(h, a, p, m, wt, c_qm, b_gate[l][None], wb_att, wb_pool, wb_mem, wb_out,
                        ln2_g[l][None], ln2_b[l][None])

        h, _ = _ffn_ln(h, w2_up, w2_up, w2_down, ln3_g[l][None], ln3_b[l][None])
    return h.reshape(BATCH, SEQ, D_MODEL)
```
